```python
import jax
import jax.numpy as jnp
from jax import lax
import numpy as np

D_MODEL = 2048
BATCH = 8
SEQ = 2048
DEPTH = 1

SSD_EXPAND = 2
SSD_D_INNER = SSD_EXPAND * D_MODEL
SSD_HEAD_DIM = 64
SSD_HEADS = SSD_D_INNER // SSD_HEAD_DIM
SSD_GROUPS = 8
SSD_STATE = 128
SSD_CONV = 4
SSD_CHUNK = 128
SSD_BC_DIM = SSD_GROUPS * SSD_STATE
SSD_CONV_DIM = SSD_D_INNER + 2 * SSD_BC_DIM

ATT_HEAD_DIM = 128
ATT_HEADS = D_MODEL // ATT_HEAD_DIM
ATT_WIDTH = ATT_HEADS * ATT_HEAD_DIM
MOBA_BLOCK = 256
MOBA_TOPK = 3
MOBA_QCHUNK = 8
ROPE_THETA = 10000.0

_SSD_COLS = SSD_D_INNER + SSD_CONV_DIM + SSD_HEADS
IN_SPLITS = (
    SSD_D_INNER,
    SSD_D_INNER + SSD_CONV_DIM,
    _SSD_COLS,
    _SSD_COLS + ATT_WIDTH,
    _SSD_COLS + 2 * ATT_WIDTH,
    _SSD_COLS + 3 * ATT_WIDTH,
    _SSD_COLS + 3 * ATT_WIDTH + D_MODEL,
)
IN_PROJ_DIM = _SSD_COLS + 3 * ATT_WIDTH + 2 * D_MODEL

N_EXPERTS = 64
TOP_K = 8
N_EXPERT_GROUPS = 8
TOPK_GROUPS = 4
EXPERT_DIM = 512
SHARED_DIM = 512
ROUTED_SCALE = 2.5
MOE_BLOCK = 128

NORM_EPS = 1e-6

kernel_name = "hybrid_ssd_moba_moe_block"


def rmsnorm(x, g):
    xf = x.astype(jnp.float32)
    y = xf * lax.rsqrt(jnp.mean(xf * xf, axis=-1, keepdims=True) + NORM_EPS)
    return (y * g.astype(jnp.float32)).astype(x.dtype)


def rotary(x, positions):
    half = x.shape[-1] // 2
    inv_freq = 1.0 / (ROPE_THETA ** (jnp.arange(half, dtype=jnp.float32) / half))
    ang = positions.astype(jnp.float32)[..., None] * inv_freq
    cos = jnp.cos(ang)[:, :, None, :]
    sin = jnp.sin(ang)[:, :, None, :]
    x1 = x[..., :half].astype(jnp.float32)
    x2 = x[..., half:].astype(jnp.float32)
    out = jnp.concatenate([x1 * cos - x2 * sin, x2 * cos + x1 * sin], axis=-1)
    return out.astype(x.dtype)


def causal_dwconv(x, w, b):
    k_width, ch = w.shape
    y = lax.conv_general_dilated(
        x, w[:, None, :].astype(x.dtype), window_strides=(1,), padding=[(k_width - 1, 0)],
        dimension_numbers=("NWC", "WIO", "NWC"), feature_group_count=ch)
    return y + b.astype(x.dtype)


def ssd_scan(xh, dt, a_neg, bm, cm, d_skip):
    bsz, s, h, p = xh.shape
    g, n = bm.shape[2], bm.shape[3]
    hg = h // g
    q = SSD_CHUNK
    nc = s // q
    a = (dt * a_neg).reshape(bsz, s, g, hg)
    xdt = (xh * dt[..., None].astype(xh.dtype)).reshape(bsz, s, g, hg, p)

    def to_chunks(t):
        return jnp.moveaxis(t.reshape(bsz, nc, q, *t.shape[2:]), 1, 0)

    causal = jnp.tril(jnp.ones((q, q), dtype=bool))[None, :, :, None, None]

    def step(state, inp):
        xc, ac, bc, cc = inp
        acum = jnp.cumsum(ac, axis=1)
        seg = acum[:, :, None] - acum[:, None, :]
        lmat = jnp.exp(jnp.where(causal, seg, -jnp.inf))
        cb = jnp.einsum("bign,bjgn->bijg", cc, bc)
        y_diag = jnp.einsum("bijg,bijgh,bjghp->bighp", cb, lmat, xc)
        y_off = jnp.einsum("bign,bghpn,bigh->bighp", cc, state, jnp.exp(acum))
        decay_to_end = jnp.exp(acum[:, -1:] - acum)
        new_state = (state * jnp.exp(acum[:, -1])[..., None, None]
                     + jnp.einsum("bjgn,bjgh,bjghp->bghpn", bc, decay_to_end, xc))
        return new_state, (y_diag + y_off).astype(jnp.float32)

    state0 = jnp.zeros((bsz, g, hg, p, n), jnp.float32)
    _, ys = lax.scan(step, state0, (to_chunks(xdt), to_chunks(a), to_chunks(bm), to_chunks(cm)))
    y = jnp.moveaxis(ys, 0, 1).reshape(bsz, s, h, p)
    y = y + xh.astype(jnp.float32) * d_skip.astype(jnp.float32)[:, None]
    return y.astype(xh.dtype)


def ssd_branch(z, xbc, dt_raw, conv_w, conv_b, dt_bias, a_log, d_skip, norm_w):
    bsz, s, _ = z.shape
    xbc = jax.nn.silu(causal_dwconv(xbc, conv_w, conv_b))
    xs, bm, cm = jnp.split(xbc, [SSD_D_INNER, SSD_D_INNER + SSD_BC_DIM], axis=-1)
    xh = xs.reshape(bsz, s, SSD_HEADS, SSD_HEAD_DIM)
    bm = bm.reshape(bsz, s, SSD_GROUPS, SSD_STATE)
    cm = cm.reshape(bsz, s, SSD_GROUPS, SSD_STATE)
    dt = jax.nn.softplus(dt_raw.astype(jnp.float32) + dt_bias.astype(jnp.float32))
    a_neg = -jnp.exp(a_log.astype(jnp.float32))
    y = ssd_scan(xh, dt, a_neg, bm, cm, d_skip).reshape(bsz, s, SSD_D_INNER)
    y = y * jax.nn.silu(z)
    yg = rmsnorm(y.reshape(bsz, s, SSD_GROUPS, SSD_D_INNER // SSD_GROUPS),
                 norm_w.reshape(SSD_GROUPS, SSD_D_INNER // SSD_GROUPS))
    return yg.reshape(bsz, s, SSD_D_INNER)


def moba_attention(q, k, v):
    bsz, s, h, dh = q.shape
    nb = -(-s // MOBA_BLOCK)
    sp = nb * MOBA_BLOCK
    pad = [(0, 0), (0, sp - s), (0, 0), (0, 0)]
    q, k, v = [jnp.pad(t, pad).transpose(0, 2, 1, 3) for t in (q, k, v)]
    kb = k.reshape(bsz, h, nb, MOBA_BLOCK, dh)
    vb = v.reshape(bsz, h, nb, MOBA_BLOCK, dh)
    kmean = jnp.mean(kb.astype(jnp.float32), axis=3).astype(k.dtype)
    topk = min(MOBA_TOPK, nb)
    scale = dh ** -0.5
    bi = jnp.arange(bsz)[:, None, None, None]
    hi = jnp.arange(h)[None, :, None, None]
    blk_ids = jnp.arange(nb)
    qc_n = MOBA_QCHUNK

    def chunk(ci):
        start = ci * qc_n
        own = start // MOBA_BLOCK
        qc = lax.dynamic_slice_in_dim(q, start, qc_n, axis=2)
        gate = jnp.einsum("bhqd,bhnd->bhqn", qc, kmean).astype(jnp.float32)
        gate = jnp.where(blk_ids < own, gate, -jnp.inf)
        _, sel = lax.top_k(gate, topk)
        valid = sel < own
        ksel = kb[bi, hi, sel]
        vsel = vb[bi, hi, sel]
        s_sel = jnp.einsum("bhqd,bhqkjd->bhqkj", qc, ksel).astype(jnp.float32) * scale
        s_sel = jnp.where(valid[..., None], s_sel, -jnp.inf)
        k_own = lax.dynamic_index_in_dim(kb, own, axis=2, keepdims=False)
        v_own = lax.dynamic_index_in_dim(vb, own, axis=2, keepdims=False)
        s_own = jnp.einsum("bhqd,bhjd->bhqj", qc, k_own).astype(jnp.float32) * scale
        qpos = start + jnp.arange(qc_n)
        kpos = own * MOBA_BLOCK + jnp.arange(MOBA_BLOCK)
        s_own = jnp.where(kpos[None, :] <= qpos[:, None], s_own, -jnp.inf)
        s_all = jnp.concatenate([s_sel.reshape(bsz, h, qc_n, topk * MOBA_BLOCK), s_own], axis=-1)
        prob = jax.nn.softmax(s_all, axis=-1).astype(v.dtype)
        p_sel = prob[..., :topk * MOBA_BLOCK].reshape(bsz, h, qc_n, topk, MOBA_BLOCK)
        p_own = prob[..., topk * MOBA_BLOCK:]
        return (jnp.einsum("bhqkj,bhqkjd->bhqd", p_sel, vsel)
                + jnp.einsum("bhqj,bhjd->bhqd", p_own, v_own))

    o = lax.map(chunk, jnp.arange(sp // qc_n))
    o = o.transpose(1, 0, 3, 2, 4).reshape(bsz, sp, h * dh)
    return o[:, :s]


def token_mixer(h, positions, w_in, conv_w, conv_b, dt_bias, a_log, d_skip, ssd_norm_w,
                w_ssd_br, w_attn_br, w_out):
    bsz, s, _ = h.shape
    proj = jnp.einsum("bsd,de->bse", h, w_in)
    z, xbc, dt_raw, q, k, v, g_ssd, g_attn = jnp.split(proj, IN_SPLITS, axis=-1)
    y_ssd = ssd_branch(z, xbc, dt_raw, conv_w, conv_b, dt_bias, a_log, d_skip, ssd_norm_w)
    q = rotary(q.reshape(bsz, s, ATT_HEADS, ATT_HEAD_DIM), positions)
    k = rotary(k.reshape(bsz, s, ATT_HEADS, ATT_HEAD_DIM), positions)
    v = v.reshape(bsz, s, ATT_HEADS, ATT_HEAD_DIM)
    y_att = moba_attention(q, k, v)
    br_ssd = jnp.einsum("bse,ed->bsd", y_ssd, w_ssd_br)
    br_att = jnp.einsum("bse,ed->bsd", y_att, w_attn_br)
    merged = jax.nn.sigmoid(g_ssd) * br_ssd + jax.nn.sigmoid(g_attn) * br_att
    return jnp.einsum("bsd,de->bse", merged, w_out)


def moe_ffn(h, w_router, router_bias, w_gate, w_up, w_down, ws_gate, ws_up, ws_down):
    bsz, s, d = h.shape
    t = bsz * s
    ht = h.reshape(t, d)
    scores = jax.nn.sigmoid(jnp.einsum("td,de->te", ht, w_router).astype(jnp.float32))
    biased = scores + router_bias.astype(jnp.float32)
    grp = biased.reshape(t, N_EXPERT_GROUPS, N_EXPERTS // N_EXPERT_GROUPS)
    grp_score = jnp.sum(lax.top_k(grp, 2)[0], axis=-1)
    _, grp_idx = lax.top_k(grp_score, TOPK_GROUPS)
    grp_mask = jnp.any(grp_idx[..., None] == jnp.arange(N_EXPERT_GROUPS), axis=-2)
    expert_mask = jnp.repeat(grp_mask, N_EXPERTS // N_EXPERT_GROUPS, axis=-1)
    _, top_idx = lax.top_k(jnp.where(expert_mask, biased, -jnp.inf), TOP_K)
    top_w = jnp.take_along_axis(scores, top_idx, axis=-1)
    top_w = top_w / jnp.sum(top_w, axis=-1, keepdims=True) * ROUTED_SCALE

    n_assign = t * TOP_K
    flat_e = top_idx.reshape(-1)
    order = jnp.argsort(flat_e)
    sorted_e = flat_e[order]
    sorted_tok = (order // TOP_K).astype(jnp.int32)
    sorted_w = top_w.reshape(-1)[order]
    counts = jnp.bincount(flat_e, length=N_EXPERTS)
    starts = jnp.cumsum(counts) - counts
    padded = (counts + MOE_BLOCK - 1) // MOE_BLOCK * MOE_BLOCK
    pad_end = jnp.cumsum(padded)
    pad_start = pad_end - padded
    dest = pad_start[sorted_e] + jnp.arange(n_assign) - starts[sorted_e]
    n_blocks = -(-n_assign // MOE_BLOCK) + N_EXPERTS
    n_rows = n_blocks * MOE_BLOCK
    row_tok = jnp.full((n_rows,), t, jnp.int32).at[dest].set(sorted_tok)
    row_w = jnp.zeros((n_rows,), jnp.float32).at[dest].set(sorted_w)
    blk_exp = jnp.minimum(
        jnp.searchsorted(pad_end, jnp.arange(n_blocks) * MOE_BLOCK, side="right"), N_EXPERTS - 1)
    h_pad = jnp.concatenate([ht, jnp.zeros((1, d), ht.dtype)], axis=0)

    def run_block(args):
        tok, wrow, e = args
        xb = h_pad[tok]
        act = jax.nn.silu(xb @ w_gate[e]) * (xb @ w_up[e])
        return (act @ w_down[e]) * wrow[:, None].astype(xb.dtype)

    y_rows = lax.map(run_block, (row_tok.reshape(n_blocks, MOE_BLOCK),
                                 row_w.reshape(n_blocks, MOE_BLOCK), blk_exp))
    routed = jax.ops.segment_sum(y_rows.reshape(n_rows, d), row_tok, num_segments=t + 1)[:t]
    shared = (jax.nn.silu(ht @ ws_gate) * (ht @ ws_up)) @ ws_down
    return (routed.astype(ht.dtype) + shared).reshape(bsz, s, d)


def hybrid_layer(x, cond, positions, w_ada, b_ada, g_pre_mix, g_post_mix, g_pre_ffn, g_post_ffn,
                 w_in, conv_w, conv_b, dt_bias, a_log, d_skip, ssd_norm_w, w_ssd_br, w_attn_br,
                 w_out, w_router, router_bias, w_gate, w_up, w_down, ws_gate, ws_up, ws_down):
    mod = (jnp.einsum("bd,de->be", cond, w_ada) + b_ada)[:, None, :]
    sh1, sc1, gt1, sh2, sc2, gt2 = jnp.split(mod, 6, axis=-1)
    h = rmsnorm(x, g_pre_mix) * (1.0 + sc1) + sh1
    y = token_mixer(h, positions, w_in, conv_w, conv_b, dt_bias, a_log, d_skip, ssd_norm_w,
                    w_ssd_br, w_attn_br, w_out)
    x = x + gt1 * rmsnorm(y, g_post_mix)
    h = rmsnorm(x, g_pre_ffn) * (1.0 + sc2) + sh2
    y = moe_ffn(h, w_router, router_bias, w_gate, w_up, w_down, ws_gate, ws_up, ws_down)
    return x + gt2 * rmsnorm(y, g_post_ffn)


def setup_inputs(seed: int = 0) -> dict:
    key = jax.random.key(seed)
    ks = jax.random.split(key, 32)
    f32 = jnp.float32
    nl = DEPTH

    def nrm(k, shape, fan_in):
        return jax.random.normal(k, shape, f32) * (fan_in ** -0.5)

    def small(k, shape):
        return 0.01 * jax.random.normal(k, shape, f32)

    def gain(k, shape):
        return 1.0 + 0.02 * jax.random.normal(k, shape, f32)

    x = jax.random.normal(ks[0], (BATCH, SEQ, D_MODEL), f32)
    c = jax.random.normal(ks[1], (BATCH, D_MODEL), f32)
    offs = jax.random.randint(ks[2], (BATCH, 1), 0, 4096, jnp.int32)
    positions = offs + jnp.arange(SEQ, dtype=jnp.int32)[None, :]
    dt0 = jnp.exp(jax.random.uniform(ks[12], (nl, SSD_HEADS), f32,
                                     float(np.log(1e-3)), float(np.log(1e-1))))
    return {
        "x": x,
        "c": c,
        "positions": positions,
        "w_ada": nrm(ks[3], (nl, D_MODEL, 6 * D_MODEL), D_MODEL),
        "b_ada": small(ks[4], (nl, 6 * D_MODEL)),
        "g_pre_mix": gain(ks[5], (nl, D_MODEL)),
        "g_post_mix": gain(ks[6], (nl, D_MODEL)),
        "g_pre_ffn": gain(ks[7], (nl, D_MODEL)),
        "g_post_ffn": gain(ks[8], (nl, D_MODEL)),
        "w_in": nrm(ks[9], (nl, D_MODEL, IN_PROJ_DIM), D_MODEL),
        "conv_w": nrm(ks[10], (nl, SSD_CONV, SSD_CONV_DIM), SSD_CONV),
        "conv_b": small(ks[11], (nl, SSD_CONV_DIM)),
        "dt_bias": dt0 + jnp.log(-jnp.expm1(-dt0)),
        "a_log": jnp.log(jax.random.uniform(ks[13], (nl, SSD_HEADS), f32, 1.0, 16.0)),
        "d_skip": 1.0 + 0.1 * jax.random.normal(ks[14], (nl, SSD_HEADS), f32),
        "ssd_norm_w": gain(ks[15], (nl, SSD_D_INNER)),
        "w_ssd_br": nrm(ks[16], (nl, SSD_D_INNER, D_MODEL), SSD_D_INNER),
        "w_attn_br": nrm(ks[17], (nl, ATT_WIDTH, D_MODEL), ATT_WIDTH),
        "w_out": nrm(ks[18], (nl, D_MODEL, D_MODEL), D_MODEL),
        "w_router": nrm(ks[19], (nl, D_MODEL, N_EXPERTS), D_MODEL),
        "router_bias": small(ks[20], (nl, N_EXPERTS)),
        "w_gate": nrm(ks[21], (nl, N_EXPERTS, D_MODEL, EXPERT_DIM), D_MODEL),
        "w_up": nrm(ks[22], (nl, N_EXPERTS, D_MODEL, EXPERT_DIM), D_MODEL),
        "w_down": nrm(ks[23], (nl, N_EXPERTS, EXPERT_DIM, D_MODEL), EXPERT_DIM),
        "ws_gate": nrm(ks[24], (nl, D_MODEL, SHARED_DIM), D_MODEL),
        "ws_up": nrm(ks[25], (nl, D_MODEL, SHARED_DIM), D_MODEL),
        "ws_down": nrm(ks[26], (nl, SHARED_DIM, D_MODEL), SHARED_DIM),
    }


def reference(x, c, positions, w_ada, b_ada, g_pre_mix, g_post_mix, g_pre_ffn, g_post_ffn,
              w_in, conv_w, conv_b, dt_bias, a_log, d_skip, ssd_norm_w, w_ssd_br, w_attn_br,
              w_out, w_router, router_bias, w_gate, w_up, w_down, ws_gate, ws_up, ws_down):
    cond = jax.nn.silu(c)
    for l in range(DEPTH):
        x = hybrid_layer(x, cond, positions, w_ada[l], b_ada[l], g_pre_mix[l], g_post_mix[l],
                         g_pre_ffn[l], g_post_ffn[l], w_in[l], conv_w[l], conv_b[l], dt_bias[l],
                         a_log[l], d_skip[l], ssd_norm_w[l], w_ssd_br[l], w_attn_br[l], w_out[l],
                         w_router[l], router_bias[l], w_gate[l], w_up[l], w_down[l],
                         ws_gate[l], ws_up[l], ws_down[l])
    return x
```

```python
import functools
import math

import jax
import jax.numpy as jnp
from jax import lax
from jax.experimental import pallas as pl
from jax.experimental.pallas import tpu as pltpu

F32 = jnp.float32
BF16 = jnp.bfloat16

NORM_EPS = 1e-6
ROPE_THETA = 10000.0

SSD_HEAD_DIM = 64
SSD_GROUPS = 8
SSD_STATE = 128
SSD_CONV = 4
SSD_CHUNK = 128

ATT_HEAD_DIM = 128
MOBA_BLOCK = 256
MOBA_TOPK = 3

TOP_K = 8
N_EXPERT_GROUPS = 8
TOPK_GROUPS = 4
ROUTED_SCALE = 2.5
MOE_ROWS = 256
COMBINE_TOKENS = 64

LANES = 128
SUBLANES = 8
VMEM_LIMIT = 56 * 1024 * 1024
NEG_BIG = -1e30


def _cparams(sem):
    return pltpu.CompilerParams(dimension_semantics=sem, vmem_limit_bytes=VMEM_LIMIT)


def _silu(x):
    return x * jax.nn.sigmoid(x)


def _rms(x):
    return x * lax.rsqrt(jnp.mean(x * x, axis=-1, keepdims=True) + NORM_EPS)


def _split3(a):
    a1 = a.astype(BF16)
    r1 = a - a1.astype(F32)
    a2 = r1.astype(BF16)
    a3 = (r1 - a2.astype(F32)).astype(BF16)
    return a1, a2, a3


def _dot_exact_rhs(a, e_bf16):
    out = None
    for p in _split3(a):
        t = jnp.dot(p, e_bf16, preferred_element_type=F32)
        out = t if out is None else out + t
    return out


def _dot_nt(a, b):
    return lax.dot_general(a, b, (((1,), (1,)), ((), ())), preferred_element_type=F32)


def _dot_nt_x3(a, b):
    a1 = a.astype(BF16)
    a2 = (a - a1.astype(F32)).astype(BF16)
    b1 = b.astype(BF16)
    b2 = (b - b1.astype(F32)).astype(BF16)
    return _dot_nt(a1, b1) + (_dot_nt(a1, b2) + _dot_nt(a2, b1))


def _ada_kernel(c_ref, w_ref, b_ref, o_ref):
    cond = _silu(c_ref[...])
    o_ref[...] = jnp.dot(cond.astype(BF16), w_ref[...].astype(BF16),
                         preferred_element_type=F32) + b_ref[...]


def _ada(c, w_ada, b_ada):
    bsz, d = c.shape
    n = w_ada.shape[1]
    tn = 1024
    return pl.pallas_call(
        _ada_kernel,
        grid=(n // tn,),
        in_specs=[pl.BlockSpec((bsz, d), lambda j: (0, 0)),
                  pl.BlockSpec((d, tn), lambda j: (0, j)),
                  pl.BlockSpec((1, tn), lambda j: (0, j))],
        out_specs=pl.BlockSpec((bsz, tn), lambda j: (0, j)),
        out_shape=jax.ShapeDtypeStruct((bsz, n), F32),
        compiler_params=_cparams(("arbitrary",)),
        name="ada_mod",
    )(c, w_ada, b_ada.reshape(1, n))


def _prenorm_kernel(x_ref, g_ref, sc_ref, sh_ref, o_ref):
    y = _rms(x_ref[0]) * g_ref[...]
    o_ref[0] = (y * (1.0 + sc_ref[0]) + sh_ref[0]).astype(o_ref.dtype)


def _prenorm(x, g, sc, sh):
    bsz, s, d = x.shape
    ts = 512
    vec = pl.BlockSpec((1, 1, d), lambda b, i: (b, 0, 0))
    return pl.pallas_call(
        _prenorm_kernel,
        grid=(bsz, s // ts),
        in_specs=[pl.BlockSpec((1, ts, d), lambda b, i: (b, i, 0)),
                  pl.BlockSpec((1, d), lambda b, i: (0, 0)), vec, vec],
        out_specs=pl.BlockSpec((1, ts, d), lambda b, i: (b, i, 0)),
        out_shape=jax.ShapeDtypeStruct((bsz, s, d), BF16),
        compiler_params=_cparams(("arbitrary", "arbitrary")),
        name="prenorm",
    )(x, g.reshape(1, d), sc.reshape(bsz, 1, d), sh.reshape(bsz, 1, d))


def _mm_kernel(a_ref, w_ref, o_ref):
    o_ref[...] = jnp.dot(a_ref[...], w_ref[...], preferred_element_type=F32).astype(o_ref.dtype)


def _matmul(a, w, out_dtype, tm, tn, name):
    m, k = a.shape
    n = w.shape[1]
    return pl.pallas_call(
        _mm_kernel,
        grid=(n // tn, m // tm),
        in_specs=[pl.BlockSpec((tm, k), lambda j, i: (i, 0)),
                  pl.BlockSpec((k, tn), lambda j, i: (0, j))],
        out_specs=pl.BlockSpec((tm, tn), lambda j, i: (i, j)),
        out_shape=jax.ShapeDtypeStruct((m, n), out_dtype),
        compiler_params=_cparams(("arbitrary", "arbitrary")),
        name=name,
    )(a, w)


def _softplus(x):
    return jnp.maximum(x, 0.0) + jnp.log(1.0 + jnp.exp(-jnp.abs(x)))


def _ssd_kernel(x_ref, b_ref, c_ref, z_ref, dt_ref, cwx_ref, cwb_ref, cwc_ref,
                cbx_ref, cbb_ref, cbc_ref, dtb_ref, alog_ref, dsk_ref, nw_ref,
                o_ref, cbuf, state):
    q = SSD_CHUNK
    gw = x_ref.shape[2]
    nst = b_ref.shape[2]
    hg = gw // SSD_HEAD_DIM
    cw = gw + 2 * nst
    pad = SUBLANES

    @pl.when(pl.program_id(2) == 0)
    def _():
        cbuf[0:pad, :] = jnp.zeros((pad, cw), F32)
        state[...] = jnp.zeros_like(state)

    xin = jnp.concatenate([x_ref[0], b_ref[0], c_ref[0]], axis=-1).astype(F32)
    cbuf[pad:pad + q, :] = xin
    w = jnp.concatenate([cwx_ref[...], cwb_ref[...], cwc_ref[...]], axis=-1)
    bias = jnp.concatenate([cbx_ref[...], cbb_ref[...], cbc_ref[...]], axis=-1)
    acc = bias + w[SSD_CONV - 1:SSD_CONV, :] * xin
    for s in range(1, SSD_CONV):
        acc = acc + w[SSD_CONV - 1 - s:SSD_CONV - s, :] * cbuf[pad - s:pad - s + q, :]
    cbuf[pad - (SSD_CONV - 1):pad, :] = cbuf[pad + q - (SSD_CONV - 1):pad + q, :]
    xbc = _silu(acc)
    xs = xbc[:, :gw]
    bm = xbc[:, gw:gw + nst].astype(BF16)
    cm = xbc[:, gw + nst:].astype(BF16)

    hrow = lax.broadcasted_iota(jnp.int32, (LANES, q), 0)
    dt_raw = jnp.concatenate([dt_ref[0, 0, 0], jnp.zeros((LANES - hg, q), F32)], axis=0)
    dt_r = jnp.where(hrow < hg, _softplus(dt_raw + dtb_ref[0]), 0.0)
    a_r = dt_r * (-jnp.exp(alog_ref[0]))
    ri = lax.broadcasted_iota(jnp.int32, (q, q), 0)
    ci = lax.broadcasted_iota(jnp.int32, (q, q), 1)
    triu = jnp.where(ri <= ci, 1.0, 0.0).astype(BF16)
    acum_r = _dot_exact_rhs(a_r, triu)
    acum_c = acum_r.T
    dt_c = dt_r.T

    er = lax.broadcasted_iota(jnp.int32, (LANES, gw), 0)
    ec = lax.broadcasted_iota(jnp.int32, (LANES, gw), 1)
    e_ch = jnp.where(ec // SSD_HEAD_DIM == er, 1.0, 0.0).astype(BF16)
    er2 = lax.broadcasted_iota(jnp.int32, (LANES, hg * q), 0)
    ec2 = lax.broadcasted_iota(jnp.int32, (LANES, hg * q), 1)
    e_t = jnp.where(ec2 // q == er2, 1.0, 0.0).astype(BF16)
    dt_full = _dot_exact_rhs(dt_c, e_ch)
    acum_full = _dot_exact_rhs(acum_c, e_ch)
    acum_colb = _dot_exact_rhs(acum_c, e_t)
    acum_rowb = jnp.concatenate(
        [jnp.broadcast_to(acum_r[h:h + 1, :], (q, q)) for h in range(hg)], axis=-1)

    ri2 = lax.broadcasted_iota(jnp.int32, (q, hg * q), 0)
    ci2 = lax.broadcasted_iota(jnp.int32, (q, hg * q), 1) % q
    lmat = jnp.exp(jnp.where(ri2 >= ci2, acum_colb - acum_rowb, NEG_BIG))
    cb = _dot_nt(cm, bm)
    mcat = (jnp.concatenate([cb] * hg, axis=-1) * lmat).astype(BF16)
    xdt = xs * dt_full
    chan_head = lax.broadcasted_iota(jnp.int32, (1, gw), 1) // SSD_HEAD_DIM
    xbd = jnp.concatenate(
        [jnp.where(chan_head == h, xdt, 0.0).astype(BF16) for h in range(hg)], axis=0)
    y = jnp.dot(mcat, xbd, preferred_element_type=F32)

    st = state[...]
    y = y + jnp.dot(cm, st.astype(BF16), preferred_element_type=F32) * jnp.exp(acum_full)
    acum_last = acum_full[q - 1:q, :]
    xw = (xdt * jnp.exp(acum_last - acum_full)).astype(BF16)
    bt = xbc[:, gw:gw + nst].T.astype(BF16)
    state[...] = st * jnp.exp(acum_last) + jnp.dot(bt, xw, preferred_element_type=F32)

    y = y + xs * dsk_ref[...]
    y = y * _silu(z_ref[0].astype(F32))
    o_ref[0] = (_rms(y) * nw_ref[...]).astype(o_ref.dtype)


def _ssd(proj, dt_rows, conv_w, conv_b, dt_bias, a_log, d_skip, norm_w, d_inner):
    bsz, s, _ = proj.shape
    g = SSD_GROUPS
    gw = d_inner // g
    hg = gw // SSD_HEAD_DIM
    nst = SSD_STATE
    q = SSD_CHUNK
    nc = s // q
    x0 = d_inner // gw
    b0 = (2 * d_inner) // nst
    c0 = (2 * d_inner + g * nst) // nst
    cwb0 = d_inner // nst
    cwc0 = (d_inner + g * nst) // nst
    conv_b2 = conv_b.reshape(1, -1)
    head_pad = lambda v: jnp.pad(v.reshape(g, hg), ((0, 0), (0, LANES - hg))).reshape(g, LANES, 1)
    return pl.pallas_call(
        _ssd_kernel,
        grid=(bsz, g, nc),
        in_specs=[
            pl.BlockSpec((1, q, gw), lambda b, gi, c: (b, c, x0 + gi)),
            pl.BlockSpec((1, q, nst), lambda b, gi, c: (b, c, b0 + gi)),
            pl.BlockSpec((1, q, nst), lambda b, gi, c: (b, c, c0 + gi)),
            pl.BlockSpec((1, q, gw), lambda b, gi, c: (b, c, gi)),
            pl.BlockSpec((1, 1, 1, hg, q), lambda b, gi, c: (b, gi, c, 0, 0)),
            pl.BlockSpec((SSD_CONV, gw), lambda b, gi, c: (0, gi)),
            pl.BlockSpec((SSD_CONV, nst), lambda b, gi, c: (0, cwb0 + gi)),
            pl.BlockSpec((SSD_CONV, nst), lambda b, gi, c: (0, cwc0 + gi)),
            pl.BlockSpec((1, gw), lambda b, gi, c: (0, gi)),
            pl.BlockSpec((1, nst), lambda b, gi, c: (0, cwb0 + gi)),
            pl.BlockSpec((1, nst), lambda b, gi, c: (0, cwc0 + gi)),
            pl.BlockSpec((1, LANES, 1), lambda b, gi, c: (gi, 0, 0)),
            pl.BlockSpec((1, LANES, 1), lambda b, gi, c: (gi, 0, 0)),
            pl.BlockSpec((1, gw), lambda b, gi, c: (0, gi)),
            pl.BlockSpec((1, gw), lambda b, gi, c: (0, gi)),
        ],
        out_specs=pl.BlockSpec((1, q, gw), lambda b, gi, c: (b, c, gi)),
        out_shape=jax.ShapeDtypeStruct((bsz, s, d_inner), BF16),
        scratch_shapes=[pltpu.VMEM((SUBLANES + q, gw + 2 * nst), F32),
                        pltpu.VMEM((nst, gw), F32)],
        compiler_params=_cparams(("arbitrary", "arbitrary", "arbitrary")),
        name="ssd_scan",
    )(proj, proj, proj, proj, dt_rows, conv_w, conv_w, conv_w, conv_b2, conv_b2, conv_b2,
      head_pad(dt_bias), head_pad(a_log),
      jnp.repeat(d_skip, SSD_HEAD_DIM).reshape(1, d_inner), norm_w.reshape(1, d_inner))


def _rope_kernel(pos_ref, inv_ref, cos_ref, sin_ref):
    ang = pos_ref[...].astype(F32) * inv_ref[...]
    lane = lax.broadcasted_iota(jnp.int32, ang.shape, 1)
    cos_ref[...] = jnp.cos(ang)
    sin_ref[...] = jnp.where(lane < ATT_HEAD_DIM // 2, -1.0, 1.0) * jnp.sin(ang)


def _rope_tables(positions):
    t = positions.size
    half = ATT_HEAD_DIM // 2
    inv = 1.0 / (ROPE_THETA ** (jnp.arange(half, dtype=F32) / half))
    inv2 = jnp.concatenate([inv, inv]).reshape(1, ATT_HEAD_DIM)
    tt = min(2048, t)
    return pl.pallas_call(
        _rope_kernel,
        grid=(t // tt,),
        in_specs=[pl.BlockSpec((tt, 1), lambda i: (i, 0)),
                  pl.BlockSpec((1, ATT_HEAD_DIM), lambda i: (0, 0))],
        out_specs=[pl.BlockSpec((tt, ATT_HEAD_DIM), lambda i: (i, 0))] * 2,
        out_shape=[jax.ShapeDtypeStruct((t, ATT_HEAD_DIM), F32)] * 2,
        compiler_params=_cparams(("arbitrary",)),
        name="rope_tables",
    )(positions.reshape(t, 1), inv2)


def _rot(x, cos2, sin2):
    return x * cos2 + pltpu.roll(x, ATT_HEAD_DIM // 2, axis=1) * sin2


def _attn_kernel(q_ref, k_ref, v_ref, cos_ref, sin_ref, o_ref, krot, kmean, m_sc, l_sc, acc_sc):
    blk = MOBA_BLOCK
    nb = k_ref.shape[1] // blk
    qi = pl.program_id(2)
    scale = ATT_HEAD_DIM ** -0.5

    @pl.when(qi == 0)
    def _():
        kmean[...] = jnp.zeros_like(kmean)
        for j in range(nb):
            rows = slice(j * blk, (j + 1) * blk)
            kr = _rot(k_ref[0, rows, :].astype(F32), cos_ref[0, rows, :], sin_ref[0, rows, :])
            krot[rows, :] = kr.astype(BF16)
            kmean[j:j + 1, :] = jnp.mean(kr, axis=0, keepdims=True)

    qrows = pl.ds(pl.multiple_of(qi * blk, blk), blk)
    qr = _rot(q_ref[0].astype(F32), cos_ref[0, qrows, :], sin_ref[0, qrows, :])
    qb = qr.astype(BF16)

    ri = lax.broadcasted_iota(jnp.int32, (blk, blk), 0)
    ci = lax.broadcasted_iota(jnp.int32, (blk, blk), 1)
    s = jnp.where(ci <= ri, _dot_nt(qb, krot[qrows, :]) * scale, NEG_BIG)
    m0 = jnp.max(s, axis=1, keepdims=True)
    p = jnp.exp(s - m0)
    m_sc[...] = m0
    l_sc[...] = jnp.sum(p, axis=1, keepdims=True)
    acc_sc[...] = jnp.dot(p.astype(BF16), v_ref[0, qrows, :], preferred_element_type=F32)

    gate = _dot_nt_x3(qr, kmean[...])
    lane = lax.broadcasted_iota(jnp.int32, gate.shape, 1)
    past = lane < qi

    for j in range(nb - 1):
        @pl.when(j < qi)
        def _(j=j):
            gj = gate[:, j:j + 1]
            ahead = (gate > gj) | ((gate == gj) & (lane < j))
            rank = jnp.sum(jnp.where(past & ahead, 1.0, 0.0), axis=1, keepdims=True)
            rows = slice(j * blk, (j + 1) * blk)
            sj = jnp.where(rank < MOBA_TOPK, _dot_nt(qb, krot[rows, :]) * scale, NEG_BIG)
            m_prev = m_sc[...]
            m_new = jnp.maximum(m_prev, jnp.max(sj, axis=1, keepdims=True))
            alpha = jnp.exp(m_prev - m_new)
            pj = jnp.exp(sj - m_new)
            l_sc[...] = alpha * l_sc[...] + jnp.sum(pj, axis=1, keepdims=True)
            acc_sc[...] = alpha * acc_sc[...] + jnp.dot(pj.astype(BF16), v_ref[0, rows, :],
                                                        preferred_element_type=F32)
            m_sc[...] = m_new

    o_ref[0] = (acc_sc[...] / l_sc[...]).astype(o_ref.dtype)


def _attention(proj, cos2, sin2, q0, k0, v0, n_heads):
    bsz, s, _ = proj.shape
    dh = ATT_HEAD_DIM
    blk = MOBA_BLOCK
    tab = pl.BlockSpec((1, s, dh), lambda b, h, i: (b, 0, 0))
    return pl.pallas_call(
        _attn_kernel,
        grid=(bsz, n_heads, s // blk),
        in_specs=[pl.BlockSpec((1, blk, dh), lambda b, h, i: (b, i, q0 + h)),
                  pl.BlockSpec((1, s, dh), lambda b, h, i: (b, 0, k0 + h)),
                  pl.BlockSpec((1, s, dh), lambda b, h, i: (b, 0, v0 + h)),
                  tab, tab],
        out_specs=pl.BlockSpec((1, blk, dh), lambda b, h, i: (b, i, h)),
        out_shape=jax.ShapeDtypeStruct((bsz, s, n_heads * dh), BF16),
        scratch_shapes=[pltpu.VMEM((s, dh), BF16), pltpu.VMEM((LANES, dh), F32),
                        pltpu.VMEM((blk, 1), F32), pltpu.VMEM((blk, 1), F32),
                        pltpu.VMEM((blk, dh), F32)],
        compiler_params=_cparams(("arbitrary", "arbitrary", "arbitrary")),
        name="moba_attn",
    )(proj, proj, proj, cos2.reshape(bsz, s, dh), sin2.reshape(bsz, s, dh))


def _merge_kernel(ys_ref, ya_ref, w1_ref, w2_ref, gs_ref, ga_ref, o_ref):
    b1 = jnp.dot(ys_ref[...], w1_ref[...], preferred_element_type=F32)
    b2 = jnp.dot(ya_ref[...], w2_ref[...], preferred_element_type=F32)
    o_ref[...] = (jax.nn.sigmoid(gs_ref[...].astype(F32)) * b1
                  + jax.nn.sigmoid(ga_ref[...].astype(F32)) * b2).astype(o_ref.dtype)


def _merge(y_ssd, y_att, w1, w2, proj2d, gs0, ga0):
    m, k1 = y_ssd.shape
    k2 = y_att.shape[1]
    n = w1.shape[1]
    tm, tn = 512, 512
    return pl.pallas_call(
        _merge_kernel,
        grid=(n // tn, m // tm),
        in_specs=[pl.BlockSpec((tm, k1), lambda j, i: (i, 0)),
                  pl.BlockSpec((tm, k2), lambda j, i: (i, 0)),
                  pl.BlockSpec((k1, tn), lambda j, i: (0, j)),
                  pl.BlockSpec((k2, tn), lambda j, i: (0, j)),
                  pl.BlockSpec((tm, tn), lambda j, i: (i, gs0 // tn + j)),
                  pl.BlockSpec((tm, tn), lambda j, i: (i, ga0 // tn + j))],
        out_specs=pl.BlockSpec((tm, tn), lambda j, i: (i, j)),
        out_shape=jax.ShapeDtypeStruct((m, n), BF16),
        compiler_params=_cparams(("arbitrary", "arbitrary")),
        name="branch_merge",
    )(y_ssd, y_att, w1, w2, proj2d, proj2d)


def _outproj_kernel(m_ref, w_ref, x_ref, gpost_ref, gt_ref, gpre_ref, sc_ref, sh_ref, wr_ref,
                    x1_ref, h2_ref, h2b_ref, lg_ref):
    y = jnp.dot(m_ref[0], w_ref[...], preferred_element_type=F32)
    x1 = x_ref[0] + gt_ref[0] * (_rms(y) * gpost_ref[...])
    h2 = (_rms(x1) * gpre_ref[...]) * (1.0 + sc_ref[0]) + sh_ref[0]
    x1_ref[0] = x1
    h2_ref[0] = h2
    h2b_ref[0] = h2.astype(BF16)
    lg_ref[...] = _dot_nt_x3(wr_ref[...], h2)


def _outproj(merged, w_out, x, g_post, gt1, g_pre, sc2, sh2, w_router_t):
    bsz, s, d = x.shape
    ne = w_router_t.shape[0]
    tm = 256
    nt = s // tm
    row = pl.BlockSpec((1, tm, d), lambda i: (i // nt, i % nt, 0))
    vec = pl.BlockSpec((1, 1, d), lambda i: (i // nt, 0, 0))
    par = pl.BlockSpec((1, d), lambda i: (0, 0))
    return pl.pallas_call(
        _outproj_kernel,
        grid=(bsz * nt,),
        in_specs=[row, pl.BlockSpec((d, d), lambda i: (0, 0)), row, par, vec, par, vec, vec,
                  pl.BlockSpec((ne, d), lambda i: (0, 0))],
        out_specs=[row, row, row, pl.BlockSpec((ne, tm), lambda i: (0, i))],
        out_shape=[jax.ShapeDtypeStruct((bsz, s, d), F32), jax.ShapeDtypeStruct((bsz, s, d), F32),
                   jax.ShapeDtypeStruct((bsz, s, d), BF16),
                   jax.ShapeDtypeStruct((ne, bsz * s), F32)],
        compiler_params=_cparams(("arbitrary",)),
        name="out_proj_norms",
    )(merged.reshape(bsz, s, d), w_out, x, g_post.reshape(1, d), gt1.reshape(bsz, 1, d),
      g_pre.reshape(1, d), sc2.reshape(bsz, 1, d), sh2.reshape(bsz, 1, d), w_router_t)


def _router_kernel(lg_ref, rb_ref, idx_ref, w_ref, gsc):
    ne, tn = lg_ref.shape
    ng = N_EXPERT_GROUPS
    eg = ne // ng
    scores = jax.nn.sigmoid(lg_ref[...])
    biased = scores + rb_ref[...]
    i8 = lax.broadcasted_iota(jnp.int32, (eg, tn), 0)
    for g in range(ng):
        v = biased[g * eg:(g + 1) * eg, :]
        m1 = jnp.max(v, axis=0, keepdims=True)
        first = jnp.min(jnp.where(v == m1, i8, eg), axis=0, keepdims=True)
        m2 = jnp.max(jnp.where(i8 == first, -jnp.inf, v), axis=0, keepdims=True)
        gsc[g:g + 1, :] = m1 + m2
    gs = gsc[...]
    gi = lax.broadcasted_iota(jnp.int32, (ng, tn), 0)
    masked = []
    for g in range(ng):
        sg = gs[g:g + 1, :]
        ahead = (gs > sg) | ((gs == sg) & (gi < g))
        rank = jnp.sum(jnp.where(ahead, 1.0, 0.0), axis=0, keepdims=True)
        masked.append(jnp.where(rank < TOPK_GROUPS, biased[g * eg:(g + 1) * eg, :], -jnp.inf))
    cur = jnp.concatenate(masked, axis=0)
    sub = lax.broadcasted_iota(jnp.int32, (ne, tn), 0)
    picks = []
    for k in range(TOP_K):
        mx = jnp.max(cur, axis=0, keepdims=True)
        idx = jnp.min(jnp.where(cur == mx, sub, ne), axis=0, keepdims=True)
        hit = sub == idx
        picks.append(jnp.sum(jnp.where(hit, scores, 0.0), axis=0, keepdims=True))
        cur = jnp.where(hit, -jnp.inf, cur)
        idx_ref[k:k + 1, :] = idx
    wsum = picks[0]
    for k in range(1, TOP_K):
        wsum = wsum + picks[k]
    for k in range(TOP_K):
        w_ref[k:k + 1, :] = picks[k] / wsum * ROUTED_SCALE


def _router(logits_t, router_bias):
    ne, t = logits_t.shape
    tn = 512
    return pl.pallas_call(
        _router_kernel,
        grid=(t // tn,),
        in_specs=[pl.BlockSpec((ne, tn), lambda i: (0, i)),
                  pl.BlockSpec((ne, 1), lambda i: (0, 0))],
        out_specs=[pl.BlockSpec((TOP_K, tn), lambda i: (0, i))] * 2,
        out_shape=[jax.ShapeDtypeStruct((TOP_K, t), jnp.int32),
                   jax.ShapeDtypeStruct((TOP_K, t), F32)],
        scratch_shapes=[pltpu.VMEM((N_EXPERT_GROUPS, tn), F32)],
        compiler_params=_cparams(("arbitrary",)),
        name="router_topk",
    )(logits_t, router_bias.reshape(ne, 1))


def _row_copy(src_hbm, idx_ref, buf, sem, slot, r, base):
    return pltpu.make_async_copy(src_hbm.at[pl.ds(idx_ref[0, 0, r], 1), :],
                                 buf.at[slot, pl.ds(base + r, 1), :], sem.at[slot])


def _gather_block(src_hbm, idx_ref, buf, sem, slot, rows):
    def body(r, carry):
        _row_copy(src_hbm, idx_ref, buf, sem, slot, r, 0).start()
        return carry
    lax.fori_loop(0, rows, body, 0)


def _wait_block(src_hbm, buf, sem, slot, rows):
    pltpu.make_async_copy(src_hbm.at[pl.ds(0, rows), :], buf.at[slot], sem.at[slot]).wait()


def _gather_kernel(idx_ref, idx_next_ref, src_hbm, o_ref, buf, sem):
    i = pl.program_id(0)
    n = pl.num_programs(0)
    rows = o_ref.shape[0]
    slot = i % 2

    @pl.when(i == 0)
    def _():
        _gather_block(src_hbm, idx_ref, buf, sem, 0, rows)

    @pl.when(i + 1 < n)
    def _():
        _gather_block(src_hbm, idx_next_ref, buf, sem, 1 - slot, rows)

    _wait_block(src_hbm, buf, sem, slot, rows)
    o_ref[...] = buf[slot].astype(o_ref.dtype)


def _gather_rows(src, idx, rows, out_dtype):
    nblk = idx.shape[0]
    d = src.shape[1]
    return pl.pallas_call(
        _gather_kernel,
        grid=(nblk,),
        in_specs=[pl.BlockSpec((1, 1, rows), lambda i: (i, 0, 0), memory_space=pltpu.SMEM),
                  pl.BlockSpec((1, 1, rows), lambda i: (jnp.minimum(i + 1, nblk - 1), 0, 0),
                               memory_space=pltpu.SMEM),
                  pl.BlockSpec(memory_space=pl.ANY)],
        out_specs=pl.BlockSpec((rows, d), lambda i: (i, 0)),
        out_shape=jax.ShapeDtypeStruct((nblk * rows, d), out_dtype),
        scratch_shapes=[pltpu.VMEM((2, rows, d), src.dtype), pltpu.SemaphoreType.DMA((2,))],
        compiler_params=_cparams(("arbitrary",)),
        name="expert_row_gather",
    )(idx, idx, src)


def _expert_kernel(be_ref, x_ref, wg_ref, wu_ref, wd_ref, rw_ref, o_ref):
    x = x_ref[...]
    gate = jnp.dot(x, wg_ref[0], preferred_element_type=F32)
    up = jnp.dot(x, wu_ref[0], preferred_element_type=F32)
    act = (_silu(gate) * up).astype(BF16)
    o_ref[...] = jnp.dot(act, wd_ref[0], preferred_element_type=F32) * rw_ref[...]


def _expert_ffn(blk_exp, x_sorted, w_gate, w_up, w_down, row_w, rows):
    n_rows, d = x_sorted.shape
    f = w_gate.shape[2]
    nblk = n_rows // rows
    grid_spec = pltpu.PrefetchScalarGridSpec(
        num_scalar_prefetch=1,
        grid=(nblk,),
        in_specs=[pl.BlockSpec((rows, d), lambda i, be: (i, 0)),
                  pl.BlockSpec((1, d, f), lambda i, be: (be[i], 0, 0)),
                  pl.BlockSpec((1, d, f), lambda i, be: (be[i], 0, 0)),
                  pl.BlockSpec((1, f, d), lambda i, be: (be[i], 0, 0)),
                  pl.BlockSpec((rows, 1), lambda i, be: (i, 0))],
        out_specs=pl.BlockSpec((rows, d), lambda i, be: (i, 0)),
    )
    return pl.pallas_call(
        _expert_kernel,
        grid_spec=grid_spec,
        out_shape=jax.ShapeDtypeStruct((n_rows, d), F32),
        compiler_params=_cparams(("arbitrary",)),
        name="expert_ffn",
    )(blk_exp, x_sorted, w_gate, w_up, w_down, row_w.reshape(n_rows, 1))


def _shared_kernel(x_ref, wg_ref, wu_ref, wd_ref, o_ref):
    x = x_ref[...]
    gate = jnp.dot(x, wg_ref[...], preferred_element_type=F32)
    up = jnp.dot(x, wu_ref[...], preferred_element_type=F32)
    act = (_silu(gate) * up).astype(BF16)
    o_ref[...] = jnp.dot(act, wd_ref[...], preferred_element_type=F32)


def _shared_ffn(x, wg, wu, wd):
    m, d = x.shape
    f = wg.shape[1]
    tm = 512
    full = lambda shape: pl.BlockSpec(shape, lambda i: (0, 0))
    return pl.pallas_call(
        _shared_kernel,
        grid=(m // tm,),
        in_specs=[pl.BlockSpec((tm, d), lambda i: (i, 0)), full((d, f)), full((d, f)), full((f, d))],
        out_specs=pl.BlockSpec((tm, d), lambda i: (i, 0)),
        out_shape=jax.ShapeDtypeStruct((m, d), F32),
        compiler_params=_cparams(("arbitrary",)),
        name="shared_ffn",
    )(x, wg, wu, wd)


def _combine_kernel(idx_ref, idx_next_ref, y_hbm, sh_ref, x1_ref, gt_ref, g_ref, o_ref, buf, sem):
    i = pl.program_id(0)
    n = pl.num_programs(0)
    tt = o_ref.shape[1]
    rows = tt * TOP_K
    slot = i % 2

    @pl.when(i == 0)
    def _():
        _gather_block(y_hbm, idx_ref, buf, sem, 0, rows)

    @pl.when(i + 1 < n)
    def _():
        _gather_block(y_hbm, idx_next_ref, buf, sem, 1 - slot, rows)

    _wait_block(y_hbm, buf, sem, slot, rows)
    y = sh_ref[...]
    for k in range(TOP_K):
        y = y + buf[slot, k * tt:(k + 1) * tt, :]
    o_ref[0] = x1_ref[0] + gt_ref[0] * (_rms(y) * g_ref[...])


def _combine(pos, y_rows, shared, x1, gt2, g_post):
    bsz, s, d = x1.shape
    tt = COMBINE_TOKENS
    nt = s // tt
    nblk = bsz * nt
    rows = tt * TOP_K
    row = pl.BlockSpec((1, tt, d), lambda i: (i // nt, i % nt, 0))
    return pl.pallas_call(
        _combine_kernel,
        grid=(nblk,),
        in_specs=[pl.BlockSpec((1, 1, rows), lambda i: (i, 0, 0), memory_space=pltpu.SMEM),
                  pl.BlockSpec((1, 1, rows), lambda i: (jnp.minimum(i + 1, nblk - 1), 0, 0),
                               memory_space=pltpu.SMEM),
                  pl.BlockSpec(memory_space=pl.ANY),
                  pl.BlockSpec((tt, d), lambda i: (i, 0)),
                  row,
                  pl.BlockSpec((1, 1, d), lambda i: (i // nt, 0, 0)),
                  pl.BlockSpec((1, d), lambda i: (0, 0))],
        out_specs=row,
        out_shape=jax.ShapeDtypeStruct((bsz, s, d), F32),
        scratch_shapes=[pltpu.VMEM((2, rows, d), F32), pltpu.SemaphoreType.DMA((2,))],
        compiler_params=_cparams(("arbitrary",)),
        name="moe_combine",
    )(pos, pos, y_rows, shared, x1, gt2.reshape(bsz, 1, d), g_post.reshape(1, d))


def _dispatch_plan(top_idx, top_w, n_experts, rows):
    t = top_idx.shape[0]
    n_assign = t * TOP_K
    flat_e = top_idx.reshape(-1)
    order = jnp.argsort(flat_e)
    sorted_e = flat_e[order]
    counts = jnp.bincount(flat_e, length=n_experts)
    starts = jnp.cumsum(counts) - counts
    padded = (counts + rows - 1) // rows * rows
    pad_end = jnp.cumsum(padded)
    pad_start = pad_end - padded
    dest = (pad_start[sorted_e] + jnp.arange(n_assign) - starts[sorted_e]).astype(jnp.int32)
    n_blocks = -(-n_assign // rows) + n_experts
    n_rows = n_blocks * rows
    row_tok = jnp.zeros((n_rows,), jnp.int32).at[dest].set((order // TOP_K).astype(jnp.int32))
    row_w = jnp.zeros((n_rows,), F32).at[dest].set(top_w.reshape(-1)[order])
    blk_exp = jnp.minimum(
        jnp.searchsorted(pad_end, jnp.arange(n_blocks) * rows, side="right"),
        n_experts - 1).astype(jnp.int32)
    pos = jnp.zeros((n_assign,), jnp.int32).at[order].set(dest).reshape(t, TOP_K)
    return row_tok, row_w, blk_exp, pos


def kernel(x, c, positions, w_ada, b_ada, g_pre_mix, g_post_mix, g_pre_ffn, g_post_ffn, w_in,
           conv_w, conv_b, dt_bias, a_log, d_skip, ssd_norm_w, w_ssd_br, w_attn_br, w_out,
           w_router, router_bias, w_gate, w_up, w_down, ws_gate, ws_up, ws_down):
    bsz, s, d = x.shape
    t = bsz * s
    for l in range(w_ada.shape[0]):
        d_inner = w_ssd_br.shape[1]
        att_w = w_attn_br.shape[1]
        n_ssd_heads = dt_bias.shape[1]
        conv_dim = conv_w.shape[2]
        n_heads = att_w // ATT_HEAD_DIM
        n_experts = w_router.shape[2]

        mod = _ada(c, w_ada[l], b_ada[l])
        sh1, sc1, gt1, sh2, sc2, gt2 = jnp.split(mod, 6, axis=-1)

        wl = w_in[l]
        o_dt = d_inner + conv_dim
        o_q = o_dt + n_ssd_heads
        w_main = jnp.concatenate([wl[:, :o_dt], wl[:, o_q:]], axis=1).astype(BF16)
        w_dt = jnp.pad(wl[:, o_dt:o_q], ((0, 0), (0, LANES - n_ssd_heads))).astype(BF16)
        h1 = _prenorm(x, g_pre_mix[l], sc1, sh1).reshape(t, d)
        proj = _matmul(h1, w_main, BF16, 1024, 1024, "in_proj")
        dt_raw = _matmul(h1, w_dt, F32, 1024, LANES, "in_proj_dt")[:, :n_ssd_heads]
        n_main = proj.shape[1]
        proj3 = proj.reshape(bsz, s, n_main)
        q_off = o_dt
        k_off = q_off + att_w
        v_off = k_off + att_w
        gs_off = v_off + att_w
        ga_off = gs_off + d

        hg = n_ssd_heads // SSD_GROUPS
        dt_rows = dt_raw.reshape(bsz, s // SSD_CHUNK, SSD_CHUNK, SSD_GROUPS, hg).transpose(0, 3, 1, 4, 2)
        y_ssd = _ssd(proj3, dt_rows, conv_w[l], conv_b[l], dt_bias[l], a_log[l], d_skip[l],
                     ssd_norm_w[l], d_inner)

        cos2, sin2 = _rope_tables(positions)
        y_att = _attention(proj3, cos2, sin2, q_off // ATT_HEAD_DIM, k_off // ATT_HEAD_DIM,
                           v_off // ATT_HEAD_DIM, n_heads)

        merged = _merge(y_ssd.reshape(t, d_inner), y_att.reshape(t, att_w),
                        w_ssd_br[l].astype(BF16), w_attn_br[l].astype(BF16), proj, gs_off, ga_off)
        x1, h2, h2b, logits_t = _outproj(merged, w_out[l].astype(BF16), x, g_post_mix[l], gt1,
                                         g_pre_ffn[l], sc2, sh2, w_router[l].T)

        idx_t, w_t = _router(logits_t, router_bias[l])
        row_tok, row_w, blk_exp, pos = _dispatch_plan(idx_t.T, w_t.T, n_experts, MOE_ROWS)
        x_sorted = _gather_rows(h2.reshape(t, d), row_tok.reshape(-1, 1, MOE_ROWS), MOE_ROWS, BF16)
        y_rows = _expert_ffn(blk_exp, x_sorted, w_gate[l].astype(BF16), w_up[l].astype(BF16),
                             w_down[l].astype(BF16), row_w, MOE_ROWS)
        shared = _shared_ffn(h2b.reshape(t, d), ws_gate[l].astype(BF16), ws_up[l].astype(BF16),
                             ws_down[l].astype(BF16))
        tt = COMBINE_TOKENS
        pos_blk = pos.reshape(t // tt, tt, TOP_K).transpose(0, 2, 1).reshape(t // tt, 1, tt * TOP_K)
        x = _combine(pos_blk, y_rows, shared, x1, gt2, g_post_ffn[l])
    return x
```

```python
import functools
import math

import jax
import jax.numpy as jnp
from jax import lax
from jax.experimental import pallas as pl
from jax.experimental.pallas import tpu as pltpu

F32 = jnp.float32
BF16 = jnp.bfloat16

NORM_EPS = 1e-6
ROPE_THETA = 10000.0

SSD_HEAD_DIM = 64
SSD_GROUPS = 8
SSD_STATE = 128
SSD_CONV = 4
SSD_CHUNK = 128

ATT_HEAD_DIM = 128
MOBA_BLOCK = 256
MOBA_TOPK = 3

TOP_K = 8
N_EXPERT_GROUPS = 8
TOPK_GROUPS = 4
ROUTED_SCALE = 2.5
MOE_ROWS = 256
COMBINE_TOKENS = 64

LANES = 128
SUBLANES = 8
VMEM_LIMIT = 56 * 1024 * 1024
NEG_BIG = -1e30


def _cparams(sem):
    return pltpu.CompilerParams(dimension_semantics=sem, vmem_limit_bytes=VMEM_LIMIT)


def _silu(x):
    return x * jax.nn.sigmoid(x)


def _rms(x):
    return x * lax.rsqrt(jnp.mean(x * x, axis=-1, keepdims=True) + NORM_EPS)


def _split3(a):
    a1 = a.astype(BF16)
    r1 = a - a1.astype(F32)
    a2 = r1.astype(BF16)
    a3 = (r1 - a2.astype(F32)).astype(BF16)
    return a1, a2, a3


def _dot_exact_rhs(a, e_bf16):
    out = None
    for p in _split3(a):
        t = jnp.dot(p, e_bf16, preferred_element_type=F32)
        out = t if out is None else out + t
    return out


def _dot_nt(a, b):
    return lax.dot_general(a, b, (((1,), (1,)), ((), ())), preferred_element_type=F32)


def _dot_nt_x3(a, b):
    a1 = a.astype(BF16)
    a2 = (a - a1.astype(F32)).astype(BF16)
    b1 = b.astype(BF16)
    b2 = (b - b1.astype(F32)).astype(BF16)
    return _dot_nt(a1, b1) + (_dot_nt(a1, b2) + _dot_nt(a2, b1))


def _ada_kernel(c_ref, w_ref, b_ref, o_ref):
    cond = _silu(c_ref[...])
    o_ref[...] = jnp.dot(cond.astype(BF16), w_ref[...].astype(BF16),
                         preferred_element_type=F32) + b_ref[...]


def _ada(c, w_ada, b_ada):
    bsz, d = c.shape
    n = w_ada.shape[1]
    tn = 1024
    return pl.pallas_call(
        _ada_kernel,
        grid=(n // tn,),
        in_specs=[pl.BlockSpec((bsz, d), lambda j: (0, 0)),
                  pl.BlockSpec((d, tn), lambda j: (0, j)),
                  pl.BlockSpec((1, tn), lambda j: (0, j))],
        out_specs=pl.BlockSpec((bsz, tn), lambda j: (0, j)),
        out_shape=jax.ShapeDtypeStruct((bsz, n), F32),
        compiler_params=_cparams(("arbitrary",)),
        name="ada_mod",
    )(c, w_ada, b_ada.reshape(1, n))


def _prenorm_kernel(x_ref, g_ref, sc_ref, sh_ref, o_ref):
    y = _rms(x_ref[0]) * g_ref[...]
    o_ref[0] = (y * (1.0 + sc_ref[0]) + sh_ref[0]).astype(o_ref.dtype)


def _prenorm(x, g, sc, sh):
    bsz, s, d = x.shape
    ts = 512
    vec = pl.BlockSpec((1, 1, d), lambda b, i: (b, 0, 0))
    return pl.pallas_call(
        _prenorm_kernel,
        grid=(bsz, s // ts),
        in_specs=[pl.BlockSpec((1, ts, d), lambda b, i: (b, i, 0)),
                  pl.BlockSpec((1, d), lambda b, i: (0, 0)), vec, vec],
        out_specs=pl.BlockSpec((1, ts, d), lambda b, i: (b, i, 0)),
        out_shape=jax.ShapeDtypeStruct((bsz, s, d), BF16),
        compiler_params=_cparams(("arbitrary", "arbitrary")),
        name="prenorm",
    )(x, g.reshape(1, d), sc.reshape(bsz, 1, d), sh.reshape(bsz, 1, d))


def _mm_kernel(a_ref, w_ref, o_ref):
    o_ref[...] = jnp.dot(a_ref[...], w_ref[...], preferred_element_type=F32).astype(o_ref.dtype)


def _matmul(a, w, out_dtype, tm, tn, name):
    m, k = a.shape
    n = w.shape[1]
    return pl.pallas_call(
        _mm_kernel,
        grid=(n // tn, m // tm),
        in_specs=[pl.BlockSpec((tm, k), lambda j, i: (i, 0)),
                  pl.BlockSpec((k, tn), lambda j, i: (0, j))],
        out_specs=pl.BlockSpec((tm, tn), lambda j, i: (i, j)),
        out_shape=jax.ShapeDtypeStruct((m, n), out_dtype),
        compiler_params=_cparams(("arbitrary", "arbitrary")),
        name=name,
    )(a, w)


def _softplus(x):
    return jnp.maximum(x, 0.0) + jnp.log(1.0 + jnp.exp(-jnp.abs(x)))


def _ssd_kernel(x_ref, b_ref, c_ref, z_ref, dt_ref, cwx_ref, cwb_ref, cwc_ref,
                cbx_ref, cbb_ref, cbc_ref, dtb_ref, alog_ref, dsk_ref, nw_ref,
                o_ref, cbuf, state):
    q = SSD_CHUNK
    gw = x_ref.shape[2]
    nst = b_ref.shape[2]
    hg = gw // SSD_HEAD_DIM
    cw = gw + 2 * nst
    pad = SUBLANES

    @pl.when(pl.program_id(2) == 0)
    def _():
        cbuf[0:pad, :] = jnp.zeros((pad, cw), F32)
        state[...] = jnp.zeros_like(state)

    xin = jnp.concatenate([x_ref[0], b_ref[0], c_ref[0]], axis=-1).astype(F32)
    cbuf[pad:pad + q, :] = xin
    w = jnp.concatenate([cwx_ref[...], cwb_ref[...], cwc_ref[...]], axis=-1)
    bias = jnp.concatenate([cbx_ref[...], cbb_ref[...], cbc_ref[...]], axis=-1)
    acc = bias + w[SSD_CONV - 1:SSD_CONV, :] * xin
    for s in range(1, SSD_CONV):
        acc = acc + w[SSD_CONV - 1 - s:SSD_CONV - s, :] * cbuf[pad - s:pad - s + q, :]
    cbuf[pad - (SSD_CONV - 1):pad, :] = cbuf[pad + q - (SSD_CONV - 1):pad + q, :]
    xbc = _silu(acc)
    xs = xbc[:, :gw]
    bm = xbc[:, gw:gw + nst].astype(BF16)
    cm = xbc[:, gw + nst:].astype(BF16)

    hrow = lax.broadcasted_iota(jnp.int32, (LANES, q), 0)
    dt_raw = jnp.concatenate([dt_ref[0, 0, 0], jnp.zeros((LANES - hg, q), F32)], axis=0)
    dt_r = jnp.where(hrow < hg, _softplus(dt_raw + dtb_ref[0]), 0.0)
    a_r = dt_r * (-jnp.exp(alog_ref[0]))
    ri = lax.broadcasted_iota(jnp.int32, (q, q), 0)
    ci = lax.broadcasted_iota(jnp.int32, (q, q), 1)
    triu = jnp.where(ri <= ci, 1.0, 0.0).astype(BF16)
    acum_r = _dot_exact_rhs(a_r, triu)
    acum_c = acum_r.T
    dt_c = dt_r.T

    er = lax.broadcasted_iota(jnp.int32, (LANES, gw), 0)
    ec = lax.broadcasted_iota(jnp.int32, (LANES, gw), 1)
    e_ch = jnp.where(ec // SSD_HEAD_DIM == er, 1.0, 0.0).astype(BF16)
    er2 = lax.broadcasted_iota(jnp.int32, (LANES, hg * q), 0)
    ec2 = lax.broadcasted_iota(jnp.int32, (LANES, hg * q), 1)
    e_t = jnp.where(ec2 // q == er2, 1.0, 0.0).astype(BF16)
    dt_full = _dot_exact_rhs(dt_c, e_ch)
    acum_full = _dot_exact_rhs(acum_c, e_ch)
    acum_colb = _dot_exact_rhs(acum_c, e_t)
    acum_rowb = jnp.concatenate(
        [jnp.broadcast_to(acum_r[h:h + 1, :], (q, q)) for h in range(hg)], axis=-1)

    ri2 = lax.broadcasted_iota(jnp.int32, (q, hg * q), 0)
    ci2 = lax.broadcasted_iota(jnp.int32, (q, hg * q), 1) % q
    lmat = jnp.exp(jnp.where(ri2 >= ci2, acum_colb - acum_rowb, NEG_BIG))
    cb = _dot_nt(cm, bm)
    mcat = (jnp.concatenate([cb] * hg, axis=-1) * lmat).astype(BF16)
    xdt = xs * dt_full
    chan_head = lax.broadcasted_iota(jnp.int32, (1, gw), 1) // SSD_HEAD_DIM
    xbd = jnp.concatenate(
        [jnp.where(chan_head == h, xdt, 0.0).astype(BF16) for h in range(hg)], axis=0)
    y = jnp.dot(mcat, xbd, preferred_element_type=F32)

    st = state[...]
    y = y + jnp.dot(cm, st.astype(BF16), preferred_element_type=F32) * jnp.exp(acum_full)
    acum_last = acum_full[q - 1:q, :]
    xw = (xdt * jnp.exp(acum_last - acum_full)).astype(BF16)
    bt = xbc[:, gw:gw + nst].T.astype(BF16)
    state[...] = st * jnp.exp(acum_last) + jnp.dot(bt, xw, preferred_element_type=F32)

    y = y + xs * dsk_ref[...]
    y = y * _silu(z_ref[0].astype(F32))
    o_ref[0] = (_rms(y) * nw_ref[...]).astype(o_ref.dtype)


def _ssd(proj, dt_rows, conv_w, conv_b, dt_bias, a_log, d_skip, norm_w, d_inner):
    bsz, s, _ = proj.shape
    g = SSD_GROUPS
    gw = d_inner // g
    hg = gw // SSD_HEAD_DIM
    nst = SSD_STATE
    q = SSD_CHUNK
    nc = s // q
    x0 = d_inner // gw
    b0 = (2 * d_inner) // nst
    c0 = (2 * d_inner + g * nst) // nst
    cwb0 = d_inner // nst
    cwc0 = (d_inner + g * nst) // nst
    conv_b2 = conv_b.reshape(1, -1)
    head_pad = lambda v: jnp.pad(v.reshape(g, hg), ((0, 0), (0, LANES - hg))).reshape(g, LANES, 1)
    return pl.pallas_call(
        _ssd_kernel,
        grid=(bsz, g, nc),
        in_specs=[
            pl.BlockSpec((1, q, gw), lambda b, gi, c: (b, c, x0 + gi)),
            pl.BlockSpec((1, q, nst), lambda b, gi, c: (b, c, b0 + gi)),
            pl.BlockSpec((1, q, nst), lambda b, gi, c: (b, c, c0 + gi)),
            pl.BlockSpec((1, q, gw), lambda b, gi, c: (b, c, gi)),
            pl.BlockSpec((1, 1, 1, hg, q), lambda b, gi, c: (b, gi, c, 0, 0)),
            pl.BlockSpec((SSD_CONV, gw), lambda b, gi, c: (0, gi)),
            pl.BlockSpec((SSD_CONV, nst), lambda b, gi, c: (0, cwb0 + gi)),
            pl.BlockSpec((SSD_CONV, nst), lambda b, gi, c: (0, cwc0 + gi)),
            pl.BlockSpec((1, gw), lambda b, gi, c: (0, gi)),
            pl.BlockSpec((1, nst), lambda b, gi, c: (0, cwb0 + gi)),
            pl.BlockSpec((1, nst), lambda b, gi, c: (0, cwc0 + gi)),
            pl.BlockSpec((1, LANES, 1), lambda b, gi, c: (gi, 0, 0)),
            pl.BlockSpec((1, LANES, 1), lambda b, gi, c: (gi, 0, 0)),
            pl.BlockSpec((1, gw), lambda b, gi, c: (0, gi)),
            pl.BlockSpec((1, gw), lambda b, gi, c: (0, gi)),
        ],
        out_specs=pl.BlockSpec((1, q, gw), lambda b, gi, c: (b, c, gi)),
        out_shape=jax.ShapeDtypeStruct((bsz, s, d_inner), BF16),
        scratch_shapes=[pltpu.VMEM((SUBLANES + q, gw + 2 * nst), F32),
                        pltpu.VMEM((nst, gw), F32)],
        compiler_params=_cparams(("arbitrary", "arbitrary", "arbitrary")),
        name="ssd_scan",
    )(proj, proj, proj, proj, dt_rows, conv_w, conv_w, conv_w, conv_b2, conv_b2, conv_b2,
      head_pad(dt_bias), head_pad(a_log),
      jnp.repeat(d_skip, SSD_HEAD_DIM).reshape(1, d_inner), norm_w.reshape(1, d_inner))


def _rope_kernel(pos_ref, inv_ref, cos_ref, sin_ref):
    ang = pos_ref[...].astype(F32) * inv_ref[...]
    lane = lax.broadcasted_iota(jnp.int32, ang.shape, 1)
    cos_ref[...] = jnp.cos(ang)
    sin_ref[...] = jnp.where(lane < ATT_HEAD_DIM // 2, -1.0, 1.0) * jnp.sin(ang)


def _rope_tables(positions):
    t = positions.size
    half = ATT_HEAD_DIM // 2
    inv = 1.0 / (ROPE_THETA ** (jnp.arange(half, dtype=F32) / half))
    inv2 = jnp.concatenate([inv, inv]).reshape(1, ATT_HEAD_DIM)
    tt = min(2048, t)
    return pl.pallas_call(
        _rope_kernel,
        grid=(t // tt,),
        in_specs=[pl.BlockSpec((tt, 1), lambda i: (i, 0)),
                  pl.BlockSpec((1, ATT_HEAD_DIM), lambda i: (0, 0))],
        out_specs=[pl.BlockSpec((tt, ATT_HEAD_DIM), lambda i: (i, 0))] * 2,
        out_shape=[jax.ShapeDtypeStruct((t, ATT_HEAD_DIM), F32)] * 2,
        compiler_params=_cparams(("arbitrary",)),
        name="rope_tables",
    )(positions.reshape(t, 1), inv2)


def _rot(x, cos2, sin2):
    return x * cos2 + pltpu.roll(x, ATT_HEAD_DIM // 2, axis=1) * sin2


def _attn_kernel(q_ref, k_ref, v_ref, cos_ref, sin_ref, o_ref, krot, vt, kmean):
    blk = MOBA_BLOCK
    nb = k_ref.shape[1] // blk
    log2e_scale = ATT_HEAD_DIM ** -0.5 * math.log2(math.e)

    kmean[...] = jnp.zeros_like(kmean)
    for j in range(nb):
        rows = slice(j * blk, (j + 1) * blk)
        kr = _rot(k_ref[0, rows, :].astype(F32), cos_ref[0, rows, :], sin_ref[0, rows, :])
        krot[rows, :] = kr.astype(BF16)
        kmean[j:j + 1, :] = jnp.mean(kr, axis=0, keepdims=True)
        vt[:, rows] = v_ref[0, rows, :].astype(F32).T.astype(BF16)
    km = kmean[...]

    ki = lax.broadcasted_iota(jnp.int32, (blk, blk), 0)
    qj = lax.broadcasted_iota(jnp.int32, (blk, blk), 1)
    sub = lax.broadcasted_iota(jnp.int32, (km.shape[0], blk), 0)
    for qi in range(nb):
        qrows = slice(qi * blk, (qi + 1) * blk)
        qr = _rot(q_ref[0, qrows, :].astype(F32), cos_ref[0, qrows, :], sin_ref[0, qrows, :])
        qs = (qr * log2e_scale).astype(BF16)
        gate = _dot_nt_x3(km, qr)
        scores = [jnp.where(ki <= qj, _dot_nt(krot[qrows, :], qs), NEG_BIG)]
        for j in range(qi):
            gj = gate[j:j + 1, :]
            ahead = (gate > gj) | ((gate == gj) & (sub < j))
            rank = jnp.sum(jnp.where((sub < qi) & ahead, 1.0, 0.0), axis=0, keepdims=True)
            bias = jnp.where(rank < MOBA_TOPK, 0.0, NEG_BIG)
            scores.append(_dot_nt(krot[j * blk:(j + 1) * blk, :], qs) + bias)
        m = scores[0]
        for sj in scores[1:]:
            m = jnp.maximum(m, sj)
        m = jnp.max(m, axis=0, keepdims=True)
        l = None
        acc = None
        for j, sj in enumerate(scores):
            src = qi if j == 0 else j - 1
            p = jnp.exp2(sj - m)
            lj = jnp.sum(p, axis=0, keepdims=True)
            aj = jnp.dot(vt[:, src * blk:(src + 1) * blk], p.astype(BF16),
                         preferred_element_type=F32)
            l = lj if l is None else l + lj
            acc = aj if acc is None else acc + aj
        o_ref[0, qrows, :] = (acc / l).T.astype(o_ref.dtype)


def _attention(proj, cos2, sin2, q0, k0, v0, n_heads):
    bsz, s, _ = proj.shape
    dh = ATT_HEAD_DIM
    col = lambda c0: pl.BlockSpec((1, s, dh), lambda b, h: (b, 0, c0 + h))
    tab = pl.BlockSpec((1, s, dh), lambda b, h: (b, 0, 0))
    return pl.pallas_call(
        _attn_kernel,
        grid=(bsz, n_heads),
        in_specs=[col(q0), col(k0), col(v0), tab, tab],
        out_specs=pl.BlockSpec((1, s, dh), lambda b, h: (b, 0, h)),
        out_shape=jax.ShapeDtypeStruct((bsz, s, n_heads * dh), BF16),
        scratch_shapes=[pltpu.VMEM((s, dh), BF16), pltpu.VMEM((dh, s), BF16),
                        pltpu.VMEM((-(-(s // MOBA_BLOCK) // SUBLANES) * SUBLANES, dh), F32)],
        compiler_params=_cparams(("arbitrary", "arbitrary")),
        name="moba_attn",
    )(proj, proj, proj, cos2.reshape(bsz, s, dh), sin2.reshape(bsz, s, dh))


def _merge_kernel(ys_ref, ya_ref, w1_ref, w2_ref, gs_ref, ga_ref, o_ref):
    b1 = jnp.dot(ys_ref[...], w1_ref[...], preferred_element_type=F32)
    b2 = jnp.dot(ya_ref[...], w2_ref[...], preferred_element_type=F32)
    o_ref[...] = (jax.nn.sigmoid(gs_ref[...].astype(F32)) * b1
                  + jax.nn.sigmoid(ga_ref[...].astype(F32)) * b2).astype(o_ref.dtype)


def _merge(y_ssd, y_att, w1, w2, proj2d, gs0, ga0):
    m, k1 = y_ssd.shape
    k2 = y_att.shape[1]
    n = w1.shape[1]
    tm, tn = 512, 512
    return pl.pallas_call(
        _merge_kernel,
        grid=(n // tn, m // tm),
        in_specs=[pl.BlockSpec((tm, k1), lambda j, i: (i, 0)),
                  pl.BlockSpec((tm, k2), lambda j, i: (i, 0)),
                  pl.BlockSpec((k1, tn), lambda j, i: (0, j)),
                  pl.BlockSpec((k2, tn), lambda j, i: (0, j)),
                  pl.BlockSpec((tm, tn), lambda j, i: (i, gs0 // tn + j)),
                  pl.BlockSpec((tm, tn), lambda j, i: (i, ga0 // tn + j))],
        out_specs=pl.BlockSpec((tm, tn), lambda j, i: (i, j)),
        out_shape=jax.ShapeDtypeStruct((m, n), BF16),
        compiler_params=_cparams(("arbitrary", "arbitrary")),
        name="branch_merge",
    )(y_ssd, y_att, w1, w2, proj2d, proj2d)


def _outproj_kernel(m_ref, w_ref, x_ref, gpost_ref, gt_ref, gpre_ref, sc_ref, sh_ref, wr_ref,
                    x1_ref, h2_ref, h2b_ref, lg_ref):
    y = jnp.dot(m_ref[0], w_ref[...], preferred_element_type=F32)
    x1 = x_ref[0] + gt_ref[0] * (_rms(y) * gpost_ref[...])
    h2 = (_rms(x1) * gpre_ref[...]) * (1.0 + sc_ref[0]) + sh_ref[0]
    x1_ref[0] = x1
    h2_ref[0] = h2
    h2b_ref[0] = h2.astype(BF16)
    lg_ref[...] = _dot_nt_x3(wr_ref[...], h2)


def _outproj(merged, w_out, x, g_post, gt1, g_pre, sc2, sh2, w_router_t):
    bsz, s, d = x.shape
    ne = w_router_t.shape[0]
    tm = 256
    nt = s // tm
    row = pl.BlockSpec((1, tm, d), lambda i: (i // nt, i % nt, 0))
    vec = pl.BlockSpec((1, 1, d), lambda i: (i // nt, 0, 0))
    par = pl.BlockSpec((1, d), lambda i: (0, 0))
    return pl.pallas_call(
        _outproj_kernel,
        grid=(bsz * nt,),
        in_specs=[row, pl.BlockSpec((d, d), lambda i: (0, 0)), row, par, vec, par, vec, vec,
                  pl.BlockSpec((ne, d), lambda i: (0, 0))],
        out_specs=[row, row, row, pl.BlockSpec((ne, tm), lambda i: (0, i))],
        out_shape=[jax.ShapeDtypeStruct((bsz, s, d), F32), jax.ShapeDtypeStruct((bsz, s, d), F32),
                   jax.ShapeDtypeStruct((bsz, s, d), BF16),
                   jax.ShapeDtypeStruct((ne, bsz * s), F32)],
        compiler_params=_cparams(("arbitrary",)),
        name="out_proj_norms",
    )(merged.reshape(bsz, s, d), w_out, x, g_post.reshape(1, d), gt1.reshape(bsz, 1, d),
      g_pre.reshape(1, d), sc2.reshape(bsz, 1, d), sh2.reshape(bsz, 1, d), w_router_t)


def _router_kernel(lg_ref, rb_ref, idx_ref, w_ref, gsc):
    ne, tn = lg_ref.shape
    ng = N_EXPERT_GROUPS
    eg = ne // ng
    scores = jax.nn.sigmoid(lg_ref[...])
    biased = scores + rb_ref[...]
    i8 = lax.broadcasted_iota(jnp.int32, (eg, tn), 0)
    for g in range(ng):
        v = biased[g * eg:(g + 1) * eg, :]
        m1 = jnp.max(v, axis=0, keepdims=True)
        first = jnp.min(jnp.where(v == m1, i8, eg), axis=0, keepdims=True)
        m2 = jnp.max(jnp.where(i8 == first, -jnp.inf, v), axis=0, keepdims=True)
        gsc[g:g + 1, :] = m1 + m2
    gs = gsc[...]
    gi = lax.broadcasted_iota(jnp.int32, (ng, tn), 0)
    masked = []
    for g in range(ng):
        sg = gs[g:g + 1, :]
        ahead = (gs > sg) | ((gs == sg) & (gi < g))
        rank = jnp.sum(jnp.where(ahead, 1.0, 0.0), axis=0, keepdims=True)
        masked.append(jnp.where(rank < TOPK_GROUPS, biased[g * eg:(g + 1) * eg, :], -jnp.inf))
    cur = jnp.concatenate(masked, axis=0)
    sub = lax.broadcasted_iota(jnp.int32, (ne, tn), 0)
    picks = []
    for k in range(TOP_K):
        mx = jnp.max(cur, axis=0, keepdims=True)
        idx = jnp.min(jnp.where(cur == mx, sub, ne), axis=0, keepdims=True)
        hit = sub == idx
        picks.append(jnp.sum(jnp.where(hit, scores, 0.0), axis=0, keepdims=True))
        cur = jnp.where(hit, -jnp.inf, cur)
        idx_ref[k:k + 1, :] = idx
    wsum = picks[0]
    for k in range(1, TOP_K):
        wsum = wsum + picks[k]
    for k in range(TOP_K):
        w_ref[k:k + 1, :] = picks[k] / wsum * ROUTED_SCALE


def _router(logits_t, router_bias):
    ne, t = logits_t.shape
    tn = 512
    return pl.pallas_call(
        _router_kernel,
        grid=(t // tn,),
        in_specs=[pl.BlockSpec((ne, tn), lambda i: (0, i)),
                  pl.BlockSpec((ne, 1), lambda i: (0, 0))],
        out_specs=[pl.BlockSpec((TOP_K, tn), lambda i: (0, i))] * 2,
        out_shape=[jax.ShapeDtypeStruct((TOP_K, t), jnp.int32),
                   jax.ShapeDtypeStruct((TOP_K, t), F32)],
        scratch_shapes=[pltpu.VMEM((N_EXPERT_GROUPS, tn), F32)],
        compiler_params=_cparams(("arbitrary",)),
        name="router_topk",
    )(logits_t, router_bias.reshape(ne, 1))


def _row_copy(src_hbm, idx_ref, buf, sem, slot, r, base):
    return pltpu.make_async_copy(src_hbm.at[pl.ds(idx_ref[0, 0, r], 1), :],
                                 buf.at[slot, pl.ds(base + r, 1), :], sem.at[slot])


def _gather_block(src_hbm, idx_ref, buf, sem, slot, rows):
    def body(r, carry):
        _row_copy(src_hbm, idx_ref, buf, sem, slot, r, 0).start()
        return carry
    lax.fori_loop(0, rows, body, 0)


def _wait_block(src_hbm, buf, sem, slot, rows):
    pltpu.make_async_copy(src_hbm.at[pl.ds(0, rows), :], buf.at[slot], sem.at[slot]).wait()


def _gather_kernel(idx_ref, idx_next_ref, src_hbm, o_ref, buf, sem):
    i = pl.program_id(0)
    n = pl.num_programs(0)
    rows = o_ref.shape[0]
    slot = i % 2

    @pl.when(i == 0)
    def _():
        _gather_block(src_hbm, idx_ref, buf, sem, 0, rows)

    @pl.when(i + 1 < n)
    def _():
        _gather_block(src_hbm, idx_next_ref, buf, sem, 1 - slot, rows)

    _wait_block(src_hbm, buf, sem, slot, rows)
    o_ref[...] = buf[slot].astype(o_ref.dtype)


def _gather_rows(src, idx, rows, out_dtype):
    nblk = idx.shape[0]
    d = src.shape[1]
    return pl.pallas_call(
        _gather_kernel,
        grid=(nblk,),
        in_specs=[pl.BlockSpec((1, 1, rows), lambda i: (i, 0, 0), memory_space=pltpu.SMEM),
                  pl.BlockSpec((1, 1, rows), lambda i: (jnp.minimum(i + 1, nblk - 1), 0, 0),
                               memory_space=pltpu.SMEM),
                  pl.BlockSpec(memory_space=pl.ANY)],
        out_specs=pl.BlockSpec((rows, d), lambda i: (i, 0)),
        out_shape=jax.ShapeDtypeStruct((nblk * rows, d), out_dtype),
        scratch_shapes=[pltpu.VMEM((2, rows, d), src.dtype), pltpu.SemaphoreType.DMA((2,))],
        compiler_params=_cparams(("arbitrary",)),
        name="expert_row_gather",
    )(idx, idx, src)


def _expert_kernel(be_ref, x_ref, wg_ref, wu_ref, wd_ref, rw_ref, o_ref):
    x = x_ref[...]
    gate = jnp.dot(x, wg_ref[0], preferred_element_type=F32)
    up = jnp.dot(x, wu_ref[0], preferred_element_type=F32)
    act = (_silu(gate) * up).astype(BF16)
    o_ref[...] = jnp.dot(act, wd_ref[0], preferred_element_type=F32) * rw_ref[...]


def _expert_ffn(blk_exp, x_sorted, w_gate, w_up, w_down, row_w, rows):
    n_rows, d = x_sorted.shape
    f = w_gate.shape[2]
    nblk = n_rows // rows
    grid_spec = pltpu.PrefetchScalarGridSpec(
        num_scalar_prefetch=1,
        grid=(nblk,),
        in_specs=[pl.BlockSpec((rows, d), lambda i, be: (i, 0)),
                  pl.BlockSpec((1, d, f), lambda i, be: (be[i], 0, 0)),
                  pl.BlockSpec((1, d, f), lambda i, be: (be[i], 0, 0)),
                  pl.BlockSpec((1, f, d), lambda i, be: (be[i], 0, 0)),
                  pl.BlockSpec((rows, 1), lambda i, be: (i, 0))],
        out_specs=pl.BlockSpec((rows, d), lambda i, be: (i, 0)),
    )
    return pl.pallas_call(
        _expert_kernel,
        grid_spec=grid_spec,
        out_shape=jax.ShapeDtypeStruct((n_rows, d), F32),
        compiler_params=_cparams(("arbitrary",)),
        name="expert_ffn",
    )(blk_exp, x_sorted, w_gate, w_up, w_down, row_w.reshape(n_rows, 1))


def _shared_kernel(x_ref, wg_ref, wu_ref, wd_ref, o_ref):
    x = x_ref[...]
    gate = jnp.dot(x, wg_ref[...], preferred_element_type=F32)
    up = jnp.dot(x, wu_ref[...], preferred_element_type=F32)
    act = (_silu(gate) * up).astype(BF16)
    o_ref[...] = jnp.dot(act, wd_ref[...], preferred_element_type=F32)


def _shared_ffn(x, wg, wu, wd):
    m, d = x.shape
    f = wg.shape[1]
    tm = 512
    full = lambda shape: pl.BlockSpec(shape, lambda i: (0, 0))
    return pl.pallas_call(
        _shared_kernel,
        grid=(m // tm,),
        in_specs=[pl.BlockSpec((tm, d), lambda i: (i, 0)), full((d, f)), full((d, f)), full((f, d))],
        out_specs=pl.BlockSpec((tm, d), lambda i: (i, 0)),
        out_shape=jax.ShapeDtypeStruct((m, d), F32),
        compiler_params=_cparams(("arbitrary",)),
        name="shared_ffn",
    )(x, wg, wu, wd)


def _combine_kernel(idx_ref, idx_next_ref, y_hbm, sh_ref, x1_ref, gt_ref, g_ref, o_ref, buf, sem):
    i = pl.program_id(0)
    n = pl.num_programs(0)
    tt = o_ref.shape[1]
    rows = tt * TOP_K
    slot = i % 2

    @pl.when(i == 0)
    def _():
        _gather_block(y_hbm, idx_ref, buf, sem, 0, rows)

    @pl.when(i + 1 < n)
    def _():
        _gather_block(y_hbm, idx_next_ref, buf, sem, 1 - slot, rows)

    _wait_block(y_hbm, buf, sem, slot, rows)
    y = sh_ref[...]
    for k in range(TOP_K):
        y = y + buf[slot, k * tt:(k + 1) * tt, :]
    o_ref[0] = x1_ref[0] + gt_ref[0] * (_rms(y) * g_ref[...])


def _combine(pos, y_rows, shared, x1, gt2, g_post):
    bsz, s, d = x1.shape
    tt = COMBINE_TOKENS
    nt = s // tt
    nblk = bsz * nt
    rows = tt * TOP_K
    row = pl.BlockSpec((1, tt, d), lambda i: (i // nt, i % nt, 0))
    return pl.pallas_call(
        _combine_kernel,
        grid=(nblk,),
        in_specs=[pl.BlockSpec((1, 1, rows), lambda i: (i, 0, 0), memory_space=pltpu.SMEM),
                  pl.BlockSpec((1, 1, rows), lambda i: (jnp.minimum(i + 1, nblk - 1), 0, 0),
                               memory_space=pltpu.SMEM),
                  pl.BlockSpec(memory_space=pl.ANY),
                  pl.BlockSpec((tt, d), lambda i: (i, 0)),
                  row,
                  pl.BlockSpec((1, 1, d), lambda i: (i // nt, 0, 0)),
                  pl.BlockSpec((1, d), lambda i: (0, 0))],
        out_specs=row,
        out_shape=jax.ShapeDtypeStruct((bsz, s, d), F32),
        scratch_shapes=[pltpu.VMEM((2, rows, d), F32), pltpu.SemaphoreType.DMA((2,))],
        compiler_params=_cparams(("arbitrary",)),
        name="moe_combine",
    )(pos, pos, y_rows, shared, x1, gt2.reshape(bsz, 1, d), g_post.reshape(1, d))


def _dispatch_plan(top_idx, top_w, n_experts, rows):
    t = top_idx.shape[0]
    n_assign = t * TOP_K
    flat_e = top_idx.reshape(-1)
    order = jnp.argsort(flat_e)
    sorted_e = flat_e[order]
    counts = jnp.bincount(flat_e, length=n_experts)
    starts = jnp.cumsum(counts) - counts
    padded = (counts + rows - 1) // rows * rows
    pad_end = jnp.cumsum(padded)
    pad_start = pad_end - padded
    dest = (pad_start[sorted_e] + jnp.arange(n_assign) - starts[sorted_e]).astype(jnp.int32)
    n_blocks = -(-n_assign // rows) + n_experts
    n_rows = n_blocks * rows
    row_tok = jnp.zeros((n_rows,), jnp.int32).at[dest].set((order // TOP_K).astype(jnp.int32))
    row_w = jnp.zeros((n_rows,), F32).at[dest].set(top_w.reshape(-1)[order])
    blk_exp = jnp.minimum(
        jnp.searchsorted(pad_end, jnp.arange(n_blocks) * rows, side="right"),
        n_experts - 1).astype(jnp.int32)
    pos = jnp.zeros((n_assign,), jnp.int32).at[order].set(dest).reshape(t, TOP_K)
    return row_tok, row_w, blk_exp, pos


def kernel(x, c, positions, w_ada, b_ada, g_pre_mix, g_post_mix, g_pre_ffn, g_post_ffn, w_in,
           conv_w, conv_b, dt_bias, a_log, d_skip, ssd_norm_w, w_ssd_br, w_attn_br, w_out,
           w_router, router_bias, w_gate, w_up, w_down, ws_gate, ws_up, ws_down):
    bsz, s, d = x.shape
    t = bsz * s
    for l in range(w_ada.shape[0]):
        d_inner = w_ssd_br.shape[1]
        att_w = w_attn_br.shape[1]
        n_ssd_heads = dt_bias.shape[1]
        conv_dim = conv_w.shape[2]
        n_heads = att_w // ATT_HEAD_DIM
        n_experts = w_router.shape[2]

        mod = _ada(c, w_ada[l], b_ada[l])
        sh1, sc1, gt1, sh2, sc2, gt2 = jnp.split(mod, 6, axis=-1)

        wl = w_in[l]
        o_dt = d_inner + conv_dim
        o_q = o_dt + n_ssd_heads
        w_main = jnp.concatenate([wl[:, :o_dt], wl[:, o_q:]], axis=1).astype(BF16)
        w_dt = jnp.pad(wl[:, o_dt:o_q], ((0, 0), (0, LANES - n_ssd_heads))).astype(BF16)
        h1 = _prenorm(x, g_pre_mix[l], sc1, sh1).reshape(t, d)
        proj = _matmul(h1, w_main, BF16, 1024, 1024, "in_proj")
        dt_raw = _matmul(h1, w_dt, F32, 1024, LANES, "in_proj_dt")[:, :n_ssd_heads]
        n_main = proj.shape[1]
        proj3 = proj.reshape(bsz, s, n_main)
        q_off = o_dt
        k_off = q_off + att_w
        v_off = k_off + att_w
        gs_off = v_off + att_w
        ga_off = gs_off + d

        hg = n_ssd_heads // SSD_GROUPS
        dt_rows = dt_raw.reshape(bsz, s // SSD_CHUNK, SSD_CHUNK, SSD_GROUPS, hg).transpose(0, 3, 1, 4, 2)
        y_ssd = _ssd(proj3, dt_rows, conv_w[l], conv_b[l], dt_bias[l], a_log[l], d_skip[l],
                     ssd_norm_w[l], d_inner)

        cos2, sin2 = _rope_tables(positions)
        y_att = _attention(proj3, cos2, sin2, q_off // ATT_HEAD_DIM, k_off // ATT_HEAD_DIM,
                           v_off // ATT_HEAD_DIM, n_heads)

        merged = _merge(y_ssd.reshape(t, d_inner), y_att.reshape(t, att_w),
                        w_ssd_br[l].astype(BF16), w_attn_br[l].astype(BF16), proj, gs_off, ga_off)
        x1, h2, h2b, logits_t = _outproj(merged, w_out[l].astype(BF16), x, g_post_mix[l], gt1,
                                         g_pre_ffn[l], sc2, sh2, w_router[l].T)

        idx_t, w_t = _router(logits_t, router_bias[l])
        row_tok, row_w, blk_exp, pos = _dispatch_plan(idx_t.T, w_t.T, n_experts, MOE_ROWS)
        x_sorted = _gather_rows(h2.reshape(t, d), row_tok.reshape(-1, 1, MOE_ROWS), MOE_ROWS, BF16)
        y_rows = _expert_ffn(blk_exp, x_sorted, w_gate[l].astype(BF16), w_up[l].astype(BF16),
                             w_down[l].astype(BF16), row_w, MOE_ROWS)
        shared = _shared_ffn(h2b.reshape(t, d), ws_gate[l].astype(BF16), ws_up[l].astype(BF16),
                             ws_down[l].astype(BF16))
        tt = COMBINE_TOKENS
        pos_blk = pos.reshape(t // tt, tt, TOP_K).transpose(0, 2, 1).reshape(t // tt, 1, tt * TOP_K)
        x = _combine(pos_blk, y_rows, shared, x1, gt2, g_post_ffn[l])
    return x
```

```python
import functools
import math

import jax
import jax.numpy as jnp
from jax import lax
from jax.experimental import pallas as pl
from jax.experimental.pallas import tpu as pltpu

F32 = jnp.float32
BF16 = jnp.bfloat16

NORM_EPS = 1e-6
ROPE_THETA = 10000.0

SSD_HEAD_DIM = 64
SSD_GROUPS = 8
SSD_STATE = 128
SSD_CONV = 4
SSD_CHUNK = 128

ATT_HEAD_DIM = 128
MOBA_BLOCK = 256
MOBA_TOPK = 3

TOP_K = 8
N_EXPERT_GROUPS = 8
TOPK_GROUPS = 4
ROUTED_SCALE = 2.5
MOE_ROWS = 512
MOE_TILE = 512
SEG_ALIGN = 16
SEG_CHUNKS = (512, 256, 128, 64, 32, 16)

LANES = 128
SUBLANES = 8
VMEM_LIMIT = 56 * 1024 * 1024
NEG_BIG = -1e30


def _cparams(sem):
    return pltpu.CompilerParams(dimension_semantics=sem, vmem_limit_bytes=VMEM_LIMIT)


def _silu(x):
    return x * jax.nn.sigmoid(x)


def _rms(x):
    return x * lax.rsqrt(jnp.mean(x * x, axis=-1, keepdims=True) + NORM_EPS)


def _split3(a):
    a1 = a.astype(BF16)
    r1 = a - a1.astype(F32)
    a2 = r1.astype(BF16)
    a3 = (r1 - a2.astype(F32)).astype(BF16)
    return a1, a2, a3


def _dot_exact_rhs(a, e_bf16):
    out = None
    for p in _split3(a):
        t = jnp.dot(p, e_bf16, preferred_element_type=F32)
        out = t if out is None else out + t
    return out


def _dot_nt(a, b):
    return lax.dot_general(a, b, (((1,), (1,)), ((), ())), preferred_element_type=F32)


def _dot_nt_x3(a, b):
    a1 = a.astype(BF16)
    a2 = (a - a1.astype(F32)).astype(BF16)
    b1 = b.astype(BF16)
    b2 = (b - b1.astype(F32)).astype(BF16)
    return _dot_nt(a1, b1) + (_dot_nt(a1, b2) + _dot_nt(a2, b1))


def _ada_kernel(c_ref, w_ref, b_ref, o_ref):
    cond = _silu(c_ref[...])
    o_ref[...] = jnp.dot(cond.astype(BF16), w_ref[...].astype(BF16),
                         preferred_element_type=F32) + b_ref[...]


def _ada(c, w_ada, b_ada):
    bsz, d = c.shape
    n = w_ada.shape[1]
    tn = 1024
    return pl.pallas_call(
        _ada_kernel,
        grid=(n // tn,),
        in_specs=[pl.BlockSpec((bsz, d), lambda j: (0, 0)),
                  pl.BlockSpec((d, tn), lambda j: (0, j)),
                  pl.BlockSpec((1, tn), lambda j: (0, j))],
        out_specs=pl.BlockSpec((bsz, tn), lambda j: (0, j)),
        out_shape=jax.ShapeDtypeStruct((bsz, n), F32),
        compiler_params=_cparams(("arbitrary",)),
        name="ada_mod",
    )(c, w_ada, b_ada.reshape(1, n))


def _prenorm_kernel(x_ref, g_ref, sc_ref, sh_ref, o_ref):
    y = _rms(x_ref[0]) * g_ref[...]
    o_ref[0] = (y * (1.0 + sc_ref[0]) + sh_ref[0]).astype(o_ref.dtype)


def _prenorm(x, g, sc, sh):
    bsz, s, d = x.shape
    ts = 512
    vec = pl.BlockSpec((1, 1, d), lambda b, i: (b, 0, 0))
    return pl.pallas_call(
        _prenorm_kernel,
        grid=(bsz, s // ts),
        in_specs=[pl.BlockSpec((1, ts, d), lambda b, i: (b, i, 0)),
                  pl.BlockSpec((1, d), lambda b, i: (0, 0)), vec, vec],
        out_specs=pl.BlockSpec((1, ts, d), lambda b, i: (b, i, 0)),
        out_shape=jax.ShapeDtypeStruct((bsz, s, d), BF16),
        compiler_params=_cparams(("arbitrary", "arbitrary")),
        name="prenorm",
    )(x, g.reshape(1, d), sc.reshape(bsz, 1, d), sh.reshape(bsz, 1, d))


def _mm_kernel(a_ref, w_ref, o_ref):
    o_ref[...] = jnp.dot(a_ref[...], w_ref[...], preferred_element_type=F32).astype(o_ref.dtype)


def _matmul(a, w, out_dtype, tm, tn, name):
    m, k = a.shape
    n = w.shape[1]
    return pl.pallas_call(
        _mm_kernel,
        grid=(n // tn, m // tm),
        in_specs=[pl.BlockSpec((tm, k), lambda j, i: (i, 0)),
                  pl.BlockSpec((k, tn), lambda j, i: (0, j))],
        out_specs=pl.BlockSpec((tm, tn), lambda j, i: (i, j)),
        out_shape=jax.ShapeDtypeStruct((m, n), out_dtype),
        compiler_params=_cparams(("arbitrary", "arbitrary")),
        name=name,
    )(a, w)


def _softplus(x):
    return jnp.maximum(x, 0.0) + jnp.log(1.0 + jnp.exp(-jnp.abs(x)))


def _ssd_kernel(x_ref, b_ref, c_ref, z_ref, dt_ref, cwx_ref, cwb_ref, cwc_ref,
                cbx_ref, cbb_ref, cbc_ref, dtb_ref, alog_ref, dsk_ref, nw_ref,
                o_ref, cbuf, state):
    q = SSD_CHUNK
    gw = x_ref.shape[2]
    nst = b_ref.shape[2]
    hg = gw // SSD_HEAD_DIM
    cw = gw + 2 * nst
    pad = SUBLANES

    @pl.when(pl.program_id(2) == 0)
    def _():
        cbuf[0:pad, :] = jnp.zeros((pad, cw), F32)
        state[...] = jnp.zeros_like(state)

    xin = jnp.concatenate([x_ref[0], b_ref[0], c_ref[0]], axis=-1).astype(F32)
    cbuf[pad:pad + q, :] = xin
    w = jnp.concatenate([cwx_ref[...], cwb_ref[...], cwc_ref[...]], axis=-1)
    bias = jnp.concatenate([cbx_ref[...], cbb_ref[...], cbc_ref[...]], axis=-1)
    acc = bias + w[SSD_CONV - 1:SSD_CONV, :] * xin
    for s in range(1, SSD_CONV):
        acc = acc + w[SSD_CONV - 1 - s:SSD_CONV - s, :] * cbuf[pad - s:pad - s + q, :]
    cbuf[pad - (SSD_CONV - 1):pad, :] = cbuf[pad + q - (SSD_CONV - 1):pad + q, :]
    xbc = _silu(acc)
    xs = xbc[:, :gw]
    bm = xbc[:, gw:gw + nst].astype(BF16)
    cm = xbc[:, gw + nst:].astype(BF16)

    hrow = lax.broadcasted_iota(jnp.int32, (LANES, q), 0)
    dt_raw = jnp.concatenate([dt_ref[0, 0, 0], jnp.zeros((LANES - hg, q), F32)], axis=0)
    dt_r = jnp.where(hrow < hg, _softplus(dt_raw + dtb_ref[0]), 0.0)
    a_r = dt_r * (-jnp.exp(alog_ref[0]))
    ri = lax.broadcasted_iota(jnp.int32, (q, q), 0)
    ci = lax.broadcasted_iota(jnp.int32, (q, q), 1)
    triu = jnp.where(ri <= ci, 1.0, 0.0).astype(BF16)
    acum_r = _dot_exact_rhs(a_r, triu)
    acum_c = acum_r.T
    dt_c = dt_r.T

    er = lax.broadcasted_iota(jnp.int32, (LANES, gw), 0)
    ec = lax.broadcasted_iota(jnp.int32, (LANES, gw), 1)
    e_ch = jnp.where(ec // SSD_HEAD_DIM == er, 1.0, 0.0).astype(BF16)
    er2 = lax.broadcasted_iota(jnp.int32, (LANES, hg * q), 0)
    ec2 = lax.broadcasted_iota(jnp.int32, (LANES, hg * q), 1)
    e_t = jnp.where(ec2 // q == er2, 1.0, 0.0).astype(BF16)
    dt_full = _dot_exact_rhs(dt_c, e_ch)
    acum_full = _dot_exact_rhs(acum_c, e_ch)
    acum_colb = _dot_exact_rhs(acum_c, e_t)
    acum_rowb = jnp.concatenate(
        [jnp.broadcast_to(acum_r[h:h + 1, :], (q, q)) for h in range(hg)], axis=-1)

    ri2 = lax.broadcasted_iota(jnp.int32, (q, hg * q), 0)
    ci2 = lax.broadcasted_iota(jnp.int32, (q, hg * q), 1) % q
    lmat = jnp.exp(jnp.where(ri2 >= ci2, acum_colb - acum_rowb, NEG_BIG))
    cb = _dot_nt(cm, bm)
    mcat = (jnp.concatenate([cb] * hg, axis=-1) * lmat).astype(BF16)
    xdt = xs * dt_full
    chan_head = lax.broadcasted_iota(jnp.int32, (1, gw), 1) // SSD_HEAD_DIM
    xbd = jnp.concatenate(
        [jnp.where(chan_head == h, xdt, 0.0).astype(BF16) for h in range(hg)], axis=0)
    y = jnp.dot(mcat, xbd, preferred_element_type=F32)

    st = state[...]
    y = y + jnp.dot(cm, st.astype(BF16), preferred_element_type=F32) * jnp.exp(acum_full)
    acum_last = acum_full[q - 1:q, :]
    xw = (xdt * jnp.exp(acum_last - acum_full)).astype(BF16)
    bt = xbc[:, gw:gw + nst].T.astype(BF16)
    state[...] = st * jnp.exp(acum_last) + jnp.dot(bt, xw, preferred_element_type=F32)

    y = y + xs * dsk_ref[...]
    y = y * _silu(z_ref[0].astype(F32))
    o_ref[0] = (_rms(y) * nw_ref[...]).astype(o_ref.dtype)


def _ssd(proj, dt_rows, conv_w, conv_b, dt_bias, a_log, d_skip, norm_w, d_inner):
    bsz, s, _ = proj.shape
    g = SSD_GROUPS
    gw = d_inner // g
    hg = gw // SSD_HEAD_DIM
    nst = SSD_STATE
    q = SSD_CHUNK
    nc = s // q
    x0 = d_inner // gw
    b0 = (2 * d_inner) // nst
    c0 = (2 * d_inner + g * nst) // nst
    cwb0 = d_inner // nst
    cwc0 = (d_inner + g * nst) // nst
    conv_b2 = conv_b.reshape(1, -1)
    head_pad = lambda v: jnp.pad(v.reshape(g, hg), ((0, 0), (0, LANES - hg))).reshape(g, LANES, 1)
    return pl.pallas_call(
        _ssd_kernel,
        grid=(bsz, g, nc),
        in_specs=[
            pl.BlockSpec((1, q, gw), lambda b, gi, c: (b, c, x0 + gi)),
            pl.BlockSpec((1, q, nst), lambda b, gi, c: (b, c, b0 + gi)),
            pl.BlockSpec((1, q, nst), lambda b, gi, c: (b, c, c0 + gi)),
            pl.BlockSpec((1, q, gw), lambda b, gi, c: (b, c, gi)),
            pl.BlockSpec((1, 1, 1, hg, q), lambda b, gi, c: (b, gi, c, 0, 0)),
            pl.BlockSpec((SSD_CONV, gw), lambda b, gi, c: (0, gi)),
            pl.BlockSpec((SSD_CONV, nst), lambda b, gi, c: (0, cwb0 + gi)),
            pl.BlockSpec((SSD_CONV, nst), lambda b, gi, c: (0, cwc0 + gi)),
            pl.BlockSpec((1, gw), lambda b, gi, c: (0, gi)),
            pl.BlockSpec((1, nst), lambda b, gi, c: (0, cwb0 + gi)),
            pl.BlockSpec((1, nst), lambda b, gi, c: (0, cwc0 + gi)),
            pl.BlockSpec((1, LANES, 1), lambda b, gi, c: (gi, 0, 0)),
            pl.BlockSpec((1, LANES, 1), lambda b, gi, c: (gi, 0, 0)),
            pl.BlockSpec((1, gw), lambda b, gi, c: (0, gi)),
            pl.BlockSpec((1, gw), lambda b, gi, c: (0, gi)),
        ],
        out_specs=pl.BlockSpec((1, q, gw), lambda b, gi, c: (b, c, gi)),
        out_shape=jax.ShapeDtypeStruct((bsz, s, d_inner), BF16),
        scratch_shapes=[pltpu.VMEM((SUBLANES + q, gw + 2 * nst), F32),
                        pltpu.VMEM((nst, gw), F32)],
        compiler_params=_cparams(("arbitrary", "arbitrary", "arbitrary")),
        name="ssd_scan",
    )(proj, proj, proj, proj, dt_rows, conv_w, conv_w, conv_w, conv_b2, conv_b2, conv_b2,
      head_pad(dt_bias), head_pad(a_log),
      jnp.repeat(d_skip, SSD_HEAD_DIM).reshape(1, d_inner), norm_w.reshape(1, d_inner))


def _rope_kernel(pos_ref, inv_ref, cos_ref, sin_ref):
    ang = pos_ref[...].astype(F32) * inv_ref[...]
    lane = lax.broadcasted_iota(jnp.int32, ang.shape, 1)
    cos_ref[...] = jnp.cos(ang)
    sin_ref[...] = jnp.where(lane < ATT_HEAD_DIM // 2, -1.0, 1.0) * jnp.sin(ang)


def _rope_tables(positions):
    t = positions.size
    half = ATT_HEAD_DIM // 2
    inv = 1.0 / (ROPE_THETA ** (jnp.arange(half, dtype=F32) / half))
    inv2 = jnp.concatenate([inv, inv]).reshape(1, ATT_HEAD_DIM)
    tt = min(2048, t)
    return pl.pallas_call(
        _rope_kernel,
        grid=(t // tt,),
        in_specs=[pl.BlockSpec((tt, 1), lambda i: (i, 0)),
                  pl.BlockSpec((1, ATT_HEAD_DIM), lambda i: (0, 0))],
        out_specs=[pl.BlockSpec((tt, ATT_HEAD_DIM), lambda i: (i, 0))] * 2,
        out_shape=[jax.ShapeDtypeStruct((t, ATT_HEAD_DIM), F32)] * 2,
        compiler_params=_cparams(("arbitrary",)),
        name="rope_tables",
    )(positions.reshape(t, 1), inv2)


def _rot(x, cos2, sin2):
    return x * cos2 + pltpu.roll(x, ATT_HEAD_DIM // 2, axis=1) * sin2


def _attn_kernel(q_ref, k_ref, v_ref, cos_ref, sin_ref, o_ref, krot, vt, kmean):
    blk = MOBA_BLOCK
    nb = k_ref.shape[1] // blk
    log2e_scale = ATT_HEAD_DIM ** -0.5 * math.log2(math.e)

    kmean[...] = jnp.zeros_like(kmean)
    for j in range(nb):
        rows = slice(j * blk, (j + 1) * blk)
        kr = _rot(k_ref[0, rows, :].astype(F32), cos_ref[0, rows, :], sin_ref[0, rows, :])
        krot[rows, :] = kr.astype(BF16)
        kmean[j:j + 1, :] = jnp.mean(kr, axis=0, keepdims=True)
        vt[:, rows] = v_ref[0, rows, :].astype(F32).T.astype(BF16)
    km = kmean[...]

    ki = lax.broadcasted_iota(jnp.int32, (blk, blk), 0)
    qj = lax.broadcasted_iota(jnp.int32, (blk, blk), 1)
    sub = lax.broadcasted_iota(jnp.int32, (km.shape[0], blk), 0)
    for qi in range(nb):
        qrows = slice(qi * blk, (qi + 1) * blk)
        qr = _rot(q_ref[0, qrows, :].astype(F32), cos_ref[0, qrows, :], sin_ref[0, qrows, :])
        qs = (qr * log2e_scale).astype(BF16)
        gate = _dot_nt_x3(km, qr)
        scores = [jnp.where(ki <= qj, _dot_nt(krot[qrows, :], qs), NEG_BIG)]
        for j in range(qi):
            gj = gate[j:j + 1, :]
            ahead = (gate > gj) | ((gate == gj) & (sub < j))
            rank = jnp.sum(jnp.where((sub < qi) & ahead, 1.0, 0.0), axis=0, keepdims=True)
            bias = jnp.where(rank < MOBA_TOPK, 0.0, NEG_BIG)
            scores.append(_dot_nt(krot[j * blk:(j + 1) * blk, :], qs) + bias)
        m = scores[0]
        for sj in scores[1:]:
            m = jnp.maximum(m, sj)
        m = jnp.max(m, axis=0, keepdims=True)
        l = None
        acc = None
        for j, sj in enumerate(scores):
            src = qi if j == 0 else j - 1
            p = jnp.exp2(sj - m)
            lj = jnp.sum(p, axis=0, keepdims=True)
            aj = jnp.dot(vt[:, src * blk:(src + 1) * blk], p.astype(BF16),
                         preferred_element_type=F32)
            l = lj if l is None else l + lj
            acc = aj if acc is None else acc + aj
        o_ref[0, qrows, :] = (acc / l).T.astype(o_ref.dtype)


def _attention(proj, cos2, sin2, q0, k0, v0, n_heads):
    bsz, s, _ = proj.shape
    dh = ATT_HEAD_DIM
    col = lambda c0: pl.BlockSpec((1, s, dh), lambda b, h: (b, 0, c0 + h))
    tab = pl.BlockSpec((1, s, dh), lambda b, h: (b, 0, 0))
    return pl.pallas_call(
        _attn_kernel,
        grid=(bsz, n_heads),
        in_specs=[col(q0), col(k0), col(v0), tab, tab],
        out_specs=pl.BlockSpec((1, s, dh), lambda b, h: (b, 0, h)),
        out_shape=jax.ShapeDtypeStruct((bsz, s, n_heads * dh), BF16),
        scratch_shapes=[pltpu.VMEM((s, dh), BF16), pltpu.VMEM((dh, s), BF16),
                        pltpu.VMEM((-(-(s // MOBA_BLOCK) // SUBLANES) * SUBLANES, dh), F32)],
        compiler_params=_cparams(("arbitrary", "arbitrary")),
        name="moba_attn",
    )(proj, proj, proj, cos2.reshape(bsz, s, dh), sin2.reshape(bsz, s, dh))


def _merge_kernel(ys_ref, ya_ref, w1_ref, w2_ref, gs_ref, ga_ref, o_ref):
    b1 = jnp.dot(ys_ref[...], w1_ref[...], preferred_element_type=F32)
    b2 = jnp.dot(ya_ref[...], w2_ref[...], preferred_element_type=F32)
    o_ref[...] = (jax.nn.sigmoid(gs_ref[...].astype(F32)) * b1
                  + jax.nn.sigmoid(ga_ref[...].astype(F32)) * b2).astype(o_ref.dtype)


def _merge(y_ssd, y_att, w1, w2, proj2d, gs0, ga0):
    m, k1 = y_ssd.shape
    k2 = y_att.shape[1]
    n = w1.shape[1]
    tm, tn = 512, 512
    return pl.pallas_call(
        _merge_kernel,
        grid=(n // tn, m // tm),
        in_specs=[pl.BlockSpec((tm, k1), lambda j, i: (i, 0)),
                  pl.BlockSpec((tm, k2), lambda j, i: (i, 0)),
                  pl.BlockSpec((k1, tn), lambda j, i: (0, j)),
                  pl.BlockSpec((k2, tn), lambda j, i: (0, j)),
                  pl.BlockSpec((tm, tn), lambda j, i: (i, gs0 // tn + j)),
                  pl.BlockSpec((tm, tn), lambda j, i: (i, ga0 // tn + j))],
        out_specs=pl.BlockSpec((tm, tn), lambda j, i: (i, j)),
        out_shape=jax.ShapeDtypeStruct((m, n), BF16),
        compiler_params=_cparams(("arbitrary", "arbitrary")),
        name="branch_merge",
    )(y_ssd, y_att, w1, w2, proj2d, proj2d)


def _outproj_kernel(m_ref, w_ref, x_ref, gpost_ref, gt_ref, gpre_ref, sc_ref, sh_ref, wr_ref,
                    x1_ref, h2b_ref, lg_ref):
    y = jnp.dot(m_ref[0], w_ref[...], preferred_element_type=F32)
    x1 = x_ref[0] + gt_ref[0] * (_rms(y) * gpost_ref[...])
    h2 = (_rms(x1) * gpre_ref[...]) * (1.0 + sc_ref[0]) + sh_ref[0]
    x1_ref[0] = x1
    h2b_ref[0] = h2.astype(BF16)
    lg_ref[...] = _dot_nt_x3(wr_ref[...], h2)


def _outproj(merged, w_out, x, g_post, gt1, g_pre, sc2, sh2, w_router_t):
    bsz, s, d = x.shape
    ne = w_router_t.shape[0]
    tm = 256
    nt = s // tm
    row = pl.BlockSpec((1, tm, d), lambda i: (i // nt, i % nt, 0))
    vec = pl.BlockSpec((1, 1, d), lambda i: (i // nt, 0, 0))
    par = pl.BlockSpec((1, d), lambda i: (0, 0))
    return pl.pallas_call(
        _outproj_kernel,
        grid=(bsz * nt,),
        in_specs=[row, pl.BlockSpec((d, d), lambda i: (0, 0)), row, par, vec, par, vec, vec,
                  pl.BlockSpec((ne, d), lambda i: (0, 0))],
        out_specs=[row, row, pl.BlockSpec((ne, tm), lambda i: (0, i))],
        out_shape=[jax.ShapeDtypeStruct((bsz, s, d), F32),
                   jax.ShapeDtypeStruct((bsz, s, d), BF16),
                   jax.ShapeDtypeStruct((ne, bsz * s), F32)],
        compiler_params=_cparams(("arbitrary",)),
        name="out_proj_norms",
    )(merged.reshape(bsz, s, d), w_out, x, g_post.reshape(1, d), gt1.reshape(bsz, 1, d),
      g_pre.reshape(1, d), sc2.reshape(bsz, 1, d), sh2.reshape(bsz, 1, d), w_router_t)


def _router_kernel(lg_ref, rb_ref, w_ref, rank_ref, cnt_ref, gsc):
    ne, tn = lg_ref.shape
    ng = N_EXPERT_GROUPS
    eg = ne // ng
    scores = jax.nn.sigmoid(lg_ref[...])
    biased = scores + rb_ref[...]
    i8 = lax.broadcasted_iota(jnp.int32, (eg, tn), 0)
    for g in range(ng):
        v = biased[g * eg:(g + 1) * eg, :]
        m1 = jnp.max(v, axis=0, keepdims=True)
        first = jnp.min(jnp.where(v == m1, i8, eg), axis=0, keepdims=True)
        m2 = jnp.max(jnp.where(i8 == first, -jnp.inf, v), axis=0, keepdims=True)
        gsc[g:g + 1, :] = m1 + m2
    gs = gsc[...]
    gi = lax.broadcasted_iota(jnp.int32, (ng, tn), 0)
    masked = []
    for g in range(ng):
        sg = gs[g:g + 1, :]
        ahead = (gs > sg) | ((gs == sg) & (gi < g))
        rank = jnp.sum(jnp.where(ahead, 1.0, 0.0), axis=0, keepdims=True)
        masked.append(jnp.where(rank < TOPK_GROUPS, biased[g * eg:(g + 1) * eg, :], -jnp.inf))
    cur = jnp.concatenate(masked, axis=0)
    sub = lax.broadcasted_iota(jnp.int32, (ne, tn), 0)
    sel = jnp.zeros((ne, tn), F32)
    for k in range(TOP_K):
        mx = jnp.max(cur, axis=0, keepdims=True)
        idx = jnp.min(jnp.where(cur == mx, sub, ne), axis=0, keepdims=True)
        hit = sub == idx
        sel = jnp.where(hit, 1.0, sel)
        cur = jnp.where(hit, -jnp.inf, cur)
    picked = sel * scores
    wsum = jnp.sum(picked, axis=0, keepdims=True)
    w_ref[...] = picked / wsum * ROUTED_SCALE
    ti = lax.broadcasted_iota(jnp.int32, (tn, tn), 0)
    tj = lax.broadcasted_iota(jnp.int32, (tn, tn), 1)
    before = jnp.where(ti < tj, 1.0, 0.0).astype(BF16)
    rank = jnp.dot(sel.astype(BF16), before, preferred_element_type=F32)
    rank_ref[...] = jnp.where(sel > 0.0, rank, NEG_BIG)
    cnt_ref[0] = jnp.broadcast_to(jnp.sum(sel, axis=1, keepdims=True), (ne, LANES))


def _router(logits_t, router_bias):
    ne, t = logits_t.shape
    tn = MOE_TILE
    tile = pl.BlockSpec((ne, tn), lambda i: (0, i))
    return pl.pallas_call(
        _router_kernel,
        grid=(t // tn,),
        in_specs=[tile, pl.BlockSpec((ne, 1), lambda i: (0, 0))],
        out_specs=[tile, tile, pl.BlockSpec((1, ne, LANES), lambda i: (i, 0, 0))],
        out_shape=[jax.ShapeDtypeStruct((ne, t), F32), jax.ShapeDtypeStruct((ne, t), F32),
                   jax.ShapeDtypeStruct((t // tn, ne, LANES), F32)],
        scratch_shapes=[pltpu.VMEM((N_EXPERT_GROUPS, tn), F32)],
        compiler_params=_cparams(("arbitrary",)),
        name="router_topk",
    )(logits_t, router_bias.reshape(ne, 1))


def _tile_rows(n_experts):
    return MOE_TILE * TOP_K + n_experts * SEG_ALIGN


def _segment_copies(src, dst, sem, n, src_off, dst_off, wait):
    off = jnp.int32(0)
    for p in SEG_CHUNKS:
        bit = n & p

        @pl.when(bit != 0)
        def _(p=p, off=off):
            cp = pltpu.make_async_copy(
                src.at[pl.ds(pl.multiple_of(src_off + off, SEG_ALIGN), p), :],
                dst.at[pl.ds(pl.multiple_of(dst_off + off, SEG_ALIGN), p), :], sem)
            if wait:
                cp.wait()
            else:
                cp.start()
        off = off + bit


def _for_each_segment(ne, fn):
    def body(e, carry):
        fn(e)
        return carry
    lax.fori_loop(0, ne, body, 0)


def _build_perm(pbuf, posb, ss_ref, c16_ref, tile, ne, value_row):
    tn = posb.shape[1]
    pbuf[...] = jnp.zeros_like(pbuf)
    rows16 = lax.broadcasted_iota(jnp.int32, (SEG_ALIGN, tn), 0)

    def per_expert(e):
        base = ss_ref[tile, e]
        groups = lax.shift_right_logical(c16_ref[tile, e], SEG_ALIGN.bit_length() - 1)
        prow = posb[pl.ds(e, 1), :]
        vrow = value_row(e)

        def per_group(g, carry):
            r0 = pl.multiple_of(base + g * SEG_ALIGN, SEG_ALIGN)
            hit = prow == (rows16 + r0).astype(F32)
            pbuf[pl.ds(r0, SEG_ALIGN), :] = jnp.where(hit, vrow, 0.0).astype(BF16)
            return carry
        lax.fori_loop(0, groups, per_group, 0)
    _for_each_segment(ne, per_expert)


def _dispatch_kernel(ss_ref, c16_ref, go_ref, ts_ref, tl_ref, rank_ref, ssv_ref, h_ref, xs_hbm,
                     pbuf, xs, posb, sem):
    i = pl.program_id(0)
    n = pl.num_programs(0)
    ne = rank_ref.shape[0]
    tn = rank_ref.shape[1]
    posb[...] = rank_ref[...] + ssv_ref[0]
    _build_perm(pbuf, posb, ss_ref, c16_ref, i, ne, lambda e: 1.0)

    def drain(tile):
        _for_each_segment(ne, lambda e: _segment_copies(
            xs, xs_hbm, sem, c16_ref[tile, e], ss_ref[tile, e], go_ref[tile, e], True))

    @pl.when(i > 0)
    def _():
        drain(i - 1)

    h = h_ref[...]
    for ch in range(xs.shape[0] // tn):
        rows = slice(ch * tn, (ch + 1) * tn)
        xs[rows, :] = jnp.dot(pbuf[rows, :], h, preferred_element_type=F32).astype(BF16)

    _for_each_segment(ne, lambda e: _segment_copies(
        xs, xs_hbm, sem, c16_ref[i, e], ss_ref[i, e], go_ref[i, e], False))

    @pl.when(i == n - 1)
    def _():
        drain(i)
        xs[0:MOE_ROWS, :] = jnp.zeros((MOE_ROWS, xs.shape[1]), BF16)
        first_free = lax.shift_right_logical(ts_ref[ne], MOE_ROWS.bit_length() - 1)
        n_blocks = xs_hbm.shape[0] // MOE_ROWS
        for wait in (False, True):
            _for_each_segment(ne, lambda e: _segment_copies(
                xs, xs_hbm, sem, tl_ref[e], 0, ts_ref[e], wait))

            def free_block(b, carry, wait=wait):
                cp = pltpu.make_async_copy(
                    xs.at[pl.ds(0, MOE_ROWS), :],
                    xs_hbm.at[pl.ds(pl.multiple_of(b * MOE_ROWS, MOE_ROWS), MOE_ROWS), :], sem)
                if wait:
                    cp.wait()
                else:
                    cp.start()
                return carry
            lax.fori_loop(first_free, n_blocks, free_block, 0)


def _dispatch(seg_start, c16, goff, tail_start, tail_len, rank, h2b, n_rows):
    ne, t = rank.shape
    d = h2b.shape[1]
    tn = MOE_TILE
    rt = _tile_rows(ne)
    grid_spec = pltpu.PrefetchScalarGridSpec(
        num_scalar_prefetch=5,
        grid=(t // tn,),
        in_specs=[pl.BlockSpec((ne, tn), lambda i, *_: (0, i)),
                  pl.BlockSpec((1, ne, 1), lambda i, *_: (i, 0, 0)),
                  pl.BlockSpec((tn, d), lambda i, *_: (i, 0))],
        out_specs=pl.BlockSpec(memory_space=pl.ANY),
        scratch_shapes=[pltpu.VMEM((rt, tn), BF16), pltpu.VMEM((rt, d), BF16),
                        pltpu.VMEM((ne, tn), F32), pltpu.SemaphoreType.DMA],
    )
    return pl.pallas_call(
        _dispatch_kernel,
        grid_spec=grid_spec,
        out_shape=jax.ShapeDtypeStruct((n_rows, d), BF16),
        compiler_params=_cparams(("arbitrary",)),
        name="moe_dispatch",
    )(seg_start, c16, goff, tail_start, tail_len, rank,
      seg_start.astype(F32).reshape(t // tn, ne, 1), h2b)


def _combine_kernel(ss_ref, c16_ref, go_ref, rank_ref, w_ref, ssv_ref, y_hbm, o_ref,
                    pbuf, ys, posb, sem):
    i = pl.program_id(0)
    ne = rank_ref.shape[0]
    tn = rank_ref.shape[1]
    ys[...] = jnp.zeros_like(ys)
    _for_each_segment(ne, lambda e: _segment_copies(
        y_hbm, ys, sem, c16_ref[i, e], go_ref[i, e], ss_ref[i, e], False))
    posb[...] = rank_ref[...] + ssv_ref[0]
    _build_perm(pbuf, posb, ss_ref, c16_ref, i, ne, lambda e: w_ref[pl.ds(e, 1), :])
    _for_each_segment(ne, lambda e: _segment_copies(
        y_hbm, ys, sem, c16_ref[i, e], go_ref[i, e], ss_ref[i, e], True))
    acc = None
    for ch in range(ys.shape[0] // tn):
        rows = slice(ch * tn, (ch + 1) * tn)
        part = lax.dot_general(pbuf[rows, :], ys[rows, :], (((0,), (0,)), ((), ())),
                               preferred_element_type=F32)
        acc = part if acc is None else acc + part
    o_ref[...] = acc


def _combine(seg_start, c16, goff, rank, selw, y_rows):
    ne, t = rank.shape
    d = y_rows.shape[1]
    tn = MOE_TILE
    rt = _tile_rows(ne)
    tile = pl.BlockSpec((ne, tn), lambda i, *_: (0, i))
    grid_spec = pltpu.PrefetchScalarGridSpec(
        num_scalar_prefetch=3,
        grid=(t // tn,),
        in_specs=[tile, tile, pl.BlockSpec((1, ne, 1), lambda i, *_: (i, 0, 0)),
                  pl.BlockSpec(memory_space=pl.ANY)],
        out_specs=pl.BlockSpec((tn, d), lambda i, *_: (i, 0)),
        scratch_shapes=[pltpu.VMEM((rt, tn), BF16), pltpu.VMEM((rt, d), BF16),
                        pltpu.VMEM((ne, tn), F32), pltpu.SemaphoreType.DMA],
    )
    return pl.pallas_call(
        _combine_kernel,
        grid_spec=grid_spec,
        out_shape=jax.ShapeDtypeStruct((t, d), F32),
        compiler_params=_cparams(("arbitrary",)),
        name="moe_combine",
    )(seg_start, c16, goff, rank, selw, seg_start.astype(F32).reshape(t // tn, ne, 1), y_rows)


def _expert_kernel(be_ref, nv_ref, x_ref, wg_ref, wu_ref, wd_ref, o_ref, wg, wu, wd):
    i = pl.program_id(0)

    @pl.when(nv_ref[i] == 0)
    def _():
        o_ref[...] = jnp.zeros_like(o_ref)

    @pl.when(nv_ref[i] > 0)
    def _():
        @pl.when((i == 0) | (be_ref[i] != be_ref[jnp.maximum(i - 1, 0)]))
        def _():
            wg[...] = wg_ref[0].astype(BF16)
            wu[...] = wu_ref[0].astype(BF16)
            wd[...] = wd_ref[0].astype(BF16)

        x = x_ref[...]
        gate = jnp.dot(x, wg[...], preferred_element_type=F32)
        up = jnp.dot(x, wu[...], preferred_element_type=F32)
        act = (_silu(gate) * up).astype(BF16)
        o_ref[...] = jnp.dot(act, wd[...], preferred_element_type=F32).astype(o_ref.dtype)


def _expert_ffn(blk_exp, blk_valid, x_sorted, w_gate, w_up, w_down):
    n_rows, d = x_sorted.shape
    f = w_gate.shape[2]
    rows = MOE_ROWS
    grid_spec = pltpu.PrefetchScalarGridSpec(
        num_scalar_prefetch=2,
        grid=(n_rows // rows,),
        in_specs=[pl.BlockSpec((rows, d), lambda i, be, nv: (i, 0)),
                  pl.BlockSpec((1, d, f), lambda i, be, nv: (be[i], 0, 0)),
                  pl.BlockSpec((1, d, f), lambda i, be, nv: (be[i], 0, 0)),
                  pl.BlockSpec((1, f, d), lambda i, be, nv: (be[i], 0, 0))],
        out_specs=pl.BlockSpec((rows, d), lambda i, be, nv: (i, 0)),
        scratch_shapes=[pltpu.VMEM((d, f), BF16), pltpu.VMEM((d, f), BF16), pltpu.VMEM((f, d), BF16)],
    )
    return pl.pallas_call(
        _expert_kernel,
        grid_spec=grid_spec,
        out_shape=jax.ShapeDtypeStruct((n_rows, d), BF16),
        compiler_params=_cparams(("arbitrary",)),
        name="expert_ffn",
    )(blk_exp, blk_valid, x_sorted, w_gate, w_up, w_down)


def _shared_kernel(x_ref, wg_ref, wu_ref, wd_ref, r_ref, x1_ref, gt_ref, g_ref, o_ref):
    x = x_ref[0]
    gate = jnp.dot(x, wg_ref[...], preferred_element_type=F32)
    up = jnp.dot(x, wu_ref[...], preferred_element_type=F32)
    act = (_silu(gate) * up).astype(BF16)
    y = r_ref[0] + jnp.dot(act, wd_ref[...], preferred_element_type=F32)
    o_ref[0] = x1_ref[0] + gt_ref[0] * (_rms(y) * g_ref[...])


def _shared_final(h2b, wg, wu, wd, routed, x1, gt2, g_post):
    bsz, s, d = x1.shape
    f = wg.shape[1]
    tm = 512
    nt = s // tm
    row = pl.BlockSpec((1, tm, d), lambda i: (i // nt, i % nt, 0))
    full = lambda shape: pl.BlockSpec(shape, lambda i: (0, 0))
    return pl.pallas_call(
        _shared_kernel,
        grid=(bsz * nt,),
        in_specs=[row, full((d, f)), full((d, f)), full((f, d)), row, row,
                  pl.BlockSpec((1, 1, d), lambda i: (i // nt, 0, 0)), full((1, d))],
        out_specs=row,
        out_shape=jax.ShapeDtypeStruct((bsz, s, d), F32),
        compiler_params=_cparams(("arbitrary",)),
        name="shared_ffn_final",
    )(h2b, wg, wu, wd, routed.reshape(bsz, s, d), x1, gt2.reshape(bsz, 1, d), g_post.reshape(1, d))


def _dispatch_plan(counts, n_blocks):
    n_tiles, ne = counts.shape
    c16 = (counts + SEG_ALIGN - 1) // SEG_ALIGN * SEG_ALIGN
    seg_start = jnp.cumsum(c16, axis=1) - c16
    tot = jnp.sum(c16, axis=0)
    padded = (tot + MOE_ROWS - 1) // MOE_ROWS * MOE_ROWS
    pad_end = jnp.cumsum(padded)
    pad_start = pad_end - padded
    goff = pad_start[None, :] + jnp.cumsum(c16, axis=0) - c16
    blk_row = jnp.arange(n_blocks, dtype=jnp.int32) * MOE_ROWS
    blk_exp = jnp.minimum(jnp.searchsorted(pad_end, blk_row, side="right"), ne - 1).astype(jnp.int32)
    blk_valid = jnp.clip(pad_start[blk_exp] + tot[blk_exp] - blk_row, 0, MOE_ROWS)
    blk_valid = jnp.where(blk_row < pad_end[-1], blk_valid, 0).astype(jnp.int32)
    i32 = lambda v: v.astype(jnp.int32)
    tail_start = jnp.concatenate([pad_start + tot, pad_end[-1:]])
    return (i32(seg_start), i32(c16), i32(goff), i32(tail_start), i32(padded - tot),
            blk_exp, blk_valid)


def kernel(x, c, positions, w_ada, b_ada, g_pre_mix, g_post_mix, g_pre_ffn, g_post_ffn, w_in,
           conv_w, conv_b, dt_bias, a_log, d_skip, ssd_norm_w, w_ssd_br, w_attn_br, w_out,
           w_router, router_bias, w_gate, w_up, w_down, ws_gate, ws_up, ws_down):
    bsz, s, d = x.shape
    t = bsz * s
    for l in range(w_ada.shape[0]):
        d_inner = w_ssd_br.shape[1]
        att_w = w_attn_br.shape[1]
        n_ssd_heads = dt_bias.shape[1]
        conv_dim = conv_w.shape[2]
        n_heads = att_w // ATT_HEAD_DIM
        n_experts = w_router.shape[2]

        mod = _ada(c, w_ada[l], b_ada[l])
        sh1, sc1, gt1, sh2, sc2, gt2 = jnp.split(mod, 6, axis=-1)

        wl = w_in[l]
        o_dt = d_inner + conv_dim
        o_q = o_dt + n_ssd_heads
        w_main = jnp.concatenate([wl[:, :o_dt], wl[:, o_q:]], axis=1).astype(BF16)
        w_dt = jnp.pad(wl[:, o_dt:o_q], ((0, 0), (0, LANES - n_ssd_heads))).astype(BF16)
        h1 = _prenorm(x, g_pre_mix[l], sc1, sh1).reshape(t, d)
        proj = _matmul(h1, w_main, BF16, 1024, 1024, "in_proj")
        dt_raw = _matmul(h1, w_dt, F32, 1024, LANES, "in_proj_dt")[:, :n_ssd_heads]
        n_main = proj.shape[1]
        proj3 = proj.reshape(bsz, s, n_main)
        q_off = o_dt
        k_off = q_off + att_w
        v_off = k_off + att_w
        gs_off = v_off + att_w
        ga_off = gs_off + d

        hg = n_ssd_heads // SSD_GROUPS
        dt_rows = dt_raw.reshape(bsz, s // SSD_CHUNK, SSD_CHUNK, SSD_GROUPS, hg).transpose(0, 3, 1, 4, 2)
        y_ssd = _ssd(proj3, dt_rows, conv_w[l], conv_b[l], dt_bias[l], a_log[l], d_skip[l],
                     ssd_norm_w[l], d_inner)

        cos2, sin2 = _rope_tables(positions)
        y_att = _attention(proj3, cos2, sin2, q_off // ATT_HEAD_DIM, k_off // ATT_HEAD_DIM,
                           v_off // ATT_HEAD_DIM, n_heads)

        merged = _merge(y_ssd.reshape(t, d_inner), y_att.reshape(t, att_w),
                        w_ssd_br[l].astype(BF16), w_attn_br[l].astype(BF16), proj, gs_off, ga_off)
        x1, h2b, logits_t = _outproj(merged, w_out[l].astype(BF16), x, g_post_mix[l], gt1,
                                         g_pre_ffn[l], sc2, sh2, w_router[l].T)

        selw, rank, cnt = _router(logits_t, router_bias[l])
        n_tiles = t // MOE_TILE
        n_blocks = -(-(t * TOP_K + n_tiles * n_experts * (SEG_ALIGN - 1)) // MOE_ROWS) + n_experts
        seg_start, c16, goff, tail_start, tail_len, blk_exp, blk_valid = _dispatch_plan(
            cnt[:, :, 0].astype(jnp.int32), n_blocks)
        x_sorted = _dispatch(seg_start, c16, goff, tail_start, tail_len, rank, h2b.reshape(t, d),
                             n_blocks * MOE_ROWS)
        y_rows = _expert_ffn(blk_exp, blk_valid, x_sorted, w_gate[l], w_up[l], w_down[l])
        routed = _combine(seg_start, c16, goff, rank, selw, y_rows)
        x = _shared_final(h2b, ws_gate[l].astype(BF16), ws_up[l].astype(BF16),
                          ws_down[l].astype(BF16), routed, x1, gt2, g_post_ffn[l])
    return x
```

```python
import functools
import math

import jax
import jax.numpy as jnp
from jax import lax
from jax.experimental import pallas as pl
from jax.experimental.pallas import tpu as pltpu

F32 = jnp.float32
BF16 = jnp.bfloat16

NORM_EPS = 1e-6
ROPE_THETA = 10000.0

SSD_HEAD_DIM = 64
SSD_GROUPS = 8
SSD_STATE = 128
SSD_CONV = 4
SSD_CHUNK = 128
SSD_GROUPS_PER_STEP = 4

ATT_HEAD_DIM = 128
MOBA_BLOCK = 256
MOBA_TOPK = 3

TOP_K = 8
N_EXPERT_GROUPS = 8
TOPK_GROUPS = 4
ROUTED_SCALE = 2.5
MOE_ROWS = 512
MOE_TILE = 512
SEG_ALIGN = 16
SEG_CHUNKS = (512, 256, 128, 64, 32, 16)

OUTPROJ_TILE = 512
OUTPROJ_SUBTILE = 256

LANES = 128
SUBLANES = 8
VMEM_LIMIT = 56 * 1024 * 1024
NEG_BIG = -1e30


def _cparams(sem):
    return pltpu.CompilerParams(dimension_semantics=sem, vmem_limit_bytes=VMEM_LIMIT)


def _silu(x):
    return x * jax.nn.sigmoid(x)


def _rms(x):
    return x * lax.rsqrt(jnp.mean(x * x, axis=-1, keepdims=True) + NORM_EPS)


def _split3(a):
    a1 = a.astype(BF16)
    r1 = a - a1.astype(F32)
    a2 = r1.astype(BF16)
    a3 = (r1 - a2.astype(F32)).astype(BF16)
    return a1, a2, a3


def _dot_exact_rhs(a, e_bf16):
    out = None
    for p in _split3(a):
        t = jnp.dot(p, e_bf16, preferred_element_type=F32)
        out = t if out is None else out + t
    return out


def _dot_nt(a, b):
    return lax.dot_general(a, b, (((1,), (1,)), ((), ())), preferred_element_type=F32)


def _dot_nt_x3(a, b):
    a1 = a.astype(BF16)
    a2 = (a - a1.astype(F32)).astype(BF16)
    b1 = b.astype(BF16)
    b2 = (b - b1.astype(F32)).astype(BF16)
    return _dot_nt(a1, b1) + (_dot_nt(a1, b2) + _dot_nt(a2, b1))


def _ada_kernel(c_ref, w_ref, b_ref, o_ref):
    cond = _silu(c_ref[...])
    o_ref[...] = jnp.dot(cond.astype(BF16), w_ref[...].astype(BF16),
                         preferred_element_type=F32) + b_ref[...]


def _ada(c, w_ada, b_ada):
    bsz, d = c.shape
    n = w_ada.shape[1]
    tn = 1024
    return pl.pallas_call(
        _ada_kernel,
        grid=(n // tn,),
        in_specs=[pl.BlockSpec((bsz, d), lambda j: (0, 0)),
                  pl.BlockSpec((d, tn), lambda j: (0, j)),
                  pl.BlockSpec((1, tn), lambda j: (0, j))],
        out_specs=pl.BlockSpec((bsz, tn), lambda j: (0, j)),
        out_shape=jax.ShapeDtypeStruct((bsz, n), F32),
        compiler_params=_cparams(("arbitrary",)),
        name="ada_mod",
    )(c, w_ada, b_ada.reshape(1, n))


def _prenorm_kernel(x_ref, g_ref, sc_ref, sh_ref, o_ref):
    y = _rms(x_ref[0]) * g_ref[...]
    o_ref[0] = (y * (1.0 + sc_ref[0]) + sh_ref[0]).astype(o_ref.dtype)


def _prenorm(x, g, sc, sh):
    bsz, s, d = x.shape
    ts = 512
    vec = pl.BlockSpec((1, 1, d), lambda b, i: (b, 0, 0))
    return pl.pallas_call(
        _prenorm_kernel,
        grid=(bsz, s // ts),
        in_specs=[pl.BlockSpec((1, ts, d), lambda b, i: (b, i, 0)),
                  pl.BlockSpec((1, d), lambda b, i: (0, 0)), vec, vec],
        out_specs=pl.BlockSpec((1, ts, d), lambda b, i: (b, i, 0)),
        out_shape=jax.ShapeDtypeStruct((bsz, s, d), BF16),
        compiler_params=_cparams(("arbitrary", "arbitrary")),
        name="prenorm",
    )(x, g.reshape(1, d), sc.reshape(bsz, 1, d), sh.reshape(bsz, 1, d))


def _mm_kernel(a_ref, w_ref, o_ref):
    o_ref[...] = jnp.dot(a_ref[...], w_ref[...], preferred_element_type=F32).astype(o_ref.dtype)


def _matmul(a, w, out_dtype, tm, tn, name):
    m, k = a.shape
    n = w.shape[1]
    return pl.pallas_call(
        _mm_kernel,
        grid=(n // tn, m // tm),
        in_specs=[pl.BlockSpec((tm, k), lambda j, i: (i, 0)),
                  pl.BlockSpec((k, tn), lambda j, i: (0, j))],
        out_specs=pl.BlockSpec((tm, tn), lambda j, i: (i, j)),
        out_shape=jax.ShapeDtypeStruct((m, n), out_dtype),
        compiler_params=_cparams(("arbitrary", "arbitrary")),
        name=name,
    )(a, w)


def _mm_f32w_kernel(a_ref, w_ref, o_ref, wb):
    @pl.when(pl.program_id(1) == 0)
    def _():
        wb[...] = w_ref[0].astype(BF16)

    o_ref[...] = jnp.dot(a_ref[...], wb[...], preferred_element_type=F32).astype(o_ref.dtype)


def _matmul_f32w(a, w_stack, layer, n, tm, tn, name):
    m, k = a.shape
    return pl.pallas_call(
        _mm_f32w_kernel,
        grid=(n // tn, m // tm),
        in_specs=[pl.BlockSpec((tm, k), lambda j, i: (i, 0)),
                  pl.BlockSpec((1, k, tn), lambda j, i: (layer, 0, j))],
        out_specs=pl.BlockSpec((tm, tn), lambda j, i: (i, j)),
        out_shape=jax.ShapeDtypeStruct((m, n), BF16),
        scratch_shapes=[pltpu.VMEM((k, tn), BF16)],
        compiler_params=_cparams(("arbitrary", "arbitrary")),
        name=name,
    )(a, w_stack)


def _softplus(x):
    return jnp.maximum(x, 0.0) + jnp.log(1.0 + jnp.exp(-jnp.abs(x)))


def _ssd_kernel(x_ref, b_ref, c_ref, z_ref, dt_ref, cwx_ref, cwb_ref, cwc_ref,
                cbx_ref, cbb_ref, cbc_ref, dtb_ref, alog_ref, dsk_ref, nw_ref,
                o_ref, cbuf, state):
    q = SSD_CHUNK
    gps = dt_ref.shape[1]
    gw = x_ref.shape[2] // gps
    nst = b_ref.shape[2] // gps
    hg = gw // SSD_HEAD_DIM
    pad = SUBLANES

    @pl.when(pl.program_id(2) == 0)
    def _():
        cbuf[:, 0:pad, :] = jnp.zeros((gps, pad, gw + 2 * nst), F32)
        state[...] = jnp.zeros_like(state)

    hrow = lax.broadcasted_iota(jnp.int32, (LANES, q), 0)
    ri = lax.broadcasted_iota(jnp.int32, (q, q), 0)
    ci = lax.broadcasted_iota(jnp.int32, (q, q), 1)
    triu = jnp.where(ri <= ci, 1.0, 0.0).astype(BF16)
    er = lax.broadcasted_iota(jnp.int32, (LANES, gw), 0)
    ec = lax.broadcasted_iota(jnp.int32, (LANES, gw), 1)
    e_ch = jnp.where(ec // SSD_HEAD_DIM == er, 1.0, 0.0).astype(BF16)
    er2 = lax.broadcasted_iota(jnp.int32, (LANES, hg * q), 0)
    ec2 = lax.broadcasted_iota(jnp.int32, (LANES, hg * q), 1)
    e_t = jnp.where(ec2 // q == er2, 1.0, 0.0).astype(BF16)
    ri2 = lax.broadcasted_iota(jnp.int32, (q, hg * q), 0)
    ci2 = lax.broadcasted_iota(jnp.int32, (q, hg * q), 1) % q
    chan_head = lax.broadcasted_iota(jnp.int32, (1, gw), 1) // SSD_HEAD_DIM

    for u in range(gps):
        xc = slice(u * gw, (u + 1) * gw)
        nc = slice(u * nst, (u + 1) * nst)
        xin = jnp.concatenate([x_ref[0, :, xc], b_ref[0, :, nc], c_ref[0, :, nc]],
                              axis=-1).astype(F32)
        cbuf[u, pad:pad + q, :] = xin
        w = jnp.concatenate([cwx_ref[:, xc], cwb_ref[:, nc], cwc_ref[:, nc]], axis=-1)
        bias = jnp.concatenate([cbx_ref[:, xc], cbb_ref[:, nc], cbc_ref[:, nc]], axis=-1)
        acc = bias + w[SSD_CONV - 1:SSD_CONV, :] * xin
        for s in range(1, SSD_CONV):
            acc = acc + w[SSD_CONV - 1 - s:SSD_CONV - s, :] * cbuf[u, pad - s:pad - s + q, :]
        cbuf[u, pad - (SSD_CONV - 1):pad, :] = cbuf[u, pad + q - (SSD_CONV - 1):pad + q, :]
        xbc = _silu(acc)
        xs = xbc[:, :gw]
        bm = xbc[:, gw:gw + nst].astype(BF16)
        cm = xbc[:, gw + nst:].astype(BF16)

        dt_raw = jnp.concatenate([dt_ref[0, u, 0], jnp.zeros((LANES - hg, q), F32)], axis=0)
        dt_r = jnp.where(hrow < hg, _softplus(dt_raw + dtb_ref[u]), 0.0)
        a_r = dt_r * (-math.log2(math.e) * jnp.exp(alog_ref[u]))
        acum_r = _dot_exact_rhs(a_r, triu)
        acum_c = acum_r.T
        dt_c = dt_r.T

        dt_full = _dot_exact_rhs(dt_c, e_ch)
        acum_full = _dot_exact_rhs(acum_c, e_ch)
        acum_colb = _dot_exact_rhs(acum_c, e_t)
        acum_rowb = jnp.concatenate(
            [jnp.broadcast_to(acum_r[h:h + 1, :], (q, q)) for h in range(hg)], axis=-1)

        lmat = jnp.exp2(jnp.where(ri2 >= ci2, acum_colb - acum_rowb, NEG_BIG))
        cb = _dot_nt(cm, bm)
        mcat = (jnp.concatenate([cb] * hg, axis=-1) * lmat).astype(BF16)
        xdt = xs * dt_full
        xbd = jnp.concatenate(
            [jnp.where(chan_head == h, xdt, 0.0).astype(BF16) for h in range(hg)], axis=0)
        y = jnp.dot(mcat, xbd, preferred_element_type=F32)

        st = state[u]
        y = y + jnp.dot(cm, st.astype(BF16), preferred_element_type=F32) * jnp.exp2(acum_full)
        acum_last = acum_full[q - 1:q, :]
        xw = (xdt * jnp.exp2(acum_last - acum_full)).astype(BF16)
        bt = xbc[:, gw:gw + nst].T.astype(BF16)
        state[u] = st * jnp.exp2(acum_last) + jnp.dot(bt, xw, preferred_element_type=F32)

        y = y + xs * dsk_ref[:, xc]
        y = y * _silu(z_ref[0, :, xc].astype(F32))
        o_ref[0, :, xc] = (_rms(y) * nw_ref[:, xc]).astype(o_ref.dtype)


def _ssd(proj, dt_rows, conv_w, conv_b, dt_bias, a_log, d_skip, norm_w, d_inner):
    bsz, s, _ = proj.shape
    g = SSD_GROUPS
    gps = SSD_GROUPS_PER_STEP
    gw = d_inner // g
    hg = gw // SSD_HEAD_DIM
    nst = SSD_STATE
    q = SSD_CHUNK
    nc = s // q
    bw, bn = gps * gw, gps * nst
    x0 = d_inner // bw
    b0 = (2 * d_inner) // bn
    c0 = (2 * d_inner + g * nst) // bn
    cwb0 = d_inner // bn
    cwc0 = (d_inner + g * nst) // bn
    conv_b2 = conv_b.reshape(1, -1)
    head_pad = lambda v: jnp.pad(v.reshape(g, hg), ((0, 0), (0, LANES - hg))).reshape(g, LANES, 1)
    return pl.pallas_call(
        _ssd_kernel,
        grid=(bsz, g // gps, nc),
        in_specs=[
            pl.BlockSpec((1, q, bw), lambda b, gi, c: (b, c, x0 + gi)),
            pl.BlockSpec((1, q, bn), lambda b, gi, c: (b, c, b0 + gi)),
            pl.BlockSpec((1, q, bn), lambda b, gi, c: (b, c, c0 + gi)),
            pl.BlockSpec((1, q, bw), lambda b, gi, c: (b, c, gi)),
            pl.BlockSpec((1, gps, 1, hg, q), lambda b, gi, c: (b, gi, c, 0, 0)),
            pl.BlockSpec((SSD_CONV, bw), lambda b, gi, c: (0, gi)),
            pl.BlockSpec((SSD_CONV, bn), lambda b, gi, c: (0, cwb0 + gi)),
            pl.BlockSpec((SSD_CONV, bn), lambda b, gi, c: (0, cwc0 + gi)),
            pl.BlockSpec((1, bw), lambda b, gi, c: (0, gi)),
            pl.BlockSpec((1, bn), lambda b, gi, c: (0, cwb0 + gi)),
            pl.BlockSpec((1, bn), lambda b, gi, c: (0, cwc0 + gi)),
            pl.BlockSpec((gps, LANES, 1), lambda b, gi, c: (gi, 0, 0)),
            pl.BlockSpec((gps, LANES, 1), lambda b, gi, c: (gi, 0, 0)),
            pl.BlockSpec((1, bw), lambda b, gi, c: (0, gi)),
            pl.BlockSpec((1, bw), lambda b, gi, c: (0, gi)),
        ],
        out_specs=pl.BlockSpec((1, q, bw), lambda b, gi, c: (b, c, gi)),
        out_shape=jax.ShapeDtypeStruct((bsz, s, d_inner), BF16),
        scratch_shapes=[pltpu.VMEM((gps, SUBLANES + q, gw + 2 * nst), F32),
                        pltpu.VMEM((gps, nst, gw), F32)],
        compiler_params=_cparams(("arbitrary", "arbitrary", "arbitrary")),
        name="ssd_scan",
    )(proj, proj, proj, proj, dt_rows, conv_w, conv_w, conv_w, conv_b2, conv_b2, conv_b2,
      head_pad(dt_bias), head_pad(a_log),
      jnp.repeat(d_skip, SSD_HEAD_DIM).reshape(1, d_inner), norm_w.reshape(1, d_inner))


def _rope_kernel(pos_ref, inv_ref, cos_ref, sin_ref):
    ang = pos_ref[...].astype(F32) * inv_ref[...]
    lane = lax.broadcasted_iota(jnp.int32, ang.shape, 1)
    cos_ref[...] = jnp.cos(ang)
    sin_ref[...] = jnp.where(lane < ATT_HEAD_DIM // 2, -1.0, 1.0) * jnp.sin(ang)


def _rope_tables(positions):
    t = positions.size
    half = ATT_HEAD_DIM // 2
    inv = 1.0 / (ROPE_THETA ** (jnp.arange(half, dtype=F32) / half))
    inv2 = jnp.concatenate([inv, inv]).reshape(1, ATT_HEAD_DIM)
    tt = min(2048, t)
    return pl.pallas_call(
        _rope_kernel,
        grid=(t // tt,),
        in_specs=[pl.BlockSpec((tt, 1), lambda i: (i, 0)),
                  pl.BlockSpec((1, ATT_HEAD_DIM), lambda i: (0, 0))],
        out_specs=[pl.BlockSpec((tt, ATT_HEAD_DIM), lambda i: (i, 0))] * 2,
        out_shape=[jax.ShapeDtypeStruct((t, ATT_HEAD_DIM), F32)] * 2,
        compiler_params=_cparams(("arbitrary",)),
        name="rope_tables",
    )(positions.reshape(t, 1), inv2)


def _rot(x, cos2, sin2):
    return x * cos2 + pltpu.roll(x, ATT_HEAD_DIM // 2, axis=1) * sin2


def _attn_kernel(q_ref, k_ref, v_ref, cos_ref, sin_ref, o_ref, krot, vt, kmean):
    blk = MOBA_BLOCK
    nb = k_ref.shape[1] // blk
    log2e_scale = ATT_HEAD_DIM ** -0.5 * math.log2(math.e)

    kmean[...] = jnp.zeros_like(kmean)
    for j in range(nb):
        rows = slice(j * blk, (j + 1) * blk)
        kr = _rot(k_ref[0, rows, :].astype(F32), cos_ref[0, rows, :], sin_ref[0, rows, :])
        krot[rows, :] = kr.astype(BF16)
        kmean[j:j + 1, :] = jnp.mean(kr, axis=0, keepdims=True)
        vt[:, rows] = v_ref[0, rows, :].astype(F32).T.astype(BF16)
    km = kmean[...]

    ki = lax.broadcasted_iota(jnp.int32, (blk, blk), 0)
    qj = lax.broadcasted_iota(jnp.int32, (blk, blk), 1)
    sub = lax.broadcasted_iota(jnp.int32, (km.shape[0], blk), 0)
    for qi in range(nb):
        qrows = slice(qi * blk, (qi + 1) * blk)
        qr = _rot(q_ref[0, qrows, :].astype(F32), cos_ref[0, qrows, :], sin_ref[0, qrows, :])
        qs = (qr * log2e_scale).astype(BF16)
        gate = _dot_nt_x3(km, qr)
        scores = [jnp.where(ki <= qj, _dot_nt(krot[qrows, :], qs), NEG_BIG)]
        for j in range(qi):
            gj = gate[j:j + 1, :]
            ahead = (gate > gj) | ((gate == gj) & (sub < j))
            rank = jnp.sum(jnp.where((sub < qi) & ahead, 1.0, 0.0), axis=0, keepdims=True)
            bias = jnp.where(rank < MOBA_TOPK, 0.0, NEG_BIG)
            scores.append(_dot_nt(krot[j * blk:(j + 1) * blk, :], qs) + bias)
        m = scores[0]
        for sj in scores[1:]:
            m = jnp.maximum(m, sj)
        m = jnp.max(m, axis=0, keepdims=True)
        l = None
        acc = None
        for j, sj in enumerate(scores):
            src = qi if j == 0 else j - 1
            p = jnp.exp2(sj - m)
            lj = jnp.sum(p, axis=0, keepdims=True)
            aj = jnp.dot(vt[:, src * blk:(src + 1) * blk], p.astype(BF16),
                         preferred_element_type=F32)
            l = lj if l is None else l + lj
            acc = aj if acc is None else acc + aj
        o_ref[0, qrows, :] = (acc / l).T.astype(o_ref.dtype)


def _attention(proj, cos2, sin2, q0, k0, v0, n_heads):
    bsz, s, _ = proj.shape
    dh = ATT_HEAD_DIM
    col = lambda c0: pl.BlockSpec((1, s, dh), lambda b, h: (b, 0, c0 + h))
    tab = pl.BlockSpec((1, s, dh), lambda b, h: (b, 0, 0))
    return pl.pallas_call(
        _attn_kernel,
        grid=(bsz, n_heads),
        in_specs=[col(q0), col(k0), col(v0), tab, tab],
        out_specs=pl.BlockSpec((1, s, dh), lambda b, h: (b, 0, h)),
        out_shape=jax.ShapeDtypeStruct((bsz, s, n_heads * dh), BF16),
        scratch_shapes=[pltpu.VMEM((s, dh), BF16), pltpu.VMEM((dh, s), BF16),
                        pltpu.VMEM((-(-(s // MOBA_BLOCK) // SUBLANES) * SUBLANES, dh), F32)],
        compiler_params=_cparams(("arbitrary", "arbitrary")),
        name="moba_attn",
    )(proj, proj, proj, cos2.reshape(bsz, s, dh), sin2.reshape(bsz, s, dh))


def _merge_kernel(ys_ref, ya_ref, w1_ref, w2_ref, gs_ref, ga_ref, o_ref):
    b1 = jnp.dot(ys_ref[...], w1_ref[...], preferred_element_type=F32)
    b2 = jnp.dot(ya_ref[...], w2_ref[...], preferred_element_type=F32)
    o_ref[...] = (jax.nn.sigmoid(gs_ref[...].astype(F32)) * b1
                  + jax.nn.sigmoid(ga_ref[...].astype(F32)) * b2).astype(o_ref.dtype)


def _merge(y_ssd, y_att, w1, w2, proj2d, gs0, ga0):
    m, k1 = y_ssd.shape
    k2 = y_att.shape[1]
    n = w1.shape[1]
    tm, tn = 512, 512
    return pl.pallas_call(
        _merge_kernel,
        grid=(n // tn, m // tm),
        in_specs=[pl.BlockSpec((tm, k1), lambda j, i: (i, 0)),
                  pl.BlockSpec((tm, k2), lambda j, i: (i, 0)),
                  pl.BlockSpec((k1, tn), lambda j, i: (0, j)),
                  pl.BlockSpec((k2, tn), lambda j, i: (0, j)),
                  pl.BlockSpec((tm, tn), lambda j, i: (i, gs0 // tn + j)),
                  pl.BlockSpec((tm, tn), lambda j, i: (i, ga0 // tn + j))],
        out_specs=pl.BlockSpec((tm, tn), lambda j, i: (i, j)),
        out_shape=jax.ShapeDtypeStruct((m, n), BF16),
        compiler_params=_cparams(("arbitrary", "arbitrary")),
        name="branch_merge",
    )(y_ssd, y_att, w1, w2, proj2d, proj2d)


def _outproj_kernel(m_ref, w_ref, x_ref, gpost_ref, gt_ref, gpre_ref, sc_ref, sh_ref, wr_ref,
                    x1_ref, h2b_ref, lg_ref):
    for r in range(m_ref.shape[1] // OUTPROJ_SUBTILE):
        rows = slice(r * OUTPROJ_SUBTILE, (r + 1) * OUTPROJ_SUBTILE)
        y = jnp.dot(m_ref[0, rows, :], w_ref[...], preferred_element_type=F32)
        x1 = x_ref[0, rows, :] + gt_ref[0] * (_rms(y) * gpost_ref[...])
        h2 = (_rms(x1) * gpre_ref[...]) * (1.0 + sc_ref[0]) + sh_ref[0]
        x1_ref[0, rows, :] = x1
        h2b_ref[0, rows, :] = h2.astype(BF16)
        h_hi = h2.astype(BF16)
        h_lo = (h2 - h_hi.astype(F32)).astype(BF16)
        wr = wr_ref[...]
        w_hi = wr.astype(BF16)
        w_lo = (wr - w_hi.astype(F32)).astype(BF16)
        lg = (jnp.dot(h_hi, w_hi, preferred_element_type=F32)
              + (jnp.dot(h_hi, w_lo, preferred_element_type=F32)
                 + jnp.dot(h_lo, w_hi, preferred_element_type=F32)))
        lg_ref[:, rows] = lg.T[:lg_ref.shape[0], :]


def _outproj(merged, w_out, x, g_post, gt1, g_pre, sc2, sh2, w_router):
    bsz, s, d = x.shape
    ne = w_router.shape[1]
    w_router_p = jnp.pad(w_router, ((0, 0), (0, LANES - ne)))
    tm = OUTPROJ_TILE
    nt = s // tm
    row = pl.BlockSpec((1, tm, d), lambda i: (i // nt, i % nt, 0))
    vec = pl.BlockSpec((1, 1, d), lambda i: (i // nt, 0, 0))
    par = pl.BlockSpec((1, d), lambda i: (0, 0))
    return pl.pallas_call(
        _outproj_kernel,
        grid=(bsz * nt,),
        in_specs=[row, pl.BlockSpec((d, d), lambda i: (0, 0)), row, par, vec, par, vec, vec,
                  pl.BlockSpec((d, LANES), lambda i: (0, 0))],
        out_specs=[row, row, pl.BlockSpec((ne, tm), lambda i: (0, i))],
        out_shape=[jax.ShapeDtypeStruct((bsz, s, d), F32),
                   jax.ShapeDtypeStruct((bsz, s, d), BF16),
                   jax.ShapeDtypeStruct((ne, bsz * s), F32)],
        compiler_params=_cparams(("arbitrary",)),
        name="out_proj_norms",
    )(merged.reshape(bsz, s, d), w_out, x, g_post.reshape(1, d), gt1.reshape(bsz, 1, d),
      g_pre.reshape(1, d), sc2.reshape(bsz, 1, d), sh2.reshape(bsz, 1, d), w_router_p)


def _router_kernel(lg_ref, rb_ref, w_ref, rank_ref, cnt_ref, gsc):
    ne, tn = lg_ref.shape
    ng = N_EXPERT_GROUPS
    eg = ne // ng
    scores = jax.nn.sigmoid(lg_ref[...])
    biased = scores + rb_ref[...]
    i8 = lax.broadcasted_iota(jnp.int32, (eg, tn), 0)
    for g in range(ng):
        v = biased[g * eg:(g + 1) * eg, :]
        m1 = jnp.max(v, axis=0, keepdims=True)
        first = jnp.min(jnp.where(v == m1, i8, eg), axis=0, keepdims=True)
        m2 = jnp.max(jnp.where(i8 == first, -jnp.inf, v), axis=0, keepdims=True)
        gsc[g:g + 1, :] = m1 + m2
    gs = gsc[...]
    gi = lax.broadcasted_iota(jnp.int32, (ng, tn), 0)
    masked = []
    for g in range(ng):
        sg = gs[g:g + 1, :]
        ahead = (gs > sg) | ((gs == sg) & (gi < g))
        rank = jnp.sum(jnp.where(ahead, 1.0, 0.0), axis=0, keepdims=True)
        masked.append(jnp.where(rank < TOPK_GROUPS, biased[g * eg:(g + 1) * eg, :], -jnp.inf))
    cur = jnp.concatenate(masked, axis=0)
    sub = lax.broadcasted_iota(jnp.int32, (ne, tn), 0)
    sel = jnp.zeros((ne, tn), F32)
    for k in range(TOP_K):
        mx = jnp.max(cur, axis=0, keepdims=True)
        idx = jnp.min(jnp.where(cur == mx, sub, ne), axis=0, keepdims=True)
        hit = sub == idx
        sel = jnp.where(hit, 1.0, sel)
        cur = jnp.where(hit, -jnp.inf, cur)
    picked = sel * scores
    wsum = jnp.sum(picked, axis=0, keepdims=True)
    w_ref[...] = picked / wsum * ROUTED_SCALE
    ti = lax.broadcasted_iota(jnp.int32, (tn, tn), 0)
    tj = lax.broadcasted_iota(jnp.int32, (tn, tn), 1)
    before = jnp.where(ti < tj, 1.0, 0.0).astype(BF16)
    rank = jnp.dot(sel.astype(BF16), before, preferred_element_type=F32)
    rank_ref[...] = jnp.where(sel > 0.0, rank, NEG_BIG)
    cnt_ref[0] = jnp.broadcast_to(jnp.sum(sel, axis=1, keepdims=True), (ne, LANES))


def _router(logits_t, router_bias):
    ne, t = logits_t.shape
    tn = MOE_TILE
    tile = pl.BlockSpec((ne, tn), lambda i: (0, i))
    return pl.pallas_call(
        _router_kernel,
        grid=(t // tn,),
        in_specs=[tile, pl.BlockSpec((ne, 1), lambda i: (0, 0))],
        out_specs=[tile, tile, pl.BlockSpec((1, ne, LANES), lambda i: (i, 0, 0))],
        out_shape=[jax.ShapeDtypeStruct((ne, t), F32), jax.ShapeDtypeStruct((ne, t), F32),
                   jax.ShapeDtypeStruct((t // tn, ne, LANES), F32)],
        scratch_shapes=[pltpu.VMEM((N_EXPERT_GROUPS, tn), F32)],
        compiler_params=_cparams(("arbitrary",)),
        name="router_topk",
    )(logits_t, router_bias.reshape(ne, 1))


def _tile_rows(n_experts):
    return MOE_TILE * TOP_K + n_experts * SEG_ALIGN


def _segment_copies(src, dst, sem, n, src_off, dst_off, wait):
    off = jnp.int32(0)
    for p in SEG_CHUNKS:
        bit = n & p

        @pl.when(bit != 0)
        def _(p=p, off=off):
            cp = pltpu.make_async_copy(
                src.at[pl.ds(pl.multiple_of(src_off + off, SEG_ALIGN), p), :],
                dst.at[pl.ds(pl.multiple_of(dst_off + off, SEG_ALIGN), p), :], sem)
            if wait:
                cp.wait()
            else:
                cp.start()
        off = off + bit


def _for_each_segment(ne, fn):
    def body(e, carry):
        fn(e)
        return carry
    lax.fori_loop(0, ne, body, 0)


def _build_perm(pbuf, posb, ss_ref, c16_ref, tile, ne, value_row):
    tn = posb.shape[1]
    pbuf[...] = jnp.zeros_like(pbuf)
    rows16 = lax.broadcasted_iota(jnp.int32, (SEG_ALIGN, tn), 0)

    def per_expert(e):
        base = ss_ref[tile, e]
        groups = lax.shift_right_logical(c16_ref[tile, e], SEG_ALIGN.bit_length() - 1)
        prow = posb[pl.ds(e, 1), :]
        vrow = value_row(e)

        def per_group(g, carry):
            r0 = pl.multiple_of(base + g * SEG_ALIGN, SEG_ALIGN)
            hit = prow == (rows16 + r0).astype(F32)
            pbuf[pl.ds(r0, SEG_ALIGN), :] = jnp.where(hit, vrow, 0.0).astype(BF16)
            return carry
        lax.fori_loop(0, groups, per_group, 0)
    _for_each_segment(ne, per_expert)


def _dispatch_kernel(ss_ref, c16_ref, go_ref, ts_ref, tl_ref, rank_ref, ssv_ref, h_ref, xs_hbm,
                     pbuf, xs, posb, sem):
    i = pl.program_id(0)
    n = pl.num_programs(0)
    ne = rank_ref.shape[0]
    tn = rank_ref.shape[1]
    posb[...] = rank_ref[...] + ssv_ref[0]
    _build_perm(pbuf, posb, ss_ref, c16_ref, i, ne, lambda e: 1.0)

    def drain(tile):
        _for_each_segment(ne, lambda e: _segment_copies(
            xs, xs_hbm, sem, c16_ref[tile, e], ss_ref[tile, e], go_ref[tile, e], True))

    @pl.when(i > 0)
    def _():
        drain(i - 1)

    h = h_ref[...]
    for ch in range(xs.shape[0] // tn):
        rows = slice(ch * tn, (ch + 1) * tn)
        xs[rows, :] = jnp.dot(pbuf[rows, :], h, preferred_element_type=F32).astype(BF16)

    _for_each_segment(ne, lambda e: _segment_copies(
        xs, xs_hbm, sem, c16_ref[i, e], ss_ref[i, e], go_ref[i, e], False))

    @pl.when(i == n - 1)
    def _():
        drain(i)
        xs[0:MOE_ROWS, :] = jnp.zeros((MOE_ROWS, xs.shape[1]), BF16)
        first_free = lax.shift_right_logical(ts_ref[ne], MOE_ROWS.bit_length() - 1)
        n_blocks = xs_hbm.shape[0] // MOE_ROWS
        for wait in (False, True):
            _for_each_segment(ne, lambda e: _segment_copies(
                xs, xs_hbm, sem, tl_ref[e], 0, ts_ref[e], wait))

            def free_block(b, carry, wait=wait):
                cp = pltpu.make_async_copy(
                    xs.at[pl.ds(0, MOE_ROWS), :],
                    xs_hbm.at[pl.ds(pl.multiple_of(b * MOE_ROWS, MOE_ROWS), MOE_ROWS), :], sem)
                if wait:
                    cp.wait()
                else:
                    cp.start()
                return carry
            lax.fori_loop(first_free, n_blocks, free_block, 0)


def _dispatch(seg_start, c16, goff, tail_start, tail_len, rank, h2b, n_rows):
    ne, t = rank.shape
    d = h2b.shape[1]
    tn = MOE_TILE
    rt = _tile_rows(ne)
    grid_spec = pltpu.PrefetchScalarGridSpec(
        num_scalar_prefetch=5,
        grid=(t // tn,),
        in_specs=[pl.BlockSpec((ne, tn), lambda i, *_: (0, i)),
                  pl.BlockSpec((1, ne, 1), lambda i, *_: (i, 0, 0)),
                  pl.BlockSpec((tn, d), lambda i, *_: (i, 0))],
        out_specs=pl.BlockSpec(memory_space=pl.ANY),
        scratch_shapes=[pltpu.VMEM((rt, tn), BF16), pltpu.VMEM((rt, d), BF16),
                        pltpu.VMEM((ne, tn), F32), pltpu.SemaphoreType.DMA],
    )
    return pl.pallas_call(
        _dispatch_kernel,
        grid_spec=grid_spec,
        out_shape=jax.ShapeDtypeStruct((n_rows, d), BF16),
        compiler_params=_cparams(("arbitrary",)),
        name="moe_dispatch",
    )(seg_start, c16, goff, tail_start, tail_len, rank,
      seg_start.astype(F32).reshape(t // tn, ne, 1), h2b)


def _combine_kernel(ss_ref, c16_ref, go_ref, rank_ref, w_ref, ssv_ref, y_hbm, o_ref,
                    pbuf, ys, posb, sem):
    i = pl.program_id(0)
    ne = rank_ref.shape[0]
    tn = rank_ref.shape[1]
    ys[...] = jnp.zeros_like(ys)
    _for_each_segment(ne, lambda e: _segment_copies(
        y_hbm, ys, sem, c16_ref[i, e], go_ref[i, e], ss_ref[i, e], False))
    posb[...] = rank_ref[...] + ssv_ref[0]
    _build_perm(pbuf, posb, ss_ref, c16_ref, i, ne, lambda e: w_ref[pl.ds(e, 1), :])
    _for_each_segment(ne, lambda e: _segment_copies(
        y_hbm, ys, sem, c16_ref[i, e], go_ref[i, e], ss_ref[i, e], True))
    acc = None
    for ch in range(ys.shape[0] // tn):
        rows = slice(ch * tn, (ch + 1) * tn)
        part = lax.dot_general(pbuf[rows, :], ys[rows, :], (((0,), (0,)), ((), ())),
                               preferred_element_type=F32)
        acc = part if acc is None else acc + part
    o_ref[...] = acc


def _combine(seg_start, c16, goff, rank, selw, y_rows):
    ne, t = rank.shape
    d = y_rows.shape[1]
    tn = MOE_TILE
    rt = _tile_rows(ne)
    tile = pl.BlockSpec((ne, tn), lambda i, *_: (0, i))
    grid_spec = pltpu.PrefetchScalarGridSpec(
        num_scalar_prefetch=3,
        grid=(t // tn,),
        in_specs=[tile, tile, pl.BlockSpec((1, ne, 1), lambda i, *_: (i, 0, 0)),
                  pl.BlockSpec(memory_space=pl.ANY)],
        out_specs=pl.BlockSpec((tn, d), lambda i, *_: (i, 0)),
        scratch_shapes=[pltpu.VMEM((rt, tn), BF16), pltpu.VMEM((rt, d), BF16),
                        pltpu.VMEM((ne, tn), F32), pltpu.SemaphoreType.DMA],
    )
    return pl.pallas_call(
        _combine_kernel,
        grid_spec=grid_spec,
        out_shape=jax.ShapeDtypeStruct((t, d), F32),
        compiler_params=_cparams(("arbitrary",)),
        name="moe_combine",
    )(seg_start, c16, goff, rank, selw, seg_start.astype(F32).reshape(t // tn, ne, 1), y_rows)


def _expert_kernel(be_ref, nv_ref, x_ref, wg_ref, wu_ref, wd_ref, o_ref, wg, wu, wd):
    i = pl.program_id(0)

    @pl.when(nv_ref[i] == 0)
    def _():
        o_ref[...] = jnp.zeros_like(o_ref)

    @pl.when(nv_ref[i] > 0)
    def _():
        @pl.when((i == 0) | (be_ref[i] != be_ref[jnp.maximum(i - 1, 0)]))
        def _():
            wg[...] = wg_ref[0].astype(BF16)
            wu[...] = wu_ref[0].astype(BF16)
            wd[...] = wd_ref[0].astype(BF16)

        x = x_ref[...]
        gate = jnp.dot(x, wg[...], preferred_element_type=F32)
        up = jnp.dot(x, wu[...], preferred_element_type=F32)
        act = (_silu(gate) * up).astype(BF16)
        o_ref[...] = jnp.dot(act, wd[...], preferred_element_type=F32).astype(o_ref.dtype)


def _expert_ffn(blk_exp, blk_valid, x_sorted, w_gate, w_up, w_down):
    n_rows, d = x_sorted.shape
    f = w_gate.shape[2]
    rows = MOE_ROWS
    grid_spec = pltpu.PrefetchScalarGridSpec(
        num_scalar_prefetch=2,
        grid=(n_rows // rows,),
        in_specs=[pl.BlockSpec((rows, d), lambda i, be, nv: (i, 0)),
                  pl.BlockSpec((1, d, f), lambda i, be, nv: (be[i], 0, 0)),
                  pl.BlockSpec((1, d, f), lambda i, be, nv: (be[i], 0, 0)),
                  pl.BlockSpec((1, f, d), lambda i, be, nv: (be[i], 0, 0))],
        out_specs=pl.BlockSpec((rows, d), lambda i, be, nv: (i, 0)),
        scratch_shapes=[pltpu.VMEM((d, f), BF16), pltpu.VMEM((d, f), BF16), pltpu.VMEM((f, d), BF16)],
    )
    return pl.pallas_call(
        _expert_kernel,
        grid_spec=grid_spec,
        out_shape=jax.ShapeDtypeStruct((n_rows, d), BF16),
        compiler_params=_cparams(("arbitrary",)),
        name="expert_ffn",
    )(blk_exp, blk_valid, x_sorted, w_gate, w_up, w_down)


def _shared_kernel(x_ref, wg_ref, wu_ref, wd_ref, r_ref, x1_ref, gt_ref, g_ref, o_ref):
    x = x_ref[0]
    gate = jnp.dot(x, wg_ref[...], preferred_element_type=F32)
    up = jnp.dot(x, wu_ref[...], preferred_element_type=F32)
    act = (_silu(gate) * up).astype(BF16)
    y = r_ref[0] + jnp.dot(act, wd_ref[...], preferred_element_type=F32)
    o_ref[0] = x1_ref[0] + gt_ref[0] * (_rms(y) * g_ref[...])


def _shared_final(h2b, wg, wu, wd, routed, x1, gt2, g_post):
    bsz, s, d = x1.shape
    f = wg.shape[1]
    tm = 512
    nt = s // tm
    row = pl.BlockSpec((1, tm, d), lambda i: (i // nt, i % nt, 0))
    full = lambda shape: pl.BlockSpec(shape, lambda i: (0, 0))
    return pl.pallas_call(
        _shared_kernel,
        grid=(bsz * nt,),
        in_specs=[row, full((d, f)), full((d, f)), full((f, d)), row, row,
                  pl.BlockSpec((1, 1, d), lambda i: (i // nt, 0, 0)), full((1, d))],
        out_specs=row,
        out_shape=jax.ShapeDtypeStruct((bsz, s, d), F32),
        compiler_params=_cparams(("arbitrary",)),
        name="shared_ffn_final",
    )(h2b, wg, wu, wd, routed.reshape(bsz, s, d), x1, gt2.reshape(bsz, 1, d), g_post.reshape(1, d))


def _dispatch_plan(counts, n_blocks):
    n_tiles, ne = counts.shape
    c16 = (counts + SEG_ALIGN - 1) // SEG_ALIGN * SEG_ALIGN
    seg_start = jnp.cumsum(c16, axis=1) - c16
    tot = jnp.sum(c16, axis=0)
    padded = (tot + MOE_ROWS - 1) // MOE_ROWS * MOE_ROWS
    pad_end = jnp.cumsum(padded)
    pad_start = pad_end - padded
    goff = pad_start[None, :] + jnp.cumsum(c16, axis=0) - c16
    blk_row = jnp.arange(n_blocks, dtype=jnp.int32) * MOE_ROWS
    blk_exp = jnp.sum(pad_end[None, :] <= blk_row[:, None], axis=1)
    blk_exp = jnp.minimum(blk_exp, ne - 1).astype(jnp.int32)
    blk_valid = jnp.clip(pad_start[blk_exp] + tot[blk_exp] - blk_row, 0, MOE_ROWS)
    blk_valid = jnp.where(blk_row < pad_end[-1], blk_valid, 0).astype(jnp.int32)
    i32 = lambda v: v.astype(jnp.int32)
    tail_start = jnp.concatenate([pad_start + tot, pad_end[-1:]])
    return (i32(seg_start), i32(c16), i32(goff), i32(tail_start), i32(padded - tot),
            blk_exp, blk_valid)


def kernel(x, c, positions, w_ada, b_ada, g_pre_mix, g_post_mix, g_pre_ffn, g_post_ffn, w_in,
           conv_w, conv_b, dt_bias, a_log, d_skip, ssd_norm_w, w_ssd_br, w_attn_br, w_out,
           w_router, router_bias, w_gate, w_up, w_down, ws_gate, ws_up, ws_down):
    bsz, s, d = x.shape
    t = bsz * s
    for l in range(w_ada.shape[0]):
        d_inner = w_ssd_br.shape[1]
        att_w = w_attn_br.shape[1]
        n_ssd_heads = dt_bias.shape[1]
        conv_dim = conv_w.shape[2]
        n_heads = att_w // ATT_HEAD_DIM
        n_experts = w_router.shape[2]

        mod = _ada(c, w_ada[l], b_ada[l])
        sh1, sc1, gt1, sh2, sc2, gt2 = jnp.split(mod, 6, axis=-1)

        o_dt = d_inner + conv_dim
        o_q = o_dt + n_ssd_heads
        w_dt = jnp.pad(w_in[l, :, o_dt:o_q], ((0, 0), (0, LANES - n_ssd_heads))).astype(BF16)
        w_att = w_in[l, :, o_q:].astype(BF16)
        h1 = _prenorm(x, g_pre_mix[l], sc1, sh1).reshape(t, d)
        proj_s = _matmul_f32w(h1, w_in, l, o_dt, 1024, 1024, "in_proj_ssd")
        proj_a = _matmul(h1, w_att, BF16, 1024, 1024, "in_proj_att")
        dt_raw = _matmul(h1, w_dt, F32, 1024, LANES, "in_proj_dt")[:, :n_ssd_heads]
        k_off = att_w
        v_off = k_off + att_w
        gs_off = v_off + att_w
        ga_off = gs_off + d

        hg = n_ssd_heads // SSD_GROUPS
        dt_rows = dt_raw.reshape(bsz, s // SSD_CHUNK, SSD_CHUNK, SSD_GROUPS, hg).transpose(0, 3, 1, 4, 2)
        y_ssd = _ssd(proj_s.reshape(bsz, s, o_dt), dt_rows, conv_w[l], conv_b[l], dt_bias[l], a_log[l],
                     d_skip[l], ssd_norm_w[l], d_inner)

        cos2, sin2 = _rope_tables(positions)
        y_att = _attention(proj_a.reshape(bsz, s, -1), cos2, sin2, 0, k_off // ATT_HEAD_DIM,
                           v_off // ATT_HEAD_DIM, n_heads)

        merged = _merge(y_ssd.reshape(t, d_inner), y_att.reshape(t, att_w),
                        w_ssd_br[l].astype(BF16), w_attn_br[l].astype(BF16), proj_a, gs_off, ga_off)
        x1, h2b, logits_t = _outproj(merged, w_out[l].astype(BF16), x, g_post_mix[l], gt1,
                                         g_pre_ffn[l], sc2, sh2, w_router[l])

        selw, rank, cnt = _router(logits_t, router_bias[l])
        n_tiles = t // MOE_TILE
        n_blocks = -(-(t * TOP_K + n_tiles * n_experts * (SEG_ALIGN - 1)) // MOE_ROWS) + n_experts
        seg_start, c16, goff, tail_start, tail_len, blk_exp, blk_valid = _dispatch_plan(
            cnt[:, :, 0].astype(jnp.int32), n_blocks)
        x_sorted = _dispatch(seg_start, c16, goff, tail_start, tail_len, rank, h2b.reshape(t, d),
                             n_blocks * MOE_ROWS)
        y_rows = _expert_ffn(blk_exp, blk_valid, x_sorted, w_gate[l], w_up[l], w_down[l])
        routed = _combine(seg_start, c16, goff, rank, selw, y_rows)
        x = _shared_final(h2b, ws_gate[l].astype(BF16), ws_up[l].astype(BF16),
                          ws_down[l].astype(BF16), routed, x1, gt2, g_post_ffn[l])
    return x
```

```python
import functools
import math

import jax
import jax.numpy as jnp
from jax import lax
from jax.experimental import pallas as pl
from jax.experimental.pallas import tpu as pltpu

F32 = jnp.float32
BF16 = jnp.bfloat16

NORM_EPS = 1e-6
ROPE_THETA = 10000.0

SSD_HEAD_DIM = 64
SSD_GROUPS = 8
SSD_STATE = 128
SSD_CONV = 4
SSD_CHUNK = 128
SSD_GROUPS_PER_STEP = 4

ATT_HEAD_DIM = 128
MOBA_BLOCK = 256
MOBA_TOPK = 3

TOP_K = 8
N_EXPERT_GROUPS = 8
TOPK_GROUPS = 4
ROUTED_SCALE = 2.5
MOE_ROWS = 512
MOE_TILE = 512
SEG_ALIGN = 16
SEG_CHUNKS = (512, 256, 128, 64, 32, 16)

OUTPROJ_TILE = 512
OUTPROJ_SUBTILE = 256

LANES = 128
SUBLANES = 8
VMEM_LIMIT = 56 * 1024 * 1024
NEG_BIG = -1e30


def _cparams(sem):
    return pltpu.CompilerParams(dimension_semantics=sem, vmem_limit_bytes=VMEM_LIMIT)


def _silu(x):
    return x * jax.nn.sigmoid(x)


def _rms(x):
    return x * lax.rsqrt(jnp.mean(x * x, axis=-1, keepdims=True) + NORM_EPS)


def _split3(a):
    a1 = a.astype(BF16)
    r1 = a - a1.astype(F32)
    a2 = r1.astype(BF16)
    a3 = (r1 - a2.astype(F32)).astype(BF16)
    return a1, a2, a3


def _dot_exact_rhs(a, e_bf16):
    out = None
    for p in _split3(a):
        t = jnp.dot(p, e_bf16, preferred_element_type=F32)
        out = t if out is None else out + t
    return out


def _dot_nt(a, b):
    return lax.dot_general(a, b, (((1,), (1,)), ((), ())), preferred_element_type=F32)


def _dot_nt_x3(a, b):
    a1 = a.astype(BF16)
    a2 = (a - a1.astype(F32)).astype(BF16)
    b1 = b.astype(BF16)
    b2 = (b - b1.astype(F32)).astype(BF16)
    return _dot_nt(a1, b1) + (_dot_nt(a1, b2) + _dot_nt(a2, b1))


def _ada_kernel(c_ref, w_ref, b_ref, o_ref):
    cond = _silu(c_ref[...])
    o_ref[...] = jnp.dot(cond.astype(BF16), w_ref[...].astype(BF16),
                         preferred_element_type=F32) + b_ref[...]


def _ada(c, w_ada, b_ada):
    bsz, d = c.shape
    n = w_ada.shape[1]
    tn = 1024
    return pl.pallas_call(
        _ada_kernel,
        grid=(n // tn,),
        in_specs=[pl.BlockSpec((bsz, d), lambda j: (0, 0)),
                  pl.BlockSpec((d, tn), lambda j: (0, j)),
                  pl.BlockSpec((1, tn), lambda j: (0, j))],
        out_specs=pl.BlockSpec((bsz, tn), lambda j: (0, j)),
        out_shape=jax.ShapeDtypeStruct((bsz, n), F32),
        compiler_params=_cparams(("arbitrary",)),
        name="ada_mod",
    )(c, w_ada, b_ada.reshape(1, n))


def _prenorm_kernel(x_ref, g_ref, sc_ref, sh_ref, o_ref):
    y = _rms(x_ref[0]) * g_ref[...]
    o_ref[0] = (y * (1.0 + sc_ref[0]) + sh_ref[0]).astype(o_ref.dtype)


def _prenorm(x, g, sc, sh):
    bsz, s, d = x.shape
    ts = 512
    vec = pl.BlockSpec((1, 1, d), lambda b, i: (b, 0, 0))
    return pl.pallas_call(
        _prenorm_kernel,
        grid=(bsz, s // ts),
        in_specs=[pl.BlockSpec((1, ts, d), lambda b, i: (b, i, 0)),
                  pl.BlockSpec((1, d), lambda b, i: (0, 0)), vec, vec],
        out_specs=pl.BlockSpec((1, ts, d), lambda b, i: (b, i, 0)),
        out_shape=jax.ShapeDtypeStruct((bsz, s, d), BF16),
        compiler_params=_cparams(("arbitrary", "arbitrary")),
        name="prenorm",
    )(x, g.reshape(1, d), sc.reshape(bsz, 1, d), sh.reshape(bsz, 1, d))


def _mm_kernel(a_ref, w_ref, o_ref):
    o_ref[...] = jnp.dot(a_ref[...], w_ref[...], preferred_element_type=F32).astype(o_ref.dtype)


def _matmul(a, w, out_dtype, tm, tn, name):
    m, k = a.shape
    n = w.shape[1]
    return pl.pallas_call(
        _mm_kernel,
        grid=(n // tn, m // tm),
        in_specs=[pl.BlockSpec((tm, k), lambda j, i: (i, 0)),
                  pl.BlockSpec((k, tn), lambda j, i: (0, j))],
        out_specs=pl.BlockSpec((tm, tn), lambda j, i: (i, j)),
        out_shape=jax.ShapeDtypeStruct((m, n), out_dtype),
        compiler_params=_cparams(("arbitrary", "arbitrary")),
        name=name,
    )(a, w)


def _mm_f32w_kernel(a_ref, w_ref, o_ref, wb):
    @pl.when(pl.program_id(1) == 0)
    def _():
        wb[...] = w_ref[0].astype(BF16)

    o_ref[...] = jnp.dot(a_ref[...], wb[...], preferred_element_type=F32).astype(o_ref.dtype)


def _mm_f32w_shift_kernel(shift, a_ref, w_ref, wnext_ref, o_ref, wb):
    @pl.when(pl.program_id(1) == 0)
    def _():
        w = jnp.concatenate([w_ref[0][:, shift:], wnext_ref[0][:, :shift]], axis=1)
        wb[...] = w.astype(BF16)

    o_ref[...] = jnp.dot(a_ref[...], wb[...], preferred_element_type=F32).astype(o_ref.dtype)


def _matmul_f32w(a, w_stack, layer, col0, n, tm, tn, name):
    m, k = a.shape
    shift = col0 % LANES
    base = col0 - shift
    assert base % tn == 0 and n % tn == 0
    a_spec = pl.BlockSpec((tm, k), lambda j, i: (i, 0))
    w_spec = pl.BlockSpec((1, k, tn), lambda j, i: (layer, 0, base // tn + j))
    if shift == 0:
        body, in_specs, args = _mm_f32w_kernel, [a_spec, w_spec], (a, w_stack)
    else:
        next_spec = pl.BlockSpec((1, k, LANES), lambda j, i: (layer, 0, (base + (j + 1) * tn) // LANES))
        body = functools.partial(_mm_f32w_shift_kernel, shift)
        in_specs, args = [a_spec, w_spec, next_spec], (a, w_stack, w_stack)
    return pl.pallas_call(
        body,
        grid=(n // tn, m // tm),
        in_specs=in_specs,
        out_specs=pl.BlockSpec((tm, tn), lambda j, i: (i, j)),
        out_shape=jax.ShapeDtypeStruct((m, n), BF16),
        scratch_shapes=[pltpu.VMEM((k, tn), BF16)],
        compiler_params=_cparams(("arbitrary", "arbitrary")),
        name=name,
    )(*args)


def _softplus(x):
    return jnp.maximum(x, 0.0) + jnp.log(1.0 + jnp.exp(-jnp.abs(x)))


def _ssd_kernel(x_ref, b_ref, c_ref, z_ref, dt_ref, cwx_ref, cwb_ref, cwc_ref,
                cbx_ref, cbb_ref, cbc_ref, dtb_ref, alog_ref, dsk_ref, nw_ref,
                o_ref, cbuf, state):
    q = SSD_CHUNK
    gps = dt_ref.shape[1]
    gw = x_ref.shape[2] // gps
    nst = b_ref.shape[2] // gps
    hg = gw // SSD_HEAD_DIM
    pad = SUBLANES

    @pl.when(pl.program_id(2) == 0)
    def _():
        cbuf[:, 0:pad, :] = jnp.zeros((gps, pad, gw + 2 * nst), F32)
        state[...] = jnp.zeros_like(state)

    hrow = lax.broadcasted_iota(jnp.int32, (LANES, q), 0)
    ri = lax.broadcasted_iota(jnp.int32, (q, q), 0)
    ci = lax.broadcasted_iota(jnp.int32, (q, q), 1)
    triu = jnp.where(ri <= ci, 1.0, 0.0).astype(BF16)
    er = lax.broadcasted_iota(jnp.int32, (LANES, gw), 0)
    ec = lax.broadcasted_iota(jnp.int32, (LANES, gw), 1)
    e_ch = jnp.where(ec // SSD_HEAD_DIM == er, 1.0, 0.0).astype(BF16)
    er2 = lax.broadcasted_iota(jnp.int32, (LANES, hg * q), 0)
    ec2 = lax.broadcasted_iota(jnp.int32, (LANES, hg * q), 1)
    e_t = jnp.where(ec2 // q == er2, 1.0, 0.0).astype(BF16)
    ri2 = lax.broadcasted_iota(jnp.int32, (q, hg * q), 0)
    ci2 = lax.broadcasted_iota(jnp.int32, (q, hg * q), 1) % q
    chan_head = lax.broadcasted_iota(jnp.int32, (1, gw), 1) // SSD_HEAD_DIM

    for u in range(gps):
        xc = slice(u * gw, (u + 1) * gw)
        nc = slice(u * nst, (u + 1) * nst)
        xin = jnp.concatenate([x_ref[0, :, xc], b_ref[0, :, nc], c_ref[0, :, nc]],
                              axis=-1).astype(F32)
        cbuf[u, pad:pad + q, :] = xin
        w = jnp.concatenate([cwx_ref[:, xc], cwb_ref[:, nc], cwc_ref[:, nc]], axis=-1)
        bias = jnp.concatenate([cbx_ref[:, xc], cbb_ref[:, nc], cbc_ref[:, nc]], axis=-1)
        acc = bias + w[SSD_CONV - 1:SSD_CONV, :] * xin
        for s in range(1, SSD_CONV):
            acc = acc + w[SSD_CONV - 1 - s:SSD_CONV - s, :] * cbuf[u, pad - s:pad - s + q, :]
        cbuf[u, pad - (SSD_CONV - 1):pad, :] = cbuf[u, pad + q - (SSD_CONV - 1):pad + q, :]
        xbc = _silu(acc)
        xs = xbc[:, :gw]
        bm = xbc[:, gw:gw + nst].astype(BF16)
        cm = xbc[:, gw + nst:].astype(BF16)

        dt_raw = jnp.concatenate([dt_ref[0, u, 0], jnp.zeros((LANES - hg, q), F32)], axis=0)
        dt_r = jnp.where(hrow < hg, _softplus(dt_raw + dtb_ref[u]), 0.0)
        a_r = dt_r * (-math.log2(math.e) * jnp.exp(alog_ref[u]))
        acum_r = _dot_exact_rhs(a_r, triu)
        acum_c = acum_r.T
        dt_c = dt_r.T

        dt_full = _dot_exact_rhs(dt_c, e_ch)
        acum_full = _dot_exact_rhs(acum_c, e_ch)
        acum_colb = _dot_exact_rhs(acum_c, e_t)
        acum_rowb = jnp.concatenate(
            [jnp.broadcast_to(acum_r[h:h + 1, :], (q, q)) for h in range(hg)], axis=-1)

        lmat = jnp.exp2(jnp.where(ri2 >= ci2, acum_colb - acum_rowb, NEG_BIG))
        cb = _dot_nt(cm, bm)
        mcat = (jnp.concatenate([cb] * hg, axis=-1) * lmat).astype(BF16)
        xdt = xs * dt_full
        xbd = jnp.concatenate(
            [jnp.where(chan_head == h, xdt, 0.0).astype(BF16) for h in range(hg)], axis=0)
        y = jnp.dot(mcat, xbd, preferred_element_type=F32)

        st = state[u]
        y = y + jnp.dot(cm, st.astype(BF16), preferred_element_type=F32) * jnp.exp2(acum_full)
        acum_last = acum_full[q - 1:q, :]
        xw = (xdt * jnp.exp2(acum_last - acum_full)).astype(BF16)
        bt = xbc[:, gw:gw + nst].T.astype(BF16)
        state[u] = st * jnp.exp2(acum_last) + jnp.dot(bt, xw, preferred_element_type=F32)

        y = y + xs * dsk_ref[:, xc]
        y = y * _silu(z_ref[0, :, xc].astype(F32))
        o_ref[0, :, xc] = (_rms(y) * nw_ref[:, xc]).astype(o_ref.dtype)


def _ssd(proj, dt_rows, conv_w, conv_b, dt_bias, a_log, d_skip, norm_w, d_inner):
    bsz, s, _ = proj.shape
    g = SSD_GROUPS
    gps = SSD_GROUPS_PER_STEP
    gw = d_inner // g
    hg = gw // SSD_HEAD_DIM
    nst = SSD_STATE
    q = SSD_CHUNK
    nc = s // q
    bw, bn = gps * gw, gps * nst
    x0 = d_inner // bw
    b0 = (2 * d_inner) // bn
    c0 = (2 * d_inner + g * nst) // bn
    cwb0 = d_inner // bn
    cwc0 = (d_inner + g * nst) // bn
    conv_b2 = conv_b.reshape(1, -1)
    head_pad = lambda v: jnp.pad(v.reshape(g, hg), ((0, 0), (0, LANES - hg))).reshape(g, LANES, 1)
    return pl.pallas_call(
        _ssd_kernel,
        grid=(bsz, g // gps, nc),
        in_specs=[
            pl.BlockSpec((1, q, bw), lambda b, gi, c: (b, c, x0 + gi)),
            pl.BlockSpec((1, q, bn), lambda b, gi, c: (b, c, b0 + gi)),
            pl.BlockSpec((1, q, bn), lambda b, gi, c: (b, c, c0 + gi)),
            pl.BlockSpec((1, q, bw), lambda b, gi, c: (b, c, gi)),
            pl.BlockSpec((1, gps, 1, hg, q), lambda b, gi, c: (b, gi, c, 0, 0)),
            pl.BlockSpec((SSD_CONV, bw), lambda b, gi, c: (0, gi)),
            pl.BlockSpec((SSD_CONV, bn), lambda b, gi, c: (0, cwb0 + gi)),
            pl.BlockSpec((SSD_CONV, bn), lambda b, gi, c: (0, cwc0 + gi)),
            pl.BlockSpec((1, bw), lambda b, gi, c: (0, gi)),
            pl.BlockSpec((1, bn), lambda b, gi, c: (0, cwb0 + gi)),
            pl.BlockSpec((1, bn), lambda b, gi, c: (0, cwc0 + gi)),
            pl.BlockSpec((gps, LANES, 1), lambda b, gi, c: (gi, 0, 0)),
            pl.BlockSpec((gps, LANES, 1), lambda b, gi, c: (gi, 0, 0)),
            pl.BlockSpec((1, bw), lambda b, gi, c: (0, gi)),
            pl.BlockSpec((1, bw), lambda b, gi, c: (0, gi)),
        ],
        out_specs=pl.BlockSpec((1, q, bw), lambda b, gi, c: (b, c, gi)),
        out_shape=jax.ShapeDtypeStruct((bsz, s, d_inner), BF16),
        scratch_shapes=[pltpu.VMEM((gps, SUBLANES + q, gw + 2 * nst), F32),
                        pltpu.VMEM((gps, nst, gw), F32)],
        compiler_params=_cparams(("arbitrary", "arbitrary", "arbitrary")),
        name="ssd_scan",
    )(proj, proj, proj, proj, dt_rows, conv_w, conv_w, conv_w, conv_b2, conv_b2, conv_b2,
      head_pad(dt_bias), head_pad(a_log),
      jnp.repeat(d_skip, SSD_HEAD_DIM).reshape(1, d_inner), norm_w.reshape(1, d_inner))


def _rope_kernel(pos_ref, inv_ref, cos_ref, sin_ref):
    ang = pos_ref[...].astype(F32) * inv_ref[...]
    lane = lax.broadcasted_iota(jnp.int32, ang.shape, 1)
    cos_ref[...] = jnp.cos(ang)
    sin_ref[...] = jnp.where(lane < ATT_HEAD_DIM // 2, -1.0, 1.0) * jnp.sin(ang)


def _rope_tables(positions):
    t = positions.size
    half = ATT_HEAD_DIM // 2
    inv = 1.0 / (ROPE_THETA ** (jnp.arange(half, dtype=F32) / half))
    inv2 = jnp.concatenate([inv, inv]).reshape(1, ATT_HEAD_DIM)
    tt = min(2048, t)
    return pl.pallas_call(
        _rope_kernel,
        grid=(t // tt,),
        in_specs=[pl.BlockSpec((tt, 1), lambda i: (i, 0)),
                  pl.BlockSpec((1, ATT_HEAD_DIM), lambda i: (0, 0))],
        out_specs=[pl.BlockSpec((tt, ATT_HEAD_DIM), lambda i: (i, 0))] * 2,
        out_shape=[jax.ShapeDtypeStruct((t, ATT_HEAD_DIM), F32)] * 2,
        compiler_params=_cparams(("arbitrary",)),
        name="rope_tables",
    )(positions.reshape(t, 1), inv2)


def _rot(x, cos2, sin2):
    return x * cos2 + pltpu.roll(x, ATT_HEAD_DIM // 2, axis=1) * sin2


def _attn_kernel(q_ref, k_ref, v_ref, cos_ref, sin_ref, o_ref, krot, vt, kmean):
    blk = MOBA_BLOCK
    nb = k_ref.shape[1] // blk
    log2e_scale = ATT_HEAD_DIM ** -0.5 * math.log2(math.e)

    kmean[...] = jnp.zeros_like(kmean)
    for j in range(nb):
        rows = slice(j * blk, (j + 1) * blk)
        kr = _rot(k_ref[0, rows, :].astype(F32), cos_ref[0, rows, :], sin_ref[0, rows, :])
        krot[rows, :] = kr.astype(BF16)
        kmean[j:j + 1, :] = jnp.mean(kr, axis=0, keepdims=True)
        vt[:, rows] = v_ref[0, rows, :].astype(F32).T.astype(BF16)
    km = kmean[...]

    ki = lax.broadcasted_iota(jnp.int32, (blk, blk), 0)
    qj = lax.broadcasted_iota(jnp.int32, (blk, blk), 1)
    sub = lax.broadcasted_iota(jnp.int32, (km.shape[0], blk), 0)
    for qi in range(nb):
        qrows = slice(qi * blk, (qi + 1) * blk)
        qr = _rot(q_ref[0, qrows, :].astype(F32), cos_ref[0, qrows, :], sin_ref[0, qrows, :])
        qs = (qr * log2e_scale).astype(BF16)
        gate = _dot_nt_x3(km, qr)
        scores = [jnp.where(ki <= qj, _dot_nt(krot[qrows, :], qs), NEG_BIG)]
        for j in range(qi):
            gj = gate[j:j + 1, :]
            ahead = (gate > gj) | ((gate == gj) & (sub < j))
            rank = jnp.sum(jnp.where((sub < qi) & ahead, 1.0, 0.0), axis=0, keepdims=True)
            bias = jnp.where(rank < MOBA_TOPK, 0.0, NEG_BIG)
            scores.append(_dot_nt(krot[j * blk:(j + 1) * blk, :], qs) + bias)
        m = scores[0]
        for sj in scores[1:]:
            m = jnp.maximum(m, sj)
        m = jnp.max(m, axis=0, keepdims=True)
        l = None
        acc = None
        for j, sj in enumerate(scores):
            src = qi if j == 0 else j - 1
            p = jnp.exp2(sj - m)
            lj = jnp.sum(p, axis=0, keepdims=True)
            aj = jnp.dot(vt[:, src * blk:(src + 1) * blk], p.astype(BF16),
                         preferred_element_type=F32)
            l = lj if l is None else l + lj
            acc = aj if acc is None else acc + aj
        o_ref[0, qrows, :] = (acc / l).T.astype(o_ref.dtype)


def _attention(proj, cos2, sin2, q0, k0, v0, n_heads):
    bsz, s, _ = proj.shape
    dh = ATT_HEAD_DIM
    col = lambda c0: pl.BlockSpec((1, s, dh), lambda b, h: (b, 0, c0 + h))
    tab = pl.BlockSpec((1, s, dh), lambda b, h: (b, 0, 0))
    return pl.pallas_call(
        _attn_kernel,
        grid=(bsz, n_heads),
        in_specs=[col(q0), col(k0), col(v0), tab, tab],
        out_specs=pl.BlockSpec((1, s, dh), lambda b, h: (b, 0, h)),
        out_shape=jax.ShapeDtypeStruct((bsz, s, n_heads * dh), BF16),
        scratch_shapes=[pltpu.VMEM((s, dh), BF16), pltpu.VMEM((dh, s), BF16),
                        pltpu.VMEM((-(-(s // MOBA_BLOCK) // SUBLANES) * SUBLANES, dh), F32)],
        compiler_params=_cparams(("arbitrary", "arbitrary")),
        name="moba_attn",
    )(proj, proj, proj, cos2.reshape(bsz, s, dh), sin2.reshape(bsz, s, dh))


def _merge_kernel(ys_ref, ya_ref, w1_ref, w2_ref, gs_ref, ga_ref, o_ref):
    b1 = jnp.dot(ys_ref[...], w1_ref[...], preferred_element_type=F32)
    b2 = jnp.dot(ya_ref[...], w2_ref[...], preferred_element_type=F32)
    o_ref[...] = (jax.nn.sigmoid(gs_ref[...].astype(F32)) * b1
                  + jax.nn.sigmoid(ga_ref[...].astype(F32)) * b2).astype(o_ref.dtype)


def _merge(y_ssd, y_att, w1, w2, proj2d, gs0, ga0):
    m, k1 = y_ssd.shape
    k2 = y_att.shape[1]
    n = w1.shape[1]
    tm, tn = 512, 512
    return pl.pallas_call(
        _merge_kernel,
        grid=(n // tn, m // tm),
        in_specs=[pl.BlockSpec((tm, k1), lambda j, i: (i, 0)),
                  pl.BlockSpec((tm, k2), lambda j, i: (i, 0)),
                  pl.BlockSpec((k1, tn), lambda j, i: (0, j)),
                  pl.BlockSpec((k2, tn), lambda j, i: (0, j)),
                  pl.BlockSpec((tm, tn), lambda j, i: (i, gs0 // tn + j)),
                  pl.BlockSpec((tm, tn), lambda j, i: (i, ga0 // tn + j))],
        out_specs=pl.BlockSpec((tm, tn), lambda j, i: (i, j)),
        out_shape=jax.ShapeDtypeStruct((m, n), BF16),
        compiler_params=_cparams(("arbitrary", "arbitrary")),
        name="branch_merge",
    )(y_ssd, y_att, w1, w2, proj2d, proj2d)


def _outproj_kernel(m_ref, w_ref, x_ref, gpost_ref, gt_ref, gpre_ref, sc_ref, sh_ref, wr_ref,
                    x1_ref, h2b_ref, lg_ref):
    for r in range(m_ref.shape[1] // OUTPROJ_SUBTILE):
        rows = slice(r * OUTPROJ_SUBTILE, (r + 1) * OUTPROJ_SUBTILE)
        y = jnp.dot(m_ref[0, rows, :], w_ref[...], preferred_element_type=F32)
        x1 = x_ref[0, rows, :] + gt_ref[0] * (_rms(y) * gpost_ref[...])
        h2 = (_rms(x1) * gpre_ref[...]) * (1.0 + sc_ref[0]) + sh_ref[0]
        x1_ref[0, rows, :] = x1
        h2b_ref[0, rows, :] = h2.astype(BF16)
        h_hi = h2.astype(BF16)
        h_lo = (h2 - h_hi.astype(F32)).astype(BF16)
        wr = wr_ref[...]
        w_hi = wr.astype(BF16)
        w_lo = (wr - w_hi.astype(F32)).astype(BF16)
        lg = (jnp.dot(h_hi, w_hi, preferred_element_type=F32)
              + (jnp.dot(h_hi, w_lo, preferred_element_type=F32)
                 + jnp.dot(h_lo, w_hi, preferred_element_type=F32)))
        lg_ref[:, rows] = lg.T[:lg_ref.shape[0], :]


def _outproj(merged, w_out, x, g_post, gt1, g_pre, sc2, sh2, w_router):
    bsz, s, d = x.shape
    ne = w_router.shape[1]
    w_router_p = jnp.pad(w_router, ((0, 0), (0, LANES - ne)))
    tm = OUTPROJ_TILE
    nt = s // tm
    row = pl.BlockSpec((1, tm, d), lambda i: (i // nt, i % nt, 0))
    vec = pl.BlockSpec((1, 1, d), lambda i: (i // nt, 0, 0))
    par = pl.BlockSpec((1, d), lambda i: (0, 0))
    return pl.pallas_call(
        _outproj_kernel,
        grid=(bsz * nt,),
        in_specs=[row, pl.BlockSpec((d, d), lambda i: (0, 0)), row, par, vec, par, vec, vec,
                  pl.BlockSpec((d, LANES), lambda i: (0, 0))],
        out_specs=[row, row, pl.BlockSpec((ne, tm), lambda i: (0, i))],
        out_shape=[jax.ShapeDtypeStruct((bsz, s, d), F32),
                   jax.ShapeDtypeStruct((bsz, s, d), BF16),
                   jax.ShapeDtypeStruct((ne, bsz * s), F32)],
        compiler_params=_cparams(("arbitrary",)),
        name="out_proj_norms",
    )(merged.reshape(bsz, s, d), w_out, x, g_post.reshape(1, d), gt1.reshape(bsz, 1, d),
      g_pre.reshape(1, d), sc2.reshape(bsz, 1, d), sh2.reshape(bsz, 1, d), w_router_p)


def _router_kernel(lg_ref, rb_ref, w_ref, rank_ref, cnt_ref, gsc):
    ne, tn = lg_ref.shape
    ng = N_EXPERT_GROUPS
    eg = ne // ng
    scores = jax.nn.sigmoid(lg_ref[...])
    biased = scores + rb_ref[...]
    i8 = lax.broadcasted_iota(jnp.int32, (eg, tn), 0)
    for g in range(ng):
        v = biased[g * eg:(g + 1) * eg, :]
        m1 = jnp.max(v, axis=0, keepdims=True)
        first = jnp.min(jnp.where(v == m1, i8, eg), axis=0, keepdims=True)
        m2 = jnp.max(jnp.where(i8 == first, -jnp.inf, v), axis=0, keepdims=True)
        gsc[g:g + 1, :] = m1 + m2
    gs = gsc[...]
    gi = lax.broadcasted_iota(jnp.int32, (ng, tn), 0)
    masked = []
    for g in range(ng):
        sg = gs[g:g + 1, :]
        ahead = (gs > sg) | ((gs == sg) & (gi < g))
        rank = jnp.sum(jnp.where(ahead, 1.0, 0.0), axis=0, keepdims=True)
        masked.append(jnp.where(rank < TOPK_GROUPS, biased[g * eg:(g + 1) * eg, :], -jnp.inf))
    cur = jnp.concatenate(masked, axis=0)
    sub = lax.broadcasted_iota(jnp.int32, (ne, tn), 0)
    sel = jnp.zeros((ne, tn), F32)
    for k in range(TOP_K):
        mx = jnp.max(cur, axis=0, keepdims=True)
        idx = jnp.min(jnp.where(cur == mx, sub, ne), axis=0, keepdims=True)
        hit = sub == idx
        sel = jnp.where(hit, 1.0, sel)
        cur = jnp.where(hit, -jnp.inf, cur)
    picked = sel * scores
    wsum = jnp.sum(picked, axis=0, keepdims=True)
    w_ref[...] = picked / wsum * ROUTED_SCALE
    ti = lax.broadcasted_iota(jnp.int32, (tn, tn), 0)
    tj = lax.broadcasted_iota(jnp.int32, (tn, tn), 1)
    before = jnp.where(ti < tj, 1.0, 0.0).astype(BF16)
    rank = jnp.dot(sel.astype(BF16), before, preferred_element_type=F32)
    rank_ref[...] = jnp.where(sel > 0.0, rank, NEG_BIG)
    cnt_ref[0] = jnp.broadcast_to(jnp.sum(sel, axis=1, keepdims=True), (ne, LANES))


def _router(logits_t, router_bias):
    ne, t = logits_t.shape
    tn = MOE_TILE
    tile = pl.BlockSpec((ne, tn), lambda i: (0, i))
    return pl.pallas_call(
        _router_kernel,
        grid=(t // tn,),
        in_specs=[tile, pl.BlockSpec((ne, 1), lambda i: (0, 0))],
        out_specs=[tile, tile, pl.BlockSpec((1, ne, LANES), lambda i: (i, 0, 0))],
        out_shape=[jax.ShapeDtypeStruct((ne, t), F32), jax.ShapeDtypeStruct((ne, t), F32),
                   jax.ShapeDtypeStruct((t // tn, ne, LANES), F32)],
        scratch_shapes=[pltpu.VMEM((N_EXPERT_GROUPS, tn), F32)],
        compiler_params=_cparams(("arbitrary",)),
        name="router_topk",
    )(logits_t, router_bias.reshape(ne, 1))


def _tile_rows(n_experts):
    return MOE_TILE * TOP_K + n_experts * SEG_ALIGN


def _segment_copies(src, dst, sem, n, src_off, dst_off, wait):
    off = jnp.int32(0)
    for p in SEG_CHUNKS:
        bit = n & p

        @pl.when(bit != 0)
        def _(p=p, off=off):
            cp = pltpu.make_async_copy(
                src.at[pl.ds(pl.multiple_of(src_off + off, SEG_ALIGN), p), :],
                dst.at[pl.ds(pl.multiple_of(dst_off + off, SEG_ALIGN), p), :], sem)
            if wait:
                cp.wait()
            else:
                cp.start()
        off = off + bit


def _for_each_segment(ne, fn):
    def body(e, carry):
        fn(e)
        return carry
    lax.fori_loop(0, ne, body, 0)


def _build_perm(pbuf, posb, ss_ref, c16_ref, tile, ne, value_row):
    tn = posb.shape[1]
    pbuf[...] = jnp.zeros_like(pbuf)
    rows16 = lax.broadcasted_iota(jnp.int32, (SEG_ALIGN, tn), 0)

    def per_expert(e):
        base = ss_ref[tile, e]
        groups = lax.shift_right_logical(c16_ref[tile, e], SEG_ALIGN.bit_length() - 1)
        prow = posb[pl.ds(e, 1), :]
        vrow = value_row(e)

        def per_group(g, carry):
            r0 = pl.multiple_of(base + g * SEG_ALIGN, SEG_ALIGN)
            hit = prow == (rows16 + r0).astype(F32)
            pbuf[pl.ds(r0, SEG_ALIGN), :] = jnp.where(hit, vrow, 0.0).astype(BF16)
            return carry
        lax.fori_loop(0, groups, per_group, 0)
    _for_each_segment(ne, per_expert)


def _dispatch_kernel(ss_ref, c16_ref, go_ref, ts_ref, tl_ref, rank_ref, ssv_ref, h_ref, xs_hbm,
                     pbuf, xs, posb, sem):
    i = pl.program_id(0)
    n = pl.num_programs(0)
    ne = rank_ref.shape[0]
    tn = rank_ref.shape[1]
    posb[...] = rank_ref[...] + ssv_ref[0]
    _build_perm(pbuf, posb, ss_ref, c16_ref, i, ne, lambda e: 1.0)

    def drain(tile):
        _for_each_segment(ne, lambda e: _segment_copies(
            xs, xs_hbm, sem, c16_ref[tile, e], ss_ref[tile, e], go_ref[tile, e], True))

    @pl.when(i > 0)
    def _():
        drain(i - 1)

    used = ss_ref[i, ne - 1] + c16_ref[i, ne - 1]
    def chunk(ch):
        rows = slice(ch * tn, (ch + 1) * tn)
        xs[rows, :] = jnp.dot(pbuf[rows, :], h_ref[...], preferred_element_type=F32).astype(BF16)

    for ch in range(TOP_K):
        chunk(ch)
    for ch in range(TOP_K, xs.shape[0] // tn):
        pl.when(ch * tn < used)(functools.partial(chunk, ch))

    _for_each_segment(ne, lambda e: _segment_copies(
        xs, xs_hbm, sem, c16_ref[i, e], ss_ref[i, e], go_ref[i, e], False))

    @pl.when(i == n - 1)
    def _():
        drain(i)
        xs[0:MOE_ROWS, :] = jnp.zeros((MOE_ROWS, xs.shape[1]), BF16)
        first_free = lax.shift_right_logical(ts_ref[ne], MOE_ROWS.bit_length() - 1)
        n_blocks = xs_hbm.shape[0] // MOE_ROWS
        for wait in (False, True):
            _for_each_segment(ne, lambda e: _segment_copies(
                xs, xs_hbm, sem, tl_ref[e], 0, ts_ref[e], wait))

            def free_block(b, carry, wait=wait):
                cp = pltpu.make_async_copy(
                    xs.at[pl.ds(0, MOE_ROWS), :],
                    xs_hbm.at[pl.ds(pl.multiple_of(b * MOE_ROWS, MOE_ROWS), MOE_ROWS), :], sem)
                if wait:
                    cp.wait()
                else:
                    cp.start()
                return carry
            lax.fori_loop(first_free, n_blocks, free_block, 0)


def _dispatch(seg_start, c16, goff, tail_start, tail_len, rank, h2b, n_rows):
    ne, t = rank.shape
    d = h2b.shape[1]
    tn = MOE_TILE
    rt = _tile_rows(ne)
    grid_spec = pltpu.PrefetchScalarGridSpec(
        num_scalar_prefetch=5,
        grid=(t // tn,),
        in_specs=[pl.BlockSpec((ne, tn), lambda i, *_: (0, i)),
                  pl.BlockSpec((1, ne, 1), lambda i, *_: (i, 0, 0)),
                  pl.BlockSpec((tn, d), lambda i, *_: (i, 0))],
        out_specs=pl.BlockSpec(memory_space=pl.ANY),
        scratch_shapes=[pltpu.VMEM((rt, tn), BF16), pltpu.VMEM((rt, d), BF16),
                        pltpu.VMEM((ne, tn), F32), pltpu.SemaphoreType.DMA],
    )
    return pl.pallas_call(
        _dispatch_kernel,
        grid_spec=grid_spec,
        out_shape=jax.ShapeDtypeStruct((n_rows, d), BF16),
        compiler_params=_cparams(("arbitrary",)),
        name="moe_dispatch",
    )(seg_start, c16, goff, tail_start, tail_len, rank,
      seg_start.astype(F32).reshape(t // tn, ne, 1), h2b)


def _combine_kernel(ss_ref, c16_ref, go_ref, rank_ref, w_ref, ssv_ref, y_hbm, o_ref,
                    pbuf, ys, posb, sem):
    i = pl.program_id(0)
    ne = rank_ref.shape[0]
    tn = rank_ref.shape[1]
    ys[...] = jnp.zeros_like(ys)
    _for_each_segment(ne, lambda e: _segment_copies(
        y_hbm, ys, sem, c16_ref[i, e], go_ref[i, e], ss_ref[i, e], False))
    posb[...] = rank_ref[...] + ssv_ref[0]
    _build_perm(pbuf, posb, ss_ref, c16_ref, i, ne, lambda e: w_ref[pl.ds(e, 1), :])
    _for_each_segment(ne, lambda e: _segment_copies(
        y_hbm, ys, sem, c16_ref[i, e], go_ref[i, e], ss_ref[i, e], True))
    used = ss_ref[i, ne - 1] + c16_ref[i, ne - 1]
    def chunk(ch):
        rows = slice(ch * tn, (ch + 1) * tn)
        return lax.dot_general(pbuf[rows, :], ys[rows, :], (((0,), (0,)), ((), ())),
                               preferred_element_type=F32)

    always = TOP_K
    acc = chunk(0)
    for ch in range(1, always):
        acc = acc + chunk(ch)
    o_ref[...] = acc
    for ch in range(always, ys.shape[0] // tn):
        @pl.when(ch * tn < used)
        def _(ch=ch):
            o_ref[...] += chunk(ch)


def _combine(seg_start, c16, goff, rank, selw, y_rows):
    ne, t = rank.shape
    d = y_rows.shape[1]
    tn = MOE_TILE
    rt = _tile_rows(ne)
    tile = pl.BlockSpec((ne, tn), lambda i, *_: (0, i))
    grid_spec = pltpu.PrefetchScalarGridSpec(
        num_scalar_prefetch=3,
        grid=(t // tn,),
        in_specs=[tile, tile, pl.BlockSpec((1, ne, 1), lambda i, *_: (i, 0, 0)),
                  pl.BlockSpec(memory_space=pl.ANY)],
        out_specs=pl.BlockSpec((tn, d), lambda i, *_: (i, 0)),
        scratch_shapes=[pltpu.VMEM((rt, tn), BF16), pltpu.VMEM((rt, d), BF16),
                        pltpu.VMEM((ne, tn), F32), pltpu.SemaphoreType.DMA],
    )
    return pl.pallas_call(
        _combine_kernel,
        grid_spec=grid_spec,
        out_shape=jax.ShapeDtypeStruct((t, d), F32),
        compiler_params=_cparams(("arbitrary",)),
        name="moe_combine",
    )(seg_start, c16, goff, rank, selw, seg_start.astype(F32).reshape(t // tn, ne, 1), y_rows)


def _expert_kernel(be_ref, nv_ref, x_ref, wg_ref, wu_ref, wd_ref, o_ref, wg, wu, wd):
    i = pl.program_id(0)

    @pl.when(nv_ref[i] == 0)
    def _():
        o_ref[...] = jnp.zeros_like(o_ref)

    @pl.when(nv_ref[i] > 0)
    def _():
        @pl.when((i == 0) | (be_ref[i] != be_ref[jnp.maximum(i - 1, 0)]))
        def _():
            wg[...] = wg_ref[0].astype(BF16)
            wu[...] = wu_ref[0].astype(BF16)
            wd[...] = wd_ref[0].astype(BF16)

        x = x_ref[...]
        gate = jnp.dot(x, wg[...], preferred_element_type=F32)
        up = jnp.dot(x, wu[...], preferred_element_type=F32)
        act = (_silu(gate) * up).astype(BF16)
        o_ref[...] = jnp.dot(act, wd[...], preferred_element_type=F32).astype(o_ref.dtype)


def _expert_ffn(blk_exp, blk_valid, x_sorted, w_gate, w_up, w_down):
    n_rows, d = x_sorted.shape
    f = w_gate.shape[2]
    rows = MOE_ROWS
    grid_spec = pltpu.PrefetchScalarGridSpec(
        num_scalar_prefetch=2,
        grid=(n_rows // rows,),
        in_specs=[pl.BlockSpec((rows, d), lambda i, be, nv: (i, 0)),
                  pl.BlockSpec((1, d, f), lambda i, be, nv: (be[i], 0, 0)),
                  pl.BlockSpec((1, d, f), lambda i, be, nv: (be[i], 0, 0)),
                  pl.BlockSpec((1, f, d), lambda i, be, nv: (be[i], 0, 0))],
        out_specs=pl.BlockSpec((rows, d), lambda i, be, nv: (i, 0)),
        scratch_shapes=[pltpu.VMEM((d, f), BF16), pltpu.VMEM((d, f), BF16), pltpu.VMEM((f, d), BF16)],
    )
    return pl.pallas_call(
        _expert_kernel,
        grid_spec=grid_spec,
        out_shape=jax.ShapeDtypeStruct((n_rows, d), BF16),
        compiler_params=_cparams(("arbitrary",)),
        name="expert_ffn",
    )(blk_exp, blk_valid, x_sorted, w_gate, w_up, w_down)


def _shared_kernel(x_ref, wg_ref, wu_ref, wd_ref, r_ref, x1_ref, gt_ref, g_ref, o_ref):
    x = x_ref[0]
    gate = jnp.dot(x, wg_ref[...], preferred_element_type=F32)
    up = jnp.dot(x, wu_ref[...], preferred_element_type=F32)
    act = (_silu(gate) * up).astype(BF16)
    y = r_ref[0] + jnp.dot(act, wd_ref[...], preferred_element_type=F32)
    o_ref[0] = x1_ref[0] + gt_ref[0] * (_rms(y) * g_ref[...])


def _shared_final(h2b, wg, wu, wd, routed, x1, gt2, g_post):
    bsz, s, d = x1.shape
    f = wg.shape[1]
    tm = 512
    nt = s // tm
    row = pl.BlockSpec((1, tm, d), lambda i: (i // nt, i % nt, 0))
    full = lambda shape: pl.BlockSpec(shape, lambda i: (0, 0))
    return pl.pallas_call(
        _shared_kernel,
        grid=(bsz * nt,),
        in_specs=[row, full((d, f)), full((d, f)), full((f, d)), row, row,
                  pl.BlockSpec((1, 1, d), lambda i: (i // nt, 0, 0)), full((1, d))],
        out_specs=row,
        out_shape=jax.ShapeDtypeStruct((bsz, s, d), F32),
        compiler_params=_cparams(("arbitrary",)),
        name="shared_ffn_final",
    )(h2b, wg, wu, wd, routed.reshape(bsz, s, d), x1, gt2.reshape(bsz, 1, d), g_post.reshape(1, d))


def _dispatch_plan(counts, n_blocks):
    n_tiles, ne = counts.shape
    c16 = (counts + SEG_ALIGN - 1) // SEG_ALIGN * SEG_ALIGN
    seg_start = jnp.cumsum(c16, axis=1) - c16
    tot = jnp.sum(c16, axis=0)
    padded = (tot + MOE_ROWS - 1) // MOE_ROWS * MOE_ROWS
    pad_end = jnp.cumsum(padded)
    pad_start = pad_end - padded
    goff = pad_start[None, :] + jnp.cumsum(c16, axis=0) - c16
    blk_row = jnp.arange(n_blocks, dtype=jnp.int32) * MOE_ROWS
    blk_exp = jnp.sum(pad_end[None, :] <= blk_row[:, None], axis=1)
    blk_exp = jnp.minimum(blk_exp, ne - 1).astype(jnp.int32)
    blk_valid = jnp.clip(pad_start[blk_exp] + tot[blk_exp] - blk_row, 0, MOE_ROWS)
    blk_valid = jnp.where(blk_row < pad_end[-1], blk_valid, 0).astype(jnp.int32)
    i32 = lambda v: v.astype(jnp.int32)
    tail_start = jnp.concatenate([pad_start + tot, pad_end[-1:]])
    return (i32(seg_start), i32(c16), i32(goff), i32(tail_start), i32(padded - tot),
            blk_exp, blk_valid)


def kernel(x, c, positions, w_ada, b_ada, g_pre_mix, g_post_mix, g_pre_ffn, g_post_ffn, w_in,
           conv_w, conv_b, dt_bias, a_log, d_skip, ssd_norm_w, w_ssd_br, w_attn_br, w_out,
           w_router, router_bias, w_gate, w_up, w_down, ws_gate, ws_up, ws_down):
    bsz, s, d = x.shape
    t = bsz * s
    for l in range(w_ada.shape[0]):
        d_inner = w_ssd_br.shape[1]
        att_w = w_attn_br.shape[1]
        n_ssd_heads = dt_bias.shape[1]
        conv_dim = conv_w.shape[2]
        n_heads = att_w // ATT_HEAD_DIM
        n_experts = w_router.shape[2]

        mod = _ada(c, w_ada[l], b_ada[l])
        sh1, sc1, gt1, sh2, sc2, gt2 = jnp.split(mod, 6, axis=-1)

        o_dt = d_inner + conv_dim
        o_q = o_dt + n_ssd_heads
        w_dt = jnp.pad(w_in[l, :, o_dt:o_q], ((0, 0), (0, LANES - n_ssd_heads))).astype(BF16)
        h1 = _prenorm(x, g_pre_mix[l], sc1, sh1).reshape(t, d)
        proj_s = _matmul_f32w(h1, w_in, l, 0, o_dt, 1024, 1024, "in_proj_ssd")
        proj_a = _matmul_f32w(h1, w_in, l, o_q, w_in.shape[2] - o_q, 1024, 1024, "in_proj_att")
        dt_raw = _matmul(h1, w_dt, F32, 1024, LANES, "in_proj_dt")[:, :n_ssd_heads]
        k_off = att_w
        v_off = k_off + att_w
        gs_off = v_off + att_w
        ga_off = gs_off + d

        hg = n_ssd_heads // SSD_GROUPS
        dt_rows = dt_raw.reshape(bsz, s // SSD_CHUNK, SSD_CHUNK, SSD_GROUPS, hg).transpose(0, 3, 1, 4, 2)
        y_ssd = _ssd(proj_s.reshape(bsz, s, o_dt), dt_rows, conv_w[l], conv_b[l], dt_bias[l], a_log[l],
                     d_skip[l], ssd_norm_w[l], d_inner)

        cos2, sin2 = _rope_tables(positions)
        y_att = _attention(proj_a.reshape(bsz, s, -1), cos2, sin2, 0, k_off // ATT_HEAD_DIM,
                           v_off // ATT_HEAD_DIM, n_heads)

        merged = _merge(y_ssd.reshape(t, d_inner), y_att.reshape(t, att_w),
                        w_ssd_br[l].astype(BF16), w_attn_br[l].astype(BF16), proj_a, gs_off, ga_off)
        x1, h2b, logits_t = _outproj(merged, w_out[l].astype(BF16), x, g_post_mix[l], gt1,
                                         g_pre_ffn[l], sc2, sh2, w_router[l])

        selw, rank, cnt = _router(logits_t, router_bias[l])
        n_tiles = t // MOE_TILE
        n_blocks = -(-(t * TOP_K + n_tiles * n_experts * (SEG_ALIGN - 1)) // MOE_ROWS) + n_experts
        seg_start, c16, goff, tail_start, tail_len, blk_exp, blk_valid = _dispatch_plan(
            cnt[:, :, 0].astype(jnp.int32), n_blocks)
        x_sorted = _dispatch(seg_start, c16, goff, tail_start, tail_len, rank, h2b.reshape(t, d),
                             n_blocks * MOE_ROWS)
        y_rows = _expert_ffn(blk_exp, blk_valid, x_sorted, w_gate[l], w_up[l], w_down[l])
        routed = _combine(seg_start, c16, goff, rank, selw, y_rows)
        x = _shared_final(h2b, ws_gate[l].astype(BF16), ws_up[l].astype(BF16),
                          ws_down[l].astype(BF16), routed, x1, gt2, g_post_ffn[l])
    return x
```

```python
import functools
import math

import jax
import jax.numpy as jnp
from jax import lax
from jax.experimental import pallas as pl
from jax.experimental.pallas import tpu as pltpu

F32 = jnp.float32
BF16 = jnp.bfloat16

NORM_EPS = 1e-6
ROPE_THETA = 10000.0

SSD_HEAD_DIM = 64
SSD_GROUPS = 8
SSD_STATE = 128
SSD_CONV = 4
SSD_CHUNK = 128
SSD_GROUPS_PER_STEP = 4

ATT_HEAD_DIM = 128
MOBA_BLOCK = 256
MOBA_TOPK = 3

TOP_K = 8
N_EXPERT_GROUPS = 8
TOPK_GROUPS = 4
ROUTED_SCALE = 2.5
MOE_ROWS = 512
MOE_TILE = 512
SEG_ALIGN = 16
SEG_CHUNKS = (512, 256, 128, 64, 32, 16)

OUTPROJ_TILE = 512
OUTPROJ_SUBTILE = 256

LANES = 128
SUBLANES = 8
VMEM_LIMIT = 56 * 1024 * 1024
NEG_BIG = -1e30


def _cparams(sem):
    return pltpu.CompilerParams(dimension_semantics=sem, vmem_limit_bytes=VMEM_LIMIT)


def _silu(x):
    return x * jax.nn.sigmoid(x)


def _rms(x):
    return x * lax.rsqrt(jnp.mean(x * x, axis=-1, keepdims=True) + NORM_EPS)


def _split3(a):
    a1 = a.astype(BF16)
    r1 = a - a1.astype(F32)
    a2 = r1.astype(BF16)
    a3 = (r1 - a2.astype(F32)).astype(BF16)
    return a1, a2, a3


def _dot_exact_rhs(a, e_bf16):
    out = None
    for p in _split3(a):
        t = jnp.dot(p, e_bf16, preferred_element_type=F32)
        out = t if out is None else out + t
    return out


def _dot_nt(a, b):
    return lax.dot_general(a, b, (((1,), (1,)), ((), ())), preferred_element_type=F32)


def _dot_nt_x3(a, b):
    a1 = a.astype(BF16)
    a2 = (a - a1.astype(F32)).astype(BF16)
    b1 = b.astype(BF16)
    b2 = (b - b1.astype(F32)).astype(BF16)
    return _dot_nt(a1, b1) + (_dot_nt(a1, b2) + _dot_nt(a2, b1))


def _ada_kernel(c_ref, w_ref, b_ref, o_ref):
    cond = _silu(c_ref[...])
    o_ref[...] = jnp.dot(cond.astype(BF16), w_ref[...].astype(BF16),
                         preferred_element_type=F32) + b_ref[...]


def _ada(c, w_ada, b_ada):
    bsz, d = c.shape
    n = w_ada.shape[1]
    tn = 1024
    return pl.pallas_call(
        _ada_kernel,
        grid=(n // tn,),
        in_specs=[pl.BlockSpec((bsz, d), lambda j: (0, 0)),
                  pl.BlockSpec((d, tn), lambda j: (0, j)),
                  pl.BlockSpec((1, tn), lambda j: (0, j))],
        out_specs=pl.BlockSpec((bsz, tn), lambda j: (0, j)),
        out_shape=jax.ShapeDtypeStruct((bsz, n), F32),
        compiler_params=_cparams(("arbitrary",)),
        name="ada_mod",
    )(c, w_ada, b_ada.reshape(1, n))


def _prenorm_kernel(x_ref, g_ref, sc_ref, sh_ref, o_ref):
    y = _rms(x_ref[0]) * g_ref[...]
    o_ref[0] = (y * (1.0 + sc_ref[0]) + sh_ref[0]).astype(o_ref.dtype)


def _prenorm(x, g, sc, sh):
    bsz, s, d = x.shape
    ts = 512
    vec = pl.BlockSpec((1, 1, d), lambda b, i: (b, 0, 0))
    return pl.pallas_call(
        _prenorm_kernel,
        grid=(bsz, s // ts),
        in_specs=[pl.BlockSpec((1, ts, d), lambda b, i: (b, i, 0)),
                  pl.BlockSpec((1, d), lambda b, i: (0, 0)), vec, vec],
        out_specs=pl.BlockSpec((1, ts, d), lambda b, i: (b, i, 0)),
        out_shape=jax.ShapeDtypeStruct((bsz, s, d), BF16),
        compiler_params=_cparams(("arbitrary", "arbitrary")),
        name="prenorm",
    )(x, g.reshape(1, d), sc.reshape(bsz, 1, d), sh.reshape(bsz, 1, d))


def _mm_kernel(a_ref, w_ref, o_ref):
    o_ref[...] = jnp.dot(a_ref[...], w_ref[...], preferred_element_type=F32).astype(o_ref.dtype)


def _matmul(a, w, out_dtype, tm, tn, name):
    m, k = a.shape
    n = w.shape[1]
    return pl.pallas_call(
        _mm_kernel,
        grid=(n // tn, m // tm),
        in_specs=[pl.BlockSpec((tm, k), lambda j, i: (i, 0)),
                  pl.BlockSpec((k, tn), lambda j, i: (0, j))],
        out_specs=pl.BlockSpec((tm, tn), lambda j, i: (i, j)),
        out_shape=jax.ShapeDtypeStruct((m, n), out_dtype),
        compiler_params=_cparams(("arbitrary", "arbitrary")),
        name=name,
    )(a, w)


def _mm_wt_kernel(shift, narrow, a_ref, *refs):
    w_ref = refs[0]
    o_ref, wb = refs[-2:]

    @pl.when(pl.program_id(1) == 0)
    def _():
        w = w_ref[0]
        if shift:
            w = jnp.concatenate([w[shift:], refs[1][0]], axis=0)
        wb[...] = (w if narrow else w.T).astype(BF16)

    if narrow:
        acc = _dot_nt(a_ref[...], wb[...])
    else:
        acc = jnp.dot(a_ref[...], wb[...], preferred_element_type=F32)
    o_ref[...] = acc.astype(o_ref.dtype)


def _matmul_wt(a, wt_stack, layer, row0, n, tm, tn, out_dtype, name):
    m, k = a.shape
    shift = row0 % tn
    base = row0 - shift
    assert n % tn == 0 and shift % SUBLANES == 0
    a_spec = pl.BlockSpec((tm, k), lambda j, i: (i, 0))
    in_specs = [a_spec, pl.BlockSpec((1, tn, k), lambda j, i: (layer, base // tn + j, 0))]
    args = [a, wt_stack]
    if shift:
        assert (base + tn) % shift == 0 and tn % shift == 0
        in_specs.append(pl.BlockSpec((1, shift, k), lambda j, i: (layer, (base + (j + 1) * tn) // shift, 0)))
        args.append(wt_stack)
    narrow = tn < LANES
    wb_shape = (tn, k) if narrow else (k, tn)
    return pl.pallas_call(
        functools.partial(_mm_wt_kernel, shift, narrow),
        grid=(n // tn, m // tm),
        in_specs=in_specs,
        out_specs=pl.BlockSpec((tm, tn), lambda j, i: (i, j)),
        out_shape=jax.ShapeDtypeStruct((m, n), out_dtype),
        scratch_shapes=[pltpu.VMEM(wb_shape, BF16)],
        compiler_params=_cparams(("arbitrary", "arbitrary")),
        name=name,
    )(*args)


def _softplus(x):
    return jnp.maximum(x, 0.0) + jnp.log(1.0 + jnp.exp(-jnp.abs(x)))


def _ssd_kernel(x_ref, b_ref, c_ref, z_ref, dt_ref, cwx_ref, cwb_ref, cwc_ref,
                cbx_ref, cbb_ref, cbc_ref, dtb_ref, alog_ref, dsk_ref, nw_ref,
                o_ref, cbuf, state):
    q = SSD_CHUNK
    gps = dt_ref.shape[1]
    gw = x_ref.shape[2] // gps
    nst = b_ref.shape[2] // gps
    hg = gw // SSD_HEAD_DIM
    pad = SUBLANES

    @pl.when(pl.program_id(2) == 0)
    def _():
        cbuf[:, 0:pad, :] = jnp.zeros((gps, pad, gw + 2 * nst), F32)
        state[...] = jnp.zeros_like(state)

    hrow = lax.broadcasted_iota(jnp.int32, (LANES, q), 0)
    ri = lax.broadcasted_iota(jnp.int32, (q, q), 0)
    ci = lax.broadcasted_iota(jnp.int32, (q, q), 1)
    triu = jnp.where(ri <= ci, 1.0, 0.0).astype(BF16)
    er = lax.broadcasted_iota(jnp.int32, (LANES, gw), 0)
    ec = lax.broadcasted_iota(jnp.int32, (LANES, gw), 1)
    e_ch = jnp.where(ec // SSD_HEAD_DIM == er, 1.0, 0.0).astype(BF16)
    er2 = lax.broadcasted_iota(jnp.int32, (LANES, hg * q), 0)
    ec2 = lax.broadcasted_iota(jnp.int32, (LANES, hg * q), 1)
    e_t = jnp.where(ec2 // q == er2, 1.0, 0.0).astype(BF16)
    ri2 = lax.broadcasted_iota(jnp.int32, (q, hg * q), 0)
    ci2 = lax.broadcasted_iota(jnp.int32, (q, hg * q), 1) % q
    chan_head = lax.broadcasted_iota(jnp.int32, (1, gw), 1) // SSD_HEAD_DIM

    for u in range(gps):
        xc = slice(u * gw, (u + 1) * gw)
        nc = slice(u * nst, (u + 1) * nst)
        xin = jnp.concatenate([x_ref[0, :, xc], b_ref[0, :, nc], c_ref[0, :, nc]],
                              axis=-1).astype(F32)
        cbuf[u, pad:pad + q, :] = xin
        w = jnp.concatenate([cwx_ref[:, xc], cwb_ref[:, nc], cwc_ref[:, nc]], axis=-1)
        bias = jnp.concatenate([cbx_ref[:, xc], cbb_ref[:, nc], cbc_ref[:, nc]], axis=-1)
        acc = bias + w[SSD_CONV - 1:SSD_CONV, :] * xin
        for s in range(1, SSD_CONV):
            acc = acc + w[SSD_CONV - 1 - s:SSD_CONV - s, :] * cbuf[u, pad - s:pad - s + q, :]
        cbuf[u, pad - (SSD_CONV - 1):pad, :] = cbuf[u, pad + q - (SSD_CONV - 1):pad + q, :]
        xbc = _silu(acc)
        xs = xbc[:, :gw]
        bm = xbc[:, gw:gw + nst].astype(BF16)
        cm = xbc[:, gw + nst:].astype(BF16)

        dt_raw = jnp.concatenate([dt_ref[0, u, 0], jnp.zeros((LANES - hg, q), F32)], axis=0)
        dt_r = jnp.where(hrow < hg, _softplus(dt_raw + dtb_ref[u]), 0.0)
        a_r = dt_r * (-math.log2(math.e) * jnp.exp(alog_ref[u]))
        acum_r = _dot_exact_rhs(a_r, triu)
        acum_c = acum_r.T
        dt_c = dt_r.T

        dt_full = _dot_exact_rhs(dt_c, e_ch)
        acum_full = _dot_exact_rhs(acum_c, e_ch)
        acum_colb = _dot_exact_rhs(acum_c, e_t)
        acum_rowb = jnp.concatenate(
            [jnp.broadcast_to(acum_r[h:h + 1, :], (q, q)) for h in range(hg)], axis=-1)

        lmat = jnp.exp2(jnp.where(ri2 >= ci2, acum_colb - acum_rowb, NEG_BIG))
        cb = _dot_nt(cm, bm)
        mcat = (jnp.concatenate([cb] * hg, axis=-1) * lmat).astype(BF16)
        xdt = xs * dt_full
        xbd = jnp.concatenate(
            [jnp.where(chan_head == h, xdt, 0.0).astype(BF16) for h in range(hg)], axis=0)
        y = jnp.dot(mcat, xbd, preferred_element_type=F32)

        st = state[u]
        y = y + jnp.dot(cm, st.astype(BF16), preferred_element_type=F32) * jnp.exp2(acum_full)
        acum_last = acum_full[q - 1:q, :]
        xw = (xdt * jnp.exp2(acum_last - acum_full)).astype(BF16)
        bt = xbc[:, gw:gw + nst].T.astype(BF16)
        state[u] = st * jnp.exp2(acum_last) + jnp.dot(bt, xw, preferred_element_type=F32)

        y = y + xs * dsk_ref[:, xc]
        y = y * _silu(z_ref[0, :, xc].astype(F32))
        o_ref[0, :, xc] = (_rms(y) * nw_ref[:, xc]).astype(o_ref.dtype)


def _ssd(proj, dt_rows, conv_w, conv_b, dt_bias, a_log, d_skip, norm_w, d_inner):
    bsz, s, _ = proj.shape
    g = SSD_GROUPS
    gps = SSD_GROUPS_PER_STEP
    gw = d_inner // g
    hg = gw // SSD_HEAD_DIM
    nst = SSD_STATE
    q = SSD_CHUNK
    nc = s // q
    bw, bn = gps * gw, gps * nst
    x0 = d_inner // bw
    b0 = (2 * d_inner) // bn
    c0 = (2 * d_inner + g * nst) // bn
    cwb0 = d_inner // bn
    cwc0 = (d_inner + g * nst) // bn
    conv_b2 = conv_b.reshape(1, -1)
    head_pad = lambda v: jnp.pad(v.reshape(g, hg), ((0, 0), (0, LANES - hg))).reshape(g, LANES, 1)
    return pl.pallas_call(
        _ssd_kernel,
        grid=(bsz, g // gps, nc),
        in_specs=[
            pl.BlockSpec((1, q, bw), lambda b, gi, c: (b, c, x0 + gi)),
            pl.BlockSpec((1, q, bn), lambda b, gi, c: (b, c, b0 + gi)),
            pl.BlockSpec((1, q, bn), lambda b, gi, c: (b, c, c0 + gi)),
            pl.BlockSpec((1, q, bw), lambda b, gi, c: (b, c, gi)),
            pl.BlockSpec((1, gps, 1, hg, q), lambda b, gi, c: (b, gi, c, 0, 0)),
            pl.BlockSpec((SSD_CONV, bw), lambda b, gi, c: (0, gi)),
            pl.BlockSpec((SSD_CONV, bn), lambda b, gi, c: (0, cwb0 + gi)),
            pl.BlockSpec((SSD_CONV, bn), lambda b, gi, c: (0, cwc0 + gi)),
            pl.BlockSpec((1, bw), lambda b, gi, c: (0, gi)),
            pl.BlockSpec((1, bn), lambda b, gi, c: (0, cwb0 + gi)),
            pl.BlockSpec((1, bn), lambda b, gi, c: (0, cwc0 + gi)),
            pl.BlockSpec((gps, LANES, 1), lambda b, gi, c: (gi, 0, 0)),
            pl.BlockSpec((gps, LANES, 1), lambda b, gi, c: (gi, 0, 0)),
            pl.BlockSpec((1, bw), lambda b, gi, c: (0, gi)),
            pl.BlockSpec((1, bw), lambda b, gi, c: (0, gi)),
        ],
        out_specs=pl.BlockSpec((1, q, bw), lambda b, gi, c: (b, c, gi)),
        out_shape=jax.ShapeDtypeStruct((bsz, s, d_inner), BF16),
        scratch_shapes=[pltpu.VMEM((gps, SUBLANES + q, gw + 2 * nst), F32),
                        pltpu.VMEM((gps, nst, gw), F32)],
        compiler_params=_cparams(("arbitrary", "arbitrary", "arbitrary")),
        name="ssd_scan",
    )(proj, proj, proj, proj, dt_rows, conv_w, conv_w, conv_w, conv_b2, conv_b2, conv_b2,
      head_pad(dt_bias), head_pad(a_log),
      jnp.repeat(d_skip, SSD_HEAD_DIM).reshape(1, d_inner), norm_w.reshape(1, d_inner))


def _rope_kernel(pos_ref, inv_ref, cos_ref, sin_ref):
    ang = pos_ref[...].astype(F32) * inv_ref[...]
    lane = lax.broadcasted_iota(jnp.int32, ang.shape, 1)
    cos_ref[...] = jnp.cos(ang)
    sin_ref[...] = jnp.where(lane < ATT_HEAD_DIM // 2, -1.0, 1.0) * jnp.sin(ang)


def _rope_tables(positions):
    t = positions.size
    half = ATT_HEAD_DIM // 2
    inv = 1.0 / (ROPE_THETA ** (jnp.arange(half, dtype=F32) / half))
    inv2 = jnp.concatenate([inv, inv]).reshape(1, ATT_HEAD_DIM)
    tt = min(2048, t)
    return pl.pallas_call(
        _rope_kernel,
        grid=(t // tt,),
        in_specs=[pl.BlockSpec((tt, 1), lambda i: (i, 0)),
                  pl.BlockSpec((1, ATT_HEAD_DIM), lambda i: (0, 0))],
        out_specs=[pl.BlockSpec((tt, ATT_HEAD_DIM), lambda i: (i, 0))] * 2,
        out_shape=[jax.ShapeDtypeStruct((t, ATT_HEAD_DIM), F32)] * 2,
        compiler_params=_cparams(("arbitrary",)),
        name="rope_tables",
    )(positions.reshape(t, 1), inv2)


def _rot(x, cos2, sin2):
    return x * cos2 + pltpu.roll(x, ATT_HEAD_DIM // 2, axis=1) * sin2


def _attn_kernel(q_ref, k_ref, v_ref, cos_ref, sin_ref, o_ref, krot, vt, kmean):
    blk = MOBA_BLOCK
    nb = k_ref.shape[1] // blk
    log2e_scale = ATT_HEAD_DIM ** -0.5 * math.log2(math.e)

    kmean[...] = jnp.zeros_like(kmean)
    for j in range(nb):
        rows = slice(j * blk, (j + 1) * blk)
        kr = _rot(k_ref[0, rows, :].astype(F32), cos_ref[0, rows, :], sin_ref[0, rows, :])
        krot[rows, :] = kr.astype(BF16)
        kmean[j:j + 1, :] = jnp.mean(kr, axis=0, keepdims=True)
        vt[:, rows] = v_ref[0, rows, :].astype(F32).T.astype(BF16)
    km = kmean[...]

    ki = lax.broadcasted_iota(jnp.int32, (blk, blk), 0)
    qj = lax.broadcasted_iota(jnp.int32, (blk, blk), 1)
    sub = lax.broadcasted_iota(jnp.int32, (km.shape[0], blk), 0)
    for qi in range(nb):
        qrows = slice(qi * blk, (qi + 1) * blk)
        qr = _rot(q_ref[0, qrows, :].astype(F32), cos_ref[0, qrows, :], sin_ref[0, qrows, :])
        qs = (qr * log2e_scale).astype(BF16)
        gate = _dot_nt_x3(km, qr)
        scores = [jnp.where(ki <= qj, _dot_nt(krot[qrows, :], qs), NEG_BIG)]
        for j in range(qi):
            gj = gate[j:j + 1, :]
            ahead = (gate > gj) | ((gate == gj) & (sub < j))
            rank = jnp.sum(jnp.where((sub < qi) & ahead, 1.0, 0.0), axis=0, keepdims=True)
            bias = jnp.where(rank < MOBA_TOPK, 0.0, NEG_BIG)
            scores.append(_dot_nt(krot[j * blk:(j + 1) * blk, :], qs) + bias)
        m = scores[0]
        for sj in scores[1:]:
            m = jnp.maximum(m, sj)
        m = jnp.max(m, axis=0, keepdims=True)
        l = None
        acc = None
        for j, sj in enumerate(scores):
            src = qi if j == 0 else j - 1
            p = jnp.exp2(sj - m)
            lj = jnp.sum(p, axis=0, keepdims=True)
            aj = jnp.dot(vt[:, src * blk:(src + 1) * blk], p.astype(BF16),
                         preferred_element_type=F32)
            l = lj if l is None else l + lj
            acc = aj if acc is None else acc + aj
        o_ref[0, qrows, :] = (acc / l).T.astype(o_ref.dtype)


def _attention(proj, cos2, sin2, q0, k0, v0, n_heads):
    bsz, s, _ = proj.shape
    dh = ATT_HEAD_DIM
    col = lambda c0: pl.BlockSpec((1, s, dh), lambda b, h: (b, 0, c0 + h))
    tab = pl.BlockSpec((1, s, dh), lambda b, h: (b, 0, 0))
    return pl.pallas_call(
        _attn_kernel,
        grid=(bsz, n_heads),
        in_specs=[col(q0), col(k0), col(v0), tab, tab],
        out_specs=pl.BlockSpec((1, s, dh), lambda b, h: (b, 0, h)),
        out_shape=jax.ShapeDtypeStruct((bsz, s, n_heads * dh), BF16),
        scratch_shapes=[pltpu.VMEM((s, dh), BF16), pltpu.VMEM((dh, s), BF16),
                        pltpu.VMEM((-(-(s // MOBA_BLOCK) // SUBLANES) * SUBLANES, dh), F32)],
        compiler_params=_cparams(("arbitrary", "arbitrary")),
        name="moba_attn",
    )(proj, proj, proj, cos2.reshape(bsz, s, dh), sin2.reshape(bsz, s, dh))


def _merge_kernel(ys_ref, ya_ref, w1_ref, w2_ref, gs_ref, ga_ref, o_ref):
    b1 = jnp.dot(ys_ref[...], w1_ref[...], preferred_element_type=F32)
    b2 = jnp.dot(ya_ref[...], w2_ref[...], preferred_element_type=F32)
    o_ref[...] = (jax.nn.sigmoid(gs_ref[...].astype(F32)) * b1
                  + jax.nn.sigmoid(ga_ref[...].astype(F32)) * b2).astype(o_ref.dtype)


def _merge(y_ssd, y_att, w1, w2, proj2d, gs0, ga0):
    m, k1 = y_ssd.shape
    k2 = y_att.shape[1]
    n = w1.shape[1]
    tm, tn = 512, 512
    return pl.pallas_call(
        _merge_kernel,
        grid=(n // tn, m // tm),
        in_specs=[pl.BlockSpec((tm, k1), lambda j, i: (i, 0)),
                  pl.BlockSpec((tm, k2), lambda j, i: (i, 0)),
                  pl.BlockSpec((k1, tn), lambda j, i: (0, j)),
                  pl.BlockSpec((k2, tn), lambda j, i: (0, j)),
                  pl.BlockSpec((tm, tn), lambda j, i: (i, gs0 // tn + j)),
                  pl.BlockSpec((tm, tn), lambda j, i: (i, ga0 // tn + j))],
        out_specs=pl.BlockSpec((tm, tn), lambda j, i: (i, j)),
        out_shape=jax.ShapeDtypeStruct((m, n), BF16),
        compiler_params=_cparams(("arbitrary", "arbitrary")),
        name="branch_merge",
    )(y_ssd, y_att, w1, w2, proj2d, proj2d)


def _outproj_kernel(m_ref, w_ref, x_ref, gpost_ref, gt_ref, gpre_ref, sc_ref, sh_ref, wr_ref,
                    x1_ref, h2b_ref, lg_ref):
    for r in range(m_ref.shape[1] // OUTPROJ_SUBTILE):
        rows = slice(r * OUTPROJ_SUBTILE, (r + 1) * OUTPROJ_SUBTILE)
        y = jnp.dot(m_ref[0, rows, :], w_ref[...], preferred_element_type=F32)
        x1 = x_ref[0, rows, :] + gt_ref[0] * (_rms(y) * gpost_ref[...])
        h2 = (_rms(x1) * gpre_ref[...]) * (1.0 + sc_ref[0]) + sh_ref[0]
        x1_ref[0, rows, :] = x1
        h2b_ref[0, rows, :] = h2.astype(BF16)
        h_hi = h2.astype(BF16)
        h_lo = (h2 - h_hi.astype(F32)).astype(BF16)
        wr = wr_ref[...]
        w_hi = wr.astype(BF16)
        w_lo = (wr - w_hi.astype(F32)).astype(BF16)
        lg = (jnp.dot(h_hi, w_hi, preferred_element_type=F32)
              + (jnp.dot(h_hi, w_lo, preferred_element_type=F32)
                 + jnp.dot(h_lo, w_hi, preferred_element_type=F32)))
        lg_ref[:, rows] = lg.T[:lg_ref.shape[0], :]


def _outproj(merged, w_out, x, g_post, gt1, g_pre, sc2, sh2, w_router):
    bsz, s, d = x.shape
    ne = w_router.shape[1]
    w_router_p = jnp.pad(w_router, ((0, 0), (0, LANES - ne)))
    tm = OUTPROJ_TILE
    nt = s // tm
    row = pl.BlockSpec((1, tm, d), lambda i: (i // nt, i % nt, 0))
    vec = pl.BlockSpec((1, 1, d), lambda i: (i // nt, 0, 0))
    par = pl.BlockSpec((1, d), lambda i: (0, 0))
    return pl.pallas_call(
        _outproj_kernel,
        grid=(bsz * nt,),
        in_specs=[row, pl.BlockSpec((d, d), lambda i: (0, 0)), row, par, vec, par, vec, vec,
                  pl.BlockSpec((d, LANES), lambda i: (0, 0))],
        out_specs=[row, row, pl.BlockSpec((ne, tm), lambda i: (0, i))],
        out_shape=[jax.ShapeDtypeStruct((bsz, s, d), F32),
                   jax.ShapeDtypeStruct((bsz, s, d), BF16),
                   jax.ShapeDtypeStruct((ne, bsz * s), F32)],
        compiler_params=_cparams(("arbitrary",)),
        name="out_proj_norms",
    )(merged.reshape(bsz, s, d), w_out, x, g_post.reshape(1, d), gt1.reshape(bsz, 1, d),
      g_pre.reshape(1, d), sc2.reshape(bsz, 1, d), sh2.reshape(bsz, 1, d), w_router_p)


def _router_kernel(lg_ref, rb_ref, w_ref, rank_ref, cnt_ref, gsc):
    ne, tn = lg_ref.shape
    ng = N_EXPERT_GROUPS
    eg = ne // ng
    scores = jax.nn.sigmoid(lg_ref[...])
    biased = scores + rb_ref[...]
    i8 = lax.broadcasted_iota(jnp.int32, (eg, tn), 0)
    for g in range(ng):
        v = biased[g * eg:(g + 1) * eg, :]
        m1 = jnp.max(v, axis=0, keepdims=True)
        first = jnp.min(jnp.where(v == m1, i8, eg), axis=0, keepdims=True)
        m2 = jnp.max(jnp.where(i8 == first, -jnp.inf, v), axis=0, keepdims=True)
        gsc[g:g + 1, :] = m1 + m2
    gs = gsc[...]
    gi = lax.broadcasted_iota(jnp.int32, (ng, tn), 0)
    masked = []
    for g in range(ng):
        sg = gs[g:g + 1, :]
        ahead = (gs > sg) | ((gs == sg) & (gi < g))
        rank = jnp.sum(jnp.where(ahead, 1.0, 0.0), axis=0, keepdims=True)
        masked.append(jnp.where(rank < TOPK_GROUPS, biased[g * eg:(g + 1) * eg, :], -jnp.inf))
    cur = jnp.concatenate(masked, axis=0)
    sub = lax.broadcasted_iota(jnp.int32, (ne, tn), 0)
    sel = jnp.zeros((ne, tn), F32)
    for k in range(TOP_K):
        mx = jnp.max(cur, axis=0, keepdims=True)
        idx = jnp.min(jnp.where(cur == mx, sub, ne), axis=0, keepdims=True)
        hit = sub == idx
        sel = jnp.where(hit, 1.0, sel)
        cur = jnp.where(hit, -jnp.inf, cur)
    picked = sel * scores
    wsum = jnp.sum(picked, axis=0, keepdims=True)
    w_ref[...] = picked / wsum * ROUTED_SCALE
    ti = lax.broadcasted_iota(jnp.int32, (tn, tn), 0)
    tj = lax.broadcasted_iota(jnp.int32, (tn, tn), 1)
    before = jnp.where(ti < tj, 1.0, 0.0).astype(BF16)
    rank = jnp.dot(sel.astype(BF16), before, preferred_element_type=F32)
    rank_ref[...] = jnp.where(sel > 0.0, rank, NEG_BIG)
    cnt_ref[0] = jnp.broadcast_to(jnp.sum(sel, axis=1, keepdims=True), (ne, LANES))


def _router(logits_t, router_bias):
    ne, t = logits_t.shape
    tn = MOE_TILE
    tile = pl.BlockSpec((ne, tn), lambda i: (0, i))
    return pl.pallas_call(
        _router_kernel,
        grid=(t // tn,),
        in_specs=[tile, pl.BlockSpec((ne, 1), lambda i: (0, 0))],
        out_specs=[tile, tile, pl.BlockSpec((1, ne, LANES), lambda i: (i, 0, 0))],
        out_shape=[jax.ShapeDtypeStruct((ne, t), F32), jax.ShapeDtypeStruct((ne, t), F32),
                   jax.ShapeDtypeStruct((t // tn, ne, LANES), F32)],
        scratch_shapes=[pltpu.VMEM((N_EXPERT_GROUPS, tn), F32)],
        compiler_params=_cparams(("arbitrary",)),
        name="router_topk",
    )(logits_t, router_bias.reshape(ne, 1))


def _tile_rows(n_experts):
    return MOE_TILE * TOP_K + n_experts * SEG_ALIGN


def _segment_copies(src, dst, sem, n, src_off, dst_off, wait):
    off = jnp.int32(0)
    for p in SEG_CHUNKS:
        bit = n & p

        @pl.when(bit != 0)
        def _(p=p, off=off):
            cp = pltpu.make_async_copy(
                src.at[pl.ds(pl.multiple_of(src_off + off, SEG_ALIGN), p), :],
                dst.at[pl.ds(pl.multiple_of(dst_off + off, SEG_ALIGN), p), :], sem)
            if wait:
                cp.wait()
            else:
                cp.start()
        off = off + bit


def _for_each_segment(ne, fn):
    def body(e, carry):
        fn(e)
        return carry
    lax.fori_loop(0, ne, body, 0)


def _build_perm(pbuf, posb, ss_ref, c16_ref, tile, ne, value_row):
    tn = posb.shape[1]
    pbuf[...] = jnp.zeros_like(pbuf)
    rows16 = lax.broadcasted_iota(jnp.int32, (SEG_ALIGN, tn), 0)

    def per_expert(e):
        base = ss_ref[tile, e]
        groups = lax.shift_right_logical(c16_ref[tile, e], SEG_ALIGN.bit_length() - 1)
        prow = posb[pl.ds(e, 1), :]
        vrow = value_row(e)

        def per_group(g, carry):
            r0 = pl.multiple_of(base + g * SEG_ALIGN, SEG_ALIGN)
            hit = prow == (rows16 + r0).astype(F32)
            pbuf[pl.ds(r0, SEG_ALIGN), :] = jnp.where(hit, vrow, 0.0).astype(BF16)
            return carry
        lax.fori_loop(0, groups, per_group, 0)
    _for_each_segment(ne, per_expert)


def _dispatch_kernel(ss_ref, c16_ref, go_ref, ts_ref, tl_ref, rank_ref, ssv_ref, h_ref, xs_hbm,
                     pbuf, xs, posb, sem):
    i = pl.program_id(0)
    n = pl.num_programs(0)
    ne = rank_ref.shape[0]
    tn = rank_ref.shape[1]
    posb[...] = rank_ref[...] + ssv_ref[0]
    _build_perm(pbuf, posb, ss_ref, c16_ref, i, ne, lambda e: 1.0)

    def drain(tile):
        _for_each_segment(ne, lambda e: _segment_copies(
            xs, xs_hbm, sem, c16_ref[tile, e], ss_ref[tile, e], go_ref[tile, e], True))

    @pl.when(i > 0)
    def _():
        drain(i - 1)

    used = ss_ref[i, ne - 1] + c16_ref[i, ne - 1]
    def chunk(ch):
        rows = slice(ch * tn, (ch + 1) * tn)
        xs[rows, :] = jnp.dot(pbuf[rows, :], h_ref[...], preferred_element_type=F32).astype(BF16)

    for ch in range(TOP_K):
        chunk(ch)
    for ch in range(TOP_K, xs.shape[0] // tn):
        pl.when(ch * tn < used)(functools.partial(chunk, ch))

    _for_each_segment(ne, lambda e: _segment_copies(
        xs, xs_hbm, sem, c16_ref[i, e], ss_ref[i, e], go_ref[i, e], False))

    @pl.when(i == n - 1)
    def _():
        drain(i)
        xs[0:MOE_ROWS, :] = jnp.zeros((MOE_ROWS, xs.shape[1]), BF16)
        first_free = lax.shift_right_logical(ts_ref[ne], MOE_ROWS.bit_length() - 1)
        n_blocks = xs_hbm.shape[0] // MOE_ROWS
        for wait in (False, True):
            _for_each_segment(ne, lambda e: _segment_copies(
                xs, xs_hbm, sem, tl_ref[e], 0, ts_ref[e], wait))

            def free_block(b, carry, wait=wait):
                cp = pltpu.make_async_copy(
                    xs.at[pl.ds(0, MOE_ROWS), :],
                    xs_hbm.at[pl.ds(pl.multiple_of(b * MOE_ROWS, MOE_ROWS), MOE_ROWS), :], sem)
                if wait:
                    cp.wait()
                else:
                    cp.start()
                return carry
            lax.fori_loop(first_free, n_blocks, free_block, 0)


def _dispatch(seg_start, c16, goff, tail_start, tail_len, rank, h2b, n_rows):
    ne, t = rank.shape
    d = h2b.shape[1]
    tn = MOE_TILE
    rt = _tile_rows(ne)
    grid_spec = pltpu.PrefetchScalarGridSpec(
        num_scalar_prefetch=5,
        grid=(t // tn,),
        in_specs=[pl.BlockSpec((ne, tn), lambda i, *_: (0, i)),
                  pl.BlockSpec((1, ne, 1), lambda i, *_: (i, 0, 0)),
                  pl.BlockSpec((tn, d), lambda i, *_: (i, 0))],
        out_specs=pl.BlockSpec(memory_space=pl.ANY),
        scratch_shapes=[pltpu.VMEM((rt, tn), BF16), pltpu.VMEM((rt, d), BF16),
                        pltpu.VMEM((ne, tn), F32), pltpu.SemaphoreType.DMA],
    )
    return pl.pallas_call(
        _dispatch_kernel,
        grid_spec=grid_spec,
        out_shape=jax.ShapeDtypeStruct((n_rows, d), BF16),
        compiler_params=_cparams(("arbitrary",)),
        name="moe_dispatch",
    )(seg_start, c16, goff, tail_start, tail_len, rank,
      seg_start.astype(F32).reshape(t // tn, ne, 1), h2b)


def _combine_kernel(ss_ref, c16_ref, go_ref, rank_ref, w_ref, ssv_ref, y_hbm, o_ref,
                    pbuf, ys, posb, sem):
    i = pl.program_id(0)
    ne = rank_ref.shape[0]
    tn = rank_ref.shape[1]
    ys[...] = jnp.zeros_like(ys)
    _for_each_segment(ne, lambda e: _segment_copies(
        y_hbm, ys, sem, c16_ref[i, e], go_ref[i, e], ss_ref[i, e], False))
    posb[...] = rank_ref[...] + ssv_ref[0]
    _build_perm(pbuf, posb, ss_ref, c16_ref, i, ne, lambda e: w_ref[pl.ds(e, 1), :])
    _for_each_segment(ne, lambda e: _segment_copies(
        y_hbm, ys, sem, c16_ref[i, e], go_ref[i, e], ss_ref[i, e], True))
    used = ss_ref[i, ne - 1] + c16_ref[i, ne - 1]
    def chunk(ch):
        rows = slice(ch * tn, (ch + 1) * tn)
        return lax.dot_general(pbuf[rows, :], ys[rows, :], (((0,), (0,)), ((), ())),
                               preferred_element_type=F32)

    always = TOP_K
    acc = chunk(0)
    for ch in range(1, always):
        acc = acc + chunk(ch)
    o_ref[...] = acc
    for ch in range(always, ys.shape[0] // tn):
        @pl.when(ch * tn < used)
        def _(ch=ch):
            o_ref[...] += chunk(ch)


def _combine(seg_start, c16, goff, rank, selw, y_rows):
    ne, t = rank.shape
    d = y_rows.shape[1]
    tn = MOE_TILE
    rt = _tile_rows(ne)
    tile = pl.BlockSpec((ne, tn), lambda i, *_: (0, i))
    grid_spec = pltpu.PrefetchScalarGridSpec(
        num_scalar_prefetch=3,
        grid=(t // tn,),
        in_specs=[tile, tile, pl.BlockSpec((1, ne, 1), lambda i, *_: (i, 0, 0)),
                  pl.BlockSpec(memory_space=pl.ANY)],
        out_specs=pl.BlockSpec((tn, d), lambda i, *_: (i, 0)),
        scratch_shapes=[pltpu.VMEM((rt, tn), BF16), pltpu.VMEM((rt, d), BF16),
                        pltpu.VMEM((ne, tn), F32), pltpu.SemaphoreType.DMA],
    )
    return pl.pallas_call(
        _combine_kernel,
        grid_spec=grid_spec,
        out_shape=jax.ShapeDtypeStruct((t, d), F32),
        compiler_params=_cparams(("arbitrary",)),
        name="moe_combine",
    )(seg_start, c16, goff, rank, selw, seg_start.astype(F32).reshape(t // tn, ne, 1), y_rows)


def _expert_kernel(be_ref, nv_ref, x_ref, wg_ref, wu_ref, wd_ref, o_ref, wg, wu, wd):
    i = pl.program_id(0)

    @pl.when(nv_ref[i] == 0)
    def _():
        o_ref[...] = jnp.zeros_like(o_ref)

    @pl.when(nv_ref[i] > 0)
    def _():
        @pl.when((i == 0) | (be_ref[i] != be_ref[jnp.maximum(i - 1, 0)]))
        def _():
            wg[...] = wg_ref[0].astype(BF16)
            wu[...] = wu_ref[0].astype(BF16)
            wd[...] = wd_ref[0].astype(BF16)

        x = x_ref[...]
        gate = jnp.dot(x, wg[...], preferred_element_type=F32)
        up = jnp.dot(x, wu[...], preferred_element_type=F32)
        act = (_silu(gate) * up).astype(BF16)
        o_ref[...] = jnp.dot(act, wd[...], preferred_element_type=F32).astype(o_ref.dtype)


def _expert_ffn(blk_exp, blk_valid, x_sorted, w_gate, w_up, w_down):
    n_rows, d = x_sorted.shape
    f = w_gate.shape[2]
    rows = MOE_ROWS
    grid_spec = pltpu.PrefetchScalarGridSpec(
        num_scalar_prefetch=2,
        grid=(n_rows // rows,),
        in_specs=[pl.BlockSpec((rows, d), lambda i, be, nv: (i, 0)),
                  pl.BlockSpec((1, d, f), lambda i, be, nv: (be[i], 0, 0)),
                  pl.BlockSpec((1, d, f), lambda i, be, nv: (be[i], 0, 0)),
                  pl.BlockSpec((1, f, d), lambda i, be, nv: (be[i], 0, 0))],
        out_specs=pl.BlockSpec((rows, d), lambda i, be, nv: (i, 0)),
        scratch_shapes=[pltpu.VMEM((d, f), BF16), pltpu.VMEM((d, f), BF16), pltpu.VMEM((f, d), BF16)],
    )
    return pl.pallas_call(
        _expert_kernel,
        grid_spec=grid_spec,
        out_shape=jax.ShapeDtypeStruct((n_rows, d), BF16),
        compiler_params=_cparams(("arbitrary",)),
        name="expert_ffn",
    )(blk_exp, blk_valid, x_sorted, w_gate, w_up, w_down)


def _shared_kernel(x_ref, wg_ref, wu_ref, wd_ref, r_ref, x1_ref, gt_ref, g_ref, o_ref):
    x = x_ref[0]
    gate = jnp.dot(x, wg_ref[...], preferred_element_type=F32)
    up = jnp.dot(x, wu_ref[...], preferred_element_type=F32)
    act = (_silu(gate) * up).astype(BF16)
    y = r_ref[0] + jnp.dot(act, wd_ref[...], preferred_element_type=F32)
    o_ref[0] = x1_ref[0] + gt_ref[0] * (_rms(y) * g_ref[...])


def _shared_final(h2b, wg, wu, wd, routed, x1, gt2, g_post):
    bsz, s, d = x1.shape
    f = wg.shape[1]
    tm = 512
    nt = s // tm
    row = pl.BlockSpec((1, tm, d), lambda i: (i // nt, i % nt, 0))
    full = lambda shape: pl.BlockSpec(shape, lambda i: (0, 0))
    return pl.pallas_call(
        _shared_kernel,
        grid=(bsz * nt,),
        in_specs=[row, full((d, f)), full((d, f)), full((f, d)), row, row,
                  pl.BlockSpec((1, 1, d), lambda i: (i // nt, 0, 0)), full((1, d))],
        out_specs=row,
        out_shape=jax.ShapeDtypeStruct((bsz, s, d), F32),
        compiler_params=_cparams(("arbitrary",)),
        name="shared_ffn_final",
    )(h2b, wg, wu, wd, routed.reshape(bsz, s, d), x1, gt2.reshape(bsz, 1, d), g_post.reshape(1, d))


def _dispatch_plan(counts, n_blocks):
    n_tiles, ne = counts.shape
    c16 = (counts + SEG_ALIGN - 1) // SEG_ALIGN * SEG_ALIGN
    seg_start = jnp.cumsum(c16, axis=1) - c16
    tot = jnp.sum(c16, axis=0)
    padded = (tot + MOE_ROWS - 1) // MOE_ROWS * MOE_ROWS
    pad_end = jnp.cumsum(padded)
    pad_start = pad_end - padded
    goff = pad_start[None, :] + jnp.cumsum(c16, axis=0) - c16
    blk_row = jnp.arange(n_blocks, dtype=jnp.int32) * MOE_ROWS
    blk_exp = jnp.sum(pad_end[None, :] <= blk_row[:, None], axis=1)
    blk_exp = jnp.minimum(blk_exp, ne - 1).astype(jnp.int32)
    blk_valid = jnp.clip(pad_start[blk_exp] + tot[blk_exp] - blk_row, 0, MOE_ROWS)
    blk_valid = jnp.where(blk_row < pad_end[-1], blk_valid, 0).astype(jnp.int32)
    i32 = lambda v: v.astype(jnp.int32)
    tail_start = jnp.concatenate([pad_start + tot, pad_end[-1:]])
    return (i32(seg_start), i32(c16), i32(goff), i32(tail_start), i32(padded - tot),
            blk_exp, blk_valid)


def kernel(x, c, positions, w_ada, b_ada, g_pre_mix, g_post_mix, g_pre_ffn, g_post_ffn, w_in,
           conv_w, conv_b, dt_bias, a_log, d_skip, ssd_norm_w, w_ssd_br, w_attn_br, w_out,
           w_router, router_bias, w_gate, w_up, w_down, ws_gate, ws_up, ws_down):
    bsz, s, d = x.shape
    t = bsz * s
    for l in range(w_ada.shape[0]):
        d_inner = w_ssd_br.shape[1]
        att_w = w_attn_br.shape[1]
        n_ssd_heads = dt_bias.shape[1]
        conv_dim = conv_w.shape[2]
        n_heads = att_w // ATT_HEAD_DIM
        n_experts = w_router.shape[2]

        mod = _ada(c, w_ada[l], b_ada[l])
        sh1, sc1, gt1, sh2, sc2, gt2 = jnp.split(mod, 6, axis=-1)

        o_dt = d_inner + conv_dim
        o_q = o_dt + n_ssd_heads
        w_in_t = jnp.swapaxes(w_in, 1, 2)
        h1 = _prenorm(x, g_pre_mix[l], sc1, sh1).reshape(t, d)
        proj_s = _matmul_wt(h1, w_in_t, l, 0, o_dt, 1024, 1024, BF16, "in_proj_ssd")
        proj_a = _matmul_wt(h1, w_in_t, l, o_q, w_in.shape[2] - o_q, 1024, 1024, BF16, "in_proj_att")
        dt_raw = _matmul_wt(h1, w_in_t, l, o_dt, n_ssd_heads, 1024, n_ssd_heads, F32, "in_proj_dt")
        k_off = att_w
        v_off = k_off + att_w
        gs_off = v_off + att_w
        ga_off = gs_off + d

        hg = n_ssd_heads // SSD_GROUPS
        dt_rows = dt_raw.reshape(bsz, s // SSD_CHUNK, SSD_CHUNK, SSD_GROUPS, hg).transpose(0, 3, 1, 4, 2)
        y_ssd = _ssd(proj_s.reshape(bsz, s, o_dt), dt_rows, conv_w[l], conv_b[l], dt_bias[l], a_log[l],
                     d_skip[l], ssd_norm_w[l], d_inner)

        cos2, sin2 = _rope_tables(positions)
        y_att = _attention(proj_a.reshape(bsz, s, -1), cos2, sin2, 0, k_off // ATT_HEAD_DIM,
                           v_off // ATT_HEAD_DIM, n_heads)

        merged = _merge(y_ssd.reshape(t, d_inner), y_att.reshape(t, att_w),
                        w_ssd_br[l].astype(BF16), w_attn_br[l].astype(BF16), proj_a, gs_off, ga_off)
        x1, h2b, logits_t = _outproj(merged, w_out[l].astype(BF16), x, g_post_mix[l], gt1,
                                         g_pre_ffn[l], sc2, sh2, w_router[l])

        selw, rank, cnt = _router(logits_t, router_bias[l])
        n_tiles = t // MOE_TILE
        n_blocks = -(-(t * TOP_K + n_tiles * n_experts * (SEG_ALIGN - 1)) // MOE_ROWS) + n_experts
        seg_start, c16, goff, tail_start, tail_len, blk_exp, blk_valid = _dispatch_plan(
            cnt[:, :, 0].astype(jnp.int32), n_blocks)
        x_sorted = _dispatch(seg_start, c16, goff, tail_start, tail_len, rank, h2b.reshape(t, d),
                             n_blocks * MOE_ROWS)
        y_rows = _expert_ffn(blk_exp, blk_valid, x_sorted, w_gate[l], w_up[l], w_down[l])
        routed = _combine(seg_start, c16, goff, rank, selw, y_rows)
        x = _shared_final(h2b, ws_gate[l].astype(BF16), ws_up[l].astype(BF16),
                          ws_down[l].astype(BF16), routed, x1, gt2, g_post_ffn[l])
    return x
```

```python
import functools
import math

import jax
import jax.numpy as jnp
from jax import lax
from jax.experimental import pallas as pl
from jax.experimental.pallas import tpu as pltpu

F32 = jnp.float32
BF16 = jnp.bfloat16

NORM_EPS = 1e-6
ROPE_THETA = 10000.0

SSD_HEAD_DIM = 64
SSD_GROUPS = 8
SSD_STATE = 128
SSD_CONV = 4
SSD_CHUNK = 128
SSD_GROUPS_PER_STEP = 4

ATT_HEAD_DIM = 128
MOBA_BLOCK = 256
MOBA_TOPK = 3

TOP_K = 8
N_EXPERT_GROUPS = 8
TOPK_GROUPS = 4
ROUTED_SCALE = 2.5
MOE_ROWS = 512
MOE_TILE = 512
SEG_ALIGN = 16
SEG_CHUNKS = (512, 256, 128, 64, 32, 16)

OUTPROJ_TILE = 512
OUTPROJ_SUBTILE = 256

LANES = 128
SUBLANES = 8
VMEM_LIMIT = 56 * 1024 * 1024
NEG_BIG = -1e30


def _cparams(sem):
    return pltpu.CompilerParams(dimension_semantics=sem, vmem_limit_bytes=VMEM_LIMIT)


def _silu(x):
    return x * jax.nn.sigmoid(x)


def _rms(x):
    return x * lax.rsqrt(jnp.mean(x * x, axis=-1, keepdims=True) + NORM_EPS)


def _split3(a):
    a1 = a.astype(BF16)
    r1 = a - a1.astype(F32)
    a2 = r1.astype(BF16)
    a3 = (r1 - a2.astype(F32)).astype(BF16)
    return a1, a2, a3


def _dot_exact_rhs(a, e_bf16):
    out = None
    for p in _split3(a):
        t = jnp.dot(p, e_bf16, preferred_element_type=F32)
        out = t if out is None else out + t
    return out


def _dot_nt(a, b):
    return lax.dot_general(a, b, (((1,), (1,)), ((), ())), preferred_element_type=F32)


def _dot_nt_x3(a, b):
    a1 = a.astype(BF16)
    a2 = (a - a1.astype(F32)).astype(BF16)
    b1 = b.astype(BF16)
    b2 = (b - b1.astype(F32)).astype(BF16)
    return _dot_nt(a1, b1) + (_dot_nt(a1, b2) + _dot_nt(a2, b1))


def _ada_kernel(c_ref, w_ref, b_ref, o_ref):
    cond = _silu(c_ref[...])
    o_ref[...] = jnp.dot(cond.astype(BF16), w_ref[...].astype(BF16),
                         preferred_element_type=F32) + b_ref[...]


def _ada(c, w_ada, b_ada):
    bsz, d = c.shape
    n = w_ada.shape[1]
    tn = 1024
    return pl.pallas_call(
        _ada_kernel,
        grid=(n // tn,),
        in_specs=[pl.BlockSpec((bsz, d), lambda j: (0, 0)),
                  pl.BlockSpec((d, tn), lambda j: (0, j)),
                  pl.BlockSpec((1, tn), lambda j: (0, j))],
        out_specs=pl.BlockSpec((bsz, tn), lambda j: (0, j)),
        out_shape=jax.ShapeDtypeStruct((bsz, n), F32),
        compiler_params=_cparams(("arbitrary",)),
        name="ada_mod",
    )(c, w_ada, b_ada.reshape(1, n))


def _prenorm_kernel(x_ref, g_ref, sc_ref, sh_ref, o_ref):
    y = _rms(x_ref[0]) * g_ref[...]
    o_ref[0] = (y * (1.0 + sc_ref[0]) + sh_ref[0]).astype(o_ref.dtype)


def _prenorm(x, g, sc, sh):
    bsz, s, d = x.shape
    ts = 512
    vec = pl.BlockSpec((1, 1, d), lambda b, i: (b, 0, 0))
    return pl.pallas_call(
        _prenorm_kernel,
        grid=(bsz, s // ts),
        in_specs=[pl.BlockSpec((1, ts, d), lambda b, i: (b, i, 0)),
                  pl.BlockSpec((1, d), lambda b, i: (0, 0)), vec, vec],
        out_specs=pl.BlockSpec((1, ts, d), lambda b, i: (b, i, 0)),
        out_shape=jax.ShapeDtypeStruct((bsz, s, d), BF16),
        compiler_params=_cparams(("arbitrary", "arbitrary")),
        name="prenorm",
    )(x, g.reshape(1, d), sc.reshape(bsz, 1, d), sh.reshape(bsz, 1, d))


def _mm_kernel(a_ref, w_ref, o_ref):
    o_ref[...] = jnp.dot(a_ref[...], w_ref[...], preferred_element_type=F32).astype(o_ref.dtype)


def _matmul(a, w, out_dtype, tm, tn, name):
    m, k = a.shape
    n = w.shape[1]
    return pl.pallas_call(
        _mm_kernel,
        grid=(n // tn, m // tm),
        in_specs=[pl.BlockSpec((tm, k), lambda j, i: (i, 0)),
                  pl.BlockSpec((k, tn), lambda j, i: (0, j))],
        out_specs=pl.BlockSpec((tm, tn), lambda j, i: (i, j)),
        out_shape=jax.ShapeDtypeStruct((m, n), out_dtype),
        compiler_params=_cparams(("arbitrary", "arbitrary")),
        name=name,
    )(a, w)


def _mm_wt_kernel(shift, a_ref, *refs):
    w_ref = refs[0]
    o_ref, wb = refs[-2:]

    @pl.when(pl.program_id(1) == 0)
    def _():
        w = w_ref[0]
        if shift:
            w = jnp.concatenate([w[shift:], refs[1][0]], axis=0)
        wb[...] = w.T.astype(BF16)

    o_ref[...] = jnp.dot(a_ref[...], wb[...], preferred_element_type=F32).astype(o_ref.dtype)


def _dt_kernel(a_ref, w_ref, o_ref):
    o_ref[...] = _dot_nt(w_ref[0].astype(BF16), a_ref[...])


def _dt_proj(a, wt_stack, layer, row0, nh, tm):
    m, k = a.shape
    assert row0 % nh == 0
    return pl.pallas_call(
        _dt_kernel,
        grid=(m // tm,),
        in_specs=[pl.BlockSpec((tm, k), lambda i: (i, 0)),
                  pl.BlockSpec((1, nh, k), lambda i: (layer, row0 // nh, 0))],
        out_specs=pl.BlockSpec((nh, tm), lambda i: (0, i)),
        out_shape=jax.ShapeDtypeStruct((nh, m), F32),
        compiler_params=_cparams(("arbitrary",)),
        name="in_proj_dt",
    )(a, wt_stack)


def _matmul_wt(a, wt_stack, layer, row0, n, tm, tn, out_dtype, name):
    m, k = a.shape
    shift = row0 % tn
    base = row0 - shift
    assert n % tn == 0 and shift % SUBLANES == 0
    a_spec = pl.BlockSpec((tm, k), lambda j, i: (i, 0))
    in_specs = [a_spec, pl.BlockSpec((1, tn, k), lambda j, i: (layer, base // tn + j, 0))]
    args = [a, wt_stack]
    if shift:
        assert (base + tn) % shift == 0 and tn % shift == 0
        in_specs.append(pl.BlockSpec((1, shift, k), lambda j, i: (layer, (base + (j + 1) * tn) // shift, 0)))
        args.append(wt_stack)
    return pl.pallas_call(
        functools.partial(_mm_wt_kernel, shift),
        grid=(n // tn, m // tm),
        in_specs=in_specs,
        out_specs=pl.BlockSpec((tm, tn), lambda j, i: (i, j)),
        out_shape=jax.ShapeDtypeStruct((m, n), out_dtype),
        scratch_shapes=[pltpu.VMEM((k, tn), BF16)],
        compiler_params=_cparams(("arbitrary", "arbitrary")),
        name=name,
    )(*args)


def _softplus(x):
    return jnp.maximum(x, 0.0) + jnp.log(1.0 + jnp.exp(-jnp.abs(x)))


def _ssd_kernel(x_ref, b_ref, c_ref, z_ref, dt_ref, cwx_ref, cwb_ref, cwc_ref,
                cbx_ref, cbb_ref, cbc_ref, dtb_ref, alog_ref, dsk_ref, nw_ref,
                o_ref, cbuf, state):
    q = SSD_CHUNK
    gps = dtb_ref.shape[0]
    gw = x_ref.shape[2] // gps
    nst = b_ref.shape[2] // gps
    hg = gw // SSD_HEAD_DIM
    pad = SUBLANES

    @pl.when(pl.program_id(2) == 0)
    def _():
        cbuf[:, 0:pad, :] = jnp.zeros((gps, pad, gw + 2 * nst), F32)
        state[...] = jnp.zeros_like(state)

    hrow = lax.broadcasted_iota(jnp.int32, (LANES, q), 0)
    ri = lax.broadcasted_iota(jnp.int32, (q, q), 0)
    ci = lax.broadcasted_iota(jnp.int32, (q, q), 1)
    triu = jnp.where(ri <= ci, 1.0, 0.0).astype(BF16)
    er = lax.broadcasted_iota(jnp.int32, (LANES, gw), 0)
    ec = lax.broadcasted_iota(jnp.int32, (LANES, gw), 1)
    e_ch = jnp.where(ec // SSD_HEAD_DIM == er, 1.0, 0.0).astype(BF16)
    er2 = lax.broadcasted_iota(jnp.int32, (LANES, hg * q), 0)
    ec2 = lax.broadcasted_iota(jnp.int32, (LANES, hg * q), 1)
    e_t = jnp.where(ec2 // q == er2, 1.0, 0.0).astype(BF16)
    ri2 = lax.broadcasted_iota(jnp.int32, (q, hg * q), 0)
    ci2 = lax.broadcasted_iota(jnp.int32, (q, hg * q), 1) % q
    first_head = lax.broadcasted_iota(jnp.int32, (1, LANES), 1) < SSD_HEAD_DIM

    for u in range(gps):
        xc = slice(u * gw, (u + 1) * gw)
        nc = slice(u * nst, (u + 1) * nst)
        xin = jnp.concatenate([x_ref[0, :, xc], b_ref[0, :, nc], c_ref[0, :, nc]],
                              axis=-1).astype(F32)
        cbuf[u, pad:pad + q, :] = xin
        w = jnp.concatenate([cwx_ref[:, xc], cwb_ref[:, nc], cwc_ref[:, nc]], axis=-1)
        bias = jnp.concatenate([cbx_ref[:, xc], cbb_ref[:, nc], cbc_ref[:, nc]], axis=-1)
        acc = bias + w[SSD_CONV - 1:SSD_CONV, :] * xin
        for s in range(1, SSD_CONV):
            acc = acc + w[SSD_CONV - 1 - s:SSD_CONV - s, :] * cbuf[u, pad - s:pad - s + q, :]
        cbuf[u, pad - (SSD_CONV - 1):pad, :] = cbuf[u, pad + q - (SSD_CONV - 1):pad + q, :]
        xbc = _silu(acc)
        xs = xbc[:, :gw]
        bm = xbc[:, gw:gw + nst].astype(BF16)
        cm = xbc[:, gw + nst:].astype(BF16)

        dt_raw = jnp.concatenate([dt_ref[u * hg:(u + 1) * hg, :], jnp.zeros((LANES - hg, q), F32)],
                                 axis=0)
        dt_r = jnp.where(hrow < hg, _softplus(dt_raw + dtb_ref[u]), 0.0)
        a_r = dt_r * (-math.log2(math.e) * jnp.exp(alog_ref[u]))
        acum_r = _dot_exact_rhs(a_r, triu)
        acum_c = acum_r.T
        dt_c = dt_r.T

        dt_full = _dot_exact_rhs(dt_c, e_ch)
        acum_full = _dot_exact_rhs(acum_c, e_ch)
        acum_colb = _dot_exact_rhs(acum_c, e_t)
        acum_rowb = jnp.concatenate(
            [jnp.broadcast_to(acum_r[h:h + 1, :], (q, q)) for h in range(hg)], axis=-1)

        lmat = jnp.exp2(jnp.where(ri2 >= ci2, acum_colb - acum_rowb, NEG_BIG))
        cb = _dot_nt(cm, bm)
        mcat = (jnp.concatenate([cb] * hg, axis=-1) * lmat).astype(BF16)
        xdt = xs * dt_full
        y_pairs = []
        for pr in range(gw // LANES):
            xp = xdt[:, pr * LANES:(pr + 1) * LANES]
            xbd = jnp.concatenate([jnp.where(first_head, xp, 0.0).astype(BF16),
                                   jnp.where(first_head, 0.0, xp).astype(BF16)], axis=0)
            y_pairs.append(jnp.dot(mcat[:, pr * 2 * q:(pr + 1) * 2 * q], xbd,
                                   preferred_element_type=F32))
        y = jnp.concatenate(y_pairs, axis=-1)

        st = state[u]
        y = y + jnp.dot(cm, st.astype(BF16), preferred_element_type=F32) * jnp.exp2(acum_full)
        acum_last = acum_full[q - 1:q, :]
        xw = (xdt * jnp.exp2(acum_last - acum_full)).astype(BF16)
        bt = xbc[:, gw:gw + nst].T.astype(BF16)
        state[u] = st * jnp.exp2(acum_last) + jnp.dot(bt, xw, preferred_element_type=F32)

        y = y + xs * dsk_ref[:, xc]
        y = y * _silu(z_ref[0, :, xc].astype(F32))
        o_ref[0, :, xc] = (_rms(y) * nw_ref[:, xc]).astype(o_ref.dtype)


def _ssd(proj, dt_rows, conv_w, conv_b, dt_bias, a_log, d_skip, norm_w, d_inner):
    bsz, s, _ = proj.shape
    g = SSD_GROUPS
    gps = SSD_GROUPS_PER_STEP
    gw = d_inner // g
    hg = gw // SSD_HEAD_DIM
    nst = SSD_STATE
    q = SSD_CHUNK
    nc = s // q
    bw, bn = gps * gw, gps * nst
    x0 = d_inner // bw
    b0 = (2 * d_inner) // bn
    c0 = (2 * d_inner + g * nst) // bn
    cwb0 = d_inner // bn
    cwc0 = (d_inner + g * nst) // bn
    conv_b2 = conv_b.reshape(1, -1)
    head_pad = lambda v: jnp.pad(v.reshape(g, hg), ((0, 0), (0, LANES - hg))).reshape(g, LANES, 1)
    return pl.pallas_call(
        _ssd_kernel,
        grid=(bsz, g // gps, nc),
        in_specs=[
            pl.BlockSpec((1, q, bw), lambda b, gi, c: (b, c, x0 + gi)),
            pl.BlockSpec((1, q, bn), lambda b, gi, c: (b, c, b0 + gi)),
            pl.BlockSpec((1, q, bn), lambda b, gi, c: (b, c, c0 + gi)),
            pl.BlockSpec((1, q, bw), lambda b, gi, c: (b, c, gi)),
            pl.BlockSpec((gps * hg, q), lambda b, gi, c: (gi, b * nc + c)),
            pl.BlockSpec((SSD_CONV, bw), lambda b, gi, c: (0, gi)),
            pl.BlockSpec((SSD_CONV, bn), lambda b, gi, c: (0, cwb0 + gi)),
            pl.BlockSpec((SSD_CONV, bn), lambda b, gi, c: (0, cwc0 + gi)),
            pl.BlockSpec((1, bw), lambda b, gi, c: (0, gi)),
            pl.BlockSpec((1, bn), lambda b, gi, c: (0, cwb0 + gi)),
            pl.BlockSpec((1, bn), lambda b, gi, c: (0, cwc0 + gi)),
            pl.BlockSpec((gps, LANES, 1), lambda b, gi, c: (gi, 0, 0)),
            pl.BlockSpec((gps, LANES, 1), lambda b, gi, c: (gi, 0, 0)),
            pl.BlockSpec((1, bw), lambda b, gi, c: (0, gi)),
            pl.BlockSpec((1, bw), lambda b, gi, c: (0, gi)),
        ],
        out_specs=pl.BlockSpec((1, q, bw), lambda b, gi, c: (b, c, gi)),
        out_shape=jax.ShapeDtypeStruct((bsz, s, d_inner), BF16),
        scratch_shapes=[pltpu.VMEM((gps, SUBLANES + q, gw + 2 * nst), F32),
                        pltpu.VMEM((gps, nst, gw), F32)],
        compiler_params=_cparams(("arbitrary", "arbitrary", "arbitrary")),
        name="ssd_scan",
    )(proj, proj, proj, proj, dt_rows, conv_w, conv_w, conv_w, conv_b2, conv_b2, conv_b2,
      head_pad(dt_bias), head_pad(a_log),
      jnp.repeat(d_skip, SSD_HEAD_DIM).reshape(1, d_inner), norm_w.reshape(1, d_inner))


def _rope_kernel(pos_ref, inv_ref, cos_ref, sin_ref):
    ang = pos_ref[...].astype(F32) * inv_ref[...]
    lane = lax.broadcasted_iota(jnp.int32, ang.shape, 1)
    cos_ref[...] = jnp.cos(ang)
    sin_ref[...] = jnp.where(lane < ATT_HEAD_DIM // 2, -1.0, 1.0) * jnp.sin(ang)


def _rope_tables(positions):
    t = positions.size
    half = ATT_HEAD_DIM // 2
    inv = 1.0 / (ROPE_THETA ** (jnp.arange(half, dtype=F32) / half))
    inv2 = jnp.concatenate([inv, inv]).reshape(1, ATT_HEAD_DIM)
    tt = min(2048, t)
    return pl.pallas_call(
        _rope_kernel,
        grid=(t // tt,),
        in_specs=[pl.BlockSpec((tt, 1), lambda i: (i, 0)),
                  pl.BlockSpec((1, ATT_HEAD_DIM), lambda i: (0, 0))],
        out_specs=[pl.BlockSpec((tt, ATT_HEAD_DIM), lambda i: (i, 0))] * 2,
        out_shape=[jax.ShapeDtypeStruct((t, ATT_HEAD_DIM), F32)] * 2,
        compiler_params=_cparams(("arbitrary",)),
        name="rope_tables",
    )(positions.reshape(t, 1), inv2)


def _rot(x, cos2, sin2):
    return x * cos2 + pltpu.roll(x, ATT_HEAD_DIM // 2, axis=1) * sin2


def _attn_kernel(q_ref, k_ref, v_ref, cos_ref, sin_ref, o_ref, krot, vt, kmean):
    blk = MOBA_BLOCK
    nb = k_ref.shape[1] // blk
    log2e_scale = ATT_HEAD_DIM ** -0.5 * math.log2(math.e)

    kmean[...] = jnp.zeros_like(kmean)
    for j in range(nb):
        rows = slice(j * blk, (j + 1) * blk)
        kr = _rot(k_ref[0, rows, :].astype(F32), cos_ref[0, rows, :], sin_ref[0, rows, :])
        krot[rows, :] = kr.astype(BF16)
        kmean[j:j + 1, :] = jnp.mean(kr, axis=0, keepdims=True)
        vt[:, rows] = v_ref[0, rows, :].astype(F32).T.astype(BF16)
    km = kmean[...]

    ki = lax.broadcasted_iota(jnp.int32, (blk, blk), 0)
    qj = lax.broadcasted_iota(jnp.int32, (blk, blk), 1)
    sub = lax.broadcasted_iota(jnp.int32, (km.shape[0], blk), 0)
    for qi in range(nb):
        qrows = slice(qi * blk, (qi + 1) * blk)
        qr = _rot(q_ref[0, qrows, :].astype(F32), cos_ref[0, qrows, :], sin_ref[0, qrows, :])
        qs = (qr * log2e_scale).T.astype(BF16)
        gate = _dot_nt_x3(km, qr)
        scores = [jnp.where(ki <= qj, jnp.dot(krot[qrows, :], qs, preferred_element_type=F32), NEG_BIG)]
        for j in range(qi):
            gj = gate[j:j + 1, :]
            ahead = (gate > gj) | ((gate == gj) & (sub < j))
            rank = jnp.sum(jnp.where((sub < qi) & ahead, 1.0, 0.0), axis=0, keepdims=True)
            bias = jnp.where(rank < MOBA_TOPK, 0.0, NEG_BIG)
            scores.append(jnp.dot(krot[j * blk:(j + 1) * blk, :], qs,
                                  preferred_element_type=F32) + bias)
        m = scores[0]
        for sj in scores[1:]:
            m = jnp.maximum(m, sj)
        m = jnp.max(m, axis=0, keepdims=True)
        l = None
        acc = None
        for j, sj in enumerate(scores):
            src = qi if j == 0 else j - 1
            p = jnp.exp2(sj - m)
            lj = jnp.sum(p, axis=0, keepdims=True)
            aj = jnp.dot(vt[:, src * blk:(src + 1) * blk], p.astype(BF16),
                         preferred_element_type=F32)
            l = lj if l is None else l + lj
            acc = aj if acc is None else acc + aj
        o_ref[0, qrows, :] = (acc / l).T.astype(o_ref.dtype)


def _attention(proj, cos2, sin2, q0, k0, v0, n_heads):
    bsz, s, _ = proj.shape
    dh = ATT_HEAD_DIM
    col = lambda c0: pl.BlockSpec((1, s, dh), lambda b, h: (b, 0, c0 + h))
    tab = pl.BlockSpec((1, s, dh), lambda b, h: (b, 0, 0))
    return pl.pallas_call(
        _attn_kernel,
        grid=(bsz, n_heads),
        in_specs=[col(q0), col(k0), col(v0), tab, tab],
        out_specs=pl.BlockSpec((1, s, dh), lambda b, h: (b, 0, h)),
        out_shape=jax.ShapeDtypeStruct((bsz, s, n_heads * dh), BF16),
        scratch_shapes=[pltpu.VMEM((s, dh), BF16), pltpu.VMEM((dh, s), BF16),
                        pltpu.VMEM((-(-(s // MOBA_BLOCK) // SUBLANES) * SUBLANES, dh), F32)],
        compiler_params=_cparams(("arbitrary", "arbitrary")),
        name="moba_attn",
    )(proj, proj, proj, cos2.reshape(bsz, s, dh), sin2.reshape(bsz, s, dh))


def _merge_kernel(ys_ref, ya_ref, w1_ref, w2_ref, gs_ref, ga_ref, o_ref):
    b1 = jnp.dot(ys_ref[...], w1_ref[...], preferred_element_type=F32)
    b2 = jnp.dot(ya_ref[...], w2_ref[...], preferred_element_type=F32)
    o_ref[...] = (jax.nn.sigmoid(gs_ref[...].astype(F32)) * b1
                  + jax.nn.sigmoid(ga_ref[...].astype(F32)) * b2).astype(o_ref.dtype)


def _merge(y_ssd, y_att, w1, w2, proj2d, gs0, ga0):
    m, k1 = y_ssd.shape
    k2 = y_att.shape[1]
    n = w1.shape[1]
    tm, tn = 512, 512
    return pl.pallas_call(
        _merge_kernel,
        grid=(n // tn, m // tm),
        in_specs=[pl.BlockSpec((tm, k1), lambda j, i: (i, 0)),
                  pl.BlockSpec((tm, k2), lambda j, i: (i, 0)),
                  pl.BlockSpec((k1, tn), lambda j, i: (0, j)),
                  pl.BlockSpec((k2, tn), lambda j, i: (0, j)),
                  pl.BlockSpec((tm, tn), lambda j, i: (i, gs0 // tn + j)),
                  pl.BlockSpec((tm, tn), lambda j, i: (i, ga0 // tn + j))],
        out_specs=pl.BlockSpec((tm, tn), lambda j, i: (i, j)),
        out_shape=jax.ShapeDtypeStruct((m, n), BF16),
        compiler_params=_cparams(("arbitrary", "arbitrary")),
        name="branch_merge",
    )(y_ssd, y_att, w1, w2, proj2d, proj2d)


def _outproj_kernel(m_ref, w_ref, x_ref, gpost_ref, gt_ref, gpre_ref, sc_ref, sh_ref, wr_ref,
                    x1_ref, h2b_ref, lg_ref):
    for r in range(m_ref.shape[1] // OUTPROJ_SUBTILE):
        rows = slice(r * OUTPROJ_SUBTILE, (r + 1) * OUTPROJ_SUBTILE)
        y = jnp.dot(m_ref[0, rows, :], w_ref[...], preferred_element_type=F32)
        x1 = x_ref[0, rows, :] + gt_ref[0] * (_rms(y) * gpost_ref[...])
        h2 = (_rms(x1) * gpre_ref[...]) * (1.0 + sc_ref[0]) + sh_ref[0]
        x1_ref[0, rows, :] = x1
        h2b_ref[0, rows, :] = h2.astype(BF16)
        h_hi = h2.astype(BF16)
        h_lo = (h2 - h_hi.astype(F32)).astype(BF16)
        wr = wr_ref[...]
        w_hi = wr.astype(BF16)
        w_lo = (wr - w_hi.astype(F32)).astype(BF16)
        lg = (jnp.dot(h_hi, w_hi, preferred_element_type=F32)
              + (jnp.dot(h_hi, w_lo, preferred_element_type=F32)
                 + jnp.dot(h_lo, w_hi, preferred_element_type=F32)))
        lg_ref[:, rows] = lg.T[:lg_ref.shape[0], :]


def _outproj(merged, w_out, x, g_post, gt1, g_pre, sc2, sh2, w_router):
    bsz, s, d = x.shape
    ne = w_router.shape[1]
    w_router_p = jnp.pad(w_router, ((0, 0), (0, LANES - ne)))
    tm = OUTPROJ_TILE
    nt = s // tm
    row = pl.BlockSpec((1, tm, d), lambda i: (i // nt, i % nt, 0))
    vec = pl.BlockSpec((1, 1, d), lambda i: (i // nt, 0, 0))
    par = pl.BlockSpec((1, d), lambda i: (0, 0))
    return pl.pallas_call(
        _outproj_kernel,
        grid=(bsz * nt,),
        in_specs=[row, pl.BlockSpec((d, d), lambda i: (0, 0)), row, par, vec, par, vec, vec,
                  pl.BlockSpec((d, LANES), lambda i: (0, 0))],
        out_specs=[row, row, pl.BlockSpec((ne, tm), lambda i: (0, i))],
        out_shape=[jax.ShapeDtypeStruct((bsz, s, d), F32),
                   jax.ShapeDtypeStruct((bsz, s, d), BF16),
                   jax.ShapeDtypeStruct((ne, bsz * s), F32)],
        compiler_params=_cparams(("arbitrary",)),
        name="out_proj_norms",
    )(merged.reshape(bsz, s, d), w_out, x, g_post.reshape(1, d), gt1.reshape(bsz, 1, d),
      g_pre.reshape(1, d), sc2.reshape(bsz, 1, d), sh2.reshape(bsz, 1, d), w_router_p)


def _router_kernel(lg_ref, rb_ref, w_ref, rank_ref, cnt_ref, gsc):
    ne, tn = lg_ref.shape
    ng = N_EXPERT_GROUPS
    eg = ne // ng
    scores = jax.nn.sigmoid(lg_ref[...])
    biased = scores + rb_ref[...]
    i8 = lax.broadcasted_iota(jnp.int32, (eg, tn), 0)
    for g in range(ng):
        v = biased[g * eg:(g + 1) * eg, :]
        m1 = jnp.max(v, axis=0, keepdims=True)
        first = jnp.min(jnp.where(v == m1, i8, eg), axis=0, keepdims=True)
        m2 = jnp.max(jnp.where(i8 == first, -jnp.inf, v), axis=0, keepdims=True)
        gsc[g:g + 1, :] = m1 + m2
    gs = gsc[...]
    gi = lax.broadcasted_iota(jnp.int32, (ng, tn), 0)
    masked = []
    for g in range(ng):
        sg = gs[g:g + 1, :]
        ahead = (gs > sg) | ((gs == sg) & (gi < g))
        rank = jnp.sum(jnp.where(ahead, 1.0, 0.0), axis=0, keepdims=True)
        masked.append(jnp.where(rank < TOPK_GROUPS, biased[g * eg:(g + 1) * eg, :], -jnp.inf))
    cur = jnp.concatenate(masked, axis=0)
    sub = lax.broadcasted_iota(jnp.int32, (ne, tn), 0)
    sel = jnp.zeros((ne, tn), F32)
    for k in range(TOP_K):
        mx = jnp.max(cur, axis=0, keepdims=True)
        idx = jnp.min(jnp.where(cur == mx, sub, ne), axis=0, keepdims=True)
        hit = sub == idx
        sel = jnp.where(hit, 1.0, sel)
        cur = jnp.where(hit, -jnp.inf, cur)
    picked = sel * scores
    wsum = jnp.sum(picked, axis=0, keepdims=True)
    w_ref[...] = picked / wsum * ROUTED_SCALE
    ti = lax.broadcasted_iota(jnp.int32, (tn, tn), 0)
    tj = lax.broadcasted_iota(jnp.int32, (tn, tn), 1)
    before = jnp.where(ti < tj, 1.0, 0.0).astype(BF16)
    rank = jnp.dot(sel.astype(BF16), before, preferred_element_type=F32)
    rank_ref[...] = jnp.where(sel > 0.0, rank, NEG_BIG)
    cnt_ref[0] = jnp.broadcast_to(jnp.sum(sel, axis=1, keepdims=True), (ne, LANES))


def _router(logits_t, router_bias):
    ne, t = logits_t.shape
    tn = MOE_TILE
    tile = pl.BlockSpec((ne, tn), lambda i: (0, i))
    return pl.pallas_call(
        _router_kernel,
        grid=(t // tn,),
        in_specs=[tile, pl.BlockSpec((ne, 1), lambda i: (0, 0))],
        out_specs=[tile, tile, pl.BlockSpec((1, ne, LANES), lambda i: (i, 0, 0))],
        out_shape=[jax.ShapeDtypeStruct((ne, t), F32), jax.ShapeDtypeStruct((ne, t), F32),
                   jax.ShapeDtypeStruct((t // tn, ne, LANES), F32)],
        scratch_shapes=[pltpu.VMEM((N_EXPERT_GROUPS, tn), F32)],
        compiler_params=_cparams(("arbitrary",)),
        name="router_topk",
    )(logits_t, router_bias.reshape(ne, 1))


def _tile_rows(n_experts):
    return MOE_TILE * TOP_K + n_experts * SEG_ALIGN


def _segment_copies(src, dst, sem, n, src_off, dst_off, wait):
    off = jnp.int32(0)
    for p in SEG_CHUNKS:
        bit = n & p

        @pl.when(bit != 0)
        def _(p=p, off=off):
            cp = pltpu.make_async_copy(
                src.at[pl.ds(pl.multiple_of(src_off + off, SEG_ALIGN), p), :],
                dst.at[pl.ds(pl.multiple_of(dst_off + off, SEG_ALIGN), p), :], sem)
            if wait:
                cp.wait()
            else:
                cp.start()
        off = off + bit


def _for_each_segment(ne, fn):
    def body(e, carry):
        fn(e)
        return carry
    lax.fori_loop(0, ne, body, 0)


def _build_perm(pbuf, posb, ss_ref, c16_ref, tile, ne, value_row):
    tn = posb.shape[1]
    pbuf[...] = jnp.zeros_like(pbuf)
    rows16 = lax.broadcasted_iota(jnp.int32, (SEG_ALIGN, tn), 0)

    def per_expert(e):
        base = ss_ref[tile, e]
        groups = lax.shift_right_logical(c16_ref[tile, e], SEG_ALIGN.bit_length() - 1)
        prow = posb[pl.ds(e, 1), :]
        vrow = value_row(e)

        def per_group(g, carry):
            r0 = pl.multiple_of(base + g * SEG_ALIGN, SEG_ALIGN)
            hit = prow == (rows16 + r0).astype(F32)
            pbuf[pl.ds(r0, SEG_ALIGN), :] = jnp.where(hit, vrow, 0.0).astype(BF16)
            return carry
        lax.fori_loop(0, groups, per_group, 0)
    _for_each_segment(ne, per_expert)


def _dispatch_kernel(ss_ref, c16_ref, go_ref, ts_ref, tl_ref, rank_ref, ssv_ref, h_ref, xs_hbm,
                     pbuf, xs, posb, sem):
    i = pl.program_id(0)
    n = pl.num_programs(0)
    ne = rank_ref.shape[0]
    tn = rank_ref.shape[1]
    posb[...] = rank_ref[...] + ssv_ref[0]
    _build_perm(pbuf, posb, ss_ref, c16_ref, i, ne, lambda e: 1.0)

    def drain(tile):
        _for_each_segment(ne, lambda e: _segment_copies(
            xs, xs_hbm, sem, c16_ref[tile, e], ss_ref[tile, e], go_ref[tile, e], True))

    @pl.when(i > 0)
    def _():
        drain(i - 1)

    used = ss_ref[i, ne - 1] + c16_ref[i, ne - 1]
    def chunk(ch):
        rows = slice(ch * tn, (ch + 1) * tn)
        xs[rows, :] = jnp.dot(pbuf[rows, :], h_ref[...], preferred_element_type=F32).astype(BF16)

    for ch in range(TOP_K):
        chunk(ch)
    for ch in range(TOP_K, xs.shape[0] // tn):
        pl.when(ch * tn < used)(functools.partial(chunk, ch))

    _for_each_segment(ne, lambda e: _segment_copies(
        xs, xs_hbm, sem, c16_ref[i, e], ss_ref[i, e], go_ref[i, e], False))

    @pl.when(i == n - 1)
    def _():
        drain(i)
        xs[0:MOE_ROWS, :] = jnp.zeros((MOE_ROWS, xs.shape[1]), BF16)
        first_free = lax.shift_right_logical(ts_ref[ne], MOE_ROWS.bit_length() - 1)
        n_blocks = xs_hbm.shape[0] // MOE_ROWS
        for wait in (False, True):
            _for_each_segment(ne, lambda e: _segment_copies(
                xs, xs_hbm, sem, tl_ref[e], 0, ts_ref[e], wait))

            def free_block(b, carry, wait=wait):
                cp = pltpu.make_async_copy(
                    xs.at[pl.ds(0, MOE_ROWS), :],
                    xs_hbm.at[pl.ds(pl.multiple_of(b * MOE_ROWS, MOE_ROWS), MOE_ROWS), :], sem)
                if wait:
                    cp.wait()
                else:
                    cp.start()
                return carry
            lax.fori_loop(first_free, n_blocks, free_block, 0)


def _dispatch(seg_start, c16, goff, tail_start, tail_len, rank, h2b, n_rows):
    ne, t = rank.shape
    d = h2b.shape[1]
    tn = MOE_TILE
    rt = _tile_rows(ne)
    grid_spec = pltpu.PrefetchScalarGridSpec(
        num_scalar_prefetch=5,
        grid=(t // tn,),
        in_specs=[pl.BlockSpec((ne, tn), lambda i, *_: (0, i)),
                  pl.BlockSpec((1, ne, 1), lambda i, *_: (i, 0, 0)),
                  pl.BlockSpec((tn, d), lambda i, *_: (i, 0))],
        out_specs=pl.BlockSpec(memory_space=pl.ANY),
        scratch_shapes=[pltpu.VMEM((rt, tn), BF16), pltpu.VMEM((rt, d), BF16),
                        pltpu.VMEM((ne, tn), F32), pltpu.SemaphoreType.DMA],
    )
    return pl.pallas_call(
        _dispatch_kernel,
        grid_spec=grid_spec,
        out_shape=jax.ShapeDtypeStruct((n_rows, d), BF16),
        compiler_params=_cparams(("arbitrary",)),
        name="moe_dispatch",
    )(seg_start, c16, goff, tail_start, tail_len, rank,
      seg_start.astype(F32).reshape(t // tn, ne, 1), h2b)


def _combine_kernel(ss_ref, c16_ref, go_ref, rank_ref, w_ref, ssv_ref, y_hbm, o_ref,
                    pbuf, ys, posb, sem):
    i = pl.program_id(0)
    ne = rank_ref.shape[0]
    tn = rank_ref.shape[1]
    ys[...] = jnp.zeros_like(ys)
    _for_each_segment(ne, lambda e: _segment_copies(
        y_hbm, ys, sem, c16_ref[i, e], go_ref[i, e], ss_ref[i, e], False))
    posb[...] = rank_ref[...] + ssv_ref[0]
    _build_perm(pbuf, posb, ss_ref, c16_ref, i, ne, lambda e: w_ref[pl.ds(e, 1), :])
    _for_each_segment(ne, lambda e: _segment_copies(
        y_hbm, ys, sem, c16_ref[i, e], go_ref[i, e], ss_ref[i, e], True))
    used = ss_ref[i, ne - 1] + c16_ref[i, ne - 1]
    def chunk(ch):
        rows = slice(ch * tn, (ch + 1) * tn)
        return lax.dot_general(pbuf[rows, :], ys[rows, :], (((0,), (0,)), ((), ())),
                               preferred_element_type=F32)

    always = TOP_K
    acc = chunk(0)
    for ch in range(1, always):
        acc = acc + chunk(ch)
    o_ref[...] = acc
    for ch in range(always, ys.shape[0] // tn):
        @pl.when(ch * tn < used)
        def _(ch=ch):
            o_ref[...] += chunk(ch)


def _combine(seg_start, c16, goff, rank, selw, y_rows):
    ne, t = rank.shape
    d = y_rows.shape[1]
    tn = MOE_TILE
    rt = _tile_rows(ne)
    tile = pl.BlockSpec((ne, tn), lambda i, *_: (0, i))
    grid_spec = pltpu.PrefetchScalarGridSpec(
        num_scalar_prefetch=3,
        grid=(t // tn,),
        in_specs=[tile, tile, pl.BlockSpec((1, ne, 1), lambda i, *_: (i, 0, 0)),
                  pl.BlockSpec(memory_space=pl.ANY)],
        out_specs=pl.BlockSpec((tn, d), lambda i, *_: (i, 0)),
        scratch_shapes=[pltpu.VMEM((rt, tn), BF16), pltpu.VMEM((rt, d), BF16),
                        pltpu.VMEM((ne, tn), F32), pltpu.SemaphoreType.DMA],
    )
    return pl.pallas_call(
        _combine_kernel,
        grid_spec=grid_spec,
        out_shape=jax.ShapeDtypeStruct((t, d), F32),
        compiler_params=_cparams(("arbitrary",)),
        name="moe_combine",
    )(seg_start, c16, goff, rank, selw, seg_start.astype(F32).reshape(t // tn, ne, 1), y_rows)


def _expert_kernel(be_ref, nv_ref, x_ref, wg_ref, wu_ref, wd_ref, o_ref, wg, wu, wd):
    i = pl.program_id(0)

    @pl.when(nv_ref[i] == 0)
    def _():
        o_ref[...] = jnp.zeros_like(o_ref)

    @pl.when(nv_ref[i] > 0)
    def _():
        @pl.when((i == 0) | (be_ref[i] != be_ref[jnp.maximum(i - 1, 0)]))
        def _():
            wg[...] = wg_ref[0].astype(BF16)
            wu[...] = wu_ref[0].astype(BF16)
            wd[...] = wd_ref[0].astype(BF16)

        x = x_ref[...]
        gate = jnp.dot(x, wg[...], preferred_element_type=F32)
        up = jnp.dot(x, wu[...], preferred_element_type=F32)
        act = (_silu(gate) * up).astype(BF16)
        o_ref[...] = jnp.dot(act, wd[...], preferred_element_type=F32).astype(o_ref.dtype)


def _expert_ffn(blk_exp, blk_valid, x_sorted, w_gate, w_up, w_down):
    n_rows, d = x_sorted.shape
    f = w_gate.shape[2]
    rows = MOE_ROWS
    grid_spec = pltpu.PrefetchScalarGridSpec(
        num_scalar_prefetch=2,
        grid=(n_rows // rows,),
        in_specs=[pl.BlockSpec((rows, d), lambda i, be, nv: (i, 0)),
                  pl.BlockSpec((1, d, f), lambda i, be, nv: (be[i], 0, 0)),
                  pl.BlockSpec((1, d, f), lambda i, be, nv: (be[i], 0, 0)),
                  pl.BlockSpec((1, f, d), lambda i, be, nv: (be[i], 0, 0))],
        out_specs=pl.BlockSpec((rows, d), lambda i, be, nv: (i, 0)),
        scratch_shapes=[pltpu.VMEM((d, f), BF16), pltpu.VMEM((d, f), BF16), pltpu.VMEM((f, d), BF16)],
    )
    return pl.pallas_call(
        _expert_kernel,
        grid_spec=grid_spec,
        out_shape=jax.ShapeDtypeStruct((n_rows, d), BF16),
        compiler_params=_cparams(("arbitrary",)),
        name="expert_ffn",
    )(blk_exp, blk_valid, x_sorted, w_gate, w_up, w_down)


def _shared_kernel(x_ref, wg_ref, wu_ref, wd_ref, r_ref, x1_ref, gt_ref, g_ref, o_ref):
    x = x_ref[0]
    gate = jnp.dot(x, wg_ref[...], preferred_element_type=F32)
    up = jnp.dot(x, wu_ref[...], preferred_element_type=F32)
    act = (_silu(gate) * up).astype(BF16)
    y = r_ref[0] + jnp.dot(act, wd_ref[...], preferred_element_type=F32)
    o_ref[0] = x1_ref[0] + gt_ref[0] * (_rms(y) * g_ref[...])


def _shared_final(h2b, wg, wu, wd, routed, x1, gt2, g_post):
    bsz, s, d = x1.shape
    f = wg.shape[1]
    tm = 512
    nt = s // tm
    row = pl.BlockSpec((1, tm, d), lambda i: (i // nt, i % nt, 0))
    full = lambda shape: pl.BlockSpec(shape, lambda i: (0, 0))
    return pl.pallas_call(
        _shared_kernel,
        grid=(bsz * nt,),
        in_specs=[row, full((d, f)), full((d, f)), full((f, d)), row, row,
                  pl.BlockSpec((1, 1, d), lambda i: (i // nt, 0, 0)), full((1, d))],
        out_specs=row,
        out_shape=jax.ShapeDtypeStruct((bsz, s, d), F32),
        compiler_params=_cparams(("arbitrary",)),
        name="shared_ffn_final",
    )(h2b, wg, wu, wd, routed.reshape(bsz, s, d), x1, gt2.reshape(bsz, 1, d), g_post.reshape(1, d))


def _dispatch_plan(counts, n_blocks):
    n_tiles, ne = counts.shape
    c16 = (counts + SEG_ALIGN - 1) // SEG_ALIGN * SEG_ALIGN
    seg_start = jnp.cumsum(c16, axis=1) - c16
    tot = jnp.sum(c16, axis=0)
    padded = (tot + MOE_ROWS - 1) // MOE_ROWS * MOE_ROWS
    pad_end = jnp.cumsum(padded)
    pad_start = pad_end - padded
    goff = pad_start[None, :] + jnp.cumsum(c16, axis=0) - c16
    blk_row = jnp.arange(n_blocks, dtype=jnp.int32) * MOE_ROWS
    blk_exp = jnp.sum(pad_end[None, :] <= blk_row[:, None], axis=1)
    blk_exp = jnp.minimum(blk_exp, ne - 1).astype(jnp.int32)
    blk_valid = jnp.clip(pad_start[blk_exp] + tot[blk_exp] - blk_row, 0, MOE_ROWS)
    blk_valid = jnp.where(blk_row < pad_end[-1], blk_valid, 0).astype(jnp.int32)
    i32 = lambda v: v.astype(jnp.int32)
    tail_start = jnp.concatenate([pad_start + tot, pad_end[-1:]])
    return (i32(seg_start), i32(c16), i32(goff), i32(tail_start), i32(padded - tot),
            blk_exp, blk_valid)


def kernel(x, c, positions, w_ada, b_ada, g_pre_mix, g_post_mix, g_pre_ffn, g_post_ffn, w_in,
           conv_w, conv_b, dt_bias, a_log, d_skip, ssd_norm_w, w_ssd_br, w_attn_br, w_out,
           w_router, router_bias, w_gate, w_up, w_down, ws_gate, ws_up, ws_down):
    bsz, s, d = x.shape
    t = bsz * s
    for l in range(w_ada.shape[0]):
        d_inner = w_ssd_br.shape[1]
        att_w = w_attn_br.shape[1]
        n_ssd_heads = dt_bias.shape[1]
        conv_dim = conv_w.shape[2]
        n_heads = att_w // ATT_HEAD_DIM
        n_experts = w_router.shape[2]

        mod = _ada(c, w_ada[l], b_ada[l])
        sh1, sc1, gt1, sh2, sc2, gt2 = jnp.split(mod, 6, axis=-1)

        o_dt = d_inner + conv_dim
        o_q = o_dt + n_ssd_heads
        w_in_t = jnp.swapaxes(w_in, 1, 2)
        h1 = _prenorm(x, g_pre_mix[l], sc1, sh1).reshape(t, d)
        proj_s = _matmul_wt(h1, w_in_t, l, 0, o_dt, 1024, 1024, BF16, "in_proj_ssd")
        proj_a = _matmul_wt(h1, w_in_t, l, o_q, w_in.shape[2] - o_q, 1024, 1024, BF16, "in_proj_att")
        dt_t = _dt_proj(h1, w_in_t, l, o_dt, n_ssd_heads, 1024)
        k_off = att_w
        v_off = k_off + att_w
        gs_off = v_off + att_w
        ga_off = gs_off + d

        y_ssd = _ssd(proj_s.reshape(bsz, s, o_dt), dt_t, conv_w[l], conv_b[l], dt_bias[l], a_log[l],
                     d_skip[l], ssd_norm_w[l], d_inner)

        cos2, sin2 = _rope_tables(positions)
        y_att = _attention(proj_a.reshape(bsz, s, -1), cos2, sin2, 0, k_off // ATT_HEAD_DIM,
                           v_off // ATT_HEAD_DIM, n_heads)

        merged = _merge(y_ssd.reshape(t, d_inner), y_att.reshape(t, att_w),
                        w_ssd_br[l].astype(BF16), w_attn_br[l].astype(BF16), proj_a, gs_off, ga_off)
        x1, h2b, logits_t = _outproj(merged, w_out[l].astype(BF16), x, g_post_mix[l], gt1,
                                         g_pre_ffn[l], sc2, sh2, w_router[l])

        selw, rank, cnt = _router(logits_t, router_bias[l])
        n_tiles = t // MOE_TILE
        n_blocks = -(-(t * TOP_K + n_tiles * n_experts * (SEG_ALIGN - 1)) // MOE_ROWS) + n_experts
        seg_start, c16, goff, tail_start, tail_len, blk_exp, blk_valid = _dispatch_plan(
            cnt[:, :, 0].astype(jnp.int32), n_blocks)
        x_sorted = _dispatch(seg_start, c16, goff, tail_start, tail_len, rank, h2b.reshape(t, d),
                             n_blocks * MOE_ROWS)
        y_rows = _expert_ffn(blk_exp, blk_valid, x_sorted, w_gate[l], w_up[l], w_down[l])
        routed = _combine(seg_start, c16, goff, rank, selw, y_rows)
        x = _shared_final(h2b, ws_gate[l].astype(BF16), ws_up[l].astype(BF16),
                          ws_down[l].astype(BF16), routed, x1, gt2, g_post_ffn[l])
    return x
```

```python
import functools
import math

import jax
import jax.numpy as jnp
from jax import lax
from jax.experimental import pallas as pl
from jax.experimental.pallas import tpu as pltpu

F32 = jnp.float32
BF16 = jnp.bfloat16

NORM_EPS = 1e-6
ROPE_THETA = 10000.0

SSD_HEAD_DIM = 64
SSD_GROUPS = 8
SSD_STATE = 128
SSD_CONV = 4
SSD_CHUNK = 128
SSD_GROUPS_PER_STEP = 8

ATT_HEAD_DIM = 128
MOBA_BLOCK = 256
MOBA_TOPK = 3

TOP_K = 8
N_EXPERT_GROUPS = 8
TOPK_GROUPS = 4
ROUTED_SCALE = 2.5
MOE_ROWS = 512
MOE_TILE = 512
SEG_ALIGN = 16
SEG_CHUNKS = (512, 256, 128, 64, 32, 16)

MERGE_TM, MERGE_TN = 512, 512
OUTPROJ_TILE = 512
OUTPROJ_SUBTILE = 256

LANES = 128
SUBLANES = 8
VMEM_LIMIT = 56 * 1024 * 1024
NEG_BIG = -1e30


def _cparams(sem):
    return pltpu.CompilerParams(dimension_semantics=sem, vmem_limit_bytes=VMEM_LIMIT)


def _silu(x):
    return x * jax.nn.sigmoid(x)


def _rms(x):
    return x * lax.rsqrt(jnp.mean(x * x, axis=-1, keepdims=True) + NORM_EPS)


def _split3(a):
    a1 = a.astype(BF16)
    r1 = a - a1.astype(F32)
    a2 = r1.astype(BF16)
    a3 = (r1 - a2.astype(F32)).astype(BF16)
    return a1, a2, a3


def _dot_exact_rhs(a, e_bf16):
    out = None
    for p in _split3(a):
        t = jnp.dot(p, e_bf16, preferred_element_type=F32)
        out = t if out is None else out + t
    return out


def _dot_nt(a, b):
    return lax.dot_general(a, b, (((1,), (1,)), ((), ())), preferred_element_type=F32)


def _dot_nt_x3(a, b):
    a1 = a.astype(BF16)
    a2 = (a - a1.astype(F32)).astype(BF16)
    b1 = b.astype(BF16)
    b2 = (b - b1.astype(F32)).astype(BF16)
    return _dot_nt(a1, b1) + (_dot_nt(a1, b2) + _dot_nt(a2, b1))


def _ada_kernel(c_ref, w_ref, b_ref, o_ref):
    cond = _silu(c_ref[...])
    o_ref[...] = jnp.dot(cond.astype(BF16), w_ref[...].astype(BF16),
                         preferred_element_type=F32) + b_ref[...]


def _ada(c, w_ada, b_ada):
    bsz, d = c.shape
    n = w_ada.shape[1]
    tn = 1024
    return pl.pallas_call(
        _ada_kernel,
        grid=(n // tn,),
        in_specs=[pl.BlockSpec((bsz, d), lambda j: (0, 0)),
                  pl.BlockSpec((d, tn), lambda j: (0, j)),
                  pl.BlockSpec((1, tn), lambda j: (0, j))],
        out_specs=pl.BlockSpec((bsz, tn), lambda j: (0, j)),
        out_shape=jax.ShapeDtypeStruct((bsz, n), F32),
        compiler_params=_cparams(("arbitrary",)),
        name="ada_mod",
    )(c, w_ada, b_ada.reshape(1, n))


def _prenorm_kernel(x_ref, g_ref, sc_ref, sh_ref, o_ref):
    y = _rms(x_ref[0]) * g_ref[...]
    o_ref[0] = (y * (1.0 + sc_ref[0]) + sh_ref[0]).astype(o_ref.dtype)


def _prenorm(x, g, sc, sh):
    bsz, s, d = x.shape
    ts = 512
    vec = pl.BlockSpec((1, 1, d), lambda b, i: (b, 0, 0))
    return pl.pallas_call(
        _prenorm_kernel,
        grid=(bsz, s // ts),
        in_specs=[pl.BlockSpec((1, ts, d), lambda b, i: (b, i, 0)),
                  pl.BlockSpec((1, d), lambda b, i: (0, 0)), vec, vec],
        out_specs=pl.BlockSpec((1, ts, d), lambda b, i: (b, i, 0)),
        out_shape=jax.ShapeDtypeStruct((bsz, s, d), BF16),
        compiler_params=_cparams(("arbitrary", "arbitrary")),
        name="prenorm",
    )(x, g.reshape(1, d), sc.reshape(bsz, 1, d), sh.reshape(bsz, 1, d))


def _mm_kernel(a_ref, w_ref, o_ref):
    o_ref[...] = jnp.dot(a_ref[...], w_ref[...], preferred_element_type=F32).astype(o_ref.dtype)


def _matmul(a, w, out_dtype, tm, tn, name):
    m, k = a.shape
    n = w.shape[1]
    return pl.pallas_call(
        _mm_kernel,
        grid=(n // tn, m // tm),
        in_specs=[pl.BlockSpec((tm, k), lambda j, i: (i, 0)),
                  pl.BlockSpec((k, tn), lambda j, i: (0, j))],
        out_specs=pl.BlockSpec((tm, tn), lambda j, i: (i, j)),
        out_shape=jax.ShapeDtypeStruct((m, n), out_dtype),
        compiler_params=_cparams(("arbitrary", "arbitrary")),
        name=name,
    )(a, w)


def _mm_wt_kernel(shift, a_ref, *refs):
    w_ref = refs[0]
    o_ref, wb = refs[-2:]

    @pl.when(pl.program_id(1) == 0)
    def _():
        w = w_ref[0]
        if shift:
            w = jnp.concatenate([w[shift:], refs[1][0]], axis=0)
        wb[...] = w.T.astype(BF16)

    o_ref[...] = jnp.dot(a_ref[...], wb[...], preferred_element_type=F32).astype(o_ref.dtype)


def _dt_kernel(a_ref, w_ref, o_ref):
    o_ref[...] = _dot_nt(w_ref[0].astype(BF16), a_ref[...])


def _dt_proj(a, wt_stack, layer, row0, nh, tm):
    m, k = a.shape
    assert row0 % nh == 0
    return pl.pallas_call(
        _dt_kernel,
        grid=(m // tm,),
        in_specs=[pl.BlockSpec((tm, k), lambda i: (i, 0)),
                  pl.BlockSpec((1, nh, k), lambda i: (layer, row0 // nh, 0))],
        out_specs=pl.BlockSpec((nh, tm), lambda i: (0, i)),
        out_shape=jax.ShapeDtypeStruct((nh, m), F32),
        compiler_params=_cparams(("arbitrary",)),
        name="in_proj_dt",
    )(a, wt_stack)


def _matmul_wt(a, wt_stack, layer, row0, n, tm, tn, out_dtype, name):
    m, k = a.shape
    shift = row0 % tn
    base = row0 - shift
    assert n % tn == 0 and shift % SUBLANES == 0
    a_spec = pl.BlockSpec((tm, k), lambda j, i: (i, 0))
    in_specs = [a_spec, pl.BlockSpec((1, tn, k), lambda j, i: (layer, base // tn + j, 0))]
    args = [a, wt_stack]
    if shift:
        assert (base + tn) % shift == 0 and tn % shift == 0
        in_specs.append(pl.BlockSpec((1, shift, k), lambda j, i: (layer, (base + (j + 1) * tn) // shift, 0)))
        args.append(wt_stack)
    return pl.pallas_call(
        functools.partial(_mm_wt_kernel, shift),
        grid=(n // tn, m // tm),
        in_specs=in_specs,
        out_specs=pl.BlockSpec((tm, tn), lambda j, i: (i, j)),
        out_shape=jax.ShapeDtypeStruct((m, n), out_dtype),
        scratch_shapes=[pltpu.VMEM((k, tn), BF16)],
        compiler_params=_cparams(("arbitrary", "arbitrary")),
        name=name,
    )(*args)


def _softplus(x):
    return jnp.maximum(x, 0.0) + jnp.log(1.0 + jnp.exp(-jnp.abs(x)))


def _ssd_kernel(x_ref, b_ref, c_ref, z_ref, dt_ref, cwx_ref, cwb_ref, cwc_ref,
                cbx_ref, cbb_ref, cbc_ref, dtb_ref, alog_ref, dsk_ref, nw_ref,
                o_ref, cbuf, state):
    q = SSD_CHUNK
    gps = dtb_ref.shape[0]
    gw = x_ref.shape[2] // gps
    nst = b_ref.shape[2] // gps
    hg = gw // SSD_HEAD_DIM
    pad = SUBLANES

    @pl.when(pl.program_id(2) == 0)
    def _():
        cbuf[:, 0:pad, :] = jnp.zeros((gps, pad, gw + 2 * nst), F32)
        state[...] = jnp.zeros_like(state)

    hrow = lax.broadcasted_iota(jnp.int32, (LANES, q), 0)
    ri = lax.broadcasted_iota(jnp.int32, (q, q), 0)
    ci = lax.broadcasted_iota(jnp.int32, (q, q), 1)
    triu = jnp.where(ri <= ci, 1.0, 0.0).astype(BF16)
    er = lax.broadcasted_iota(jnp.int32, (LANES, gw), 0)
    ec = lax.broadcasted_iota(jnp.int32, (LANES, gw), 1)
    e_ch = jnp.where(ec // SSD_HEAD_DIM == er, 1.0, 0.0).astype(BF16)
    er2 = lax.broadcasted_iota(jnp.int32, (LANES, hg * q), 0)
    ec2 = lax.broadcasted_iota(jnp.int32, (LANES, hg * q), 1)
    e_t = jnp.where(ec2 // q == er2, 1.0, 0.0).astype(BF16)
    ri2 = lax.broadcasted_iota(jnp.int32, (q, hg * q), 0)
    ci2 = lax.broadcasted_iota(jnp.int32, (q, hg * q), 1) % q
    first_head = lax.broadcasted_iota(jnp.int32, (1, LANES), 1) < SSD_HEAD_DIM

    for u in range(gps):
        xc = slice(u * gw, (u + 1) * gw)
        nc = slice(u * nst, (u + 1) * nst)
        xin = jnp.concatenate([x_ref[0, :, xc], b_ref[0, :, nc], c_ref[0, :, nc]],
                              axis=-1).astype(F32)
        cbuf[u, pad:pad + q, :] = xin
        w = jnp.concatenate([cwx_ref[:, xc], cwb_ref[:, nc], cwc_ref[:, nc]], axis=-1)
        bias = jnp.concatenate([cbx_ref[:, xc], cbb_ref[:, nc], cbc_ref[:, nc]], axis=-1)
        acc = bias + w[SSD_CONV - 1:SSD_CONV, :] * xin
        for s in range(1, SSD_CONV):
            acc = acc + w[SSD_CONV - 1 - s:SSD_CONV - s, :] * cbuf[u, pad - s:pad - s + q, :]
        cbuf[u, pad - (SSD_CONV - 1):pad, :] = cbuf[u, pad + q - (SSD_CONV - 1):pad + q, :]
        xbc = _silu(acc)
        xs = xbc[:, :gw]
        bm = xbc[:, gw:gw + nst].astype(BF16)
        cm = xbc[:, gw + nst:].astype(BF16)

        dt_raw = jnp.concatenate([dt_ref[u * hg:(u + 1) * hg, :], jnp.zeros((LANES - hg, q), F32)],
                                 axis=0)
        dt_r = jnp.where(hrow < hg, _softplus(dt_raw + dtb_ref[u]), 0.0)
        a_r = dt_r * (-math.log2(math.e) * jnp.exp(alog_ref[u]))
        acum_r = _dot_exact_rhs(a_r, triu)
        acum_c = acum_r.T
        dt_c = dt_r.T

        dt_full = _dot_exact_rhs(dt_c, e_ch)
        acum_full = _dot_exact_rhs(acum_c, e_ch)
        acum_colb = _dot_exact_rhs(acum_c, e_t)
        acum_rowb = jnp.concatenate(
            [jnp.broadcast_to(acum_r[h:h + 1, :], (q, q)) for h in range(hg)], axis=-1)

        lmat = jnp.exp2(jnp.where(ri2 >= ci2, acum_colb - acum_rowb, NEG_BIG))
        cb = _dot_nt(cm, bm)
        mcat = (jnp.concatenate([cb] * hg, axis=-1) * lmat).astype(BF16)
        xdt = xs * dt_full
        y_pairs = []
        for pr in range(gw // LANES):
            xp = xdt[:, pr * LANES:(pr + 1) * LANES]
            xbd = jnp.concatenate([jnp.where(first_head, xp, 0.0).astype(BF16),
                                   jnp.where(first_head, 0.0, xp).astype(BF16)], axis=0)
            y_pairs.append(jnp.dot(mcat[:, pr * 2 * q:(pr + 1) * 2 * q], xbd,
                                   preferred_element_type=F32))
        y = jnp.concatenate(y_pairs, axis=-1)

        st = state[u]
        y = y + jnp.dot(cm, st.astype(BF16), preferred_element_type=F32) * jnp.exp2(acum_full)
        acum_last = acum_full[q - 1:q, :]
        xw = (xdt * jnp.exp2(acum_last - acum_full)).astype(BF16)
        bt = xbc[:, gw:gw + nst].T.astype(BF16)
        state[u] = st * jnp.exp2(acum_last) + jnp.dot(bt, xw, preferred_element_type=F32)

        y = y + xs * dsk_ref[:, xc]
        y = y * _silu(z_ref[0, :, xc].astype(F32))
        o_ref[0, :, xc] = (_rms(y) * nw_ref[:, xc]).astype(o_ref.dtype)


def _ssd(proj, dt_rows, conv_w, conv_b, dt_bias, a_log, d_skip, norm_w, d_inner):
    bsz, s, _ = proj.shape
    g = SSD_GROUPS
    gps = SSD_GROUPS_PER_STEP
    gw = d_inner // g
    hg = gw // SSD_HEAD_DIM
    nst = SSD_STATE
    q = SSD_CHUNK
    nc = s // q
    bw, bn = gps * gw, gps * nst
    x0 = d_inner // bw
    b0 = (2 * d_inner) // bn
    c0 = (2 * d_inner + g * nst) // bn
    cwb0 = d_inner // bn
    cwc0 = (d_inner + g * nst) // bn
    conv_b2 = conv_b.reshape(1, -1)
    head_pad = lambda v: jnp.pad(v.reshape(g, hg), ((0, 0), (0, LANES - hg))).reshape(g, LANES, 1)
    return pl.pallas_call(
        _ssd_kernel,
        grid=(bsz, g // gps, nc),
        in_specs=[
            pl.BlockSpec((1, q, bw), lambda b, gi, c: (b, c, x0 + gi)),
            pl.BlockSpec((1, q, bn), lambda b, gi, c: (b, c, b0 + gi)),
            pl.BlockSpec((1, q, bn), lambda b, gi, c: (b, c, c0 + gi)),
            pl.BlockSpec((1, q, bw), lambda b, gi, c: (b, c, gi)),
            pl.BlockSpec((gps * hg, q), lambda b, gi, c: (gi, b * nc + c)),
            pl.BlockSpec((SSD_CONV, bw), lambda b, gi, c: (0, gi)),
            pl.BlockSpec((SSD_CONV, bn), lambda b, gi, c: (0, cwb0 + gi)),
            pl.BlockSpec((SSD_CONV, bn), lambda b, gi, c: (0, cwc0 + gi)),
            pl.BlockSpec((1, bw), lambda b, gi, c: (0, gi)),
            pl.BlockSpec((1, bn), lambda b, gi, c: (0, cwb0 + gi)),
            pl.BlockSpec((1, bn), lambda b, gi, c: (0, cwc0 + gi)),
            pl.BlockSpec((gps, LANES, 1), lambda b, gi, c: (gi, 0, 0)),
            pl.BlockSpec((gps, LANES, 1), lambda b, gi, c: (gi, 0, 0)),
            pl.BlockSpec((1, bw), lambda b, gi, c: (0, gi)),
            pl.BlockSpec((1, bw), lambda b, gi, c: (0, gi)),
        ],
        out_specs=pl.BlockSpec((1, q, bw), lambda b, gi, c: (b, c, gi)),
        out_shape=jax.ShapeDtypeStruct((bsz, s, d_inner), BF16),
        scratch_shapes=[pltpu.VMEM((gps, SUBLANES + q, gw + 2 * nst), F32),
                        pltpu.VMEM((gps, nst, gw), F32)],
        compiler_params=_cparams(("arbitrary", "arbitrary", "arbitrary")),
        name="ssd_scan",
    )(proj, proj, proj, proj, dt_rows, conv_w, conv_w, conv_w, conv_b2, conv_b2, conv_b2,
      head_pad(dt_bias), head_pad(a_log),
      jnp.repeat(d_skip, SSD_HEAD_DIM).reshape(1, d_inner), norm_w.reshape(1, d_inner))


def _rope_kernel(pos_ref, inv_ref, cos_ref, sin_ref):
    ang = pos_ref[...].astype(F32) * inv_ref[...]
    lane = lax.broadcasted_iota(jnp.int32, ang.shape, 1)
    cos_ref[...] = jnp.cos(ang)
    sin_ref[...] = jnp.where(lane < ATT_HEAD_DIM // 2, -1.0, 1.0) * jnp.sin(ang)


def _rope_tables(positions):
    t = positions.size
    half = ATT_HEAD_DIM // 2
    inv = 1.0 / (ROPE_THETA ** (jnp.arange(half, dtype=F32) / half))
    inv2 = jnp.concatenate([inv, inv]).reshape(1, ATT_HEAD_DIM)
    tt = min(2048, t)
    return pl.pallas_call(
        _rope_kernel,
        grid=(t // tt,),
        in_specs=[pl.BlockSpec((tt, 1), lambda i: (i, 0)),
                  pl.BlockSpec((1, ATT_HEAD_DIM), lambda i: (0, 0))],
        out_specs=[pl.BlockSpec((tt, ATT_HEAD_DIM), lambda i: (i, 0))] * 2,
        out_shape=[jax.ShapeDtypeStruct((t, ATT_HEAD_DIM), F32)] * 2,
        compiler_params=_cparams(("arbitrary",)),
        name="rope_tables",
    )(positions.reshape(t, 1), inv2)


def _rot(x, cos2, sin2):
    return x * cos2 + pltpu.roll(x, ATT_HEAD_DIM // 2, axis=1) * sin2


def _attn_kernel(q_ref, k_ref, v_ref, cos_ref, sin_ref, o_ref, krot, vt, kmean):
    blk = MOBA_BLOCK
    nb = k_ref.shape[1] // blk
    log2e_scale = ATT_HEAD_DIM ** -0.5 * math.log2(math.e)

    kmean[...] = jnp.zeros_like(kmean)
    for j in range(nb):
        rows = slice(j * blk, (j + 1) * blk)
        kr = _rot(k_ref[0, rows, :].astype(F32), cos_ref[0, rows, :], sin_ref[0, rows, :])
        krot[rows, :] = kr.astype(BF16)
        kmean[j:j + 1, :] = jnp.mean(kr, axis=0, keepdims=True)
        vt[:, rows] = v_ref[0, rows, :].astype(F32).T.astype(BF16)
    km = kmean[...]

    ki = lax.broadcasted_iota(jnp.int32, (blk, blk), 0)
    qj = lax.broadcasted_iota(jnp.int32, (blk, blk), 1)
    sub = lax.broadcasted_iota(jnp.int32, (km.shape[0], blk), 0)
    for qi in range(nb):
        qrows = slice(qi * blk, (qi + 1) * blk)
        qr = _rot(q_ref[0, qrows, :].astype(F32), cos_ref[0, qrows, :], sin_ref[0, qrows, :])
        qs = (qr * log2e_scale).T.astype(BF16)
        gate = _dot_nt_x3(km, qr)
        scores = [jnp.where(ki <= qj, jnp.dot(krot[qrows, :], qs, preferred_element_type=F32), NEG_BIG)]
        for j in range(qi):
            gj = gate[j:j + 1, :]
            ahead = (gate > gj) | ((gate == gj) & (sub < j))
            rank = jnp.sum(jnp.where((sub < qi) & ahead, 1.0, 0.0), axis=0, keepdims=True)
            bias = jnp.where(rank < MOBA_TOPK, 0.0, NEG_BIG)
            scores.append(jnp.dot(krot[j * blk:(j + 1) * blk, :], qs,
                                  preferred_element_type=F32) + bias)
        m = scores[0]
        for sj in scores[1:]:
            m = jnp.maximum(m, sj)
        m = jnp.max(m, axis=0, keepdims=True)
        l = None
        acc = None
        for j, sj in enumerate(scores):
            src = qi if j == 0 else j - 1
            p = jnp.exp2(sj - m)
            lj = jnp.sum(p, axis=0, keepdims=True)
            aj = jnp.dot(vt[:, src * blk:(src + 1) * blk], p.astype(BF16),
                         preferred_element_type=F32)
            l = lj if l is None else l + lj
            acc = aj if acc is None else acc + aj
        o_ref[0, qrows, :] = (acc / l).T.astype(o_ref.dtype)


def _attention(proj, cos2, sin2, q0, k0, v0, n_heads):
    bsz, s, _ = proj.shape
    dh = ATT_HEAD_DIM
    col = lambda c0: pl.BlockSpec((1, s, dh), lambda b, h: (b, 0, c0 + h))
    tab = pl.BlockSpec((1, s, dh), lambda b, h: (b, 0, 0))
    return pl.pallas_call(
        _attn_kernel,
        grid=(bsz, n_heads),
        in_specs=[col(q0), col(k0), col(v0), tab, tab],
        out_specs=pl.BlockSpec((1, s, dh), lambda b, h: (b, 0, h)),
        out_shape=jax.ShapeDtypeStruct((bsz, s, n_heads * dh), BF16),
        scratch_shapes=[pltpu.VMEM((s, dh), BF16), pltpu.VMEM((dh, s), BF16),
                        pltpu.VMEM((-(-(s // MOBA_BLOCK) // SUBLANES) * SUBLANES, dh), F32)],
        compiler_params=_cparams(("arbitrary", "arbitrary")),
        name="moba_attn",
    )(proj, proj, proj, cos2.reshape(bsz, s, dh), sin2.reshape(bsz, s, dh))


def _merge_kernel(ys_ref, ya_ref, w1_ref, w2_ref, gs_ref, ga_ref, o_ref):
    b1 = jnp.dot(ys_ref[...], w1_ref[...], preferred_element_type=F32)
    b2 = jnp.dot(ya_ref[...], w2_ref[...], preferred_element_type=F32)
    o_ref[...] = (jax.nn.sigmoid(gs_ref[...].astype(F32)) * b1
                  + jax.nn.sigmoid(ga_ref[...].astype(F32)) * b2).astype(o_ref.dtype)


def _merge(y_ssd, y_att, w1, w2, proj2d, gs0, ga0):
    m, k1 = y_ssd.shape
    k2 = y_att.shape[1]
    n = w1.shape[1]
    tm, tn = MERGE_TM, MERGE_TN
    return pl.pallas_call(
        _merge_kernel,
        grid=(n // tn, m // tm),
        in_specs=[pl.BlockSpec((tm, k1), lambda j, i: (i, 0)),
                  pl.BlockSpec((tm, k2), lambda j, i: (i, 0)),
                  pl.BlockSpec((k1, tn), lambda j, i: (0, j)),
                  pl.BlockSpec((k2, tn), lambda j, i: (0, j)),
                  pl.BlockSpec((tm, tn), lambda j, i: (i, gs0 // tn + j)),
                  pl.BlockSpec((tm, tn), lambda j, i: (i, ga0 // tn + j))],
        out_specs=pl.BlockSpec((tm, tn), lambda j, i: (i, j)),
        out_shape=jax.ShapeDtypeStruct((m, n), BF16),
        compiler_params=_cparams(("arbitrary", "arbitrary")),
        name="branch_merge",
    )(y_ssd, y_att, w1, w2, proj2d, proj2d)


def _outproj_kernel(m_ref, w_ref, x_ref, gpost_ref, gt_ref, gpre_ref, sc_ref, sh_ref, wr_ref,
                    x1_ref, h2b_ref, lg_ref):
    for r in range(m_ref.shape[1] // OUTPROJ_SUBTILE):
        rows = slice(r * OUTPROJ_SUBTILE, (r + 1) * OUTPROJ_SUBTILE)
        y = jnp.dot(m_ref[0, rows, :], w_ref[...], preferred_element_type=F32)
        x1 = x_ref[0, rows, :] + gt_ref[0] * (_rms(y) * gpost_ref[...])
        h2 = (_rms(x1) * gpre_ref[...]) * (1.0 + sc_ref[0]) + sh_ref[0]
        x1_ref[0, rows, :] = x1
        h2b_ref[0, rows, :] = h2.astype(BF16)
        h_hi = h2.astype(BF16)
        h_lo = (h2 - h_hi.astype(F32)).astype(BF16)
        wr = wr_ref[...]
        w_hi = wr.astype(BF16)
        w_lo = (wr - w_hi.astype(F32)).astype(BF16)
        lg = (jnp.dot(h_hi, w_hi, preferred_element_type=F32)
              + (jnp.dot(h_hi, w_lo, preferred_element_type=F32)
                 + jnp.dot(h_lo, w_hi, preferred_element_type=F32)))
        lg_ref[:, rows] = lg.T[:lg_ref.shape[0], :]


def _outproj(merged, w_out, x, g_post, gt1, g_pre, sc2, sh2, w_router):
    bsz, s, d = x.shape
    ne = w_router.shape[1]
    w_router_p = jnp.pad(w_router, ((0, 0), (0, LANES - ne)))
    tm = OUTPROJ_TILE
    nt = s // tm
    row = pl.BlockSpec((1, tm, d), lambda i: (i // nt, i % nt, 0))
    vec = pl.BlockSpec((1, 1, d), lambda i: (i // nt, 0, 0))
    par = pl.BlockSpec((1, d), lambda i: (0, 0))
    return pl.pallas_call(
        _outproj_kernel,
        grid=(bsz * nt,),
        in_specs=[row, pl.BlockSpec((d, d), lambda i: (0, 0)), row, par, vec, par, vec, vec,
                  pl.BlockSpec((d, LANES), lambda i: (0, 0))],
        out_specs=[row, row, pl.BlockSpec((ne, tm), lambda i: (0, i))],
        out_shape=[jax.ShapeDtypeStruct((bsz, s, d), F32),
                   jax.ShapeDtypeStruct((bsz, s, d), BF16),
                   jax.ShapeDtypeStruct((ne, bsz * s), F32)],
        compiler_params=_cparams(("arbitrary",)),
        name="out_proj_norms",
    )(merged.reshape(bsz, s, d), w_out, x, g_post.reshape(1, d), gt1.reshape(bsz, 1, d),
      g_pre.reshape(1, d), sc2.reshape(bsz, 1, d), sh2.reshape(bsz, 1, d), w_router_p)


def _router_kernel(lg_ref, rb_ref, w_ref, rank_ref, cnt_ref, gsc):
    ne, tn = lg_ref.shape
    ng = N_EXPERT_GROUPS
    eg = ne // ng
    scores = jax.nn.sigmoid(lg_ref[...])
    biased = scores + rb_ref[...]
    i8 = lax.broadcasted_iota(jnp.int32, (eg, tn), 0)
    for g in range(ng):
        v = biased[g * eg:(g + 1) * eg, :]
        m1 = jnp.max(v, axis=0, keepdims=True)
        first = jnp.min(jnp.where(v == m1, i8, eg), axis=0, keepdims=True)
        m2 = jnp.max(jnp.where(i8 == first, -jnp.inf, v), axis=0, keepdims=True)
        gsc[g:g + 1, :] = m1 + m2
    gs = gsc[...]
    gi = lax.broadcasted_iota(jnp.int32, (ng, tn), 0)
    masked = []
    for g in range(ng):
        sg = gs[g:g + 1, :]
        ahead = (gs > sg) | ((gs == sg) & (gi < g))
        rank = jnp.sum(jnp.where(ahead, 1.0, 0.0), axis=0, keepdims=True)
        masked.append(jnp.where(rank < TOPK_GROUPS, biased[g * eg:(g + 1) * eg, :], -jnp.inf))
    cur = jnp.concatenate(masked, axis=0)
    sub = lax.broadcasted_iota(jnp.int32, (ne, tn), 0)
    sel = jnp.zeros((ne, tn), F32)
    for k in range(TOP_K):
        mx = jnp.max(cur, axis=0, keepdims=True)
        idx = jnp.min(jnp.where(cur == mx, sub, ne), axis=0, keepdims=True)
        hit = sub == idx
        sel = jnp.where(hit, 1.0, sel)
        cur = jnp.where(hit, -jnp.inf, cur)
    picked = sel * scores
    wsum = jnp.sum(picked, axis=0, keepdims=True)
    w_ref[...] = picked / wsum * ROUTED_SCALE
    ti = lax.broadcasted_iota(jnp.int32, (tn, tn), 0)
    tj = lax.broadcasted_iota(jnp.int32, (tn, tn), 1)
    before = jnp.where(ti < tj, 1.0, 0.0).astype(BF16)
    rank = jnp.dot(sel.astype(BF16), before, preferred_element_type=F32)
    rank_ref[...] = jnp.where(sel > 0.0, rank, NEG_BIG)
    cnt_ref[0] = jnp.broadcast_to(jnp.sum(sel, axis=1, keepdims=True), (ne, LANES))


def _router(logits_t, router_bias):
    ne, t = logits_t.shape
    tn = MOE_TILE
    tile = pl.BlockSpec((ne, tn), lambda i: (0, i))
    return pl.pallas_call(
        _router_kernel,
        grid=(t // tn,),
        in_specs=[tile, pl.BlockSpec((ne, 1), lambda i: (0, 0))],
        out_specs=[tile, tile, pl.BlockSpec((1, ne, LANES), lambda i: (i, 0, 0))],
        out_shape=[jax.ShapeDtypeStruct((ne, t), F32), jax.ShapeDtypeStruct((ne, t), F32),
                   jax.ShapeDtypeStruct((t // tn, ne, LANES), F32)],
        scratch_shapes=[pltpu.VMEM((N_EXPERT_GROUPS, tn), F32)],
        compiler_params=_cparams(("arbitrary",)),
        name="router_topk",
    )(logits_t, router_bias.reshape(ne, 1))


def _tile_rows(n_experts):
    return MOE_TILE * TOP_K + n_experts * SEG_ALIGN


def _segment_copies(src, dst, sem, n, src_off, dst_off, wait):
    off = jnp.int32(0)
    for p in SEG_CHUNKS:
        bit = n & p

        @pl.when(bit != 0)
        def _(p=p, off=off):
            cp = pltpu.make_async_copy(
                src.at[pl.ds(pl.multiple_of(src_off + off, SEG_ALIGN), p), :],
                dst.at[pl.ds(pl.multiple_of(dst_off + off, SEG_ALIGN), p), :], sem)
            if wait:
                cp.wait()
            else:
                cp.start()
        off = off + bit


def _for_each_segment(ne, fn):
    def body(e, carry):
        fn(e)
        return carry
    lax.fori_loop(0, ne, body, 0)


def _build_perm(pbuf, posb, ss_ref, c16_ref, tile, ne, value_row):
    tn = posb.shape[1]
    pbuf[...] = jnp.zeros_like(pbuf)
    rows16 = lax.broadcasted_iota(jnp.int32, (SEG_ALIGN, tn), 0)

    def per_expert(e):
        base = ss_ref[tile, e]
        groups = lax.shift_right_logical(c16_ref[tile, e], SEG_ALIGN.bit_length() - 1)
        prow = posb[pl.ds(e, 1), :]
        vrow = value_row(e)

        def per_group(g, carry):
            r0 = pl.multiple_of(base + g * SEG_ALIGN, SEG_ALIGN)
            hit = prow == (rows16 + r0).astype(F32)
            pbuf[pl.ds(r0, SEG_ALIGN), :] = jnp.where(hit, vrow, 0.0).astype(BF16)
            return carry
        lax.fori_loop(0, groups, per_group, 0)
    _for_each_segment(ne, per_expert)


def _dispatch_kernel(ss_ref, c16_ref, go_ref, ts_ref, tl_ref, rank_ref, ssv_ref, h_ref, xs_hbm,
                     pbuf, xs, posb, sem):
    i = pl.program_id(0)
    n = pl.num_programs(0)
    ne = rank_ref.shape[0]
    tn = rank_ref.shape[1]
    posb[...] = rank_ref[...] + ssv_ref[0]
    _build_perm(pbuf, posb, ss_ref, c16_ref, i, ne, lambda e: 1.0)

    def drain(tile):
        _for_each_segment(ne, lambda e: _segment_copies(
            xs, xs_hbm, sem, c16_ref[tile, e], ss_ref[tile, e], go_ref[tile, e], True))

    @pl.when(i > 0)
    def _():
        drain(i - 1)

    used = ss_ref[i, ne - 1] + c16_ref[i, ne - 1]
    def chunk(ch):
        rows = slice(ch * tn, (ch + 1) * tn)
        xs[rows, :] = jnp.dot(pbuf[rows, :], h_ref[...], preferred_element_type=F32).astype(BF16)

    for ch in range(TOP_K):
        chunk(ch)
    for ch in range(TOP_K, xs.shape[0] // tn):
        pl.when(ch * tn < used)(functools.partial(chunk, ch))

    _for_each_segment(ne, lambda e: _segment_copies(
        xs, xs_hbm, sem, c16_ref[i, e], ss_ref[i, e], go_ref[i, e], False))

    @pl.when(i == n - 1)
    def _():
        drain(i)
        xs[0:MOE_ROWS, :] = jnp.zeros((MOE_ROWS, xs.shape[1]), BF16)
        first_free = lax.shift_right_logical(ts_ref[ne], MOE_ROWS.bit_length() - 1)
        n_blocks = xs_hbm.shape[0] // MOE_ROWS
        for wait in (False, True):
            _for_each_segment(ne, lambda e: _segment_copies(
                xs, xs_hbm, sem, tl_ref[e], 0, ts_ref[e], wait))

            def free_block(b, carry, wait=wait):
                cp = pltpu.make_async_copy(
                    xs.at[pl.ds(0, MOE_ROWS), :],
                    xs_hbm.at[pl.ds(pl.multiple_of(b * MOE_ROWS, MOE_ROWS), MOE_ROWS), :], sem)
                if wait:
                    cp.wait()
                else:
                    cp.start()
                return carry
            lax.fori_loop(first_free, n_blocks, free_block, 0)


def _dispatch(seg_start, c16, goff, tail_start, tail_len, rank, h2b, n_rows):
    ne, t = rank.shape
    d = h2b.shape[1]
    tn = MOE_TILE
    rt = _tile_rows(ne)
    grid_spec = pltpu.PrefetchScalarGridSpec(
        num_scalar_prefetch=5,
        grid=(t // tn,),
        in_specs=[pl.BlockSpec((ne, tn), lambda i, *_: (0, i)),
                  pl.BlockSpec((1, ne, 1), lambda i, *_: (i, 0, 0)),
                  pl.BlockSpec((tn, d), lambda i, *_: (i, 0))],
        out_specs=pl.BlockSpec(memory_space=pl.ANY),
        scratch_shapes=[pltpu.VMEM((rt, tn), BF16), pltpu.VMEM((rt, d), BF16),
                        pltpu.VMEM((ne, tn), F32), pltpu.SemaphoreType.DMA],
    )
    return pl.pallas_call(
        _dispatch_kernel,
        grid_spec=grid_spec,
        out_shape=jax.ShapeDtypeStruct((n_rows, d), BF16),
        compiler_params=_cparams(("arbitrary",)),
        name="moe_dispatch",
    )(seg_start, c16, goff, tail_start, tail_len, rank,
      seg_start.astype(F32).reshape(t // tn, ne, 1), h2b)


def _combine_kernel(ss_ref, c16_ref, go_ref, rank_ref, w_ref, ssv_ref, y_hbm, o_ref,
                    pbuf, ys, posb, sem):
    i = pl.program_id(0)
    ne = rank_ref.shape[0]
    tn = rank_ref.shape[1]
    ys[...] = jnp.zeros_like(ys)
    _for_each_segment(ne, lambda e: _segment_copies(
        y_hbm, ys, sem, c16_ref[i, e], go_ref[i, e], ss_ref[i, e], False))
    posb[...] = rank_ref[...] + ssv_ref[0]
    _build_perm(pbuf, posb, ss_ref, c16_ref, i, ne, lambda e: w_ref[pl.ds(e, 1), :])
    _for_each_segment(ne, lambda e: _segment_copies(
        y_hbm, ys, sem, c16_ref[i, e], go_ref[i, e], ss_ref[i, e], True))
    used = ss_ref[i, ne - 1] + c16_ref[i, ne - 1]
    def chunk(ch):
        rows = slice(ch * tn, (ch + 1) * tn)
        return lax.dot_general(pbuf[rows, :], ys[rows, :], (((0,), (0,)), ((), ())),
                               preferred_element_type=F32)

    always = TOP_K
    acc = chunk(0)
    for ch in range(1, always):
        acc = acc + chunk(ch)
    o_ref[...] = acc
    for ch in range(always, ys.shape[0] // tn):
        @pl.when(ch * tn < used)
        def _(ch=ch):
            o_ref[...] += chunk(ch)


def _combine(seg_start, c16, goff, rank, selw, y_rows):
    ne, t = rank.shape
    d = y_rows.shape[1]
    tn = MOE_TILE
    rt = _tile_rows(ne)
    tile = pl.BlockSpec((ne, tn), lambda i, *_: (0, i))
    grid_spec = pltpu.PrefetchScalarGridSpec(
        num_scalar_prefetch=3,
        grid=(t // tn,),
        in_specs=[tile, tile, pl.BlockSpec((1, ne, 1), lambda i, *_: (i, 0, 0)),
                  pl.BlockSpec(memory_space=pl.ANY)],
        out_specs=pl.BlockSpec((tn, d), lambda i, *_: (i, 0)),
        scratch_shapes=[pltpu.VMEM((rt, tn), BF16), pltpu.VMEM((rt, d), BF16),
                        pltpu.VMEM((ne, tn), F32), pltpu.SemaphoreType.DMA],
    )
    return pl.pallas_call(
        _combine_kernel,
        grid_spec=grid_spec,
        out_shape=jax.ShapeDtypeStruct((t, d), F32),
        compiler_params=_cparams(("arbitrary",)),
        name="moe_combine",
    )(seg_start, c16, goff, rank, selw, seg_start.astype(F32).reshape(t // tn, ne, 1), y_rows)


EXPERT_LOOKAHEAD = 2


def _expert_kernel(be_ref, nv_ref, sl_ref, x_ref, wg_ref, wu_ref, wd_ref, o_ref, wg, wu, wd):
    s = pl.program_id(0)
    n = pl.num_programs(0) - EXPERT_LOOKAHEAD
    clampi = lambda j: jnp.clip(j, 0, n - 1)

    def starts_expert(j):
        jc = clampi(j)
        first = (jc == 0) | (be_ref[jc] != be_ref[clampi(jc - 1)])
        return (j >= 0) & (j < n) & (nv_ref[jc] > 0) & first

    i = s - EXPERT_LOOKAHEAD
    ic = clampi(i)
    slot = sl_ref[ic]

    @pl.when(starts_expert(i))
    def _():
        wd[slot] = wd_ref[0].astype(BF16)

    @pl.when((i >= 0) & (nv_ref[ic] == 0))
    def _():
        o_ref[...] = jnp.zeros_like(o_ref)

    @pl.when((i >= 0) & (nv_ref[ic] > 0))
    def _():
        x = x_ref[...]
        gate = jnp.dot(x, wg[slot], preferred_element_type=F32)
        up = jnp.dot(x, wu[slot], preferred_element_type=F32)
        act = (_silu(gate) * up).astype(BF16)
        o_ref[...] = jnp.dot(act, wd[slot], preferred_element_type=F32).astype(o_ref.dtype)

    @pl.when(starts_expert(s))
    def _():
        wg[sl_ref[clampi(s)]] = wg_ref[0].astype(BF16)

    @pl.when(starts_expert(s - 1))
    def _():
        wu[sl_ref[clampi(s - 1)]] = wu_ref[0].astype(BF16)


def _expert_ffn(blk_exp, blk_valid, blk_slot, x_sorted, w_gate, w_up, w_down):
    n_rows, d = x_sorted.shape
    f = w_gate.shape[2]
    rows = MOE_ROWS
    n = n_rows // rows
    blk = lambda back: (lambda s, be, nv, sl: (jnp.clip(s - back, 0, n - 1), 0))
    wgt = lambda back: (lambda s, be, nv, sl: (be[jnp.clip(s - back, 0, n - 1)], 0, 0))
    grid_spec = pltpu.PrefetchScalarGridSpec(
        num_scalar_prefetch=3,
        grid=(n + EXPERT_LOOKAHEAD,),
        in_specs=[pl.BlockSpec((rows, d), blk(2)),
                  pl.BlockSpec((1, d, f), wgt(0)),
                  pl.BlockSpec((1, d, f), wgt(1)),
                  pl.BlockSpec((1, f, d), wgt(2))],
        out_specs=pl.BlockSpec((rows, d), blk(2)),
        scratch_shapes=[pltpu.VMEM((2, d, f), BF16), pltpu.VMEM((2, d, f), BF16),
                        pltpu.VMEM((2, f, d), BF16)],
    )
    return pl.pallas_call(
        _expert_kernel,
        grid_spec=grid_spec,
        out_shape=jax.ShapeDtypeStruct((n_rows, d), BF16),
        compiler_params=_cparams(("arbitrary",)),
        name="expert_ffn",
    )(blk_exp, blk_valid, blk_slot, x_sorted, w_gate, w_up, w_down)


def _shared_kernel(x_ref, wg_ref, wu_ref, wd_ref, r_ref, x1_ref, gt_ref, g_ref, o_ref):
    x = x_ref[0]
    gate = jnp.dot(x, wg_ref[...], preferred_element_type=F32)
    up = jnp.dot(x, wu_ref[...], preferred_element_type=F32)
    act = (_silu(gate) * up).astype(BF16)
    y = r_ref[0] + jnp.dot(act, wd_ref[...], preferred_element_type=F32)
    o_ref[0] = x1_ref[0] + gt_ref[0] * (_rms(y) * g_ref[...])


def _shared_final(h2b, wg, wu, wd, routed, x1, gt2, g_post):
    bsz, s, d = x1.shape
    f = wg.shape[1]
    tm = 512
    nt = s // tm
    row = pl.BlockSpec((1, tm, d), lambda i: (i // nt, i % nt, 0))
    full = lambda shape: pl.BlockSpec(shape, lambda i: (0, 0))
    return pl.pallas_call(
        _shared_kernel,
        grid=(bsz * nt,),
        in_specs=[row, full((d, f)), full((d, f)), full((f, d)), row, row,
                  pl.BlockSpec((1, 1, d), lambda i: (i // nt, 0, 0)), full((1, d))],
        out_specs=row,
        out_shape=jax.ShapeDtypeStruct((bsz, s, d), F32),
        compiler_params=_cparams(("arbitrary",)),
        name="shared_ffn_final",
    )(h2b, wg, wu, wd, routed.reshape(bsz, s, d), x1, gt2.reshape(bsz, 1, d), g_post.reshape(1, d))


def _dispatch_plan(counts, n_blocks):
    n_tiles, ne = counts.shape
    c16 = (counts + SEG_ALIGN - 1) // SEG_ALIGN * SEG_ALIGN
    seg_start = jnp.cumsum(c16, axis=1) - c16
    tot = jnp.sum(c16, axis=0)
    padded = (tot + MOE_ROWS - 1) // MOE_ROWS * MOE_ROWS
    pad_end = jnp.cumsum(padded)
    pad_start = pad_end - padded
    goff = pad_start[None, :] + jnp.cumsum(c16, axis=0) - c16
    blk_row = jnp.arange(n_blocks, dtype=jnp.int32) * MOE_ROWS
    blk_exp = jnp.sum(pad_end[None, :] <= blk_row[:, None], axis=1)
    blk_exp = jnp.minimum(blk_exp, ne - 1).astype(jnp.int32)
    owns = (pad_start[None, :] <= blk_row[:, None]) & (blk_row[:, None] < pad_end[None, :])
    rows_left = jnp.clip((pad_start + tot)[None, :] - blk_row[:, None], 0, MOE_ROWS)
    blk_valid = jnp.sum(jnp.where(owns, rows_left, 0), axis=1).astype(jnp.int32)
    blk_slot = jnp.sum((padded > 0)[None, :] & (pad_start[None, :] <= blk_row[:, None]), axis=1) % 2
    i32 = lambda v: v.astype(jnp.int32)
    tail_start = jnp.concatenate([pad_start + tot, pad_end[-1:]])
    return (i32(seg_start), i32(c16), i32(goff), i32(tail_start), i32(padded - tot),
            blk_exp, blk_valid, i32(blk_slot))


def kernel(x, c, positions, w_ada, b_ada, g_pre_mix, g_post_mix, g_pre_ffn, g_post_ffn, w_in,
           conv_w, conv_b, dt_bias, a_log, d_skip, ssd_norm_w, w_ssd_br, w_attn_br, w_out,
           w_router, router_bias, w_gate, w_up, w_down, ws_gate, ws_up, ws_down):
    bsz, s, d = x.shape
    t = bsz * s
    for l in range(w_ada.shape[0]):
        d_inner = w_ssd_br.shape[1]
        att_w = w_attn_br.shape[1]
        n_ssd_heads = dt_bias.shape[1]
        conv_dim = conv_w.shape[2]
        n_heads = att_w // ATT_HEAD_DIM
        n_experts = w_router.shape[2]

        mod = _ada(c, w_ada[l], b_ada[l])
        sh1, sc1, gt1, sh2, sc2, gt2 = jnp.split(mod, 6, axis=-1)

        o_dt = d_inner + conv_dim
        o_q = o_dt + n_ssd_heads
        w_in_t = jnp.swapaxes(w_in, 1, 2)
        h1 = _prenorm(x, g_pre_mix[l], sc1, sh1).reshape(t, d)
        proj_s = _matmul_wt(h1, w_in_t, l, 0, o_dt, 1024, 1024, BF16, "in_proj_ssd")
        proj_a = _matmul_wt(h1, w_in_t, l, o_q, w_in.shape[2] - o_q, 1024, 1024, BF16, "in_proj_att")
        dt_t = _dt_proj(h1, w_in_t, l, o_dt, n_ssd_heads, 1024)
        k_off = att_w
        v_off = k_off + att_w
        gs_off = v_off + att_w
        ga_off = gs_off + d

        y_ssd = _ssd(proj_s.reshape(bsz, s, o_dt), dt_t, conv_w[l], conv_b[l], dt_bias[l], a_log[l],
                     d_skip[l], ssd_norm_w[l], d_inner)

        cos2, sin2 = _rope_tables(positions)
        y_att = _attention(proj_a.reshape(bsz, s, -1), cos2, sin2, 0, k_off // ATT_HEAD_DIM,
                           v_off // ATT_HEAD_DIM, n_heads)

        merged = _merge(y_ssd.reshape(t, d_inner), y_att.reshape(t, att_w),
                        w_ssd_br[l].astype(BF16), w_attn_br[l].astype(BF16), proj_a, gs_off, ga_off)
        x1, h2b, logits_t = _outproj(merged, w_out[l].astype(BF16), x, g_post_mix[l], gt1,
                                         g_pre_ffn[l], sc2, sh2, w_router[l])

        selw, rank, cnt = _router(logits_t, router_bias[l])
        n_tiles = t // MOE_TILE
        n_blocks = -(-(t * TOP_K + n_tiles * n_experts * (SEG_ALIGN - 1)) // MOE_ROWS) + n_experts
        seg_start, c16, goff, tail_start, tail_len, blk_exp, blk_valid, blk_slot = _dispatch_plan(
            cnt[:, :, 0].astype(jnp.int32), n_blocks)
        x_sorted = _dispatch(seg_start, c16, goff, tail_start, tail_len, rank, h2b.reshape(t, d),
                             n_blocks * MOE_ROWS)
        y_rows = _expert_ffn(blk_exp, blk_valid, blk_slot, x_sorted, w_gate[l], w_up[l], w_down[l])
        routed = _combine(seg_start, c16, goff, rank, selw, y_rows)
        x = _shared_final(h2b, ws_gate[l].astype(BF16), ws_up[l].astype(BF16),
                          ws_down[l].astype(BF16), routed, x1, gt2, g_post_ffn[l])
    return x
```

```python
import functools
import math

import jax
import jax.numpy as jnp
from jax import lax
from jax.experimental import pallas as pl
from jax.experimental.pallas import tpu as pltpu

F32 = jnp.float32
BF16 = jnp.bfloat16

NORM_EPS = 1e-6
ROPE_THETA = 10000.0

SSD_HEAD_DIM = 64
SSD_GROUPS = 8
SSD_STATE = 128
SSD_CONV = 4
SSD_CHUNK = 128
SSD_GROUPS_PER_STEP = 8

ATT_HEAD_DIM = 128
MOBA_BLOCK = 256
MOBA_TOPK = 3

TOP_K = 8
N_EXPERT_GROUPS = 8
TOPK_GROUPS = 4
ROUTED_SCALE = 2.5
MOE_ROWS = 512
MOE_TILE = 512
SEG_ALIGN = 16
SEG_CHUNKS = (512, 256, 128, 64, 32, 16)
PERM_WINDOW = 128
MOE_COL_SPLIT = 2

MERGE_TM, MERGE_TN = 512, 512
OUTPROJ_TILE = 512
OUTPROJ_SUBTILE = 256

LANES = 128
SUBLANES = 8
VMEM_LIMIT = 56 * 1024 * 1024
NEG_BIG = -1e30


def _cparams(sem):
    return pltpu.CompilerParams(dimension_semantics=sem, vmem_limit_bytes=VMEM_LIMIT)


def _silu(x):
    return x * jax.nn.sigmoid(x)


def _rms(x):
    return x * lax.rsqrt(jnp.mean(x * x, axis=-1, keepdims=True) + NORM_EPS)


def _split3(a):
    a1 = a.astype(BF16)
    r1 = a - a1.astype(F32)
    a2 = r1.astype(BF16)
    a3 = (r1 - a2.astype(F32)).astype(BF16)
    return a1, a2, a3


def _dot_exact_rhs(a, e_bf16):
    out = None
    for p in _split3(a):
        t = jnp.dot(p, e_bf16, preferred_element_type=F32)
        out = t if out is None else out + t
    return out


def _dot_nt(a, b):
    return lax.dot_general(a, b, (((1,), (1,)), ((), ())), preferred_element_type=F32)


def _dot_nt_x3(a, b):
    a1 = a.astype(BF16)
    a2 = (a - a1.astype(F32)).astype(BF16)
    b1 = b.astype(BF16)
    b2 = (b - b1.astype(F32)).astype(BF16)
    return _dot_nt(a1, b1) + (_dot_nt(a1, b2) + _dot_nt(a2, b1))


def _ada_kernel(c_ref, w_ref, b_ref, o_ref):
    cond = _silu(c_ref[...])
    o_ref[...] = jnp.dot(cond.astype(BF16), w_ref[...].astype(BF16),
                         preferred_element_type=F32) + b_ref[...]


def _ada(c, w_ada, b_ada):
    bsz, d = c.shape
    n = w_ada.shape[1]
    tn = 1024
    return pl.pallas_call(
        _ada_kernel,
        grid=(n // tn,),
        in_specs=[pl.BlockSpec((bsz, d), lambda j: (0, 0)),
                  pl.BlockSpec((d, tn), lambda j: (0, j)),
                  pl.BlockSpec((1, tn), lambda j: (0, j))],
        out_specs=pl.BlockSpec((bsz, tn), lambda j: (0, j)),
        out_shape=jax.ShapeDtypeStruct((bsz, n), F32),
        compiler_params=_cparams(("arbitrary",)),
        name="ada_mod",
    )(c, w_ada, b_ada.reshape(1, n))


def _prenorm_kernel(x_ref, g_ref, sc_ref, sh_ref, o_ref):
    y = _rms(x_ref[0]) * g_ref[...]
    o_ref[0] = (y * (1.0 + sc_ref[0]) + sh_ref[0]).astype(o_ref.dtype)


def _prenorm(x, g, sc, sh):
    bsz, s, d = x.shape
    ts = 512
    vec = pl.BlockSpec((1, 1, d), lambda b, i: (b, 0, 0))
    return pl.pallas_call(
        _prenorm_kernel,
        grid=(bsz, s // ts),
        in_specs=[pl.BlockSpec((1, ts, d), lambda b, i: (b, i, 0)),
                  pl.BlockSpec((1, d), lambda b, i: (0, 0)), vec, vec],
        out_specs=pl.BlockSpec((1, ts, d), lambda b, i: (b, i, 0)),
        out_shape=jax.ShapeDtypeStruct((bsz, s, d), BF16),
        compiler_params=_cparams(("arbitrary", "arbitrary")),
        name="prenorm",
    )(x, g.reshape(1, d), sc.reshape(bsz, 1, d), sh.reshape(bsz, 1, d))


def _mm_kernel(a_ref, w_ref, o_ref):
    o_ref[...] = jnp.dot(a_ref[...], w_ref[...], preferred_element_type=F32).astype(o_ref.dtype)


def _matmul(a, w, out_dtype, tm, tn, name):
    m, k = a.shape
    n = w.shape[1]
    return pl.pallas_call(
        _mm_kernel,
        grid=(n // tn, m // tm),
        in_specs=[pl.BlockSpec((tm, k), lambda j, i: (i, 0)),
                  pl.BlockSpec((k, tn), lambda j, i: (0, j))],
        out_specs=pl.BlockSpec((tm, tn), lambda j, i: (i, j)),
        out_shape=jax.ShapeDtypeStruct((m, n), out_dtype),
        compiler_params=_cparams(("arbitrary", "arbitrary")),
        name=name,
    )(a, w)


def _mm_wt_kernel(shift, a_ref, *refs):
    w_ref = refs[0]
    o_ref, wb = refs[-2:]

    @pl.when(pl.program_id(1) == 0)
    def _():
        w = w_ref[0]
        if shift:
            w = jnp.concatenate([w[shift:], refs[1][0]], axis=0)
        wb[...] = w.T.astype(BF16)

    o_ref[...] = jnp.dot(a_ref[...], wb[...], preferred_element_type=F32).astype(o_ref.dtype)


def _dt_kernel(a_ref, w_ref, o_ref):
    o_ref[...] = _dot_nt(w_ref[0].astype(BF16), a_ref[...])


def _dt_proj(a, wt_stack, layer, row0, nh, tm):
    m, k = a.shape
    assert row0 % nh == 0
    return pl.pallas_call(
        _dt_kernel,
        grid=(m // tm,),
        in_specs=[pl.BlockSpec((tm, k), lambda i: (i, 0)),
                  pl.BlockSpec((1, nh, k), lambda i: (layer, row0 // nh, 0))],
        out_specs=pl.BlockSpec((nh, tm), lambda i: (0, i)),
        out_shape=jax.ShapeDtypeStruct((nh, m), F32),
        compiler_params=_cparams(("arbitrary",)),
        name="in_proj_dt",
    )(a, wt_stack)


def _matmul_wt(a, wt_stack, layer, row0, n, tm, tn, out_dtype, name):
    m, k = a.shape
    shift = row0 % tn
    base = row0 - shift
    assert n % tn == 0 and shift % SUBLANES == 0
    a_spec = pl.BlockSpec((tm, k), lambda j, i: (i, 0))
    in_specs = [a_spec, pl.BlockSpec((1, tn, k), lambda j, i: (layer, base // tn + j, 0))]
    args = [a, wt_stack]
    if shift:
        assert (base + tn) % shift == 0 and tn % shift == 0
        in_specs.append(pl.BlockSpec((1, shift, k), lambda j, i: (layer, (base + (j + 1) * tn) // shift, 0)))
        args.append(wt_stack)
    return pl.pallas_call(
        functools.partial(_mm_wt_kernel, shift),
        grid=(n // tn, m // tm),
        in_specs=in_specs,
        out_specs=pl.BlockSpec((tm, tn), lambda j, i: (i, j)),
        out_shape=jax.ShapeDtypeStruct((m, n), out_dtype),
        scratch_shapes=[pltpu.VMEM((k, tn), BF16)],
        compiler_params=_cparams(("arbitrary", "arbitrary")),
        name=name,
    )(*args)


def _softplus(x):
    return jnp.maximum(x, 0.0) + jnp.log(1.0 + jnp.exp(-jnp.abs(x)))


def _ssd_kernel(x_ref, b_ref, c_ref, z_ref, dt_ref, cwx_ref, cwb_ref, cwc_ref,
                cbx_ref, cbb_ref, cbc_ref, dtb_ref, alog_ref, dsk_ref, nw_ref,
                o_ref, cbuf, state):
    q = SSD_CHUNK
    gps = dtb_ref.shape[0]
    gw = x_ref.shape[2] // gps
    nst = b_ref.shape[2] // gps
    hg = gw // SSD_HEAD_DIM
    pad = SUBLANES

    @pl.when(pl.program_id(2) == 0)
    def _():
        cbuf[:, 0:pad, :] = jnp.zeros((gps, pad, gw + 2 * nst), F32)
        state[...] = jnp.zeros_like(state)

    hrow = lax.broadcasted_iota(jnp.int32, (LANES, q), 0)
    ri = lax.broadcasted_iota(jnp.int32, (q, q), 0)
    ci = lax.broadcasted_iota(jnp.int32, (q, q), 1)
    triu = jnp.where(ri <= ci, 1.0, 0.0).astype(BF16)
    er = lax.broadcasted_iota(jnp.int32, (LANES, gw), 0)
    ec = lax.broadcasted_iota(jnp.int32, (LANES, gw), 1)
    e_ch = jnp.where(ec // SSD_HEAD_DIM == er, 1.0, 0.0).astype(BF16)
    er2 = lax.broadcasted_iota(jnp.int32, (LANES, hg * q), 0)
    ec2 = lax.broadcasted_iota(jnp.int32, (LANES, hg * q), 1)
    e_t = jnp.where(ec2 // q == er2, 1.0, 0.0).astype(BF16)
    ri2 = lax.broadcasted_iota(jnp.int32, (q, hg * q), 0)
    ci2 = lax.broadcasted_iota(jnp.int32, (q, hg * q), 1) % q
    first_head = lax.broadcasted_iota(jnp.int32, (1, LANES), 1) < SSD_HEAD_DIM

    for u in range(gps):
        xc = slice(u * gw, (u + 1) * gw)
        nc = slice(u * nst, (u + 1) * nst)
        xin = jnp.concatenate([x_ref[0, :, xc], b_ref[0, :, nc], c_ref[0, :, nc]],
                              axis=-1).astype(F32)
        cbuf[u, pad:pad + q, :] = xin
        w = jnp.concatenate([cwx_ref[:, xc], cwb_ref[:, nc], cwc_ref[:, nc]], axis=-1)
        bias = jnp.concatenate([cbx_ref[:, xc], cbb_ref[:, nc], cbc_ref[:, nc]], axis=-1)
        acc = bias + w[SSD_CONV - 1:SSD_CONV, :] * xin
        for s in range(1, SSD_CONV):
            acc = acc + w[SSD_CONV - 1 - s:SSD_CONV - s, :] * cbuf[u, pad - s:pad - s + q, :]
        cbuf[u, pad - (SSD_CONV - 1):pad, :] = cbuf[u, pad + q - (SSD_CONV - 1):pad + q, :]
        xbc = _silu(acc)
        xs = xbc[:, :gw]
        bm = xbc[:, gw:gw + nst].astype(BF16)
        cm = xbc[:, gw + nst:].astype(BF16)

        dt_raw = jnp.concatenate([dt_ref[u * hg:(u + 1) * hg, :], jnp.zeros((LANES - hg, q), F32)],
                                 axis=0)
        dt_r = jnp.where(hrow < hg, _softplus(dt_raw + dtb_ref[u]), 0.0)
        a_r = dt_r * (-math.log2(math.e) * jnp.exp(alog_ref[u]))
        acum_r = _dot_exact_rhs(a_r, triu)
        acum_c = acum_r.T
        dt_c = dt_r.T

        dt_full = _dot_exact_rhs(dt_c, e_ch)
        acum_colb = _dot_exact_rhs(acum_c, e_t)
        acum_full = jnp.concatenate(
            [jnp.where(first_head, acum_colb[:, (2 * pr) * q:(2 * pr + 1) * q],
                       acum_colb[:, (2 * pr + 1) * q:(2 * pr + 2) * q])
             for pr in range(hg // 2)], axis=-1)
        acum_rowb = jnp.concatenate(
            [jnp.broadcast_to(acum_r[h:h + 1, :], (q, q)) for h in range(hg)], axis=-1)

        lmat = jnp.exp2(jnp.where(ri2 >= ci2, acum_colb - acum_rowb, NEG_BIG))
        cb = _dot_nt(cm, bm)
        mcat = (jnp.concatenate([cb] * hg, axis=-1) * lmat).astype(BF16)
        xdt = xs * dt_full
        y_pairs = []
        for pr in range(gw // LANES):
            xp = xdt[:, pr * LANES:(pr + 1) * LANES]
            xbd = jnp.concatenate([jnp.where(first_head, xp, 0.0).astype(BF16),
                                   jnp.where(first_head, 0.0, xp).astype(BF16)], axis=0)
            y_pairs.append(jnp.dot(mcat[:, pr * 2 * q:(pr + 1) * 2 * q], xbd,
                                   preferred_element_type=F32))
        y = jnp.concatenate(y_pairs, axis=-1)

        st = state[u]
        y = y + jnp.dot(cm, st.astype(BF16), preferred_element_type=F32) * jnp.exp2(acum_full)
        acum_last = acum_full[q - 1:q, :]
        xw = (xdt * jnp.exp2(acum_last - acum_full)).astype(BF16)
        bt = xbc[:, gw:gw + nst].T.astype(BF16)
        state[u] = st * jnp.exp2(acum_last) + jnp.dot(bt, xw, preferred_element_type=F32)

        y = y + xs * dsk_ref[:, xc]
        y = y * _silu(z_ref[0, :, xc].astype(F32))
        o_ref[0, :, xc] = (_rms(y) * nw_ref[:, xc]).astype(o_ref.dtype)


def _ssd(proj, dt_rows, conv_w, conv_b, dt_bias, a_log, d_skip, norm_w, d_inner):
    bsz, s, _ = proj.shape
    g = SSD_GROUPS
    gps = SSD_GROUPS_PER_STEP
    assert SSD_CHUNK == LANES and 2 * SSD_HEAD_DIM == LANES
    gw = d_inner // g
    hg = gw // SSD_HEAD_DIM
    nst = SSD_STATE
    q = SSD_CHUNK
    nc = s // q
    bw, bn = gps * gw, gps * nst
    x0 = d_inner // bw
    b0 = (2 * d_inner) // bn
    c0 = (2 * d_inner + g * nst) // bn
    cwb0 = d_inner // bn
    cwc0 = (d_inner + g * nst) // bn
    conv_b2 = conv_b.reshape(1, -1)
    head_pad = lambda v: jnp.pad(v.reshape(g, hg), ((0, 0), (0, LANES - hg))).reshape(g, LANES, 1)
    return pl.pallas_call(
        _ssd_kernel,
        grid=(bsz, g // gps, nc),
        in_specs=[
            pl.BlockSpec((1, q, bw), lambda b, gi, c: (b, c, x0 + gi)),
            pl.BlockSpec((1, q, bn), lambda b, gi, c: (b, c, b0 + gi)),
            pl.BlockSpec((1, q, bn), lambda b, gi, c: (b, c, c0 + gi)),
            pl.BlockSpec((1, q, bw), lambda b, gi, c: (b, c, gi)),
            pl.BlockSpec((gps * hg, q), lambda b, gi, c: (gi, b * nc + c)),
            pl.BlockSpec((SSD_CONV, bw), lambda b, gi, c: (0, gi)),
            pl.BlockSpec((SSD_CONV, bn), lambda b, gi, c: (0, cwb0 + gi)),
            pl.BlockSpec((SSD_CONV, bn), lambda b, gi, c: (0, cwc0 + gi)),
            pl.BlockSpec((1, bw), lambda b, gi, c: (0, gi)),
            pl.BlockSpec((1, bn), lambda b, gi, c: (0, cwb0 + gi)),
            pl.BlockSpec((1, bn), lambda b, gi, c: (0, cwc0 + gi)),
            pl.BlockSpec((gps, LANES, 1), lambda b, gi, c: (gi, 0, 0)),
            pl.BlockSpec((gps, LANES, 1), lambda b, gi, c: (gi, 0, 0)),
            pl.BlockSpec((1, bw), lambda b, gi, c: (0, gi)),
            pl.BlockSpec((1, bw), lambda b, gi, c: (0, gi)),
        ],
        out_specs=pl.BlockSpec((1, q, bw), lambda b, gi, c: (b, c, gi)),
        out_shape=jax.ShapeDtypeStruct((bsz, s, d_inner), BF16),
        scratch_shapes=[pltpu.VMEM((gps, SUBLANES + q, gw + 2 * nst), F32),
                        pltpu.VMEM((gps, nst, gw), F32)],
        compiler_params=_cparams(("arbitrary", "arbitrary", "arbitrary")),
        name="ssd_scan",
    )(proj, proj, proj, proj, dt_rows, conv_w, conv_w, conv_w, conv_b2, conv_b2, conv_b2,
      head_pad(dt_bias), head_pad(a_log),
      jnp.repeat(d_skip, SSD_HEAD_DIM).reshape(1, d_inner), norm_w.reshape(1, d_inner))


def _rope_kernel(pos_ref, inv_ref, cos_ref, sin_ref):
    ang = pos_ref[...].astype(F32) * inv_ref[...]
    lane = lax.broadcasted_iota(jnp.int32, ang.shape, 1)
    cos_ref[...] = jnp.cos(ang)
    sin_ref[...] = jnp.where(lane < ATT_HEAD_DIM // 2, -1.0, 1.0) * jnp.sin(ang)


def _rope_tables(positions):
    t = positions.size
    half = ATT_HEAD_DIM // 2
    inv = 1.0 / (ROPE_THETA ** (jnp.arange(half, dtype=F32) / half))
    inv2 = jnp.concatenate([inv, inv]).reshape(1, ATT_HEAD_DIM)
    tt = min(2048, t)
    return pl.pallas_call(
        _rope_kernel,
        grid=(t // tt,),
        in_specs=[pl.BlockSpec((tt, 1), lambda i: (i, 0)),
                  pl.BlockSpec((1, ATT_HEAD_DIM), lambda i: (0, 0))],
        out_specs=[pl.BlockSpec((tt, ATT_HEAD_DIM), lambda i: (i, 0))] * 2,
        out_shape=[jax.ShapeDtypeStruct((t, ATT_HEAD_DIM), F32)] * 2,
        compiler_params=_cparams(("arbitrary",)),
        name="rope_tables",
    )(positions.reshape(t, 1), inv2)


def _rot(x, cos2, sin2):
    return x * cos2 + pltpu.roll(x, ATT_HEAD_DIM // 2, axis=1) * sin2


def _attn_kernel(q_ref, k_ref, v_ref, cos_ref, sin_ref, o_ref, krot, vt, kmean):
    blk = MOBA_BLOCK
    nb = k_ref.shape[1] // blk
    log2e_scale = ATT_HEAD_DIM ** -0.5 * math.log2(math.e)

    kmean[...] = jnp.zeros_like(kmean)
    for j in range(nb):
        rows = slice(j * blk, (j + 1) * blk)
        kr = _rot(k_ref[0, rows, :].astype(F32), cos_ref[0, rows, :], sin_ref[0, rows, :])
        krot[rows, :] = kr.astype(BF16)
        kmean[j:j + 1, :] = jnp.mean(kr, axis=0, keepdims=True)
        vt[:, rows] = v_ref[0, rows, :].astype(F32).T.astype(BF16)
    km = kmean[...]

    ki = lax.broadcasted_iota(jnp.int32, (blk, blk), 0)
    qj = lax.broadcasted_iota(jnp.int32, (blk, blk), 1)
    sub = lax.broadcasted_iota(jnp.int32, (km.shape[0], blk), 0)
    for qi in range(nb):
        qrows = slice(qi * blk, (qi + 1) * blk)
        qr = _rot(q_ref[0, qrows, :].astype(F32), cos_ref[0, qrows, :], sin_ref[0, qrows, :])
        qs = (qr * log2e_scale).T.astype(BF16)
        gate = _dot_nt_x3(km, qr)
        scores = [jnp.where(ki <= qj, jnp.dot(krot[qrows, :], qs, preferred_element_type=F32), NEG_BIG)]
        for j in range(qi):
            gj = gate[j:j + 1, :]
            ahead = (gate > gj) | ((gate == gj) & (sub < j))
            rank = jnp.sum(jnp.where((sub < qi) & ahead, 1.0, 0.0), axis=0, keepdims=True)
            bias = jnp.where(rank < MOBA_TOPK, 0.0, NEG_BIG)
            scores.append(jnp.dot(krot[j * blk:(j + 1) * blk, :], qs,
                                  preferred_element_type=F32) + bias)
        m = scores[0]
        for sj in scores[1:]:
            m = jnp.maximum(m, sj)
        m = jnp.max(m, axis=0, keepdims=True)
        l = None
        acc = None
        for j, sj in enumerate(scores):
            src = qi if j == 0 else j - 1
            p = jnp.exp2(sj - m)
            lj = jnp.sum(p, axis=0, keepdims=True)
            aj = jnp.dot(vt[:, src * blk:(src + 1) * blk], p.astype(BF16),
                         preferred_element_type=F32)
            l = lj if l is None else l + lj
            acc = aj if acc is None else acc + aj
        o_ref[0, qrows, :] = (acc / l).T.astype(o_ref.dtype)


def _attention(proj, cos2, sin2, q0, k0, v0, n_heads):
    bsz, s, _ = proj.shape
    dh = ATT_HEAD_DIM
    col = lambda c0: pl.BlockSpec((1, s, dh), lambda b, h: (b, 0, c0 + h))
    tab = pl.BlockSpec((1, s, dh), lambda b, h: (b, 0, 0))
    return pl.pallas_call(
        _attn_kernel,
        grid=(bsz, n_heads),
        in_specs=[col(q0), col(k0), col(v0), tab, tab],
        out_specs=pl.BlockSpec((1, s, dh), lambda b, h: (b, 0, h)),
        out_shape=jax.ShapeDtypeStruct((bsz, s, n_heads * dh), BF16),
        scratch_shapes=[pltpu.VMEM((s, dh), BF16), pltpu.VMEM((dh, s), BF16),
                        pltpu.VMEM((-(-(s // MOBA_BLOCK) // SUBLANES) * SUBLANES, dh), F32)],
        compiler_params=_cparams(("arbitrary", "arbitrary")),
        name="moba_attn",
    )(proj, proj, proj, cos2.reshape(bsz, s, dh), sin2.reshape(bsz, s, dh))


def _merge_kernel(ys_ref, ya_ref, w1_ref, w2_ref, gs_ref, ga_ref, o_ref):
    b1 = jnp.dot(ys_ref[...], w1_ref[...], preferred_element_type=F32)
    b2 = jnp.dot(ya_ref[...], w2_ref[...], preferred_element_type=F32)
    o_ref[...] = (jax.nn.sigmoid(gs_ref[...].astype(F32)) * b1
                  + jax.nn.sigmoid(ga_ref[...].astype(F32)) * b2).astype(o_ref.dtype)


def _merge(y_ssd, y_att, w1, w2, proj2d, gs0, ga0):
    m, k1 = y_ssd.shape
    k2 = y_att.shape[1]
    n = w1.shape[1]
    tm, tn = MERGE_TM, MERGE_TN
    return pl.pallas_call(
        _merge_kernel,
        grid=(n // tn, m // tm),
        in_specs=[pl.BlockSpec((tm, k1), lambda j, i: (i, 0)),
                  pl.BlockSpec((tm, k2), lambda j, i: (i, 0)),
                  pl.BlockSpec((k1, tn), lambda j, i: (0, j)),
                  pl.BlockSpec((k2, tn), lambda j, i: (0, j)),
                  pl.BlockSpec((tm, tn), lambda j, i: (i, gs0 // tn + j)),
                  pl.BlockSpec((tm, tn), lambda j, i: (i, ga0 // tn + j))],
        out_specs=pl.BlockSpec((tm, tn), lambda j, i: (i, j)),
        out_shape=jax.ShapeDtypeStruct((m, n), BF16),
        compiler_params=_cparams(("arbitrary", "arbitrary")),
        name="branch_merge",
    )(y_ssd, y_att, w1, w2, proj2d, proj2d)


def _outproj_kernel(m_ref, w_ref, x_ref, gpost_ref, gt_ref, gpre_ref, sc_ref, sh_ref, wr_ref,
                    x1_ref, h2b_ref, lg_ref):
    for r in range(m_ref.shape[1] // OUTPROJ_SUBTILE):
        rows = slice(r * OUTPROJ_SUBTILE, (r + 1) * OUTPROJ_SUBTILE)
        y = jnp.dot(m_ref[0, rows, :], w_ref[...], preferred_element_type=F32)
        x1 = x_ref[0, rows, :] + gt_ref[0] * (_rms(y) * gpost_ref[...])
        h2 = (_rms(x1) * gpre_ref[...]) * (1.0 + sc_ref[0]) + sh_ref[0]
        x1_ref[0, rows, :] = x1
        h2b_ref[0, rows, :] = h2.astype(BF16)
        h_hi = h2.astype(BF16)
        h_lo = (h2 - h_hi.astype(F32)).astype(BF16)
        wr = wr_ref[...]
        w_hi = wr.astype(BF16)
        w_lo = (wr - w_hi.astype(F32)).astype(BF16)
        lg = (jnp.dot(h_hi, w_hi, preferred_element_type=F32)
              + (jnp.dot(h_hi, w_lo, preferred_element_type=F32)
                 + jnp.dot(h_lo, w_hi, preferred_element_type=F32)))
        lg_ref[:, rows] = lg.T[:lg_ref.shape[0], :]


def _outproj(merged, w_out, x, g_post, gt1, g_pre, sc2, sh2, w_router):
    bsz, s, d = x.shape
    ne = w_router.shape[1]
    w_router_p = jnp.pad(w_router, ((0, 0), (0, LANES - ne)))
    tm = OUTPROJ_TILE
    nt = s // tm
    row = pl.BlockSpec((1, tm, d), lambda i: (i // nt, i % nt, 0))
    vec = pl.BlockSpec((1, 1, d), lambda i: (i // nt, 0, 0))
    par = pl.BlockSpec((1, d), lambda i: (0, 0))
    return pl.pallas_call(
        _outproj_kernel,
        grid=(bsz * nt,),
        in_specs=[row, pl.BlockSpec((d, d), lambda i: (0, 0)), row, par, vec, par, vec, vec,
                  pl.BlockSpec((d, LANES), lambda i: (0, 0))],
        out_specs=[row, row, pl.BlockSpec((ne, tm), lambda i: (0, i))],
        out_shape=[jax.ShapeDtypeStruct((bsz, s, d), F32),
                   jax.ShapeDtypeStruct((bsz, s, d), BF16),
                   jax.ShapeDtypeStruct((ne, bsz * s), F32)],
        compiler_params=_cparams(("arbitrary",)),
        name="out_proj_norms",
    )(merged.reshape(bsz, s, d), w_out, x, g_post.reshape(1, d), gt1.reshape(bsz, 1, d),
      g_pre.reshape(1, d), sc2.reshape(bsz, 1, d), sh2.reshape(bsz, 1, d), w_router_p)


def _router_kernel(lg_ref, rb_ref, w_ref, rank_ref, cnt_ref, gsc):
    ne, tn = lg_ref.shape
    ng = N_EXPERT_GROUPS
    eg = ne // ng
    scores = jax.nn.sigmoid(lg_ref[...])
    biased = scores + rb_ref[...]
    i8 = lax.broadcasted_iota(jnp.int32, (eg, tn), 0)
    for g in range(ng):
        v = biased[g * eg:(g + 1) * eg, :]
        m1 = jnp.max(v, axis=0, keepdims=True)
        first = jnp.min(jnp.where(v == m1, i8, eg), axis=0, keepdims=True)
        m2 = jnp.max(jnp.where(i8 == first, -jnp.inf, v), axis=0, keepdims=True)
        gsc[g:g + 1, :] = m1 + m2
    gs = gsc[...]
    gi = lax.broadcasted_iota(jnp.int32, (ng, tn), 0)
    masked = []
    for g in range(ng):
        sg = gs[g:g + 1, :]
        ahead = (gs > sg) | ((gs == sg) & (gi < g))
        rank = jnp.sum(jnp.where(ahead, 1.0, 0.0), axis=0, keepdims=True)
        masked.append(jnp.where(rank < TOPK_GROUPS, biased[g * eg:(g + 1) * eg, :], -jnp.inf))
    cur = jnp.concatenate(masked, axis=0)
    sub = lax.broadcasted_iota(jnp.int32, (ne, tn), 0)
    sel = jnp.zeros((ne, tn), F32)
    for k in range(TOP_K):
        mx = jnp.max(cur, axis=0, keepdims=True)
        idx = jnp.min(jnp.where(cur == mx, sub, ne), axis=0, keepdims=True)
        hit = sub == idx
        sel = jnp.where(hit, 1.0, sel)
        cur = jnp.where(hit, -jnp.inf, cur)
    picked = sel * scores
    wsum = jnp.sum(picked, axis=0, keepdims=True)
    w_ref[...] = picked / wsum * ROUTED_SCALE
    ti = lax.broadcasted_iota(jnp.int32, (tn, tn), 0)
    tj = lax.broadcasted_iota(jnp.int32, (tn, tn), 1)
    before = jnp.where(ti < tj, 1.0, 0.0).astype(BF16)
    rank = jnp.dot(sel.astype(BF16), before, preferred_element_type=F32)
    rank_ref[...] = jnp.where(sel > 0.0, rank, NEG_BIG)
    cnt_ref[0] = jnp.broadcast_to(jnp.sum(sel, axis=1, keepdims=True), (ne, LANES))


def _router(logits_t, router_bias):
    ne, t = logits_t.shape
    tn = MOE_TILE
    tile = pl.BlockSpec((ne, tn), lambda i: (0, i))
    return pl.pallas_call(
        _router_kernel,
        grid=(t // tn,),
        in_specs=[tile, pl.BlockSpec((ne, 1), lambda i: (0, 0))],
        out_specs=[tile, tile, pl.BlockSpec((1, ne, LANES), lambda i: (i, 0, 0))],
        out_shape=[jax.ShapeDtypeStruct((ne, t), F32), jax.ShapeDtypeStruct((ne, t), F32),
                   jax.ShapeDtypeStruct((t // tn, ne, LANES), F32)],
        scratch_shapes=[pltpu.VMEM((N_EXPERT_GROUPS, tn), F32)],
        compiler_params=_cparams(("arbitrary",)),
        name="router_topk",
    )(logits_t, router_bias.reshape(ne, 1))


def _tile_rows(n_experts):
    return MOE_TILE * TOP_K + n_experts * SEG_ALIGN


def _segment_copies(src, dst, sem, n, src_off, dst_off, wait, cols=slice(None)):
    off = jnp.int32(0)
    for p in SEG_CHUNKS:
        bit = n & p

        @pl.when(bit != 0)
        def _(p=p, off=off):
            cp = pltpu.make_async_copy(
                src.at[pl.ds(pl.multiple_of(src_off + off, SEG_ALIGN), p), cols],
                dst.at[pl.ds(pl.multiple_of(dst_off + off, SEG_ALIGN), p), cols], sem)
            if wait:
                cp.wait()
            else:
                cp.start()
        off = off + bit


def _column_halves(d):
    return [pl.ds(k * (d // MOE_COL_SPLIT), d // MOE_COL_SPLIT) for k in range(MOE_COL_SPLIT)]


def _for_each_segment(ne, fn):
    def body(e, carry):
        fn(e)
        return carry
    lax.fori_loop(0, ne, body, 0)


def _build_perm(pbuf, posb, ss_ref, c16_ref, tile, ne, value_row):
    tn = posb.shape[1]
    pbuf[...] = jnp.zeros_like(pbuf)
    win = lax.broadcasted_iota(jnp.int32, (PERM_WINDOW, tn), 0)

    def per_expert(e):
        base = ss_ref[tile, e]
        windows = lax.shift_right_logical(c16_ref[tile, e] + (PERM_WINDOW - 1),
                                          PERM_WINDOW.bit_length() - 1)
        prow = posb[pl.ds(e, 1), :]
        vrow = value_row(e)

        def per_window(g, carry):
            r0 = pl.multiple_of(base + g * PERM_WINDOW, SEG_ALIGN)
            hit = prow == (win + r0).astype(F32)
            pbuf[pl.ds(r0, PERM_WINDOW), :] = jnp.where(hit, vrow, 0.0).astype(BF16)
            return carry
        lax.fori_loop(0, windows, per_window, 0)
    _for_each_segment(ne, per_expert)


def _dispatch_kernel(ss_ref, c16_ref, go_ref, ts_ref, tl_ref, rank_ref, ssv_ref, h_ref, xs_hbm,
                     pbuf, xs, posb, sem):
    i = pl.program_id(0)
    n = pl.num_programs(0)
    ne = rank_ref.shape[0]
    tn = rank_ref.shape[1]
    posb[...] = rank_ref[...] + ssv_ref[0]
    _build_perm(pbuf, posb, ss_ref, c16_ref, i, ne, lambda e: 1.0)

    def copies(tile, k, cols, wait):
        _for_each_segment(ne, lambda e: _segment_copies(
            xs, xs_hbm, sem.at[k], c16_ref[tile, e], ss_ref[tile, e], go_ref[tile, e], wait, cols))

    used = ss_ref[i, ne - 1] + c16_ref[i, ne - 1]

    halves = _column_halves(xs.shape[1])
    for k, cols in enumerate(halves):
        @pl.when(i > 0)
        def _(k=k, cols=cols):
            copies(i - 1, k, cols, True)

        def chunk(ch, cols=cols):
            rows = slice(ch * tn, (ch + 1) * tn)
            xs[rows, cols] = jnp.dot(pbuf[rows, :], h_ref[:, cols],
                                     preferred_element_type=F32).astype(BF16)

        for ch in range(TOP_K):
            chunk(ch)
        for ch in range(TOP_K, xs.shape[0] // tn):
            pl.when(ch * tn < used)(functools.partial(chunk, ch))
        copies(i, k, cols, False)

    @pl.when(i == n - 1)
    def _():
        for k, cols in enumerate(halves):
            copies(i, k, cols, True)
        xs[0:MOE_ROWS, :] = jnp.zeros((MOE_ROWS, xs.shape[1]), BF16)
        first_free = lax.shift_right_logical(ts_ref[ne], MOE_ROWS.bit_length() - 1)
        n_blocks = xs_hbm.shape[0] // MOE_ROWS
        for wait in (False, True):
            _for_each_segment(ne, lambda e: _segment_copies(
                xs, xs_hbm, sem.at[0], tl_ref[e], 0, ts_ref[e], wait))

            def free_block(b, carry, wait=wait):
                cp = pltpu.make_async_copy(
                    xs.at[pl.ds(0, MOE_ROWS), :],
                    xs_hbm.at[pl.ds(pl.multiple_of(b * MOE_ROWS, MOE_ROWS), MOE_ROWS), :],
                    sem.at[0])
                if wait:
                    cp.wait()
                else:
                    cp.start()
                return carry
            lax.fori_loop(first_free, n_blocks, free_block, 0)


def _dispatch(seg_start, c16, goff, tail_start, tail_len, rank, h2b, n_rows):
    ne, t = rank.shape
    d = h2b.shape[1]
    tn = MOE_TILE
    rt = _tile_rows(ne)
    grid_spec = pltpu.PrefetchScalarGridSpec(
        num_scalar_prefetch=5,
        grid=(t // tn,),
        in_specs=[pl.BlockSpec((ne, tn), lambda i, *_: (0, i)),
                  pl.BlockSpec((1, ne, 1), lambda i, *_: (i, 0, 0)),
                  pl.BlockSpec((tn, d), lambda i, *_: (i, 0))],
        out_specs=pl.BlockSpec(memory_space=pl.ANY),
        scratch_shapes=[pltpu.VMEM((rt + PERM_WINDOW, tn), BF16), pltpu.VMEM((rt, d), BF16),
                        pltpu.VMEM((ne, tn), F32), pltpu.SemaphoreType.DMA((MOE_COL_SPLIT,))],
    )
    return pl.pallas_call(
        _dispatch_kernel,
        grid_spec=grid_spec,
        out_shape=jax.ShapeDtypeStruct((n_rows, d), BF16),
        compiler_params=_cparams(("arbitrary",)),
        name="moe_dispatch",
    )(seg_start, c16, goff, tail_start, tail_len, rank,
      seg_start.astype(F32).reshape(t // tn, ne, 1), h2b)


def _combine_kernel(ss_ref, c16_ref, go_ref, rank_ref, w_ref, ssv_ref, y_hbm, o_ref,
                    pbuf, ys, posb, sem):
    i = pl.program_id(0)
    ne = rank_ref.shape[0]
    tn = rank_ref.shape[1]
    ys[...] = jnp.zeros_like(ys)

    def copies(k, cols, wait):
        _for_each_segment(ne, lambda e: _segment_copies(
            y_hbm, ys, sem.at[k], c16_ref[i, e], go_ref[i, e], ss_ref[i, e], wait, cols))

    halves = _column_halves(ys.shape[1])
    for k, cols in enumerate(halves):
        copies(k, cols, False)
    posb[...] = rank_ref[...] + ssv_ref[0]
    _build_perm(pbuf, posb, ss_ref, c16_ref, i, ne, lambda e: w_ref[pl.ds(e, 1), :])
    used = ss_ref[i, ne - 1] + c16_ref[i, ne - 1]
    for k, cols in enumerate(halves):
        copies(k, cols, True)

        def chunk(ch, cols=cols):
            rows = slice(ch * tn, (ch + 1) * tn)
            return lax.dot_general(pbuf[rows, :], ys[rows, cols], (((0,), (0,)), ((), ())),
                                   preferred_element_type=F32)

        acc = chunk(0)
        for ch in range(1, TOP_K):
            acc = acc + chunk(ch)
        o_ref[:, cols] = acc
        for ch in range(TOP_K, ys.shape[0] // tn):
            @pl.when(ch * tn < used)
            def _(ch=ch, cols=cols, chunk=chunk):
                o_ref[:, cols] += chunk(ch)


def _combine(seg_start, c16, goff, rank, selw, y_rows):
    ne, t = rank.shape
    d = y_rows.shape[1]
    tn = MOE_TILE
    rt = _tile_rows(ne)
    tile = pl.BlockSpec((ne, tn), lambda i, *_: (0, i))
    grid_spec = pltpu.PrefetchScalarGridSpec(
        num_scalar_prefetch=3,
        grid=(t // tn,),
        in_specs=[tile, tile, pl.BlockSpec((1, ne, 1), lambda i, *_: (i, 0, 0)),
                  pl.BlockSpec(memory_space=pl.ANY)],
        out_specs=pl.BlockSpec((tn, d), lambda i, *_: (i, 0)),
        scratch_shapes=[pltpu.VMEM((rt + PERM_WINDOW, tn), BF16), pltpu.VMEM((rt, d), BF16),
                        pltpu.VMEM((ne, tn), F32), pltpu.SemaphoreType.DMA((MOE_COL_SPLIT,))],
    )
    return pl.pallas_call(
        _combine_kernel,
        grid_spec=grid_spec,
        out_shape=jax.ShapeDtypeStruct((t, d), F32),
        compiler_params=_cparams(("arbitrary",)),
        name="moe_combine",
    )(seg_start, c16, goff, rank, selw, seg_start.astype(F32).reshape(t // tn, ne, 1), y_rows)


EXPERT_LOOKAHEAD = 2


def _expert_kernel(be_ref, nv_ref, sl_ref, x_ref, wg_ref, wu_ref, wd_ref, o_ref, wg, wu, wd):
    s = pl.program_id(0)
    n = pl.num_programs(0) - EXPERT_LOOKAHEAD
    clampi = lambda j: jnp.clip(j, 0, n - 1)

    def starts_expert(j):
        jc = clampi(j)
        first = (jc == 0) | (be_ref[jc] != be_ref[clampi(jc - 1)])
        return (j >= 0) & (j < n) & (nv_ref[jc] > 0) & first

    i = s - EXPERT_LOOKAHEAD
    ic = clampi(i)
    slot = sl_ref[ic]

    @pl.when(starts_expert(i))
    def _():
        wd[slot] = wd_ref[0].astype(BF16)

    @pl.when((i >= 0) & (nv_ref[ic] == 0))
    def _():
        o_ref[...] = jnp.zeros_like(o_ref)

    @pl.when((i >= 0) & (nv_ref[ic] > 0))
    def _():
        x = x_ref[...]
        gate = jnp.dot(x, wg[slot], preferred_element_type=F32)
        up = jnp.dot(x, wu[slot], preferred_element_type=F32)
        act = (_silu(gate) * up).astype(BF16)
        o_ref[...] = jnp.dot(act, wd[slot], preferred_element_type=F32).astype(o_ref.dtype)

    @pl.when(starts_expert(s))
    def _():
        wg[sl_ref[clampi(s)]] = wg_ref[0].astype(BF16)

    @pl.when(starts_expert(s - 1))
    def _():
        wu[sl_ref[clampi(s - 1)]] = wu_ref[0].astype(BF16)


def _expert_ffn(blk_exp, blk_valid, blk_slot, x_sorted, w_gate, w_up, w_down):
    n_rows, d = x_sorted.shape
    f = w_gate.shape[2]
    rows = MOE_ROWS
    n = n_rows // rows
    blk = lambda back: (lambda s, be, nv, sl: (jnp.clip(s - back, 0, n - 1), 0))
    wgt = lambda back: (lambda s, be, nv, sl: (be[jnp.clip(s - back, 0, n - 1)], 0, 0))
    grid_spec = pltpu.PrefetchScalarGridSpec(
        num_scalar_prefetch=3,
        grid=(n + EXPERT_LOOKAHEAD,),
        in_specs=[pl.BlockSpec((rows, d), blk(2)),
                  pl.BlockSpec((1, d, f), wgt(0)),
                  pl.BlockSpec((1, d, f), wgt(1)),
                  pl.BlockSpec((1, f, d), wgt(2))],
        out_specs=pl.BlockSpec((rows, d), blk(2)),
        scratch_shapes=[pltpu.VMEM((2, d, f), BF16), pltpu.VMEM((2, d, f), BF16),
                        pltpu.VMEM((2, f, d), BF16)],
    )
    return pl.pallas_call(
        _expert_kernel,
        grid_spec=grid_spec,
        out_shape=jax.ShapeDtypeStruct((n_rows, d), BF16),
        compiler_params=_cparams(("arbitrary",)),
        name="expert_ffn",
    )(blk_exp, blk_valid, blk_slot, x_sorted, w_gate, w_up, w_down)


def _shared_kernel(x_ref, wg_ref, wu_ref, wd_ref, r_ref, x1_ref, gt_ref, g_ref, o_ref):
    x = x_ref[0]
    gate = jnp.dot(x, wg_ref[...], preferred_element_type=F32)
    up = jnp.dot(x, wu_ref[...], preferred_element_type=F32)
    act = (_silu(gate) * up).astype(BF16)
    y = r_ref[0] + jnp.dot(act, wd_ref[...], preferred_element_type=F32)
    o_ref[0] = x1_ref[0] + gt_ref[0] * (_rms(y) * g_ref[...])


def _shared_final(h2b, wg, wu, wd, routed, x1, gt2, g_post):
    bsz, s, d = x1.shape
    f = wg.shape[1]
    tm = 512
    nt = s // tm
    row = pl.BlockSpec((1, tm, d), lambda i: (i // nt, i % nt, 0))
    full = lambda shape: pl.BlockSpec(shape, lambda i: (0, 0))
    return pl.pallas_call(
        _shared_kernel,
        grid=(bsz * nt,),
        in_specs=[row, full((d, f)), full((d, f)), full((f, d)), row, row,
                  pl.BlockSpec((1, 1, d), lambda i: (i // nt, 0, 0)), full((1, d))],
        out_specs=row,
        out_shape=jax.ShapeDtypeStruct((bsz, s, d), F32),
        compiler_params=_cparams(("arbitrary",)),
        name="shared_ffn_final",
    )(h2b, wg, wu, wd, routed.reshape(bsz, s, d), x1, gt2.reshape(bsz, 1, d), g_post.reshape(1, d))


def _dispatch_plan(counts, n_blocks):
    n_tiles, ne = counts.shape
    c16 = (counts + SEG_ALIGN - 1) // SEG_ALIGN * SEG_ALIGN
    seg_start = jnp.cumsum(c16, axis=1) - c16
    tot = jnp.sum(c16, axis=0)
    padded = (tot + MOE_ROWS - 1) // MOE_ROWS * MOE_ROWS
    pad_end = jnp.cumsum(padded)
    pad_start = pad_end - padded
    goff = pad_start[None, :] + jnp.cumsum(c16, axis=0) - c16
    blk_row = jnp.arange(n_blocks, dtype=jnp.int32) * MOE_ROWS
    blk_exp = jnp.sum(pad_end[None, :] <= blk_row[:, None], axis=1)
    blk_exp = jnp.minimum(blk_exp, ne - 1).astype(jnp.int32)
    owns = (pad_start[None, :] <= blk_row[:, None]) & (blk_row[:, None] < pad_end[None, :])
    rows_left = jnp.clip((pad_start + tot)[None, :] - blk_row[:, None], 0, MOE_ROWS)
    blk_valid = jnp.sum(jnp.where(owns, rows_left, 0), axis=1).astype(jnp.int32)
    blk_slot = jnp.sum((padded > 0)[None, :] & (pad_start[None, :] <= blk_row[:, None]), axis=1) % 2
    i32 = lambda v: v.astype(jnp.int32)
    tail_start = jnp.concatenate([pad_start + tot, pad_end[-1:]])
    return (i32(seg_start), i32(c16), i32(goff), i32(tail_start), i32(padded - tot),
            blk_exp, blk_valid, i32(blk_slot))


def kernel(x, c, positions, w_ada, b_ada, g_pre_mix, g_post_mix, g_pre_ffn, g_post_ffn, w_in,
           conv_w, conv_b, dt_bias, a_log, d_skip, ssd_norm_w, w_ssd_br, w_attn_br, w_out,
           w_router, router_bias, w_gate, w_up, w_down, ws_gate, ws_up, ws_down):
    bsz, s, d = x.shape
    t = bsz * s
    for l in range(w_ada.shape[0]):
        d_inner = w_ssd_br.shape[1]
        att_w = w_attn_br.shape[1]
        n_ssd_heads = dt_bias.shape[1]
        conv_dim = conv_w.shape[2]
        n_heads = att_w // ATT_HEAD_DIM
        n_experts = w_router.shape[2]

        mod = _ada(c, w_ada[l], b_ada[l])
        sh1, sc1, gt1, sh2, sc2, gt2 = jnp.split(mod, 6, axis=-1)

        o_dt = d_inner + conv_dim
        o_q = o_dt + n_ssd_heads
        w_in_t = jnp.swapaxes(w_in, 1, 2)
        h1 = _prenorm(x, g_pre_mix[l], sc1, sh1).reshape(t, d)
        proj_s = _matmul_wt(h1, w_in_t, l, 0, o_dt, 1024, 1024, BF16, "in_proj_ssd")
        proj_a = _matmul_wt(h1, w_in_t, l, o_q, w_in.shape[2] - o_q, 1024, 1024, BF16, "in_proj_att")
        dt_t = _dt_proj(h1, w_in_t, l, o_dt, n_ssd_heads, 1024)
        k_off = att_w
        v_off = k_off + att_w
        gs_off = v_off + att_w
        ga_off = gs_off + d

        y_ssd = _ssd(proj_s.reshape(bsz, s, o_dt), dt_t, conv_w[l], conv_b[l], dt_bias[l], a_log[l],
                     d_skip[l], ssd_norm_w[l], d_inner)

        cos2, sin2 = _rope_tables(positions)
        y_att = _attention(proj_a.reshape(bsz, s, -1), cos2, sin2, 0, k_off // ATT_HEAD_DIM,
                           v_off // ATT_HEAD_DIM, n_heads)

        merged = _merge(y_ssd.reshape(t, d_inner), y_att.reshape(t, att_w),
                        w_ssd_br[l].astype(BF16), w_attn_br[l].astype(BF16), proj_a, gs_off, ga_off)
        x1, h2b, logits_t = _outproj(merged, w_out[l].astype(BF16), x, g_post_mix[l], gt1,
                                         g_pre_ffn[l], sc2, sh2, w_router[l])

        selw, rank, cnt = _router(logits_t, router_bias[l])
        n_tiles = t // MOE_TILE
        n_blocks = -(-(t * TOP_K + n_tiles * n_experts * (SEG_ALIGN - 1)) // MOE_ROWS) + n_experts
        seg_start, c16, goff, tail_start, tail_len, blk_exp, blk_valid, blk_slot = _dispatch_plan(
            cnt[:, :, 0].astype(jnp.int32), n_blocks)
        x_sorted = _dispatch(seg_start, c16, goff, tail_start, tail_len, rank, h2b.reshape(t, d),
                             n_blocks * MOE_ROWS)
        y_rows = _expert_ffn(blk_exp, blk_valid, blk_slot, x_sorted, w_gate[l], w_up[l], w_down[l])
        routed = _combine(seg_start, c16, goff, rank, selw, y_rows)
        x = _shared_final(h2b, ws_gate[l].astype(BF16), ws_up[l].astype(BF16),
                          ws_down[l].astype(BF16), routed, x1, gt2, g_post_ffn[l])
    return x
```

```python
import functools
import math

import jax
import jax.numpy as jnp
from jax import lax
from jax.experimental import pallas as pl
from jax.experimental.pallas import tpu as pltpu

F32 = jnp.float32
BF16 = jnp.bfloat16

NORM_EPS = 1e-6
ROPE_THETA = 10000.0

SSD_HEAD_DIM = 64
SSD_GROUPS = 8
SSD_STATE = 128
SSD_CONV = 4
SSD_CHUNK = 128
SSD_GROUPS_PER_STEP = 8

ATT_HEAD_DIM = 128
MOBA_BLOCK = 256
MOBA_TOPK = 3

TOP_K = 8
N_EXPERT_GROUPS = 8
TOPK_GROUPS = 4
ROUTED_SCALE = 2.5
MOE_ROWS = 512
MOE_TILE = 512
SEG_ALIGN = 16
SEG_CHUNKS = (512, 256, 128, 64, 32, 16)
FILL_CHUNKS = (1024,) + SEG_CHUNKS
PERM_WINDOW = 128

MERGE_TM, MERGE_TN = 512, 512
OUTPROJ_TILE = 512
OUTPROJ_SUBTILE = 256

LANES = 128
SUBLANES = 8
VMEM_LIMIT = 56 * 1024 * 1024
NEG_BIG = -1e30


def _cparams(sem):
    return pltpu.CompilerParams(dimension_semantics=sem, vmem_limit_bytes=VMEM_LIMIT)


def _silu(x):
    return x * jax.nn.sigmoid(x)


def _rms(x):
    return x * lax.rsqrt(jnp.mean(x * x, axis=-1, keepdims=True) + NORM_EPS)


def _split3(a):
    a1 = a.astype(BF16)
    r1 = a - a1.astype(F32)
    a2 = r1.astype(BF16)
    a3 = (r1 - a2.astype(F32)).astype(BF16)
    return a1, a2, a3


def _dot_exact_rhs(a, e_bf16):
    out = None
    for p in _split3(a):
        t = jnp.dot(p, e_bf16, preferred_element_type=F32)
        out = t if out is None else out + t
    return out


def _dot_nt(a, b):
    return lax.dot_general(a, b, (((1,), (1,)), ((), ())), preferred_element_type=F32)


def _dot_nt_x3(a, b):
    a1 = a.astype(BF16)
    a2 = (a - a1.astype(F32)).astype(BF16)
    b1 = b.astype(BF16)
    b2 = (b - b1.astype(F32)).astype(BF16)
    return _dot_nt(a1, b1) + (_dot_nt(a1, b2) + _dot_nt(a2, b1))


def _ada_kernel(c_ref, w_ref, b_ref, o_ref):
    cond = _silu(c_ref[...])
    o_ref[...] = jnp.dot(cond.astype(BF16), w_ref[...].astype(BF16),
                         preferred_element_type=F32) + b_ref[...]


def _ada(c, w_ada, b_ada):
    bsz, d = c.shape
    n = w_ada.shape[1]
    tn = 1024
    return pl.pallas_call(
        _ada_kernel,
        grid=(n // tn,),
        in_specs=[pl.BlockSpec((bsz, d), lambda j: (0, 0)),
                  pl.BlockSpec((d, tn), lambda j: (0, j)),
                  pl.BlockSpec((1, tn), lambda j: (0, j))],
        out_specs=pl.BlockSpec((bsz, tn), lambda j: (0, j)),
        out_shape=jax.ShapeDtypeStruct((bsz, n), F32),
        compiler_params=_cparams(("arbitrary",)),
        name="ada_mod",
    )(c, w_ada, b_ada.reshape(1, n))


def _prenorm_kernel(x_ref, g_ref, sc_ref, sh_ref, o_ref):
    y = _rms(x_ref[0]) * g_ref[...]
    o_ref[0] = (y * (1.0 + sc_ref[0]) + sh_ref[0]).astype(o_ref.dtype)


def _prenorm(x, g, sc, sh):
    bsz, s, d = x.shape
    ts = 512
    vec = pl.BlockSpec((1, 1, d), lambda b, i: (b, 0, 0))
    return pl.pallas_call(
        _prenorm_kernel,
        grid=(bsz, s // ts),
        in_specs=[pl.BlockSpec((1, ts, d), lambda b, i: (b, i, 0)),
                  pl.BlockSpec((1, d), lambda b, i: (0, 0)), vec, vec],
        out_specs=pl.BlockSpec((1, ts, d), lambda b, i: (b, i, 0)),
        out_shape=jax.ShapeDtypeStruct((bsz, s, d), BF16),
        compiler_params=_cparams(("arbitrary", "arbitrary")),
        name="prenorm",
    )(x, g.reshape(1, d), sc.reshape(bsz, 1, d), sh.reshape(bsz, 1, d))


def _mm_kernel(a_ref, w_ref, o_ref):
    o_ref[...] = jnp.dot(a_ref[...], w_ref[...], preferred_element_type=F32).astype(o_ref.dtype)


def _matmul(a, w, out_dtype, tm, tn, name):
    m, k = a.shape
    n = w.shape[1]
    return pl.pallas_call(
        _mm_kernel,
        grid=(n // tn, m // tm),
        in_specs=[pl.BlockSpec((tm, k), lambda j, i: (i, 0)),
                  pl.BlockSpec((k, tn), lambda j, i: (0, j))],
        out_specs=pl.BlockSpec((tm, tn), lambda j, i: (i, j)),
        out_shape=jax.ShapeDtypeStruct((m, n), out_dtype),
        compiler_params=_cparams(("arbitrary", "arbitrary")),
        name=name,
    )(a, w)


def _mm_wt_kernel(shift, a_ref, *refs):
    w_ref = refs[0]
    o_ref, wb = refs[-2:]

    @pl.when(pl.program_id(1) == 0)
    def _():
        w = w_ref[0]
        if shift:
            w = jnp.concatenate([w[shift:], refs[1][0]], axis=0)
        wb[...] = w.T.astype(BF16)

    o_ref[...] = jnp.dot(a_ref[...], wb[...], preferred_element_type=F32).astype(o_ref.dtype)


def _dt_kernel(a_ref, w_ref, o_ref):
    o_ref[...] = _dot_nt(w_ref[0].astype(BF16), a_ref[...])


def _dt_proj(a, wt_stack, layer, row0, nh, tm):
    m, k = a.shape
    assert row0 % nh == 0
    return pl.pallas_call(
        _dt_kernel,
        grid=(m // tm,),
        in_specs=[pl.BlockSpec((tm, k), lambda i: (i, 0)),
                  pl.BlockSpec((1, nh, k), lambda i: (layer, row0 // nh, 0))],
        out_specs=pl.BlockSpec((nh, tm), lambda i: (0, i)),
        out_shape=jax.ShapeDtypeStruct((nh, m), F32),
        compiler_params=_cparams(("arbitrary",)),
        name="in_proj_dt",
    )(a, wt_stack)


def _matmul_wt(a, wt_stack, layer, row0, n, tm, tn, out_dtype, name):
    m, k = a.shape
    shift = row0 % tn
    base = row0 - shift
    assert n % tn == 0 and shift % SUBLANES == 0
    a_spec = pl.BlockSpec((tm, k), lambda j, i: (i, 0))
    in_specs = [a_spec, pl.BlockSpec((1, tn, k), lambda j, i: (layer, base // tn + j, 0))]
    args = [a, wt_stack]
    if shift:
        assert (base + tn) % shift == 0 and tn % shift == 0
        in_specs.append(pl.BlockSpec((1, shift, k), lambda j, i: (layer, (base + (j + 1) * tn) // shift, 0)))
        args.append(wt_stack)
    return pl.pallas_call(
        functools.partial(_mm_wt_kernel, shift),
        grid=(n // tn, m // tm),
        in_specs=in_specs,
        out_specs=pl.BlockSpec((tm, tn), lambda j, i: (i, j)),
        out_shape=jax.ShapeDtypeStruct((m, n), out_dtype),
        scratch_shapes=[pltpu.VMEM((k, tn), BF16)],
        compiler_params=_cparams(("arbitrary", "arbitrary")),
        name=name,
    )(*args)


def _softplus(x):
    return jnp.maximum(x, 0.0) + jnp.log(1.0 + jnp.exp(-jnp.abs(x)))


def _ssd_kernel(x_ref, b_ref, c_ref, z_ref, dt_ref, cwx_ref, cwb_ref, cwc_ref,
                cbx_ref, cbb_ref, cbc_ref, dtb_ref, alog_ref, dsk_ref, nw_ref,
                o_ref, cbuf, state):
    q = SSD_CHUNK
    gps = dtb_ref.shape[0]
    gw = x_ref.shape[2] // gps
    nst = b_ref.shape[2] // gps
    hg = gw // SSD_HEAD_DIM
    pad = SUBLANES

    @pl.when(pl.program_id(2) == 0)
    def _():
        cbuf[:, 0:pad, :] = jnp.zeros((gps, pad, gw + 2 * nst), F32)
        state[...] = jnp.zeros_like(state)

    hrow = lax.broadcasted_iota(jnp.int32, (LANES, q), 0)
    ri = lax.broadcasted_iota(jnp.int32, (q, q), 0)
    ci = lax.broadcasted_iota(jnp.int32, (q, q), 1)
    triu = jnp.where(ri <= ci, 1.0, 0.0).astype(BF16)
    er = lax.broadcasted_iota(jnp.int32, (LANES, gw), 0)
    ec = lax.broadcasted_iota(jnp.int32, (LANES, gw), 1)
    e_ch = jnp.where(ec // SSD_HEAD_DIM == er, 1.0, 0.0).astype(BF16)
    er2 = lax.broadcasted_iota(jnp.int32, (LANES, hg * q), 0)
    ec2 = lax.broadcasted_iota(jnp.int32, (LANES, hg * q), 1)
    e_t = jnp.where(ec2 // q == er2, 1.0, 0.0).astype(BF16)
    ri2 = lax.broadcasted_iota(jnp.int32, (q, hg * q), 0)
    ci2 = lax.broadcasted_iota(jnp.int32, (q, hg * q), 1) % q
    first_head = lax.broadcasted_iota(jnp.int32, (1, LANES), 1) < SSD_HEAD_DIM

    for u in range(gps):
        xc = slice(u * gw, (u + 1) * gw)
        nc = slice(u * nst, (u + 1) * nst)
        xin = jnp.concatenate([x_ref[0, :, xc], b_ref[0, :, nc], c_ref[0, :, nc]],
                              axis=-1).astype(F32)
        cbuf[u, pad:pad + q, :] = xin
        w = jnp.concatenate([cwx_ref[:, xc], cwb_ref[:, nc], cwc_ref[:, nc]], axis=-1)
        bias = jnp.concatenate([cbx_ref[:, xc], cbb_ref[:, nc], cbc_ref[:, nc]], axis=-1)
        acc = bias + w[SSD_CONV - 1:SSD_CONV, :] * xin
        for s in range(1, SSD_CONV):
            acc = acc + w[SSD_CONV - 1 - s:SSD_CONV - s, :] * cbuf[u, pad - s:pad - s + q, :]
        cbuf[u, pad - (SSD_CONV - 1):pad, :] = cbuf[u, pad + q - (SSD_CONV - 1):pad + q, :]
        xbc = _silu(acc)
        xs = xbc[:, :gw]
        bm = xbc[:, gw:gw + nst].astype(BF16)
        cm = xbc[:, gw + nst:].astype(BF16)

        dt_raw = jnp.concatenate([dt_ref[u * hg:(u + 1) * hg, :], jnp.zeros((LANES - hg, q), F32)],
                                 axis=0)
        dt_r = jnp.where(hrow < hg, _softplus(dt_raw + dtb_ref[u]), 0.0)
        a_r = dt_r * (-math.log2(math.e) * jnp.exp(alog_ref[u]))
        acum_r = _dot_exact_rhs(a_r, triu)
        acum_c = acum_r.T
        dt_c = dt_r.T

        dt_full = _dot_exact_rhs(dt_c, e_ch)
        acum_colb = _dot_exact_rhs(acum_c, e_t)
        acum_full = jnp.concatenate(
            [jnp.where(first_head, acum_colb[:, (2 * pr) * q:(2 * pr + 1) * q],
                       acum_colb[:, (2 * pr + 1) * q:(2 * pr + 2) * q])
             for pr in range(hg // 2)], axis=-1)
        acum_rowb = jnp.concatenate(
            [jnp.broadcast_to(acum_r[h:h + 1, :], (q, q)) for h in range(hg)], axis=-1)

        lmat = jnp.exp2(jnp.where(ri2 >= ci2, acum_colb - acum_rowb, NEG_BIG))
        cb = _dot_nt(cm, bm)
        mcat = (jnp.concatenate([cb] * hg, axis=-1) * lmat).astype(BF16)
        xdt = xs * dt_full
        y_pairs = []
        for pr in range(gw // LANES):
            xp = xdt[:, pr * LANES:(pr + 1) * LANES]
            xbd = jnp.concatenate([jnp.where(first_head, xp, 0.0).astype(BF16),
                                   jnp.where(first_head, 0.0, xp).astype(BF16)], axis=0)
            y_pairs.append(jnp.dot(mcat[:, pr * 2 * q:(pr + 1) * 2 * q], xbd,
                                   preferred_element_type=F32))
        y = jnp.concatenate(y_pairs, axis=-1)

        st = state[u]
        y = y + jnp.dot(cm, st.astype(BF16), preferred_element_type=F32) * jnp.exp2(acum_full)
        acum_last = acum_full[q - 1:q, :]
        xw = (xdt * jnp.exp2(acum_last - acum_full)).astype(BF16)
        bt = xbc[:, gw:gw + nst].T.astype(BF16)
        state[u] = st * jnp.exp2(acum_last) + jnp.dot(bt, xw, preferred_element_type=F32)

        y = y + xs * dsk_ref[:, xc]
        y = y * _silu(z_ref[0, :, xc].astype(F32))
        o_ref[0, :, xc] = (_rms(y) * nw_ref[:, xc]).astype(o_ref.dtype)


def _ssd(proj, dt_rows, conv_w, conv_b, dt_bias, a_log, d_skip, norm_w, d_inner):
    bsz, s, _ = proj.shape
    g = SSD_GROUPS
    gps = SSD_GROUPS_PER_STEP
    assert SSD_CHUNK == LANES and 2 * SSD_HEAD_DIM == LANES
    gw = d_inner // g
    hg = gw // SSD_HEAD_DIM
    nst = SSD_STATE
    q = SSD_CHUNK
    nc = s // q
    bw, bn = gps * gw, gps * nst
    x0 = d_inner // bw
    b0 = (2 * d_inner) // bn
    c0 = (2 * d_inner + g * nst) // bn
    cwb0 = d_inner // bn
    cwc0 = (d_inner + g * nst) // bn
    conv_b2 = conv_b.reshape(1, -1)
    head_pad = lambda v: jnp.pad(v.reshape(g, hg), ((0, 0), (0, LANES - hg))).reshape(g, LANES, 1)
    return pl.pallas_call(
        _ssd_kernel,
        grid=(bsz, g // gps, nc),
        in_specs=[
            pl.BlockSpec((1, q, bw), lambda b, gi, c: (b, c, x0 + gi)),
            pl.BlockSpec((1, q, bn), lambda b, gi, c: (b, c, b0 + gi)),
            pl.BlockSpec((1, q, bn), lambda b, gi, c: (b, c, c0 + gi)),
            pl.BlockSpec((1, q, bw), lambda b, gi, c: (b, c, gi)),
            pl.BlockSpec((gps * hg, q), lambda b, gi, c: (gi, b * nc + c)),
            pl.BlockSpec((SSD_CONV, bw), lambda b, gi, c: (0, gi)),
            pl.BlockSpec((SSD_CONV, bn), lambda b, gi, c: (0, cwb0 + gi)),
            pl.BlockSpec((SSD_CONV, bn), lambda b, gi, c: (0, cwc0 + gi)),
            pl.BlockSpec((1, bw), lambda b, gi, c: (0, gi)),
            pl.BlockSpec((1, bn), lambda b, gi, c: (0, cwb0 + gi)),
            pl.BlockSpec((1, bn), lambda b, gi, c: (0, cwc0 + gi)),
            pl.BlockSpec((gps, LANES, 1), lambda b, gi, c: (gi, 0, 0)),
            pl.BlockSpec((gps, LANES, 1), lambda b, gi, c: (gi, 0, 0)),
            pl.BlockSpec((1, bw), lambda b, gi, c: (0, gi)),
            pl.BlockSpec((1, bw), lambda b, gi, c: (0, gi)),
        ],
        out_specs=pl.BlockSpec((1, q, bw), lambda b, gi, c: (b, c, gi)),
        out_shape=jax.ShapeDtypeStruct((bsz, s, d_inner), BF16),
        scratch_shapes=[pltpu.VMEM((gps, SUBLANES + q, gw + 2 * nst), F32),
                        pltpu.VMEM((gps, nst, gw), F32)],
        compiler_params=_cparams(("arbitrary", "arbitrary", "arbitrary")),
        name="ssd_scan",
    )(proj, proj, proj, proj, dt_rows, conv_w, conv_w, conv_w, conv_b2, conv_b2, conv_b2,
      head_pad(dt_bias), head_pad(a_log),
      jnp.repeat(d_skip, SSD_HEAD_DIM).reshape(1, d_inner), norm_w.reshape(1, d_inner))


def _rope_kernel(pos_ref, inv_ref, cos_ref, sin_ref):
    ang = pos_ref[...].astype(F32) * inv_ref[...]
    lane = lax.broadcasted_iota(jnp.int32, ang.shape, 1)
    cos_ref[...] = jnp.cos(ang)
    sin_ref[...] = jnp.where(lane < ATT_HEAD_DIM // 2, -1.0, 1.0) * jnp.sin(ang)


def _rope_tables(positions):
    t = positions.size
    half = ATT_HEAD_DIM // 2
    inv = 1.0 / (ROPE_THETA ** (jnp.arange(half, dtype=F32) / half))
    inv2 = jnp.concatenate([inv, inv]).reshape(1, ATT_HEAD_DIM)
    tt = min(2048, t)
    return pl.pallas_call(
        _rope_kernel,
        grid=(t // tt,),
        in_specs=[pl.BlockSpec((tt, 1), lambda i: (i, 0)),
                  pl.BlockSpec((1, ATT_HEAD_DIM), lambda i: (0, 0))],
        out_specs=[pl.BlockSpec((tt, ATT_HEAD_DIM), lambda i: (i, 0))] * 2,
        out_shape=[jax.ShapeDtypeStruct((t, ATT_HEAD_DIM), F32)] * 2,
        compiler_params=_cparams(("arbitrary",)),
        name="rope_tables",
    )(positions.reshape(t, 1), inv2)


def _rot(x, cos2, sin2):
    return x * cos2 + pltpu.roll(x, ATT_HEAD_DIM // 2, axis=1) * sin2


def _attn_kernel(q_ref, k_ref, v_ref, cos_ref, sin_ref, o_ref, krot, vt, kmean):
    blk = MOBA_BLOCK
    nb = k_ref.shape[1] // blk
    log2e_scale = ATT_HEAD_DIM ** -0.5 * math.log2(math.e)

    kmean[...] = jnp.zeros_like(kmean)
    for j in range(nb):
        rows = slice(j * blk, (j + 1) * blk)
        kr = _rot(k_ref[0, rows, :].astype(F32), cos_ref[0, rows, :], sin_ref[0, rows, :])
        krot[rows, :] = kr.astype(BF16)
        kmean[j:j + 1, :] = jnp.mean(kr, axis=0, keepdims=True)
        vt[:, rows] = v_ref[0, rows, :].astype(F32).T.astype(BF16)
    km = kmean[...]

    ki = lax.broadcasted_iota(jnp.int32, (blk, blk), 0)
    qj = lax.broadcasted_iota(jnp.int32, (blk, blk), 1)
    sub = lax.broadcasted_iota(jnp.int32, (km.shape[0], blk), 0)
    for qi in range(nb):
        qrows = slice(qi * blk, (qi + 1) * blk)
        qr = _rot(q_ref[0, qrows, :].astype(F32), cos_ref[0, qrows, :], sin_ref[0, qrows, :])
        qs = (qr * log2e_scale).T.astype(BF16)
        gate = _dot_nt_x3(km, qr)
        scores = [jnp.where(ki <= qj, jnp.dot(krot[qrows, :], qs, preferred_element_type=F32), NEG_BIG)]
        for j in range(qi):
            gj = gate[j:j + 1, :]
            ahead = (gate > gj) | ((gate == gj) & (sub < j))
            rank = jnp.sum(jnp.where((sub < qi) & ahead, 1.0, 0.0), axis=0, keepdims=True)
            bias = jnp.where(rank < MOBA_TOPK, 0.0, NEG_BIG)
            scores.append(jnp.dot(krot[j * blk:(j + 1) * blk, :], qs,
                                  preferred_element_type=F32) + bias)
        m = scores[0]
        for sj in scores[1:]:
            m = jnp.maximum(m, sj)
        m = jnp.max(m, axis=0, keepdims=True)
        l = None
        acc = None
        for j, sj in enumerate(scores):
            src = qi if j == 0 else j - 1
            p = jnp.exp2(sj - m)
            lj = jnp.sum(p, axis=0, keepdims=True)
            aj = jnp.dot(vt[:, src * blk:(src + 1) * blk], p.astype(BF16),
                         preferred_element_type=F32)
            l = lj if l is None else l + lj
            acc = aj if acc is None else acc + aj
        o_ref[0, qrows, :] = (acc / l).T.astype(o_ref.dtype)


def _attention(proj, cos2, sin2, q0, k0, v0, n_heads):
    bsz, s, _ = proj.shape
    dh = ATT_HEAD_DIM
    col = lambda c0: pl.BlockSpec((1, s, dh), lambda b, h: (b, 0, c0 + h))
    tab = pl.BlockSpec((1, s, dh), lambda b, h: (b, 0, 0))
    return pl.pallas_call(
        _attn_kernel,
        grid=(bsz, n_heads),
        in_specs=[col(q0), col(k0), col(v0), tab, tab],
        out_specs=pl.BlockSpec((1, s, dh), lambda b, h: (b, 0, h)),
        out_shape=jax.ShapeDtypeStruct((bsz, s, n_heads * dh), BF16),
        scratch_shapes=[pltpu.VMEM((s, dh), BF16), pltpu.VMEM((dh, s), BF16),
                        pltpu.VMEM((-(-(s // MOBA_BLOCK) // SUBLANES) * SUBLANES, dh), F32)],
        compiler_params=_cparams(("arbitrary", "arbitrary")),
        name="moba_attn",
    )(proj, proj, proj, cos2.reshape(bsz, s, dh), sin2.reshape(bsz, s, dh))


def _merge_kernel(ys_ref, ya_ref, w1_ref, w2_ref, gs_ref, ga_ref, o_ref):
    b1 = jnp.dot(ys_ref[...], w1_ref[...], preferred_element_type=F32)
    b2 = jnp.dot(ya_ref[...], w2_ref[...], preferred_element_type=F32)
    o_ref[...] = (jax.nn.sigmoid(gs_ref[...].astype(F32)) * b1
                  + jax.nn.sigmoid(ga_ref[...].astype(F32)) * b2).astype(o_ref.dtype)


def _merge(y_ssd, y_att, w1, w2, proj2d, gs0, ga0):
    m, k1 = y_ssd.shape
    k2 = y_att.shape[1]
    n = w1.shape[1]
    tm, tn = MERGE_TM, MERGE_TN
    return pl.pallas_call(
        _merge_kernel,
        grid=(n // tn, m // tm),
        in_specs=[pl.BlockSpec((tm, k1), lambda j, i: (i, 0)),
                  pl.BlockSpec((tm, k2), lambda j, i: (i, 0)),
                  pl.BlockSpec((k1, tn), lambda j, i: (0, j)),
                  pl.BlockSpec((k2, tn), lambda j, i: (0, j)),
                  pl.BlockSpec((tm, tn), lambda j, i: (i, gs0 // tn + j)),
                  pl.BlockSpec((tm, tn), lambda j, i: (i, ga0 // tn + j))],
        out_specs=pl.BlockSpec((tm, tn), lambda j, i: (i, j)),
        out_shape=jax.ShapeDtypeStruct((m, n), BF16),
        compiler_params=_cparams(("arbitrary", "arbitrary")),
        name="branch_merge",
    )(y_ssd, y_att, w1, w2, proj2d, proj2d)


def _outproj_kernel(m_ref, w_ref, x_ref, gpost_ref, gt_ref, gpre_ref, sc_ref, sh_ref, wr_ref,
                    x1_ref, h2b_ref, lg_ref):
    for r in range(m_ref.shape[1] // OUTPROJ_SUBTILE):
        rows = slice(r * OUTPROJ_SUBTILE, (r + 1) * OUTPROJ_SUBTILE)
        y = jnp.dot(m_ref[0, rows, :], w_ref[...], preferred_element_type=F32)
        x1 = x_ref[0, rows, :] + gt_ref[0] * (_rms(y) * gpost_ref[...])
        h2 = (_rms(x1) * gpre_ref[...]) * (1.0 + sc_ref[0]) + sh_ref[0]
        x1_ref[0, rows, :] = x1
        h2b_ref[0, rows, :] = h2.astype(BF16)
        h_hi = h2.astype(BF16)
        h_lo = (h2 - h_hi.astype(F32)).astype(BF16)
        wr = wr_ref[...]
        w_hi = wr.astype(BF16)
        w_lo = (wr - w_hi.astype(F32)).astype(BF16)
        lg = (jnp.dot(h_hi, w_hi, preferred_element_type=F32)
              + (jnp.dot(h_hi, w_lo, preferred_element_type=F32)
                 + jnp.dot(h_lo, w_hi, preferred_element_type=F32)))
        lg_ref[:, rows] = lg.T[:lg_ref.shape[0], :]


def _outproj(merged, w_out, x, g_post, gt1, g_pre, sc2, sh2, w_router):
    bsz, s, d = x.shape
    ne = w_router.shape[1]
    w_router_p = jnp.pad(w_router, ((0, 0), (0, LANES - ne)))
    tm = OUTPROJ_TILE
    nt = s // tm
    row = pl.BlockSpec((1, tm, d), lambda i: (i // nt, i % nt, 0))
    vec = pl.BlockSpec((1, 1, d), lambda i: (i // nt, 0, 0))
    par = pl.BlockSpec((1, d), lambda i: (0, 0))
    return pl.pallas_call(
        _outproj_kernel,
        grid=(bsz * nt,),
        in_specs=[row, pl.BlockSpec((d, d), lambda i: (0, 0)), row, par, vec, par, vec, vec,
                  pl.BlockSpec((d, LANES), lambda i: (0, 0))],
        out_specs=[row, row, pl.BlockSpec((ne, tm), lambda i: (0, i))],
        out_shape=[jax.ShapeDtypeStruct((bsz, s, d), F32),
                   jax.ShapeDtypeStruct((bsz, s, d), BF16),
                   jax.ShapeDtypeStruct((ne, bsz * s), F32)],
        compiler_params=_cparams(("arbitrary",)),
        name="out_proj_norms",
    )(merged.reshape(bsz, s, d), w_out, x, g_post.reshape(1, d), gt1.reshape(bsz, 1, d),
      g_pre.reshape(1, d), sc2.reshape(bsz, 1, d), sh2.reshape(bsz, 1, d), w_router_p)


def _router_kernel(lg_ref, rb_ref, w_ref, rank_ref, cnt_ref, gsc):
    ne, tn = lg_ref.shape
    ng = N_EXPERT_GROUPS
    eg = ne // ng
    scores = jax.nn.sigmoid(lg_ref[...])
    biased = scores + rb_ref[...]
    i8 = lax.broadcasted_iota(jnp.int32, (eg, tn), 0)
    for g in range(ng):
        v = biased[g * eg:(g + 1) * eg, :]
        m1 = jnp.max(v, axis=0, keepdims=True)
        first = jnp.min(jnp.where(v == m1, i8, eg), axis=0, keepdims=True)
        m2 = jnp.max(jnp.where(i8 == first, -jnp.inf, v), axis=0, keepdims=True)
        gsc[g:g + 1, :] = m1 + m2
    gs = gsc[...]
    gi = lax.broadcasted_iota(jnp.int32, (ng, tn), 0)
    masked = []
    for g in range(ng):
        sg = gs[g:g + 1, :]
        ahead = (gs > sg) | ((gs == sg) & (gi < g))
        rank = jnp.sum(jnp.where(ahead, 1.0, 0.0), axis=0, keepdims=True)
        masked.append(jnp.where(rank < TOPK_GROUPS, biased[g * eg:(g + 1) * eg, :], -jnp.inf))
    cur = jnp.concatenate(masked, axis=0)
    sub = lax.broadcasted_iota(jnp.int32, (ne, tn), 0)
    sel = jnp.zeros((ne, tn), F32)
    for k in range(TOP_K):
        mx = jnp.max(cur, axis=0, keepdims=True)
        idx = jnp.min(jnp.where(cur == mx, sub, ne), axis=0, keepdims=True)
        hit = sub == idx
        sel = jnp.where(hit, 1.0, sel)
        cur = jnp.where(hit, -jnp.inf, cur)
    picked = sel * scores
    wsum = jnp.sum(picked, axis=0, keepdims=True)
    w_ref[...] = picked / wsum * ROUTED_SCALE
    ti = lax.broadcasted_iota(jnp.int32, (tn, tn), 0)
    tj = lax.broadcasted_iota(jnp.int32, (tn, tn), 1)
    before = jnp.where(ti < tj, 1.0, 0.0).astype(BF16)
    rank = jnp.dot(sel.astype(BF16), before, preferred_element_type=F32)
    rank_ref[...] = jnp.where(sel > 0.0, rank, NEG_BIG)
    cnt_ref[0] = jnp.broadcast_to(jnp.sum(sel, axis=1, keepdims=True), (ne, LANES))


def _router(logits_t, router_bias):
    ne, t = logits_t.shape
    tn = MOE_TILE
    tile = pl.BlockSpec((ne, tn), lambda i: (0, i))
    return pl.pallas_call(
        _router_kernel,
        grid=(t // tn,),
        in_specs=[tile, pl.BlockSpec((ne, 1), lambda i: (0, 0))],
        out_specs=[tile, tile, pl.BlockSpec((1, ne, LANES), lambda i: (i, 0, 0))],
        out_shape=[jax.ShapeDtypeStruct((ne, t), F32), jax.ShapeDtypeStruct((ne, t), F32),
                   jax.ShapeDtypeStruct((t // tn, ne, LANES), F32)],
        scratch_shapes=[pltpu.VMEM((N_EXPERT_GROUPS, tn), F32)],
        compiler_params=_cparams(("arbitrary",)),
        name="router_topk",
    )(logits_t, router_bias.reshape(ne, 1))


def _tile_rows(n_experts):
    return MOE_TILE * TOP_K + n_experts * SEG_ALIGN


def _spare_rows(n_experts):
    return -(-(n_experts * SEG_ALIGN) // MOE_ROWS) * MOE_ROWS


def _segment_copies(src, dst, sem, n, src_off, dst_off, wait, chunks=SEG_CHUNKS):
    off = jnp.int32(0)
    for p in chunks:
        bit = n & p

        @pl.when(bit != 0)
        def _(p=p, off=off):
            cp = pltpu.make_async_copy(
                src.at[pl.ds(pl.multiple_of(src_off + off, SEG_ALIGN), p), :],
                dst.at[pl.ds(pl.multiple_of(dst_off + off, SEG_ALIGN), p), :], sem)
            if wait:
                cp.wait()
            else:
                cp.start()
        off = off + bit


def _for_each_segment(ne, fn):
    def body(e, carry):
        fn(e)
        return carry
    lax.fori_loop(0, ne, body, 0)


def _build_perm(pbuf, posb, ss_ref, c16_ref, tile, ne, value_row):
    tn = posb.shape[1]
    pbuf[...] = jnp.zeros_like(pbuf)
    win = lax.broadcasted_iota(jnp.int32, (PERM_WINDOW, tn), 0)

    def per_expert(e):
        base = ss_ref[tile, e]
        windows = lax.shift_right_logical(c16_ref[tile, e] + (PERM_WINDOW - 1),
                                          PERM_WINDOW.bit_length() - 1)
        prow = posb[pl.ds(e, 1), :]
        vrow = value_row(e)

        def per_window(g, carry):
            r0 = pl.multiple_of(base + g * PERM_WINDOW, SEG_ALIGN)
            hit = prow == (win + r0).astype(F32)
            pbuf[pl.ds(r0, PERM_WINDOW), :] = jnp.where(hit, vrow, 0.0).astype(BF16)
            return carry
        lax.fori_loop(0, windows, per_window, 0)
    _for_each_segment(ne, per_expert)


def _dispatch_kernel(ss_ref, c16_ref, go_ref, ts_ref, tl_ref, rank_ref, ssv_ref, h_ref, xs_hbm,
                     pbuf, xs, posb, sem):
    i = pl.program_id(0)
    n = pl.num_programs(0)
    ne = rank_ref.shape[0]
    tn = rank_ref.shape[1]
    posb[...] = rank_ref[...] + ssv_ref[0]
    _build_perm(pbuf, posb, ss_ref, c16_ref, i, ne, lambda e: 1.0)

    def drain():
        pltpu.make_async_copy(xs, xs_hbm.at[pl.ds(0, xs.shape[0]), :], sem).wait()

    pl.when(i > 0)(drain)

    used = ss_ref[i, ne - 1] + c16_ref[i, ne - 1]
    def chunk(ch):
        rows = slice(ch * tn, (ch + 1) * tn)
        xs[rows, :] = jnp.dot(pbuf[rows, :], h_ref[...], preferred_element_type=F32).astype(BF16)

    for ch in range(TOP_K):
        chunk(ch)
    for ch in range(TOP_K, xs.shape[0] // tn):
        pl.when(ch * tn < used)(functools.partial(chunk, ch))

    _for_each_segment(ne, lambda e: _segment_copies(
        xs, xs_hbm, sem, c16_ref[i, e], ss_ref[i, e], go_ref[i, e], False))
    _segment_copies(xs, xs_hbm, sem, xs.shape[0] - used, 0, xs_hbm.shape[0] - _spare_rows(ne),
                    False, FILL_CHUNKS)

    @pl.when(i == n - 1)
    def _():
        drain()
        xs[0:MOE_ROWS, :] = jnp.zeros((MOE_ROWS, xs.shape[1]), BF16)
        first_free = lax.shift_right_logical(ts_ref[ne], MOE_ROWS.bit_length() - 1)
        n_blocks = xs_hbm.shape[0] // MOE_ROWS
        for wait in (False, True):
            _for_each_segment(ne, lambda e: _segment_copies(
                xs, xs_hbm, sem, tl_ref[e], 0, ts_ref[e], wait))

            def free_block(b, carry, wait=wait):
                cp = pltpu.make_async_copy(
                    xs.at[pl.ds(0, MOE_ROWS), :],
                    xs_hbm.at[pl.ds(pl.multiple_of(b * MOE_ROWS, MOE_ROWS), MOE_ROWS), :], sem)
                if wait:
                    cp.wait()
                else:
                    cp.start()
                return carry
            lax.fori_loop(first_free, n_blocks, free_block, 0)


def _dispatch(seg_start, c16, goff, tail_start, tail_len, rank, h2b, n_rows):
    ne, t = rank.shape
    d = h2b.shape[1]
    tn = MOE_TILE
    rt = _tile_rows(ne)
    grid_spec = pltpu.PrefetchScalarGridSpec(
        num_scalar_prefetch=5,
        grid=(t // tn,),
        in_specs=[pl.BlockSpec((ne, tn), lambda i, *_: (0, i)),
                  pl.BlockSpec((1, ne, 1), lambda i, *_: (i, 0, 0)),
                  pl.BlockSpec((tn, d), lambda i, *_: (i, 0))],
        out_specs=pl.BlockSpec(memory_space=pl.ANY),
        scratch_shapes=[pltpu.VMEM((rt + PERM_WINDOW, tn), BF16), pltpu.VMEM((rt, d), BF16),
                        pltpu.VMEM((ne, tn), F32), pltpu.SemaphoreType.DMA],
    )
    return pl.pallas_call(
        _dispatch_kernel,
        grid_spec=grid_spec,
        out_shape=jax.ShapeDtypeStruct((n_rows, d), BF16),
        compiler_params=_cparams(("arbitrary",)),
        name="moe_dispatch",
    )(seg_start, c16, goff, tail_start, tail_len, rank,
      seg_start.astype(F32).reshape(t // tn, ne, 1), h2b)


def _combine_kernel(ss_ref, c16_ref, go_ref, rank_ref, w_ref, ssv_ref, y_hbm, o_ref,
                    pbuf, ys, posb, sem):
    i = pl.program_id(0)
    ne = rank_ref.shape[0]
    tn = rank_ref.shape[1]
    used = ss_ref[i, ne - 1] + c16_ref[i, ne - 1]
    _for_each_segment(ne, lambda e: _segment_copies(
        y_hbm, ys, sem, c16_ref[i, e], go_ref[i, e], ss_ref[i, e], False))
    _segment_copies(y_hbm, ys, sem, ys.shape[0] - used, 0, used, False, FILL_CHUNKS)
    posb[...] = rank_ref[...] + ssv_ref[0]
    _build_perm(pbuf, posb, ss_ref, c16_ref, i, ne, lambda e: w_ref[pl.ds(e, 1), :])
    pltpu.make_async_copy(y_hbm.at[pl.ds(0, ys.shape[0]), :], ys, sem).wait()
    def chunk(ch):
        rows = slice(ch * tn, (ch + 1) * tn)
        return lax.dot_general(pbuf[rows, :], ys[rows, :], (((0,), (0,)), ((), ())),
                               preferred_element_type=F32)

    always = TOP_K
    acc = chunk(0)
    for ch in range(1, always):
        acc = acc + chunk(ch)
    o_ref[...] = acc
    for ch in range(always, ys.shape[0] // tn):
        @pl.when(ch * tn < used)
        def _(ch=ch):
            o_ref[...] += chunk(ch)


def _combine(seg_start, c16, goff, rank, selw, y_rows):
    ne, t = rank.shape
    d = y_rows.shape[1]
    tn = MOE_TILE
    rt = _tile_rows(ne)
    tile = pl.BlockSpec((ne, tn), lambda i, *_: (0, i))
    grid_spec = pltpu.PrefetchScalarGridSpec(
        num_scalar_prefetch=3,
        grid=(t // tn,),
        in_specs=[tile, tile, pl.BlockSpec((1, ne, 1), lambda i, *_: (i, 0, 0)),
                  pl.BlockSpec(memory_space=pl.ANY)],
        out_specs=pl.BlockSpec((tn, d), lambda i, *_: (i, 0)),
        scratch_shapes=[pltpu.VMEM((rt + PERM_WINDOW, tn), BF16), pltpu.VMEM((rt, d), BF16),
                        pltpu.VMEM((ne, tn), F32), pltpu.SemaphoreType.DMA],
    )
    return pl.pallas_call(
        _combine_kernel,
        grid_spec=grid_spec,
        out_shape=jax.ShapeDtypeStruct((t, d), F32),
        compiler_params=_cparams(("arbitrary",)),
        name="moe_combine",
    )(seg_start, c16, goff, rank, selw, seg_start.astype(F32).reshape(t // tn, ne, 1), y_rows)


EXPERT_LOOKAHEAD = 2


def _expert_kernel(be_ref, nv_ref, sl_ref, x_ref, wg_ref, wu_ref, wd_ref, o_ref, wg, wu, wd):
    s = pl.program_id(0)
    n = pl.num_programs(0) - EXPERT_LOOKAHEAD
    clampi = lambda j: jnp.clip(j, 0, n - 1)

    def starts_expert(j):
        jc = clampi(j)
        first = (jc == 0) | (be_ref[jc] != be_ref[clampi(jc - 1)])
        return (j >= 0) & (j < n) & (nv_ref[jc] > 0) & first

    i = s - EXPERT_LOOKAHEAD
    ic = clampi(i)
    slot = sl_ref[ic]

    @pl.when(starts_expert(i))
    def _():
        wd[slot] = wd_ref[0].astype(BF16)

    @pl.when((i >= 0) & (nv_ref[ic] == 0))
    def _():
        o_ref[...] = jnp.zeros_like(o_ref)

    @pl.when((i >= 0) & (nv_ref[ic] > 0))
    def _():
        x = x_ref[...]
        gate = jnp.dot(x, wg[slot], preferred_element_type=F32)
        up = jnp.dot(x, wu[slot], preferred_element_type=F32)
        act = (_silu(gate) * up).astype(BF16)
        o_ref[...] = jnp.dot(act, wd[slot], preferred_element_type=F32).astype(o_ref.dtype)

    @pl.when(starts_expert(s))
    def _():
        wg[sl_ref[clampi(s)]] = wg_ref[0].astype(BF16)

    @pl.when(starts_expert(s - 1))
    def _():
        wu[sl_ref[clampi(s - 1)]] = wu_ref[0].astype(BF16)


def _expert_ffn(blk_exp, blk_valid, blk_slot, x_sorted, w_gate, w_up, w_down):
    n_rows, d = x_sorted.shape
    f = w_gate.shape[2]
    rows = MOE_ROWS
    n = n_rows // rows
    blk = lambda back: (lambda s, be, nv, sl: (jnp.clip(s - back, 0, n - 1), 0))
    wgt = lambda back: (lambda s, be, nv, sl: (be[jnp.clip(s - back, 0, n - 1)], 0, 0))
    grid_spec = pltpu.PrefetchScalarGridSpec(
        num_scalar_prefetch=3,
        grid=(n + EXPERT_LOOKAHEAD,),
        in_specs=[pl.BlockSpec((rows, d), blk(2)),
                  pl.BlockSpec((1, d, f), wgt(0)),
                  pl.BlockSpec((1, d, f), wgt(1)),
                  pl.BlockSpec((1, f, d), wgt(2))],
        out_specs=pl.BlockSpec((rows, d), blk(2)),
        scratch_shapes=[pltpu.VMEM((2, d, f), BF16), pltpu.VMEM((2, d, f), BF16),
                        pltpu.VMEM((2, f, d), BF16)],
    )
    return pl.pallas_call(
        _expert_kernel,
        grid_spec=grid_spec,
        out_shape=jax.ShapeDtypeStruct((n_rows, d), BF16),
        compiler_params=_cparams(("arbitrary",)),
        name="expert_ffn",
    )(blk_exp, blk_valid, blk_slot, x_sorted, w_gate, w_up, w_down)


def _shared_kernel(x_ref, wg_ref, wu_ref, wd_ref, r_ref, x1_ref, gt_ref, g_ref, o_ref):
    x = x_ref[0]
    gate = jnp.dot(x, wg_ref[...], preferred_element_type=F32)
    up = jnp.dot(x, wu_ref[...], preferred_element_type=F32)
    act = (_silu(gate) * up).astype(BF16)
    y = r_ref[0] + jnp.dot(act, wd_ref[...], preferred_element_type=F32)
    o_ref[0] = x1_ref[0] + gt_ref[0] * (_rms(y) * g_ref[...])


def _shared_final(h2b, wg, wu, wd, routed, x1, gt2, g_post):
    bsz, s, d = x1.shape
    f = wg.shape[1]
    tm = 512
    nt = s // tm
    row = pl.BlockSpec((1, tm, d), lambda i: (i // nt, i % nt, 0))
    full = lambda shape: pl.BlockSpec(shape, lambda i: (0, 0))
    return pl.pallas_call(
        _shared_kernel,
        grid=(bsz * nt,),
        in_specs=[row, full((d, f)), full((d, f)), full((f, d)), row, row,
                  pl.BlockSpec((1, 1, d), lambda i: (i // nt, 0, 0)), full((1, d))],
        out_specs=row,
        out_shape=jax.ShapeDtypeStruct((bsz, s, d), F32),
        compiler_params=_cparams(("arbitrary",)),
        name="shared_ffn_final",
    )(h2b, wg, wu, wd, routed.reshape(bsz, s, d), x1, gt2.reshape(bsz, 1, d), g_post.reshape(1, d))


def _dispatch_plan(counts, n_blocks):
    n_tiles, ne = counts.shape
    c16 = (counts + SEG_ALIGN - 1) // SEG_ALIGN * SEG_ALIGN
    seg_start = jnp.cumsum(c16, axis=1) - c16
    tot = jnp.sum(c16, axis=0)
    padded = (tot + MOE_ROWS - 1) // MOE_ROWS * MOE_ROWS
    pad_end = jnp.cumsum(padded)
    pad_start = pad_end - padded
    goff = pad_start[None, :] + jnp.cumsum(c16, axis=0) - c16
    blk_row = jnp.arange(n_blocks, dtype=jnp.int32) * MOE_ROWS
    blk_exp = jnp.sum(pad_end[None, :] <= blk_row[:, None], axis=1)
    blk_exp = jnp.minimum(blk_exp, ne - 1).astype(jnp.int32)
    owns = (pad_start[None, :] <= blk_row[:, None]) & (blk_row[:, None] < pad_end[None, :])
    rows_left = jnp.clip((pad_start + tot)[None, :] - blk_row[:, None], 0, MOE_ROWS)
    blk_valid = jnp.sum(jnp.where(owns, rows_left, 0), axis=1).astype(jnp.int32)
    blk_slot = jnp.sum((padded > 0)[None, :] & (pad_start[None, :] <= blk_row[:, None]), axis=1) % 2
    i32 = lambda v: v.astype(jnp.int32)
    tail_start = jnp.concatenate([pad_start + tot, pad_end[-1:]])
    return (i32(seg_start), i32(c16), i32(goff), i32(tail_start), i32(padded - tot),
            blk_exp, blk_valid, i32(blk_slot))


def kernel(x, c, positions, w_ada, b_ada, g_pre_mix, g_post_mix, g_pre_ffn, g_post_ffn, w_in,
           conv_w, conv_b, dt_bias, a_log, d_skip, ssd_norm_w, w_ssd_br, w_attn_br, w_out,
           w_router, router_bias, w_gate, w_up, w_down, ws_gate, ws_up, ws_down):
    bsz, s, d = x.shape
    t = bsz * s
    for l in range(w_ada.shape[0]):
        d_inner = w_ssd_br.shape[1]
        att_w = w_attn_br.shape[1]
        n_ssd_heads = dt_bias.shape[1]
        conv_dim = conv_w.shape[2]
        n_heads = att_w // ATT_HEAD_DIM
        n_experts = w_router.shape[2]

        mod = _ada(c, w_ada[l], b_ada[l])
        sh1, sc1, gt1, sh2, sc2, gt2 = jnp.split(mod, 6, axis=-1)

        o_dt = d_inner + conv_dim
        o_q = o_dt + n_ssd_heads
        w_in_t = jnp.swapaxes(w_in, 1, 2)
        h1 = _prenorm(x, g_pre_mix[l], sc1, sh1).reshape(t, d)
        proj_s = _matmul_wt(h1, w_in_t, l, 0, o_dt, 1024, 1024, BF16, "in_proj_ssd")
        proj_a = _matmul_wt(h1, w_in_t, l, o_q, w_in.shape[2] - o_q, 1024, 1024, BF16, "in_proj_att")
        dt_t = _dt_proj(h1, w_in_t, l, o_dt, n_ssd_heads, 1024)
        k_off = att_w
        v_off = k_off + att_w
        gs_off = v_off + att_w
        ga_off = gs_off + d

        y_ssd = _ssd(proj_s.reshape(bsz, s, o_dt), dt_t, conv_w[l], conv_b[l], dt_bias[l], a_log[l],
                     d_skip[l], ssd_norm_w[l], d_inner)

        cos2, sin2 = _rope_tables(positions)
        y_att = _attention(proj_a.reshape(bsz, s, -1), cos2, sin2, 0, k_off // ATT_HEAD_DIM,
                           v_off // ATT_HEAD_DIM, n_heads)

        merged = _merge(y_ssd.reshape(t, d_inner), y_att.reshape(t, att_w),
                        w_ssd_br[l].astype(BF16), w_attn_br[l].astype(BF16), proj_a, gs_off, ga_off)
        x1, h2b, logits_t = _outproj(merged, w_out[l].astype(BF16), x, g_post_mix[l], gt1,
                                         g_pre_ffn[l], sc2, sh2, w_router[l])

        selw, rank, cnt = _router(logits_t, router_bias[l])
        n_tiles = t // MOE_TILE
        n_blocks = (-(-(t * TOP_K + n_tiles * n_experts * (SEG_ALIGN - 1)) // MOE_ROWS) + n_experts
                    + _spare_rows(n_experts) // MOE_ROWS)
        seg_start, c16, goff, tail_start, tail_len, blk_exp, blk_valid, blk_slot = _dispatch_plan(
            cnt[:, :, 0].astype(jnp.int32), n_blocks)
        x_sorted = _dispatch(seg_start, c16, goff, tail_start, tail_len, rank, h2b.reshape(t, d),
                             n_blocks * MOE_ROWS)
        y_rows = _expert_ffn(blk_exp, blk_valid, blk_slot, x_sorted, w_gate[l], w_up[l], w_down[l])
        routed = _combine(seg_start, c16, goff, rank, selw, y_rows)
        x = _shared_final(h2b, ws_gate[l].astype(BF16), ws_up[l].astype(BF16),
                          ws_down[l].astype(BF16), routed, x1, gt2, g_post_ffn[l])
    return x
```

```python
import functools
import math

import jax
import jax.numpy as jnp
from jax import lax
from jax.experimental import pallas as pl
from jax.experimental.pallas import tpu as pltpu

F32 = jnp.float32
BF16 = jnp.bfloat16

NORM_EPS = 1e-6
ROPE_THETA = 10000.0

SSD_HEAD_DIM = 64
SSD_GROUPS = 8
SSD_STATE = 128
SSD_CONV = 4
SSD_CHUNK = 128
SSD_GROUPS_PER_STEP = 8

ATT_HEAD_DIM = 128
MOBA_BLOCK = 256
MOBA_TOPK = 3

TOP_K = 8
N_EXPERT_GROUPS = 8
TOPK_GROUPS = 4
ROUTED_SCALE = 2.5
MOE_ROWS = 512
MOE_TILE = 512
SEG_ALIGN = 16
SEG_CHUNKS = (512, 256, 128, 64, 32, 16)
PERM_WINDOW = 128

MERGE_TM, MERGE_TN = 512, 512
OUTPROJ_TILE = 512
OUTPROJ_SUBTILE = 256

LANES = 128
SUBLANES = 8
VMEM_LIMIT = 56 * 1024 * 1024
NEG_BIG = -1e30


def _cparams(sem):
    return pltpu.CompilerParams(dimension_semantics=sem, vmem_limit_bytes=VMEM_LIMIT)


def _silu(x):
    return x * jax.nn.sigmoid(x)


def _rms(x):
    return x * lax.rsqrt(jnp.mean(x * x, axis=-1, keepdims=True) + NORM_EPS)


def _split3(a):
    a1 = a.astype(BF16)
    r1 = a - a1.astype(F32)
    a2 = r1.astype(BF16)
    a3 = (r1 - a2.astype(F32)).astype(BF16)
    return a1, a2, a3


def _dot_exact_rhs(a, e_bf16):
    out = None
    for p in _split3(a):
        t = jnp.dot(p, e_bf16, preferred_element_type=F32)
        out = t if out is None else out + t
    return out


def _dot_nt(a, b):
    return lax.dot_general(a, b, (((1,), (1,)), ((), ())), preferred_element_type=F32)


def _dot_nt_x3(a, b):
    a1 = a.astype(BF16)
    a2 = (a - a1.astype(F32)).astype(BF16)
    b1 = b.astype(BF16)
    b2 = (b - b1.astype(F32)).astype(BF16)
    return _dot_nt(a1, b1) + (_dot_nt(a1, b2) + _dot_nt(a2, b1))


def _ada_kernel(c_ref, w_ref, b_ref, o_ref):
    cond = _silu(c_ref[...])
    o_ref[...] = jnp.dot(cond.astype(BF16), w_ref[...].astype(BF16),
                         preferred_element_type=F32) + b_ref[...]


def _ada(c, w_ada, b_ada):
    bsz, d = c.shape
    n = w_ada.shape[1]
    tn = 1024
    return pl.pallas_call(
        _ada_kernel,
        grid=(n // tn,),
        in_specs=[pl.BlockSpec((bsz, d), lambda j: (0, 0)),
                  pl.BlockSpec((d, tn), lambda j: (0, j)),
                  pl.BlockSpec((1, tn), lambda j: (0, j))],
        out_specs=pl.BlockSpec((bsz, tn), lambda j: (0, j)),
        out_shape=jax.ShapeDtypeStruct((bsz, n), F32),
        compiler_params=_cparams(("arbitrary",)),
        name="ada_mod",
    )(c, w_ada, b_ada.reshape(1, n))


def _prenorm_kernel(x_ref, g_ref, sc_ref, sh_ref, o_ref):
    y = _rms(x_ref[0]) * g_ref[...]
    o_ref[0] = (y * (1.0 + sc_ref[0]) + sh_ref[0]).astype(o_ref.dtype)


def _prenorm(x, g, sc, sh):
    bsz, s, d = x.shape
    ts = 512
    vec = pl.BlockSpec((1, 1, d), lambda b, i: (b, 0, 0))
    return pl.pallas_call(
        _prenorm_kernel,
        grid=(bsz, s // ts),
        in_specs=[pl.BlockSpec((1, ts, d), lambda b, i: (b, i, 0)),
                  pl.BlockSpec((1, d), lambda b, i: (0, 0)), vec, vec],
        out_specs=pl.BlockSpec((1, ts, d), lambda b, i: (b, i, 0)),
        out_shape=jax.ShapeDtypeStruct((bsz, s, d), BF16),
        compiler_params=_cparams(("arbitrary", "arbitrary")),
        name="prenorm",
    )(x, g.reshape(1, d), sc.reshape(bsz, 1, d), sh.reshape(bsz, 1, d))


def _mm_kernel(a_ref, w_ref, o_ref):
    o_ref[...] = jnp.dot(a_ref[...], w_ref[...], preferred_element_type=F32).astype(o_ref.dtype)


def _matmul(a, w, out_dtype, tm, tn, name):
    m, k = a.shape
    n = w.shape[1]
    return pl.pallas_call(
        _mm_kernel,
        grid=(n // tn, m // tm),
        in_specs=[pl.BlockSpec((tm, k), lambda j, i: (i, 0)),
                  pl.BlockSpec((k, tn), lambda j, i: (0, j))],
        out_specs=pl.BlockSpec((tm, tn), lambda j, i: (i, j)),
        out_shape=jax.ShapeDtypeStruct((m, n), out_dtype),
        compiler_params=_cparams(("arbitrary", "arbitrary")),
        name=name,
    )(a, w)


def _mm_wt_kernel(shift, a_ref, *refs):
    w_ref = refs[0]
    o_ref, wb = refs[-2:]

    @pl.when(pl.program_id(1) == 0)
    def _():
        w = w_ref[0]
        if shift:
            w = jnp.concatenate([w[shift:], refs[1][0]], axis=0)
        wb[...] = w.T.astype(BF16)

    o_ref[...] = jnp.dot(a_ref[...], wb[...], preferred_element_type=F32).astype(o_ref.dtype)


def _dt_kernel(a_ref, w_ref, o_ref):
    o_ref[...] = _dot_nt(w_ref[0].astype(BF16), a_ref[...])


def _dt_proj(a, wt_stack, layer, row0, nh, tm):
    m, k = a.shape
    assert row0 % nh == 0
    return pl.pallas_call(
        _dt_kernel,
        grid=(m // tm,),
        in_specs=[pl.BlockSpec((tm, k), lambda i: (i, 0)),
                  pl.BlockSpec((1, nh, k), lambda i: (layer, row0 // nh, 0))],
        out_specs=pl.BlockSpec((nh, tm), lambda i: (0, i)),
        out_shape=jax.ShapeDtypeStruct((nh, m), F32),
        compiler_params=_cparams(("arbitrary",)),
        name="in_proj_dt",
    )(a, wt_stack)


def _matmul_wt(a, wt_stack, layer, row0, n, tm, tn, out_dtype, name):
    m, k = a.shape
    shift = row0 % tn
    base = row0 - shift
    assert n % tn == 0 and shift % SUBLANES == 0
    a_spec = pl.BlockSpec((tm, k), lambda j, i: (i, 0))
    in_specs = [a_spec, pl.BlockSpec((1, tn, k), lambda j, i: (layer, base // tn + j, 0))]
    args = [a, wt_stack]
    if shift:
        assert (base + tn) % shift == 0 and tn % shift == 0
        in_specs.append(pl.BlockSpec((1, shift, k), lambda j, i: (layer, (base + (j + 1) * tn) // shift, 0)))
        args.append(wt_stack)
    return pl.pallas_call(
        functools.partial(_mm_wt_kernel, shift),
        grid=(n // tn, m // tm),
        in_specs=in_specs,
        out_specs=pl.BlockSpec((tm, tn), lambda j, i: (i, j)),
        out_shape=jax.ShapeDtypeStruct((m, n), out_dtype),
        scratch_shapes=[pltpu.VMEM((k, tn), BF16)],
        compiler_params=_cparams(("arbitrary", "arbitrary")),
        name=name,
    )(*args)


def _softplus(x):
    return jnp.maximum(x, 0.0) + jnp.log(1.0 + jnp.exp(-jnp.abs(x)))


def _ssd_kernel(x_ref, b_ref, c_ref, z_ref, dt_ref, cwx_ref, cwb_ref, cwc_ref,
                cbx_ref, cbb_ref, cbc_ref, dtb_ref, alog_ref, dsk_ref, nw_ref,
                o_ref, cbuf, state):
    q = SSD_CHUNK
    gps = dtb_ref.shape[0]
    gw = x_ref.shape[2] // gps
    nst = b_ref.shape[2] // gps
    hg = gw // SSD_HEAD_DIM
    pad = SUBLANES

    @pl.when(pl.program_id(2) == 0)
    def _():
        cbuf[:, 0:pad, :] = jnp.zeros((gps, pad, gw + 2 * nst), F32)
        state[...] = jnp.zeros_like(state)

    hrow = lax.broadcasted_iota(jnp.int32, (LANES, q), 0)
    ri = lax.broadcasted_iota(jnp.int32, (q, q), 0)
    ci = lax.broadcasted_iota(jnp.int32, (q, q), 1)
    triu = jnp.where(ri <= ci, 1.0, 0.0).astype(BF16)
    er = lax.broadcasted_iota(jnp.int32, (LANES, gw), 0)
    ec = lax.broadcasted_iota(jnp.int32, (LANES, gw), 1)
    e_ch = jnp.where(ec // SSD_HEAD_DIM == er, 1.0, 0.0).astype(BF16)
    er2 = lax.broadcasted_iota(jnp.int32, (LANES, hg * q), 0)
    ec2 = lax.broadcasted_iota(jnp.int32, (LANES, hg * q), 1)
    e_t = jnp.where(ec2 // q == er2, 1.0, 0.0).astype(BF16)
    ri2 = lax.broadcasted_iota(jnp.int32, (q, hg * q), 0)
    ci2 = lax.broadcasted_iota(jnp.int32, (q, hg * q), 1) % q
    first_head = lax.broadcasted_iota(jnp.int32, (1, LANES), 1) < SSD_HEAD_DIM

    for u in range(gps):
        xc = slice(u * gw, (u + 1) * gw)
        nc = slice(u * nst, (u + 1) * nst)
        xin = jnp.concatenate([x_ref[0, :, xc], b_ref[0, :, nc], c_ref[0, :, nc]],
                              axis=-1).astype(F32)
        cbuf[u, pad:pad + q, :] = xin
        w = jnp.concatenate([cwx_ref[:, xc], cwb_ref[:, nc], cwc_ref[:, nc]], axis=-1)
        bias = jnp.concatenate([cbx_ref[:, xc], cbb_ref[:, nc], cbc_ref[:, nc]], axis=-1)
        acc = bias + w[SSD_CONV - 1:SSD_CONV, :] * xin
        for s in range(1, SSD_CONV):
            acc = acc + w[SSD_CONV - 1 - s:SSD_CONV - s, :] * cbuf[u, pad - s:pad - s + q, :]
        cbuf[u, pad - (SSD_CONV - 1):pad, :] = cbuf[u, pad + q - (SSD_CONV - 1):pad + q, :]
        xbc = _silu(acc)
        xs = xbc[:, :gw]
        bm = xbc[:, gw:gw + nst].astype(BF16)
        cm = xbc[:, gw + nst:].astype(BF16)

        dt_raw = jnp.concatenate([dt_ref[u * hg:(u + 1) * hg, :], jnp.zeros((LANES - hg, q), F32)],
                                 axis=0)
        dt_r = jnp.where(hrow < hg, _softplus(dt_raw + dtb_ref[u]), 0.0)
        a_r = dt_r * (-math.log2(math.e) * jnp.exp(alog_ref[u]))
        acum_r = _dot_exact_rhs(a_r, triu)
        acum_c = acum_r.T
        dt_c = dt_r.T

        dt_full = _dot_exact_rhs(dt_c, e_ch)
        acum_colb = _dot_exact_rhs(acum_c, e_t)
        acum_full = jnp.concatenate(
            [jnp.where(first_head, acum_colb[:, (2 * pr) * q:(2 * pr + 1) * q],
                       acum_colb[:, (2 * pr + 1) * q:(2 * pr + 2) * q])
             for pr in range(hg // 2)], axis=-1)
        acum_rowb = jnp.concatenate(
            [jnp.broadcast_to(acum_r[h:h + 1, :], (q, q)) for h in range(hg)], axis=-1)

        lmat = jnp.exp2(jnp.where(ri2 >= ci2, acum_colb - acum_rowb, NEG_BIG))
        cb = _dot_nt(cm, bm)
        mcat = (jnp.concatenate([cb] * hg, axis=-1) * lmat).astype(BF16)
        xdt = xs * dt_full
        y_pairs = []
        for pr in range(gw // LANES):
            xp = xdt[:, pr * LANES:(pr + 1) * LANES]
            xbd = jnp.concatenate([jnp.where(first_head, xp, 0.0).astype(BF16),
                                   jnp.where(first_head, 0.0, xp).astype(BF16)], axis=0)
            y_pairs.append(jnp.dot(mcat[:, pr * 2 * q:(pr + 1) * 2 * q], xbd,
                                   preferred_element_type=F32))
        y = jnp.concatenate(y_pairs, axis=-1)

        st = state[u]
        y = y + jnp.dot(cm, st.astype(BF16), preferred_element_type=F32) * jnp.exp2(acum_full)
        acum_last = acum_full[q - 1:q, :]
        xw = (xdt * jnp.exp2(acum_last - acum_full)).astype(BF16)
        bt = xbc[:, gw:gw + nst].T.astype(BF16)
        state[u] = st * jnp.exp2(acum_last) + jnp.dot(bt, xw, preferred_element_type=F32)

        y = y + xs * dsk_ref[:, xc]
        y = y * _silu(z_ref[0, :, xc].astype(F32))
        o_ref[0, :, xc] = (_rms(y) * nw_ref[:, xc]).astype(o_ref.dtype)


def _ssd(proj, dt_rows, conv_w, conv_b, dt_bias, a_log, d_skip, norm_w, d_inner):
    bsz, s, _ = proj.shape
    g = SSD_GROUPS
    gps = SSD_GROUPS_PER_STEP
    assert SSD_CHUNK == LANES and 2 * SSD_HEAD_DIM == LANES
    gw = d_inner // g
    hg = gw // SSD_HEAD_DIM
    nst = SSD_STATE
    q = SSD_CHUNK
    nc = s // q
    bw, bn = gps * gw, gps * nst
    x0 = d_inner // bw
    b0 = (2 * d_inner) // bn
    c0 = (2 * d_inner + g * nst) // bn
    cwb0 = d_inner // bn
    cwc0 = (d_inner + g * nst) // bn
    conv_b2 = conv_b.reshape(1, -1)
    head_pad = lambda v: jnp.pad(v.reshape(g, hg), ((0, 0), (0, LANES - hg))).reshape(g, LANES, 1)
    return pl.pallas_call(
        _ssd_kernel,
        grid=(bsz, g // gps, nc),
        in_specs=[
            pl.BlockSpec((1, q, bw), lambda b, gi, c: (b, c, x0 + gi)),
            pl.BlockSpec((1, q, bn), lambda b, gi, c: (b, c, b0 + gi)),
            pl.BlockSpec((1, q, bn), lambda b, gi, c: (b, c, c0 + gi)),
            pl.BlockSpec((1, q, bw), lambda b, gi, c: (b, c, gi)),
            pl.BlockSpec((gps * hg, q), lambda b, gi, c: (gi, b * nc + c)),
            pl.BlockSpec((SSD_CONV, bw), lambda b, gi, c: (0, gi)),
            pl.BlockSpec((SSD_CONV, bn), lambda b, gi, c: (0, cwb0 + gi)),
            pl.BlockSpec((SSD_CONV, bn), lambda b, gi, c: (0, cwc0 + gi)),
            pl.BlockSpec((1, bw), lambda b, gi, c: (0, gi)),
            pl.BlockSpec((1, bn), lambda b, gi, c: (0, cwb0 + gi)),
            pl.BlockSpec((1, bn), lambda b, gi, c: (0, cwc0 + gi)),
            pl.BlockSpec((gps, LANES, 1), lambda b, gi, c: (gi, 0, 0)),
            pl.BlockSpec((gps, LANES, 1), lambda b, gi, c: (gi, 0, 0)),
            pl.BlockSpec((1, bw), lambda b, gi, c: (0, gi)),
            pl.BlockSpec((1, bw), lambda b, gi, c: (0, gi)),
        ],
        out_specs=pl.BlockSpec((1, q, bw), lambda b, gi, c: (b, c, gi)),
        out_shape=jax.ShapeDtypeStruct((bsz, s, d_inner), BF16),
        scratch_shapes=[pltpu.VMEM((gps, SUBLANES + q, gw + 2 * nst), F32),
                        pltpu.VMEM((gps, nst, gw), F32)],
        compiler_params=_cparams(("arbitrary", "arbitrary", "arbitrary")),
        name="ssd_scan",
    )(proj, proj, proj, proj, dt_rows, conv_w, conv_w, conv_w, conv_b2, conv_b2, conv_b2,
      head_pad(dt_bias), head_pad(a_log),
      jnp.repeat(d_skip, SSD_HEAD_DIM).reshape(1, d_inner), norm_w.reshape(1, d_inner))


def _rope_kernel(pos_ref, inv_ref, cos_ref, sin_ref):
    ang = pos_ref[...].astype(F32) * inv_ref[...]
    lane = lax.broadcasted_iota(jnp.int32, ang.shape, 1)
    cos_ref[...] = jnp.cos(ang)
    sin_ref[...] = jnp.where(lane < ATT_HEAD_DIM // 2, -1.0, 1.0) * jnp.sin(ang)


def _rope_tables(positions):
    t = positions.size
    half = ATT_HEAD_DIM // 2
    inv = 1.0 / (ROPE_THETA ** (jnp.arange(half, dtype=F32) / half))
    inv2 = jnp.concatenate([inv, inv]).reshape(1, ATT_HEAD_DIM)
    tt = min(2048, t)
    return pl.pallas_call(
        _rope_kernel,
        grid=(t // tt,),
        in_specs=[pl.BlockSpec((tt, 1), lambda i: (i, 0)),
                  pl.BlockSpec((1, ATT_HEAD_DIM), lambda i: (0, 0))],
        out_specs=[pl.BlockSpec((tt, ATT_HEAD_DIM), lambda i: (i, 0))] * 2,
        out_shape=[jax.ShapeDtypeStruct((t, ATT_HEAD_DIM), F32)] * 2,
        compiler_params=_cparams(("arbitrary",)),
        name="rope_tables",
    )(positions.reshape(t, 1), inv2)


def _rot(x, cos2, sin2):
    return x * cos2 + pltpu.roll(x, ATT_HEAD_DIM // 2, axis=1) * sin2


def _attn_kernel(q_ref, k_ref, v_ref, cos_ref, sin_ref, o_ref, krot, vt, kmean):
    blk = MOBA_BLOCK
    nb = k_ref.shape[1] // blk
    log2e_scale = ATT_HEAD_DIM ** -0.5 * math.log2(math.e)

    kmean[...] = jnp.zeros_like(kmean)
    for j in range(nb):
        rows = slice(j * blk, (j + 1) * blk)
        kr = _rot(k_ref[0, rows, :].astype(F32), cos_ref[0, rows, :], sin_ref[0, rows, :])
        krot[rows, :] = kr.astype(BF16)
        kmean[j:j + 1, :] = jnp.mean(kr, axis=0, keepdims=True)
        vt[:, rows] = v_ref[0, rows, :].astype(F32).T.astype(BF16)
    km = kmean[...]

    ki = lax.broadcasted_iota(jnp.int32, (blk, blk), 0)
    qj = lax.broadcasted_iota(jnp.int32, (blk, blk), 1)
    sub = lax.broadcasted_iota(jnp.int32, (km.shape[0], blk), 0)
    for qi in range(nb):
        qrows = slice(qi * blk, (qi + 1) * blk)
        qr = _rot(q_ref[0, qrows, :].astype(F32), cos_ref[0, qrows, :], sin_ref[0, qrows, :])
        qs = (qr * log2e_scale).T.astype(BF16)
        gate = _dot_nt_x3(km, qr)
        scores = [jnp.where(ki <= qj, jnp.dot(krot[qrows, :], qs, preferred_element_type=F32), NEG_BIG)]
        for j in range(qi):
            gj = gate[j:j + 1, :]
            ahead = (gate > gj) | ((gate == gj) & (sub < j))
            rank = jnp.sum(jnp.where((sub < qi) & ahead, 1.0, 0.0), axis=0, keepdims=True)
            bias = jnp.where(rank < MOBA_TOPK, 0.0, NEG_BIG)
            scores.append(jnp.dot(krot[j * blk:(j + 1) * blk, :], qs,
                                  preferred_element_type=F32) + bias)
        m = scores[0]
        for sj in scores[1:]:
            m = jnp.maximum(m, sj)
        m = jnp.max(m, axis=0, keepdims=True)
        l = None
        acc = None
        for j, sj in enumerate(scores):
            src = qi if j == 0 else j - 1
            p = jnp.exp2(sj - m)
            lj = jnp.sum(p, axis=0, keepdims=True)
            aj = jnp.dot(vt[:, src * blk:(src + 1) * blk], p.astype(BF16),
                         preferred_element_type=F32)
            l = lj if l is None else l + lj
            acc = aj if acc is None else acc + aj
        o_ref[0, qrows, :] = (acc / l).T.astype(o_ref.dtype)


def _attention(proj, cos2, sin2, q0, k0, v0, n_heads):
    bsz, s, _ = proj.shape
    dh = ATT_HEAD_DIM
    col = lambda c0: pl.BlockSpec((1, s, dh), lambda b, h: (b, 0, c0 + h))
    tab = pl.BlockSpec((1, s, dh), lambda b, h: (b, 0, 0))
    return pl.pallas_call(
        _attn_kernel,
        grid=(bsz, n_heads),
        in_specs=[col(q0), col(k0), col(v0), tab, tab],
        out_specs=pl.BlockSpec((1, s, dh), lambda b, h: (b, 0, h)),
        out_shape=jax.ShapeDtypeStruct((bsz, s, n_heads * dh), BF16),
        scratch_shapes=[pltpu.VMEM((s, dh), BF16), pltpu.VMEM((dh, s), BF16),
                        pltpu.VMEM((-(-(s // MOBA_BLOCK) // SUBLANES) * SUBLANES, dh), F32)],
        compiler_params=_cparams(("arbitrary", "arbitrary")),
        name="moba_attn",
    )(proj, proj, proj, cos2.reshape(bsz, s, dh), sin2.reshape(bsz, s, dh))


def _merge_kernel(ys_ref, ya_ref, w1_ref, w2_ref, gs_ref, ga_ref, o_ref):
    b1 = jnp.dot(ys_ref[...], w1_ref[...], preferred_element_type=F32)
    b2 = jnp.dot(ya_ref[...], w2_ref[...], preferred_element_type=F32)
    o_ref[...] = (jax.nn.sigmoid(gs_ref[...].astype(F32)) * b1
                  + jax.nn.sigmoid(ga_ref[...].astype(F32)) * b2).astype(o_ref.dtype)


def _merge(y_ssd, y_att, w1, w2, proj2d, gs0, ga0):
    m, k1 = y_ssd.shape
    k2 = y_att.shape[1]
    n = w1.shape[1]
    tm, tn = MERGE_TM, MERGE_TN
    return pl.pallas_call(
        _merge_kernel,
        grid=(n // tn, m // tm),
        in_specs=[pl.BlockSpec((tm, k1), lambda j, i: (i, 0)),
                  pl.BlockSpec((tm, k2), lambda j, i: (i, 0)),
                  pl.BlockSpec((k1, tn), lambda j, i: (0, j)),
                  pl.BlockSpec((k2, tn), lambda j, i: (0, j)),
                  pl.BlockSpec((tm, tn), lambda j, i: (i, gs0 // tn + j)),
                  pl.BlockSpec((tm, tn), lambda j, i: (i, ga0 // tn + j))],
        out_specs=pl.BlockSpec((tm, tn), lambda j, i: (i, j)),
        out_shape=jax.ShapeDtypeStruct((m, n), BF16),
        compiler_params=_cparams(("arbitrary", "arbitrary")),
        name="branch_merge",
    )(y_ssd, y_att, w1, w2, proj2d, proj2d)


def _outproj_kernel(m_ref, w_ref, x_ref, gpost_ref, gt_ref, gpre_ref, sc_ref, sh_ref, wr_ref,
                    x1_ref, h2b_ref, lg_ref):
    for r in range(m_ref.shape[1] // OUTPROJ_SUBTILE):
        rows = slice(r * OUTPROJ_SUBTILE, (r + 1) * OUTPROJ_SUBTILE)
        y = jnp.dot(m_ref[0, rows, :], w_ref[...], preferred_element_type=F32)
        x1 = x_ref[0, rows, :] + gt_ref[0] * (_rms(y) * gpost_ref[...])
        h2 = (_rms(x1) * gpre_ref[...]) * (1.0 + sc_ref[0]) + sh_ref[0]
        x1_ref[0, rows, :] = x1
        h2b_ref[0, rows, :] = h2.astype(BF16)
        h_hi = h2.astype(BF16)
        h_lo = (h2 - h_hi.astype(F32)).astype(BF16)
        wr = wr_ref[...]
        w_hi = wr.astype(BF16)
        w_lo = (wr - w_hi.astype(F32)).astype(BF16)
        lg = (jnp.dot(h_hi, w_hi, preferred_element_type=F32)
              + (jnp.dot(h_hi, w_lo, preferred_element_type=F32)
                 + jnp.dot(h_lo, w_hi, preferred_element_type=F32)))
        lg_ref[:, rows] = lg.T[:lg_ref.shape[0], :]


def _outproj(merged, w_out, x, g_post, gt1, g_pre, sc2, sh2, w_router):
    bsz, s, d = x.shape
    ne = w_router.shape[1]
    w_router_p = jnp.pad(w_router, ((0, 0), (0, LANES - ne)))
    tm = OUTPROJ_TILE
    nt = s // tm
    row = pl.BlockSpec((1, tm, d), lambda i: (i // nt, i % nt, 0))
    vec = pl.BlockSpec((1, 1, d), lambda i: (i // nt, 0, 0))
    par = pl.BlockSpec((1, d), lambda i: (0, 0))
    return pl.pallas_call(
        _outproj_kernel,
        grid=(bsz * nt,),
        in_specs=[row, pl.BlockSpec((d, d), lambda i: (0, 0)), row, par, vec, par, vec, vec,
                  pl.BlockSpec((d, LANES), lambda i: (0, 0))],
        out_specs=[row, row, pl.BlockSpec((ne, tm), lambda i: (0, i))],
        out_shape=[jax.ShapeDtypeStruct((bsz, s, d), F32),
                   jax.ShapeDtypeStruct((bsz, s, d), BF16),
                   jax.ShapeDtypeStruct((ne, bsz * s), F32)],
        compiler_params=_cparams(("arbitrary",)),
        name="out_proj_norms",
    )(merged.reshape(bsz, s, d), w_out, x, g_post.reshape(1, d), gt1.reshape(bsz, 1, d),
      g_pre.reshape(1, d), sc2.reshape(bsz, 1, d), sh2.reshape(bsz, 1, d), w_router_p)


def _router_kernel(lg_ref, rb_ref, w_ref, rank_ref, cnt_ref, gsc):
    ne, tn = lg_ref.shape
    ng = N_EXPERT_GROUPS
    eg = ne // ng
    scores = jax.nn.sigmoid(lg_ref[...])
    biased = scores + rb_ref[...]
    i8 = lax.broadcasted_iota(jnp.int32, (eg, tn), 0)
    for g in range(ng):
        v = biased[g * eg:(g + 1) * eg, :]
        m1 = jnp.max(v, axis=0, keepdims=True)
        first = jnp.min(jnp.where(v == m1, i8, eg), axis=0, keepdims=True)
        m2 = jnp.max(jnp.where(i8 == first, -jnp.inf, v), axis=0, keepdims=True)
        gsc[g:g + 1, :] = m1 + m2
    gs = gsc[...]
    gi = lax.broadcasted_iota(jnp.int32, (ng, tn), 0)
    masked = []
    for g in range(ng):
        sg = gs[g:g + 1, :]
        ahead = (gs > sg) | ((gs == sg) & (gi < g))
        rank = jnp.sum(jnp.where(ahead, 1.0, 0.0), axis=0, keepdims=True)
        masked.append(jnp.where(rank < TOPK_GROUPS, biased[g * eg:(g + 1) * eg, :], -jnp.inf))
    cur = jnp.concatenate(masked, axis=0)
    sub = lax.broadcasted_iota(jnp.int32, (ne, tn), 0)
    sel = jnp.zeros((ne, tn), F32)
    for k in range(TOP_K):
        mx = jnp.max(cur, axis=0, keepdims=True)
        idx = jnp.min(jnp.where(cur == mx, sub, ne), axis=0, keepdims=True)
        hit = sub == idx
        sel = jnp.where(hit, 1.0, sel)
        cur = jnp.where(hit, -jnp.inf, cur)
    picked = sel * scores
    wsum = jnp.sum(picked, axis=0, keepdims=True)
    w_ref[...] = picked / wsum * ROUTED_SCALE
    ti = lax.broadcasted_iota(jnp.int32, (tn, tn), 0)
    tj = lax.broadcasted_iota(jnp.int32, (tn, tn), 1)
    before = jnp.where(ti < tj, 1.0, 0.0).astype(BF16)
    rank = jnp.dot(sel.astype(BF16), before, preferred_element_type=F32)
    rank_ref[...] = jnp.where(sel > 0.0, rank, NEG_BIG)
    cnt_ref[0] = jnp.broadcast_to(jnp.sum(sel, axis=1, keepdims=True), (ne, LANES))


def _router(logits_t, router_bias):
    ne, t = logits_t.shape
    tn = MOE_TILE
    tile = pl.BlockSpec((ne, tn), lambda i: (0, i))
    return pl.pallas_call(
        _router_kernel,
        grid=(t // tn,),
        in_specs=[tile, pl.BlockSpec((ne, 1), lambda i: (0, 0))],
        out_specs=[tile, tile, pl.BlockSpec((1, ne, LANES), lambda i: (i, 0, 0))],
        out_shape=[jax.ShapeDtypeStruct((ne, t), F32), jax.ShapeDtypeStruct((ne, t), F32),
                   jax.ShapeDtypeStruct((t // tn, ne, LANES), F32)],
        scratch_shapes=[pltpu.VMEM((N_EXPERT_GROUPS, tn), F32)],
        compiler_params=_cparams(("arbitrary",)),
        name="router_topk",
    )(logits_t, router_bias.reshape(ne, 1))


def _tile_rows(n_experts):
    return MOE_TILE * TOP_K + n_experts * SEG_ALIGN


def _segment_copies(src, dst, sem, n, src_off, dst_off, wait, priority=0):
    off = jnp.int32(0)
    for p in SEG_CHUNKS:
        bit = n & p

        @pl.when(bit != 0)
        def _(p=p, off=off):
            cp = pltpu.make_async_copy(
                src.at[pl.ds(pl.multiple_of(src_off + off, SEG_ALIGN), p), :],
                dst.at[pl.ds(pl.multiple_of(dst_off + off, SEG_ALIGN), p), :], sem)
            if wait:
                cp.wait()
            else:
                cp.start(priority=priority)
        off = off + bit


def _for_each_segment(ne, fn):
    def body(e, carry):
        fn(e)
        return carry
    lax.fori_loop(0, ne, body, 0)


def _for_each_segment_alternating(ne, fn):
    def body(k, carry):
        fn(2 * k, 0)
        fn(2 * k + 1, 1)
        return carry
    lax.fori_loop(0, ne // 2, body, 0)


def _build_perm(pbuf, posb, ss_ref, c16_ref, tile, ne, value_row):
    tn = posb.shape[1]
    pbuf[...] = jnp.zeros_like(pbuf)
    win = lax.broadcasted_iota(jnp.int32, (PERM_WINDOW, tn), 0)

    def per_expert(e):
        base = ss_ref[tile, e]
        windows = lax.shift_right_logical(c16_ref[tile, e] + (PERM_WINDOW - 1),
                                          PERM_WINDOW.bit_length() - 1)
        prow = posb[pl.ds(e, 1), :]
        vrow = value_row(e)

        def per_window(g, carry):
            r0 = pl.multiple_of(base + g * PERM_WINDOW, SEG_ALIGN)
            hit = prow == (win + r0).astype(F32)
            pbuf[pl.ds(r0, PERM_WINDOW), :] = jnp.where(hit, vrow, 0.0).astype(BF16)
            return carry
        lax.fori_loop(0, windows, per_window, 0)
    _for_each_segment(ne, per_expert)


def _dispatch_kernel(ss_ref, c16_ref, go_ref, ts_ref, tl_ref, rank_ref, ssv_ref, h_ref, xs_hbm,
                     pbuf, xs, posb, sem):
    i = pl.program_id(0)
    n = pl.num_programs(0)
    ne = rank_ref.shape[0]
    tn = rank_ref.shape[1]
    posb[...] = rank_ref[...] + ssv_ref[0]
    _build_perm(pbuf, posb, ss_ref, c16_ref, i, ne, lambda e: 1.0)

    def drain(tile):
        _for_each_segment(ne, lambda e: _segment_copies(
            xs, xs_hbm, sem, c16_ref[tile, e], ss_ref[tile, e], go_ref[tile, e], True))

    @pl.when(i > 0)
    def _():
        drain(i - 1)

    used = ss_ref[i, ne - 1] + c16_ref[i, ne - 1]
    def chunk(ch):
        rows = slice(ch * tn, (ch + 1) * tn)
        xs[rows, :] = jnp.dot(pbuf[rows, :], h_ref[...], preferred_element_type=F32).astype(BF16)

    for ch in range(TOP_K):
        chunk(ch)
    for ch in range(TOP_K, xs.shape[0] // tn):
        pl.when(ch * tn < used)(functools.partial(chunk, ch))

    _for_each_segment_alternating(ne, lambda e, par: _segment_copies(
        xs, xs_hbm, sem, c16_ref[i, e], ss_ref[i, e], go_ref[i, e], False, par))

    @pl.when(i == n - 1)
    def _():
        drain(i)
        xs[0:MOE_ROWS, :] = jnp.zeros((MOE_ROWS, xs.shape[1]), BF16)
        first_free = lax.shift_right_logical(ts_ref[ne], MOE_ROWS.bit_length() - 1)
        n_blocks = xs_hbm.shape[0] // MOE_ROWS
        for wait in (False, True):
            _for_each_segment(ne, lambda e: _segment_copies(
                xs, xs_hbm, sem, tl_ref[e], 0, ts_ref[e], wait))

            def free_block(b, carry, wait=wait):
                cp = pltpu.make_async_copy(
                    xs.at[pl.ds(0, MOE_ROWS), :],
                    xs_hbm.at[pl.ds(pl.multiple_of(b * MOE_ROWS, MOE_ROWS), MOE_ROWS), :], sem)
                if wait:
                    cp.wait()
                else:
                    cp.start()
                return carry
            lax.fori_loop(first_free, n_blocks, free_block, 0)


def _dispatch(seg_start, c16, goff, tail_start, tail_len, rank, h2b, n_rows):
    ne, t = rank.shape
    d = h2b.shape[1]
    tn = MOE_TILE
    rt = _tile_rows(ne)
    grid_spec = pltpu.PrefetchScalarGridSpec(
        num_scalar_prefetch=5,
        grid=(t // tn,),
        in_specs=[pl.BlockSpec((ne, tn), lambda i, *_: (0, i)),
                  pl.BlockSpec((1, ne, 1), lambda i, *_: (i, 0, 0)),
                  pl.BlockSpec((tn, d), lambda i, *_: (i, 0))],
        out_specs=pl.BlockSpec(memory_space=pl.ANY),
        scratch_shapes=[pltpu.VMEM((rt + PERM_WINDOW, tn), BF16), pltpu.VMEM((rt, d), BF16),
                        pltpu.VMEM((ne, tn), F32), pltpu.SemaphoreType.DMA],
    )
    return pl.pallas_call(
        _dispatch_kernel,
        grid_spec=grid_spec,
        out_shape=jax.ShapeDtypeStruct((n_rows, d), BF16),
        compiler_params=_cparams(("arbitrary",)),
        name="moe_dispatch",
    )(seg_start, c16, goff, tail_start, tail_len, rank,
      seg_start.astype(F32).reshape(t // tn, ne, 1), h2b)


def _combine_kernel(ss_ref, c16_ref, go_ref, rank_ref, w_ref, ssv_ref, y_hbm, o_ref,
                    pbuf, ys, posb, sem):
    i = pl.program_id(0)
    ne = rank_ref.shape[0]
    tn = rank_ref.shape[1]
    ys[...] = jnp.zeros_like(ys)
    _for_each_segment_alternating(ne, lambda e, par: _segment_copies(
        y_hbm, ys, sem, c16_ref[i, e], go_ref[i, e], ss_ref[i, e], False, par))
    posb[...] = rank_ref[...] + ssv_ref[0]
    _build_perm(pbuf, posb, ss_ref, c16_ref, i, ne, lambda e: w_ref[pl.ds(e, 1), :])
    _for_each_segment(ne, lambda e: _segment_copies(
        y_hbm, ys, sem, c16_ref[i, e], go_ref[i, e], ss_ref[i, e], True))
    used = ss_ref[i, ne - 1] + c16_ref[i, ne - 1]
    def chunk(ch):
        rows = slice(ch * tn, (ch + 1) * tn)
        return lax.dot_general(pbuf[rows, :], ys[rows, :], (((0,), (0,)), ((), ())),
                               preferred_element_type=F32)

    always = TOP_K
    acc = chunk(0)
    for ch in range(1, always):
        acc = acc + chunk(ch)
    o_ref[...] = acc
    for ch in range(always, ys.shape[0] // tn):
        @pl.when(ch * tn < used)
        def _(ch=ch):
            o_ref[...] += chunk(ch)


def _combine(seg_start, c16, goff, rank, selw, y_rows):
    ne, t = rank.shape
    d = y_rows.shape[1]
    tn = MOE_TILE
    rt = _tile_rows(ne)
    tile = pl.BlockSpec((ne, tn), lambda i, *_: (0, i))
    grid_spec = pltpu.PrefetchScalarGridSpec(
        num_scalar_prefetch=3,
        grid=(t // tn,),
        in_specs=[tile, tile, pl.BlockSpec((1, ne, 1), lambda i, *_: (i, 0, 0)),
                  pl.BlockSpec(memory_space=pl.ANY)],
        out_specs=pl.BlockSpec((tn, d), lambda i, *_: (i, 0)),
        scratch_shapes=[pltpu.VMEM((rt + PERM_WINDOW, tn), BF16), pltpu.VMEM((rt, d), BF16),
                        pltpu.VMEM((ne, tn), F32), pltpu.SemaphoreType.DMA],
    )
    return pl.pallas_call(
        _combine_kernel,
        grid_spec=grid_spec,
        out_shape=jax.ShapeDtypeStruct((t, d), F32),
        compiler_params=_cparams(("arbitrary",)),
        name="moe_combine",
    )(seg_start, c16, goff, rank, selw, seg_start.astype(F32).reshape(t // tn, ne, 1), y_rows)


EXPERT_LOOKAHEAD = 2


def _expert_kernel(be_ref, nv_ref, sl_ref, x_ref, wg_ref, wu_ref, wd_ref, o_ref, wg, wu, wd):
    s = pl.program_id(0)
    n = pl.num_programs(0) - EXPERT_LOOKAHEAD
    clampi = lambda j: jnp.clip(j, 0, n - 1)

    def starts_expert(j):
        jc = clampi(j)
        first = (jc == 0) | (be_ref[jc] != be_ref[clampi(jc - 1)])
        return (j >= 0) & (j < n) & (nv_ref[jc] > 0) & first

    i = s - EXPERT_LOOKAHEAD
    ic = clampi(i)
    slot = sl_ref[ic]

    @pl.when(starts_expert(i))
    def _():
        wd[slot] = wd_ref[0].astype(BF16)

    @pl.when((i >= 0) & (nv_ref[ic] == 0))
    def _():
        o_ref[...] = jnp.zeros_like(o_ref)

    @pl.when((i >= 0) & (nv_ref[ic] > 0))
    def _():
        x = x_ref[...]
        gate = jnp.dot(x, wg[slot], preferred_element_type=F32)
        up = jnp.dot(x, wu[slot], preferred_element_type=F32)
        act = (_silu(gate) * up).astype(BF16)
        o_ref[...] = jnp.dot(act, wd[slot], preferred_element_type=F32).astype(o_ref.dtype)

    @pl.when(starts_expert(s))
    def _():
        wg[sl_ref[clampi(s)]] = wg_ref[0].astype(BF16)

    @pl.when(starts_expert(s - 1))
    def _():
        wu[sl_ref[clampi(s - 1)]] = wu_ref[0].astype(BF16)


def _expert_ffn(blk_exp, blk_valid, blk_slot, x_sorted, w_gate, w_up, w_down):
    n_rows, d = x_sorted.shape
    f = w_gate.shape[2]
    rows = MOE_ROWS
    n = n_rows // rows
    blk = lambda back: (lambda s, be, nv, sl: (jnp.clip(s - back, 0, n - 1), 0))
    wgt = lambda back: (lambda s, be, nv, sl: (be[jnp.clip(s - back, 0, n - 1)], 0, 0))
    grid_spec = pltpu.PrefetchScalarGridSpec(
        num_scalar_prefetch=3,
        grid=(n + EXPERT_LOOKAHEAD,),
        in_specs=[pl.BlockSpec((rows, d), blk(2)),
                  pl.BlockSpec((1, d, f), wgt(0)),
                  pl.BlockSpec((1, d, f), wgt(1)),
                  pl.BlockSpec((1, f, d), wgt(2))],
        out_specs=pl.BlockSpec((rows, d), blk(2)),
        scratch_shapes=[pltpu.VMEM((2, d, f), BF16), pltpu.VMEM((2, d, f), BF16),
                        pltpu.VMEM((2, f, d), BF16)],
    )
    return pl.pallas_call(
        _expert_kernel,
        grid_spec=grid_spec,
        out_shape=jax.ShapeDtypeStruct((n_rows, d), BF16),
        compiler_params=_cparams(("arbitrary",)),
        name="expert_ffn",
    )(blk_exp, blk_valid, blk_slot, x_sorted, w_gate, w_up, w_down)


def _shared_kernel(x_ref, wg_ref, wu_ref, wd_ref, r_ref, x1_ref, gt_ref, g_ref, o_ref):
    x = x_ref[0]
    gate = jnp.dot(x, wg_ref[...], preferred_element_type=F32)
    up = jnp.dot(x, wu_ref[...], preferred_element_type=F32)
    act = (_silu(gate) * up).astype(BF16)
    y = r_ref[0] + jnp.dot(act, wd_ref[...], preferred_element_type=F32)
    o_ref[0] = x1_ref[0] + gt_ref[0] * (_rms(y) * g_ref[...])


def _shared_final(h2b, wg, wu, wd, routed, x1, gt2, g_post):
    bsz, s, d = x1.shape
    f = wg.shape[1]
    tm = 512
    nt = s // tm
    row = pl.BlockSpec((1, tm, d), lambda i: (i // nt, i % nt, 0))
    full = lambda shape: pl.BlockSpec(shape, lambda i: (0, 0))
    return pl.pallas_call(
        _shared_kernel,
        grid=(bsz * nt,),
        in_specs=[row, full((d, f)), full((d, f)), full((f, d)), row, row,
                  pl.BlockSpec((1, 1, d), lambda i: (i // nt, 0, 0)), full((1, d))],
        out_specs=row,
        out_shape=jax.ShapeDtypeStruct((bsz, s, d), F32),
        compiler_params=_cparams(("arbitrary",)),
        name="shared_ffn_final",
    )(h2b, wg, wu, wd, routed.reshape(bsz, s, d), x1, gt2.reshape(bsz, 1, d), g_post.reshape(1, d))


def _dispatch_plan(counts, n_blocks):
    n_tiles, ne = counts.shape
    c16 = (counts + SEG_ALIGN - 1) // SEG_ALIGN * SEG_ALIGN
    seg_start = jnp.cumsum(c16, axis=1) - c16
    tot = jnp.sum(c16, axis=0)
    padded = (tot + MOE_ROWS - 1) // MOE_ROWS * MOE_ROWS
    pad_end = jnp.cumsum(padded)
    pad_start = pad_end - padded
    goff = pad_start[None, :] + jnp.cumsum(c16, axis=0) - c16
    blk_row = jnp.arange(n_blocks, dtype=jnp.int32) * MOE_ROWS
    blk_exp = jnp.sum(pad_end[None, :] <= blk_row[:, None], axis=1)
    blk_exp = jnp.minimum(blk_exp, ne - 1).astype(jnp.int32)
    owns = (pad_start[None, :] <= blk_row[:, None]) & (blk_row[:, None] < pad_end[None, :])
    rows_left = jnp.clip((pad_start + tot)[None, :] - blk_row[:, None], 0, MOE_ROWS)
    blk_valid = jnp.sum(jnp.where(owns, rows_left, 0), axis=1).astype(jnp.int32)
    blk_slot = jnp.sum((padded > 0)[None, :] & (pad_start[None, :] <= blk_row[:, None]), axis=1) % 2
    i32 = lambda v: v.astype(jnp.int32)
    tail_start = jnp.concatenate([pad_start + tot, pad_end[-1:]])
    return (i32(seg_start), i32(c16), i32(goff), i32(tail_start), i32(padded - tot),
            blk_exp, blk_valid, i32(blk_slot))


def kernel(x, c, positions, w_ada, b_ada, g_pre_mix, g_post_mix, g_pre_ffn, g_post_ffn, w_in,
           conv_w, conv_b, dt_bias, a_log, d_skip, ssd_norm_w, w_ssd_br, w_attn_br, w_out,
           w_router, router_bias, w_gate, w_up, w_down, ws_gate, ws_up, ws_down):
    bsz, s, d = x.shape
    t = bsz * s
    for l in range(w_ada.shape[0]):
        d_inner = w_ssd_br.shape[1]
        att_w = w_attn_br.shape[1]
        n_ssd_heads = dt_bias.shape[1]
        conv_dim = conv_w.shape[2]
        n_heads = att_w // ATT_HEAD_DIM
        n_experts = w_router.shape[2]

        mod = _ada(c, w_ada[l], b_ada[l])
        sh1, sc1, gt1, sh2, sc2, gt2 = jnp.split(mod, 6, axis=-1)

        o_dt = d_inner + conv_dim
        o_q = o_dt + n_ssd_heads
        w_in_t = jnp.swapaxes(w_in, 1, 2)
        h1 = _prenorm(x, g_pre_mix[l], sc1, sh1).reshape(t, d)
        proj_s = _matmul_wt(h1, w_in_t, l, 0, o_dt, 1024, 1024, BF16, "in_proj_ssd")
        proj_a = _matmul_wt(h1, w_in_t, l, o_q, w_in.shape[2] - o_q, 1024, 1024, BF16, "in_proj_att")
        dt_t = _dt_proj(h1, w_in_t, l, o_dt, n_ssd_heads, 1024)
        k_off = att_w
        v_off = k_off + att_w
        gs_off = v_off + att_w
        ga_off = gs_off + d

        y_ssd = _ssd(proj_s.reshape(bsz, s, o_dt), dt_t, conv_w[l], conv_b[l], dt_bias[l], a_log[l],
                     d_skip[l], ssd_norm_w[l], d_inner)

        cos2, sin2 = _rope_tables(positions)
        y_att = _attention(proj_a.reshape(bsz, s, -1), cos2, sin2, 0, k_off // ATT_HEAD_DIM,
                           v_off // ATT_HEAD_DIM, n_heads)

        merged = _merge(y_ssd.reshape(t, d_inner), y_att.reshape(t, att_w),
                        w_ssd_br[l].astype(BF16), w_attn_br[l].astype(BF16), proj_a, gs_off, ga_off)
        x1, h2b, logits_t = _outproj(merged, w_out[l].astype(BF16), x, g_post_mix[l], gt1,
                                         g_pre_ffn[l], sc2, sh2, w_router[l])

        selw, rank, cnt = _router(logits_t, router_bias[l])
        n_tiles = t // MOE_TILE
        n_blocks = -(-(t * TOP_K + n_tiles * n_experts * (SEG_ALIGN - 1)) // MOE_ROWS) + n_experts
        seg_start, c16, goff, tail_start, tail_len, blk_exp, blk_valid, blk_slot = _dispatch_plan(
            cnt[:, :, 0].astype(jnp.int32), n_blocks)
        x_sorted = _dispatch(seg_start, c16, goff, tail_start, tail_len, rank, h2b.reshape(t, d),
                             n_blocks * MOE_ROWS)
        y_rows = _expert_ffn(blk_exp, blk_valid, blk_slot, x_sorted, w_gate[l], w_up[l], w_down[l])
        routed = _combine(seg_start, c16, goff, rank, selw, y_rows)
        x = _shared_final(h2b, ws_gate[l].astype(BF16), ws_up[l].astype(BF16),
                          ws_down[l].astype(BF16), routed, x1, gt2, g_post_ffn[l])
    return x
```

```python
import functools
import math

import jax
import jax.numpy as jnp
from jax import lax
from jax.experimental import pallas as pl
from jax.experimental.pallas import tpu as pltpu

F32 = jnp.float32
BF16 = jnp.bfloat16

NORM_EPS = 1e-6
ROPE_THETA = 10000.0

SSD_HEAD_DIM = 64
SSD_GROUPS = 8
SSD_STATE = 128
SSD_CONV = 4
SSD_CHUNK = 128
SSD_GROUPS_PER_STEP = 8

ATT_HEAD_DIM = 128
MOBA_BLOCK = 256
MOBA_TOPK = 3

TOP_K = 8
N_EXPERT_GROUPS = 8
TOPK_GROUPS = 4
ROUTED_SCALE = 2.5
MOE_ROWS = 512
MOE_TILE = 512
SEG_ALIGN = 16
SEG_CHUNKS = (512, 256, 128, 64, 32, 16)
PERM_WINDOW = 128

INPROJ_TM, INPROJ_TN = 1024, 1024
MERGE_TM, MERGE_TN = 512, 512
OUTPROJ_TILE = 512
OUTPROJ_SUBTILE = 256

LANES = 128
SUBLANES = 8
VMEM_LIMIT = 56 * 1024 * 1024
NEG_BIG = -1e30


def _cparams(sem):
    return pltpu.CompilerParams(dimension_semantics=sem, vmem_limit_bytes=VMEM_LIMIT)


def _silu(x):
    return x * jax.nn.sigmoid(x)


def _rms(x):
    return x * lax.rsqrt(jnp.mean(x * x, axis=-1, keepdims=True) + NORM_EPS)


def _split3(a):
    a1 = a.astype(BF16)
    r1 = a - a1.astype(F32)
    a2 = r1.astype(BF16)
    a3 = (r1 - a2.astype(F32)).astype(BF16)
    return a1, a2, a3


def _dot_exact_rhs(a, e_bf16):
    out = None
    for p in _split3(a):
        t = jnp.dot(p, e_bf16, preferred_element_type=F32)
        out = t if out is None else out + t
    return out


def _dot_nt(a, b):
    return lax.dot_general(a, b, (((1,), (1,)), ((), ())), preferred_element_type=F32)


def _dot_nt_x3(a, b):
    a1 = a.astype(BF16)
    a2 = (a - a1.astype(F32)).astype(BF16)
    b1 = b.astype(BF16)
    b2 = (b - b1.astype(F32)).astype(BF16)
    return _dot_nt(a1, b1) + (_dot_nt(a1, b2) + _dot_nt(a2, b1))


def _ada_kernel(c_ref, w_ref, b_ref, o_ref):
    cond = _silu(c_ref[...])
    o_ref[...] = jnp.dot(cond.astype(BF16), w_ref[...].astype(BF16),
                         preferred_element_type=F32) + b_ref[...]


def _ada(c, w_ada, b_ada):
    bsz, d = c.shape
    n = w_ada.shape[1]
    tn = 1024
    return pl.pallas_call(
        _ada_kernel,
        grid=(n // tn,),
        in_specs=[pl.BlockSpec((bsz, d), lambda j: (0, 0)),
                  pl.BlockSpec((d, tn), lambda j: (0, j)),
                  pl.BlockSpec((1, tn), lambda j: (0, j))],
        out_specs=pl.BlockSpec((bsz, tn), lambda j: (0, j)),
        out_shape=jax.ShapeDtypeStruct((bsz, n), F32),
        compiler_params=_cparams(("arbitrary",)),
        name="ada_mod",
    )(c, w_ada, b_ada.reshape(1, n))


def _prenorm_kernel(x_ref, g_ref, sc_ref, sh_ref, o_ref):
    y = _rms(x_ref[0]) * g_ref[...]
    o_ref[0] = (y * (1.0 + sc_ref[0]) + sh_ref[0]).astype(o_ref.dtype)


def _prenorm(x, g, sc, sh):
    bsz, s, d = x.shape
    ts = 512
    vec = pl.BlockSpec((1, 1, d), lambda b, i: (b, 0, 0))
    return pl.pallas_call(
        _prenorm_kernel,
        grid=(bsz, s // ts),
        in_specs=[pl.BlockSpec((1, ts, d), lambda b, i: (b, i, 0)),
                  pl.BlockSpec((1, d), lambda b, i: (0, 0)), vec, vec],
        out_specs=pl.BlockSpec((1, ts, d), lambda b, i: (b, i, 0)),
        out_shape=jax.ShapeDtypeStruct((bsz, s, d), BF16),
        compiler_params=_cparams(("arbitrary", "arbitrary")),
        name="prenorm",
    )(x, g.reshape(1, d), sc.reshape(bsz, 1, d), sh.reshape(bsz, 1, d))


def _mm_wt_kernel(shift, a_ref, *refs):
    w_ref = refs[0]
    o_ref, wb = refs[-2:]

    @pl.when(pl.program_id(1) == 0)
    def _():
        w = w_ref[0]
        if shift:
            w = jnp.concatenate([w[shift:], refs[1][0]], axis=0)
        wb[...] = w.T.astype(BF16)

    o_ref[...] = jnp.dot(a_ref[...], wb[...], preferred_element_type=F32).astype(o_ref.dtype)


def _dt_kernel(a_ref, w_ref, o_ref):
    o_ref[...] = _dot_nt(w_ref[0].astype(BF16), a_ref[...])


def _dt_proj(a, wt_stack, layer, row0, nh, tm):
    m, k = a.shape
    assert row0 % nh == 0
    return pl.pallas_call(
        _dt_kernel,
        grid=(m // tm,),
        in_specs=[pl.BlockSpec((tm, k), lambda i: (i, 0)),
                  pl.BlockSpec((1, nh, k), lambda i: (layer, row0 // nh, 0))],
        out_specs=pl.BlockSpec((nh, tm), lambda i: (0, i)),
        out_shape=jax.ShapeDtypeStruct((nh, m), F32),
        compiler_params=_cparams(("arbitrary",)),
        name="in_proj_dt",
    )(a, wt_stack)


def _matmul_wt(a, wt_stack, layer, row0, n, tm, tn, out_dtype, name):
    m, k = a.shape
    shift = row0 % tn
    base = row0 - shift
    assert n % tn == 0 and shift % SUBLANES == 0
    a_spec = pl.BlockSpec((tm, k), lambda j, i: (i, 0))
    in_specs = [a_spec, pl.BlockSpec((1, tn, k), lambda j, i: (layer, base // tn + j, 0))]
    args = [a, wt_stack]
    if shift:
        assert (base + tn) % shift == 0 and tn % shift == 0
        in_specs.append(pl.BlockSpec((1, shift, k), lambda j, i: (layer, (base + (j + 1) * tn) // shift, 0)))
        args.append(wt_stack)
    return pl.pallas_call(
        functools.partial(_mm_wt_kernel, shift),
        grid=(n // tn, m // tm),
        in_specs=in_specs,
        out_specs=pl.BlockSpec((tm, tn), lambda j, i: (i, j)),
        out_shape=jax.ShapeDtypeStruct((m, n), out_dtype),
        scratch_shapes=[pltpu.VMEM((k, tn), BF16)],
        compiler_params=_cparams(("arbitrary", "arbitrary")),
        name=name,
    )(*args)


def _softplus(x):
    return jnp.maximum(x, 0.0) + jnp.log(1.0 + jnp.exp(-jnp.abs(x)))


def _ssd_kernel(x_ref, b_ref, c_ref, z_ref, dt_ref, cwx_ref, cwb_ref, cwc_ref,
                cbx_ref, cbb_ref, cbc_ref, dtb_ref, alog_ref, dsk_ref, nw_ref,
                o_ref, cbuf, state):
    q = SSD_CHUNK
    gps = dtb_ref.shape[0]
    gw = x_ref.shape[2] // gps
    nst = b_ref.shape[2] // gps
    hg = gw // SSD_HEAD_DIM
    pad = SUBLANES

    @pl.when(pl.program_id(2) == 0)
    def _():
        cbuf[:, 0:pad, :] = jnp.zeros((gps, pad, gw + 2 * nst), F32)
        state[...] = jnp.zeros_like(state)

    hrow = lax.broadcasted_iota(jnp.int32, (LANES, q), 0)
    ri = lax.broadcasted_iota(jnp.int32, (q, q), 0)
    ci = lax.broadcasted_iota(jnp.int32, (q, q), 1)
    triu = jnp.where(ri <= ci, 1.0, 0.0).astype(BF16)
    er = lax.broadcasted_iota(jnp.int32, (LANES, gw), 0)
    ec = lax.broadcasted_iota(jnp.int32, (LANES, gw), 1)
    e_ch = jnp.where(ec // SSD_HEAD_DIM == er, 1.0, 0.0).astype(BF16)
    er2 = lax.broadcasted_iota(jnp.int32, (LANES, hg * q), 0)
    ec2 = lax.broadcasted_iota(jnp.int32, (LANES, hg * q), 1)
    e_t = jnp.where(ec2 // q == er2, 1.0, 0.0).astype(BF16)
    ri2 = lax.broadcasted_iota(jnp.int32, (q, hg * q), 0)
    ci2 = lax.broadcasted_iota(jnp.int32, (q, hg * q), 1) % q
    first_head = lax.broadcasted_iota(jnp.int32, (1, LANES), 1) < SSD_HEAD_DIM

    for u in range(gps):
        xc = slice(u * gw, (u + 1) * gw)
        nc = slice(u * nst, (u + 1) * nst)
        xin = jnp.concatenate([x_ref[0, :, xc], b_ref[0, :, nc], c_ref[0, :, nc]],
                              axis=-1).astype(F32)
        cbuf[u, pad:pad + q, :] = xin
        w = jnp.concatenate([cwx_ref[:, xc], cwb_ref[:, nc], cwc_ref[:, nc]], axis=-1)
        bias = jnp.concatenate([cbx_ref[:, xc], cbb_ref[:, nc], cbc_ref[:, nc]], axis=-1)
        acc = bias + w[SSD_CONV - 1:SSD_CONV, :] * xin
        for s in range(1, SSD_CONV):
            acc = acc + w[SSD_CONV - 1 - s:SSD_CONV - s, :] * cbuf[u, pad - s:pad - s + q, :]
        cbuf[u, pad - (SSD_CONV - 1):pad, :] = cbuf[u, pad + q - (SSD_CONV - 1):pad + q, :]
        xbc = _silu(acc)
        xs = xbc[:, :gw]
        bm = xbc[:, gw:gw + nst].astype(BF16)
        cm = xbc[:, gw + nst:].astype(BF16)

        dt_raw = jnp.concatenate([dt_ref[u * hg:(u + 1) * hg, :], jnp.zeros((LANES - hg, q), F32)],
                                 axis=0)
        dt_r = jnp.where(hrow < hg, _softplus(dt_raw + dtb_ref[u]), 0.0)
        a_r = dt_r * (-math.log2(math.e) * jnp.exp(alog_ref[u]))
        acum_r = _dot_exact_rhs(a_r, triu)
        acum_c = acum_r.T
        dt_c = dt_r.T

        dt_full = _dot_exact_rhs(dt_c, e_ch)
        acum_colb = _dot_exact_rhs(acum_c, e_t)
        acum_full = jnp.concatenate(
            [jnp.where(first_head, acum_colb[:, (2 * pr) * q:(2 * pr + 1) * q],
                       acum_colb[:, (2 * pr + 1) * q:(2 * pr + 2) * q])
             for pr in range(hg // 2)], axis=-1)
        acum_rowb = jnp.concatenate(
            [jnp.broadcast_to(acum_r[h:h + 1, :], (q, q)) for h in range(hg)], axis=-1)

        lmat = jnp.exp2(jnp.where(ri2 >= ci2, acum_colb - acum_rowb, NEG_BIG))
        cb = _dot_nt(cm, bm)
        mcat = (jnp.concatenate([cb] * hg, axis=-1) * lmat).astype(BF16)
        xdt = xs * dt_full
        y_pairs = []
        for pr in range(gw // LANES):
            xp = xdt[:, pr * LANES:(pr + 1) * LANES]
            xbd = jnp.concatenate([jnp.where(first_head, xp, 0.0).astype(BF16),
                                   jnp.where(first_head, 0.0, xp).astype(BF16)], axis=0)
            y_pairs.append(jnp.dot(mcat[:, pr * 2 * q:(pr + 1) * 2 * q], xbd,
                                   preferred_element_type=F32))
        y = jnp.concatenate(y_pairs, axis=-1)

        st = state[u]
        y = y + jnp.dot(cm, st.astype(BF16), preferred_element_type=F32) * jnp.exp2(acum_full)
        acum_last = acum_full[q - 1:q, :]
        xw = (xdt * jnp.exp2(acum_last - acum_full)).astype(BF16)
        bt = xbc[:, gw:gw + nst].T.astype(BF16)
        state[u] = st * jnp.exp2(acum_last) + jnp.dot(bt, xw, preferred_element_type=F32)

        y = y + xs * dsk_ref[:, xc]
        y = y * _silu(z_ref[0, :, xc].astype(F32))
        o_ref[0, :, xc] = (_rms(y) * nw_ref[:, xc]).astype(o_ref.dtype)


def _ssd(proj, dt_rows, conv_w, conv_b, dt_bias, a_log, d_skip, norm_w, d_inner):
    bsz, s, _ = proj.shape
    g = SSD_GROUPS
    gps = SSD_GROUPS_PER_STEP
    assert SSD_CHUNK == LANES and 2 * SSD_HEAD_DIM == LANES
    gw = d_inner // g
    hg = gw // SSD_HEAD_DIM
    nst = SSD_STATE
    q = SSD_CHUNK
    nc = s // q
    bw, bn = gps * gw, gps * nst
    x0 = d_inner // bw
    b0 = (2 * d_inner) // bn
    c0 = (2 * d_inner + g * nst) // bn
    cwb0 = d_inner // bn
    cwc0 = (d_inner + g * nst) // bn
    conv_b2 = conv_b.reshape(1, -1)
    head_pad = lambda v: jnp.pad(v.reshape(g, hg), ((0, 0), (0, LANES - hg))).reshape(g, LANES, 1)
    return pl.pallas_call(
        _ssd_kernel,
        grid=(bsz, g // gps, nc),
        in_specs=[
            pl.BlockSpec((1, q, bw), lambda b, gi, c: (b, c, x0 + gi)),
            pl.BlockSpec((1, q, bn), lambda b, gi, c: (b, c, b0 + gi)),
            pl.BlockSpec((1, q, bn), lambda b, gi, c: (b, c, c0 + gi)),
            pl.BlockSpec((1, q, bw), lambda b, gi, c: (b, c, gi)),
            pl.BlockSpec((gps * hg, q), lambda b, gi, c: (gi, b * nc + c)),
            pl.BlockSpec((SSD_CONV, bw), lambda b, gi, c: (0, gi)),
            pl.BlockSpec((SSD_CONV, bn), lambda b, gi, c: (0, cwb0 + gi)),
            pl.BlockSpec((SSD_CONV, bn), lambda b, gi, c: (0, cwc0 + gi)),
            pl.BlockSpec((1, bw), lambda b, gi, c: (0, gi)),
            pl.BlockSpec((1, bn), lambda b, gi, c: (0, cwb0 + gi)),
            pl.BlockSpec((1, bn), lambda b, gi, c: (0, cwc0 + gi)),
            pl.BlockSpec((gps, LANES, 1), lambda b, gi, c: (gi, 0, 0)),
            pl.BlockSpec((gps, LANES, 1), lambda b, gi, c: (gi, 0, 0)),
            pl.BlockSpec((1, bw), lambda b, gi, c: (0, gi)),
            pl.BlockSpec((1, bw), lambda b, gi, c: (0, gi)),
        ],
        out_specs=pl.BlockSpec((1, q, bw), lambda b, gi, c: (b, c, gi)),
        out_shape=jax.ShapeDtypeStruct((bsz, s, d_inner), BF16),
        scratch_shapes=[pltpu.VMEM((gps, SUBLANES + q, gw + 2 * nst), F32),
                        pltpu.VMEM((gps, nst, gw), F32)],
        compiler_params=_cparams(("arbitrary", "arbitrary", "arbitrary")),
        name="ssd_scan",
    )(proj, proj, proj, proj, dt_rows, conv_w, conv_w, conv_w, conv_b2, conv_b2, conv_b2,
      head_pad(dt_bias), head_pad(a_log),
      jnp.repeat(d_skip, SSD_HEAD_DIM).reshape(1, d_inner), norm_w.reshape(1, d_inner))


def _rope_kernel(pos_ref, inv_ref, cos_ref, sin_ref):
    ang = pos_ref[...].astype(F32) * inv_ref[...]
    lane = lax.broadcasted_iota(jnp.int32, ang.shape, 1)
    cos_ref[...] = jnp.cos(ang)
    sin_ref[...] = jnp.where(lane < ATT_HEAD_DIM // 2, -1.0, 1.0) * jnp.sin(ang)


def _rope_tables(positions):
    t = positions.size
    half = ATT_HEAD_DIM // 2
    inv = 1.0 / (ROPE_THETA ** (jnp.arange(half, dtype=F32) / half))
    inv2 = jnp.concatenate([inv, inv]).reshape(1, ATT_HEAD_DIM)
    tt = min(2048, t)
    return pl.pallas_call(
        _rope_kernel,
        grid=(t // tt,),
        in_specs=[pl.BlockSpec((tt, 1), lambda i: (i, 0)),
                  pl.BlockSpec((1, ATT_HEAD_DIM), lambda i: (0, 0))],
        out_specs=[pl.BlockSpec((tt, ATT_HEAD_DIM), lambda i: (i, 0))] * 2,
        out_shape=[jax.ShapeDtypeStruct((t, ATT_HEAD_DIM), F32)] * 2,
        compiler_params=_cparams(("arbitrary",)),
        name="rope_tables",
    )(positions.reshape(t, 1), inv2)


def _rot(x, cos2, sin2):
    return x * cos2 + pltpu.roll(x, ATT_HEAD_DIM // 2, axis=1) * sin2


def _attn_kernel(q_ref, k_ref, v_ref, cos_ref, sin_ref, o_ref, krot, vt, kmean):
    blk = MOBA_BLOCK
    nb = k_ref.shape[1] // blk
    log2e_scale = ATT_HEAD_DIM ** -0.5 * math.log2(math.e)

    kmean[...] = jnp.zeros_like(kmean)
    for j in range(nb):
        rows = slice(j * blk, (j + 1) * blk)
        kr = _rot(k_ref[0, rows, :].astype(F32), cos_ref[0, rows, :], sin_ref[0, rows, :])
        krot[rows, :] = kr.astype(BF16)
        kmean[j:j + 1, :] = jnp.mean(kr, axis=0, keepdims=True)
        vt[:, rows] = v_ref[0, rows, :].astype(F32).T.astype(BF16)
    km = kmean[...]

    ki = lax.broadcasted_iota(jnp.int32, (blk, blk), 0)
    qj = lax.broadcasted_iota(jnp.int32, (blk, blk), 1)
    sub = lax.broadcasted_iota(jnp.int32, (km.shape[0], blk), 0)
    for qi in range(nb):
        qrows = slice(qi * blk, (qi + 1) * blk)
        qr = _rot(q_ref[0, qrows, :].astype(F32), cos_ref[0, qrows, :], sin_ref[0, qrows, :])
        qs = (qr * log2e_scale).T.astype(BF16)
        gate = _dot_nt_x3(km, qr)
        scores = [jnp.where(ki <= qj, jnp.dot(krot[qrows, :], qs, preferred_element_type=F32), NEG_BIG)]
        for j in range(qi):
            gj = gate[j:j + 1, :]
            ahead = (gate > gj) | ((gate == gj) & (sub < j))
            rank = jnp.sum(jnp.where((sub < qi) & ahead, 1.0, 0.0), axis=0, keepdims=True)
            bias = jnp.where(rank < MOBA_TOPK, 0.0, NEG_BIG)
            scores.append(jnp.dot(krot[j * blk:(j + 1) * blk, :], qs,
                                  preferred_element_type=F32) + bias)
        m = scores[0]
        for sj in scores[1:]:
            m = jnp.maximum(m, sj)
        m = jnp.max(m, axis=0, keepdims=True)
        l = None
        acc = None
        for j, sj in enumerate(scores):
            src = qi if j == 0 else j - 1
            p = jnp.exp2(sj - m)
            lj = jnp.sum(p, axis=0, keepdims=True)
            aj = jnp.dot(vt[:, src * blk:(src + 1) * blk], p.astype(BF16),
                         preferred_element_type=F32)
            l = lj if l is None else l + lj
            acc = aj if acc is None else acc + aj
        o_ref[0, qrows, :] = (acc / l).T.astype(o_ref.dtype)


def _attention(proj, cos2, sin2, q0, k0, v0, n_heads):
    bsz, s, _ = proj.shape
    dh = ATT_HEAD_DIM
    col = lambda c0: pl.BlockSpec((1, s, dh), lambda b, h: (b, 0, c0 + h))
    tab = pl.BlockSpec((1, s, dh), lambda b, h: (b, 0, 0))
    return pl.pallas_call(
        _attn_kernel,
        grid=(bsz, n_heads),
        in_specs=[col(q0), col(k0), col(v0), tab, tab],
        out_specs=pl.BlockSpec((1, s, dh), lambda b, h: (b, 0, h)),
        out_shape=jax.ShapeDtypeStruct((bsz, s, n_heads * dh), BF16),
        scratch_shapes=[pltpu.VMEM((s, dh), BF16), pltpu.VMEM((dh, s), BF16),
                        pltpu.VMEM((-(-(s // MOBA_BLOCK) // SUBLANES) * SUBLANES, dh), F32)],
        compiler_params=_cparams(("arbitrary", "arbitrary")),
        name="moba_attn",
    )(proj, proj, proj, cos2.reshape(bsz, s, dh), sin2.reshape(bsz, s, dh))


def _merge_kernel(ys_ref, ya_ref, w1_ref, w2_ref, gs_ref, ga_ref, o_ref):
    b1 = jnp.dot(ys_ref[...], w1_ref[...], preferred_element_type=F32)
    b2 = jnp.dot(ya_ref[...], w2_ref[...], preferred_element_type=F32)
    o_ref[...] = (jax.nn.sigmoid(gs_ref[...].astype(F32)) * b1
                  + jax.nn.sigmoid(ga_ref[...].astype(F32)) * b2).astype(o_ref.dtype)


def _merge(y_ssd, y_att, w1, w2, proj2d, gs0, ga0):
    m, k1 = y_ssd.shape
    k2 = y_att.shape[1]
    n = w1.shape[1]
    tm, tn = MERGE_TM, MERGE_TN
    return pl.pallas_call(
        _merge_kernel,
        grid=(n // tn, m // tm),
        in_specs=[pl.BlockSpec((tm, k1), lambda j, i: (i, 0)),
                  pl.BlockSpec((tm, k2), lambda j, i: (i, 0)),
                  pl.BlockSpec((k1, tn), lambda j, i: (0, j)),
                  pl.BlockSpec((k2, tn), lambda j, i: (0, j)),
                  pl.BlockSpec((tm, tn), lambda j, i: (i, gs0 // tn + j)),
                  pl.BlockSpec((tm, tn), lambda j, i: (i, ga0 // tn + j))],
        out_specs=pl.BlockSpec((tm, tn), lambda j, i: (i, j)),
        out_shape=jax.ShapeDtypeStruct((m, n), BF16),
        compiler_params=_cparams(("arbitrary", "arbitrary")),
        name="branch_merge",
    )(y_ssd, y_att, w1, w2, proj2d, proj2d)


def _outproj_kernel(m_ref, w_ref, x_ref, gpost_ref, gt_ref, gpre_ref, sc_ref, sh_ref, wr_ref,
                    x1_ref, h2b_ref, lg_ref):
    for r in range(m_ref.shape[1] // OUTPROJ_SUBTILE):
        rows = slice(r * OUTPROJ_SUBTILE, (r + 1) * OUTPROJ_SUBTILE)
        y = jnp.dot(m_ref[0, rows, :], w_ref[...], preferred_element_type=F32)
        x1 = x_ref[0, rows, :] + gt_ref[0] * (_rms(y) * gpost_ref[...])
        h2 = (_rms(x1) * gpre_ref[...]) * (1.0 + sc_ref[0]) + sh_ref[0]
        x1_ref[0, rows, :] = x1
        h2b_ref[0, rows, :] = h2.astype(BF16)
        h_hi = h2.astype(BF16)
        h_lo = (h2 - h_hi.astype(F32)).astype(BF16)
        wr = wr_ref[...]
        w_hi = wr.astype(BF16)
        w_lo = (wr - w_hi.astype(F32)).astype(BF16)
        lg = (jnp.dot(h_hi, w_hi, preferred_element_type=F32)
              + (jnp.dot(h_hi, w_lo, preferred_element_type=F32)
                 + jnp.dot(h_lo, w_hi, preferred_element_type=F32)))
        lg_ref[:, rows] = lg.T[:lg_ref.shape[0], :]


def _outproj(merged, w_out, x, g_post, gt1, g_pre, sc2, sh2, w_router):
    bsz, s, d = x.shape
    ne = w_router.shape[1]
    w_router_p = jnp.pad(w_router, ((0, 0), (0, LANES - ne)))
    tm = OUTPROJ_TILE
    nt = s // tm
    row = pl.BlockSpec((1, tm, d), lambda i: (i // nt, i % nt, 0))
    vec = pl.BlockSpec((1, 1, d), lambda i: (i // nt, 0, 0))
    par = pl.BlockSpec((1, d), lambda i: (0, 0))
    return pl.pallas_call(
        _outproj_kernel,
        grid=(bsz * nt,),
        in_specs=[row, pl.BlockSpec((d, d), lambda i: (0, 0)), row, par, vec, par, vec, vec,
                  pl.BlockSpec((d, LANES), lambda i: (0, 0))],
        out_specs=[row, row, pl.BlockSpec((ne, tm), lambda i: (0, i))],
        out_shape=[jax.ShapeDtypeStruct((bsz, s, d), F32),
                   jax.ShapeDtypeStruct((bsz, s, d), BF16),
                   jax.ShapeDtypeStruct((ne, bsz * s), F32)],
        compiler_params=_cparams(("arbitrary",)),
        name="out_proj_norms",
    )(merged.reshape(bsz, s, d), w_out, x, g_post.reshape(1, d), gt1.reshape(bsz, 1, d),
      g_pre.reshape(1, d), sc2.reshape(bsz, 1, d), sh2.reshape(bsz, 1, d), w_router_p)


def _router_kernel(lg_ref, rb_ref, w_ref, rank_ref, cnt_ref, gsc):
    ne, tn = lg_ref.shape
    ng = N_EXPERT_GROUPS
    eg = ne // ng
    scores = jax.nn.sigmoid(lg_ref[...])
    biased = scores + rb_ref[...]
    i8 = lax.broadcasted_iota(jnp.int32, (eg, tn), 0)
    for g in range(ng):
        v = biased[g * eg:(g + 1) * eg, :]
        m1 = jnp.max(v, axis=0, keepdims=True)
        first = jnp.min(jnp.where(v == m1, i8, eg), axis=0, keepdims=True)
        m2 = jnp.max(jnp.where(i8 == first, -jnp.inf, v), axis=0, keepdims=True)
        gsc[g:g + 1, :] = m1 + m2
    gs = gsc[...]
    gi = lax.broadcasted_iota(jnp.int32, (ng, tn), 0)
    masked = []
    for g in range(ng):
        sg = gs[g:g + 1, :]
        ahead = (gs > sg) | ((gs == sg) & (gi < g))
        rank = jnp.sum(jnp.where(ahead, 1.0, 0.0), axis=0, keepdims=True)
        masked.append(jnp.where(rank < TOPK_GROUPS, biased[g * eg:(g + 1) * eg, :], -jnp.inf))
    cur = jnp.concatenate(masked, axis=0)
    sub = lax.broadcasted_iota(jnp.int32, (ne, tn), 0)
    sel = jnp.zeros((ne, tn), F32)
    for k in range(TOP_K):
        mx = jnp.max(cur, axis=0, keepdims=True)
        idx = jnp.min(jnp.where(cur == mx, sub, ne), axis=0, keepdims=True)
        hit = sub == idx
        sel = jnp.where(hit, 1.0, sel)
        cur = jnp.where(hit, -jnp.inf, cur)
    picked = sel * scores
    wsum = jnp.sum(picked, axis=0, keepdims=True)
    w_ref[...] = picked / wsum * ROUTED_SCALE
    ti = lax.broadcasted_iota(jnp.int32, (tn, tn), 0)
    tj = lax.broadcasted_iota(jnp.int32, (tn, tn), 1)
    before = jnp.where(ti < tj, 1.0, 0.0).astype(BF16)
    rank = jnp.dot(sel.astype(BF16), before, preferred_element_type=F32)
    rank_ref[...] = jnp.where(sel > 0.0, rank, NEG_BIG)
    cnt_ref[0] = jnp.broadcast_to(jnp.sum(sel, axis=1, keepdims=True), (ne, LANES))


def _router(logits_t, router_bias):
    ne, t = logits_t.shape
    tn = MOE_TILE
    tile = pl.BlockSpec((ne, tn), lambda i: (0, i))
    return pl.pallas_call(
        _router_kernel,
        grid=(t // tn,),
        in_specs=[tile, pl.BlockSpec((ne, 1), lambda i: (0, 0))],
        out_specs=[tile, tile, pl.BlockSpec((1, ne, LANES), lambda i: (i, 0, 0))],
        out_shape=[jax.ShapeDtypeStruct((ne, t), F32), jax.ShapeDtypeStruct((ne, t), F32),
                   jax.ShapeDtypeStruct((t // tn, ne, LANES), F32)],
        scratch_shapes=[pltpu.VMEM((N_EXPERT_GROUPS, tn), F32)],
        compiler_params=_cparams(("arbitrary",)),
        name="router_topk",
    )(logits_t, router_bias.reshape(ne, 1))


def _tile_rows(n_experts):
    return MOE_TILE * TOP_K + n_experts * SEG_ALIGN


def _segment_copies(src, dst, sem, n, src_off, dst_off, wait):
    off = jnp.int32(0)
    for p in SEG_CHUNKS:
        bit = n & p

        @pl.when(bit != 0)
        def _(p=p, off=off):
            cp = pltpu.make_async_copy(
                src.at[pl.ds(pl.multiple_of(src_off + off, SEG_ALIGN), p), :],
                dst.at[pl.ds(pl.multiple_of(dst_off + off, SEG_ALIGN), p), :], sem)
            if wait:
                cp.wait()
            else:
                cp.start()
        off = off + bit


def _for_each_segment(ne, fn):
    def body(e, carry):
        fn(e)
        return carry
    lax.fori_loop(0, ne, body, 0)


def _build_perm(pbuf, posb, ss_ref, c16_ref, tile, ne, value_row):
    tn = posb.shape[1]
    pbuf[...] = jnp.zeros_like(pbuf)
    win = lax.broadcasted_iota(jnp.int32, (PERM_WINDOW, tn), 0)

    def per_expert(e):
        base = ss_ref[tile, e]
        windows = lax.shift_right_logical(c16_ref[tile, e] + (PERM_WINDOW - 1),
                                          PERM_WINDOW.bit_length() - 1)
        prow = posb[pl.ds(e, 1), :]
        vrow = value_row(e)

        def per_window(g, carry):
            r0 = pl.multiple_of(base + g * PERM_WINDOW, SEG_ALIGN)
            hit = prow == (win + r0).astype(F32)
            pbuf[pl.ds(r0, PERM_WINDOW), :] = jnp.where(hit, vrow, 0.0).astype(BF16)
            return carry
        lax.fori_loop(0, windows, per_window, 0)
    _for_each_segment(ne, per_expert)


def _dispatch_kernel(ss_ref, c16_ref, go_ref, ts_ref, tl_ref, rank_ref, ssv_ref, h_ref, xs_hbm,
                     pbuf, xs, posb, sem):
    i = pl.program_id(0)
    n = pl.num_programs(0)
    ne = rank_ref.shape[0]
    tn = rank_ref.shape[1]
    posb[...] = rank_ref[...] + ssv_ref[0]
    _build_perm(pbuf, posb, ss_ref, c16_ref, i, ne, lambda e: 1.0)

    def drain(tile):
        _for_each_segment(ne, lambda e: _segment_copies(
            xs, xs_hbm, sem, c16_ref[tile, e], ss_ref[tile, e], go_ref[tile, e], True))

    @pl.when(i > 0)
    def _():
        drain(i - 1)

    used = ss_ref[i, ne - 1] + c16_ref[i, ne - 1]
    def chunk(ch):
        rows = slice(ch * tn, (ch + 1) * tn)
        xs[rows, :] = jnp.dot(pbuf[rows, :], h_ref[...], preferred_element_type=F32).astype(BF16)

    for ch in range(TOP_K):
        chunk(ch)
    for ch in range(TOP_K, xs.shape[0] // tn):
        pl.when(ch * tn < used)(functools.partial(chunk, ch))

    _for_each_segment(ne, lambda e: _segment_copies(
        xs, xs_hbm, sem, c16_ref[i, e], ss_ref[i, e], go_ref[i, e], False))

    @pl.when(i == n - 1)
    def _():
        drain(i)
        xs[0:MOE_ROWS, :] = jnp.zeros((MOE_ROWS, xs.shape[1]), BF16)
        first_free = lax.shift_right_logical(ts_ref[ne], MOE_ROWS.bit_length() - 1)
        n_blocks = xs_hbm.shape[0] // MOE_ROWS
        for wait in (False, True):
            _for_each_segment(ne, lambda e: _segment_copies(
                xs, xs_hbm, sem, tl_ref[e], 0, ts_ref[e], wait))

            def free_block(b, carry, wait=wait):
                cp = pltpu.make_async_copy(
                    xs.at[pl.ds(0, MOE_ROWS), :],
                    xs_hbm.at[pl.ds(pl.multiple_of(b * MOE_ROWS, MOE_ROWS), MOE_ROWS), :], sem)
                if wait:
                    cp.wait()
                else:
                    cp.start()
                return carry
            lax.fori_loop(first_free, n_blocks, free_block, 0)


def _dispatch(seg_start, c16, goff, tail_start, tail_len, rank, h2b, n_rows):
    ne, t = rank.shape
    d = h2b.shape[1]
    tn = MOE_TILE
    rt = _tile_rows(ne)
    grid_spec = pltpu.PrefetchScalarGridSpec(
        num_scalar_prefetch=5,
        grid=(t // tn,),
        in_specs=[pl.BlockSpec((ne, tn), lambda i, *_: (0, i)),
                  pl.BlockSpec((1, ne, 1), lambda i, *_: (i, 0, 0)),
                  pl.BlockSpec((tn, d), lambda i, *_: (i, 0))],
        out_specs=pl.BlockSpec(memory_space=pl.ANY),
        scratch_shapes=[pltpu.VMEM((rt + PERM_WINDOW, tn), BF16), pltpu.VMEM((rt, d), BF16),
                        pltpu.VMEM((ne, tn), F32), pltpu.SemaphoreType.DMA],
    )
    return pl.pallas_call(
        _dispatch_kernel,
        grid_spec=grid_spec,
        out_shape=jax.ShapeDtypeStruct((n_rows, d), BF16),
        compiler_params=_cparams(("arbitrary",)),
        name="moe_dispatch",
    )(seg_start, c16, goff, tail_start, tail_len, rank,
      seg_start.astype(F32).reshape(t // tn, ne, 1), h2b)


def _combine_kernel(ss_ref, c16_ref, go_ref, rank_ref, w_ref, ssv_ref, y_hbm, o_ref,
                    pbuf, ys, posb, sem):
    i = pl.program_id(0)
    ne = rank_ref.shape[0]
    tn = rank_ref.shape[1]
    ys[...] = jnp.zeros_like(ys)
    _for_each_segment(ne, lambda e: _segment_copies(
        y_hbm, ys, sem, c16_ref[i, e], go_ref[i, e], ss_ref[i, e], False))
    posb[...] = rank_ref[...] + ssv_ref[0]
    _build_perm(pbuf, posb, ss_ref, c16_ref, i, ne, lambda e: w_ref[pl.ds(e, 1), :])
    _for_each_segment(ne, lambda e: _segment_copies(
        y_hbm, ys, sem, c16_ref[i, e], go_ref[i, e], ss_ref[i, e], True))
    used = ss_ref[i, ne - 1] + c16_ref[i, ne - 1]
    def chunk(ch):
        rows = slice(ch * tn, (ch + 1) * tn)
        return lax.dot_general(pbuf[rows, :], ys[rows, :], (((0,), (0,)), ((), ())),
                               preferred_element_type=F32)

    always = TOP_K
    acc = chunk(0)
    for ch in range(1, always):
        acc = acc + chunk(ch)
    o_ref[...] = acc
    for ch in range(always, ys.shape[0] // tn):
        @pl.when(ch * tn < used)
        def _(ch=ch):
            o_ref[...] += chunk(ch)


def _combine(seg_start, c16, goff, rank, selw, y_rows):
    ne, t = rank.shape
    d = y_rows.shape[1]
    tn = MOE_TILE
    rt = _tile_rows(ne)
    tile = pl.BlockSpec((ne, tn), lambda i, *_: (0, i))
    grid_spec = pltpu.PrefetchScalarGridSpec(
        num_scalar_prefetch=3,
        grid=(t // tn,),
        in_specs=[tile, tile, pl.BlockSpec((1, ne, 1), lambda i, *_: (i, 0, 0)),
                  pl.BlockSpec(memory_space=pl.ANY)],
        out_specs=pl.BlockSpec((tn, d), lambda i, *_: (i, 0)),
        scratch_shapes=[pltpu.VMEM((rt + PERM_WINDOW, tn), BF16), pltpu.VMEM((rt, d), BF16),
                        pltpu.VMEM((ne, tn), F32), pltpu.SemaphoreType.DMA],
    )
    return pl.pallas_call(
        _combine_kernel,
        grid_spec=grid_spec,
        out_shape=jax.ShapeDtypeStruct((t, d), F32),
        compiler_params=_cparams(("arbitrary",)),
        name="moe_combine",
    )(seg_start, c16, goff, rank, selw, seg_start.astype(F32).reshape(t // tn, ne, 1), y_rows)


EXPERT_LOOKAHEAD = 2


def _expert_kernel(be_ref, nv_ref, sl_ref, x_ref, wg_ref, wu_ref, wd_ref, o_ref, wg, wu, wd):
    s = pl.program_id(0)
    n = pl.num_programs(0) - EXPERT_LOOKAHEAD
    clampi = lambda j: jnp.clip(j, 0, n - 1)

    def starts_expert(j):
        jc = clampi(j)
        first = (jc == 0) | (be_ref[jc] != be_ref[clampi(jc - 1)])
        return (j >= 0) & (j < n) & (nv_ref[jc] > 0) & first

    i = s - EXPERT_LOOKAHEAD
    ic = clampi(i)
    slot = sl_ref[ic]

    @pl.when(starts_expert(i))
    def _():
        wd[slot] = wd_ref[0].astype(BF16)

    @pl.when((i >= 0) & (nv_ref[ic] == 0))
    def _():
        o_ref[...] = jnp.zeros_like(o_ref)

    @pl.when((i >= 0) & (nv_ref[ic] > 0))
    def _():
        x = x_ref[...]
        gate = jnp.dot(x, wg[slot], preferred_element_type=F32)
        up = jnp.dot(x, wu[slot], preferred_element_type=F32)
        act = (_silu(gate) * up).astype(BF16)
        o_ref[...] = jnp.dot(act, wd[slot], preferred_element_type=F32).astype(o_ref.dtype)

    @pl.when(starts_expert(s))
    def _():
        wg[sl_ref[clampi(s)]] = wg_ref[0].astype(BF16)

    @pl.when(starts_expert(s - 1))
    def _():
        wu[sl_ref[clampi(s - 1)]] = wu_ref[0].astype(BF16)


def _expert_ffn(blk_exp, blk_valid, blk_slot, x_sorted, w_gate, w_up, w_down):
    n_rows, d = x_sorted.shape
    f = w_gate.shape[2]
    rows = MOE_ROWS
    n = n_rows // rows
    blk = lambda back: (lambda s, be, nv, sl: (jnp.clip(s - back, 0, n - 1), 0))
    wgt = lambda back: (lambda s, be, nv, sl: (be[jnp.clip(s - back, 0, n - 1)], 0, 0))
    grid_spec = pltpu.PrefetchScalarGridSpec(
        num_scalar_prefetch=3,
        grid=(n + EXPERT_LOOKAHEAD,),
        in_specs=[pl.BlockSpec((rows, d), blk(2)),
                  pl.BlockSpec((1, d, f), wgt(0)),
                  pl.BlockSpec((1, d, f), wgt(1)),
                  pl.BlockSpec((1, f, d), wgt(2))],
        out_specs=pl.BlockSpec((rows, d), blk(2)),
        scratch_shapes=[pltpu.VMEM((2, d, f), BF16), pltpu.VMEM((2, d, f), BF16),
                        pltpu.VMEM((2, f, d), BF16)],
    )
    return pl.pallas_call(
        _expert_kernel,
        grid_spec=grid_spec,
        out_shape=jax.ShapeDtypeStruct((n_rows, d), BF16),
        compiler_params=_cparams(("arbitrary",)),
        name="expert_ffn",
    )(blk_exp, blk_valid, blk_slot, x_sorted, w_gate, w_up, w_down)


def _shared_kernel(x_ref, wg_ref, wu_ref, wd_ref, r_ref, x1_ref, gt_ref, g_ref, o_ref):
    x = x_ref[0]
    gate = jnp.dot(x, wg_ref[...], preferred_element_type=F32)
    up = jnp.dot(x, wu_ref[...], preferred_element_type=F32)
    act = (_silu(gate) * up).astype(BF16)
    y = r_ref[0] + jnp.dot(act, wd_ref[...], preferred_element_type=F32)
    o_ref[0] = x1_ref[0] + gt_ref[0] * (_rms(y) * g_ref[...])


def _shared_final(h2b, wg, wu, wd, routed, x1, gt2, g_post):
    bsz, s, d = x1.shape
    f = wg.shape[1]
    tm = 512
    nt = s // tm
    row = pl.BlockSpec((1, tm, d), lambda i: (i // nt, i % nt, 0))
    full = lambda shape: pl.BlockSpec(shape, lambda i: (0, 0))
    return pl.pallas_call(
        _shared_kernel,
        grid=(bsz * nt,),
        in_specs=[row, full((d, f)), full((d, f)), full((f, d)), row, row,
                  pl.BlockSpec((1, 1, d), lambda i: (i // nt, 0, 0)), full((1, d))],
        out_specs=row,
        out_shape=jax.ShapeDtypeStruct((bsz, s, d), F32),
        compiler_params=_cparams(("arbitrary",)),
        name="shared_ffn_final",
    )(h2b, wg, wu, wd, routed.reshape(bsz, s, d), x1, gt2.reshape(bsz, 1, d), g_post.reshape(1, d))


def _dispatch_plan(counts, n_blocks):
    n_tiles, ne = counts.shape
    c16 = (counts + SEG_ALIGN - 1) // SEG_ALIGN * SEG_ALIGN
    seg_start = jnp.cumsum(c16, axis=1) - c16
    tot = jnp.sum(c16, axis=0)
    padded = (tot + MOE_ROWS - 1) // MOE_ROWS * MOE_ROWS
    pad_end = jnp.cumsum(padded)
    pad_start = pad_end - padded
    goff = pad_start[None, :] + jnp.cumsum(c16, axis=0) - c16
    blk_row = jnp.arange(n_blocks, dtype=jnp.int32) * MOE_ROWS
    blk_exp = jnp.sum(pad_end[None, :] <= blk_row[:, None], axis=1)
    blk_exp = jnp.minimum(blk_exp, ne - 1).astype(jnp.int32)
    owns = (pad_start[None, :] <= blk_row[:, None]) & (blk_row[:, None] < pad_end[None, :])
    rows_left = jnp.clip((pad_start + tot)[None, :] - blk_row[:, None], 0, MOE_ROWS)
    blk_valid = jnp.sum(jnp.where(owns, rows_left, 0), axis=1).astype(jnp.int32)
    blk_slot = jnp.sum((padded > 0)[None, :] & (pad_start[None, :] <= blk_row[:, None]), axis=1) % 2
    i32 = lambda v: v.astype(jnp.int32)
    tail_start = jnp.concatenate([pad_start + tot, pad_end[-1:]])
    return (i32(seg_start), i32(c16), i32(goff), i32(tail_start), i32(padded - tot),
            blk_exp, blk_valid, i32(blk_slot))


def kernel(x, c, positions, w_ada, b_ada, g_pre_mix, g_post_mix, g_pre_ffn, g_post_ffn, w_in,
           conv_w, conv_b, dt_bias, a_log, d_skip, ssd_norm_w, w_ssd_br, w_attn_br, w_out,
           w_router, router_bias, w_gate, w_up, w_down, ws_gate, ws_up, ws_down):
    bsz, s, d = x.shape
    t = bsz * s
    for l in range(w_ada.shape[0]):
        d_inner = w_ssd_br.shape[1]
        att_w = w_attn_br.shape[1]
        n_ssd_heads = dt_bias.shape[1]
        conv_dim = conv_w.shape[2]
        n_heads = att_w // ATT_HEAD_DIM
        n_experts = w_router.shape[2]

        mod = _ada(c, w_ada[l], b_ada[l])
        sh1, sc1, gt1, sh2, sc2, gt2 = jnp.split(mod, 6, axis=-1)

        o_dt = d_inner + conv_dim
        o_q = o_dt + n_ssd_heads
        w_in_t = jnp.swapaxes(w_in, 1, 2)
        h1 = _prenorm(x, g_pre_mix[l], sc1, sh1).reshape(t, d)
        proj_s = _matmul_wt(h1, w_in_t, l, 0, o_dt, INPROJ_TM, INPROJ_TN, BF16, "in_proj_ssd")
        proj_a = _matmul_wt(h1, w_in_t, l, o_q, w_in.shape[2] - o_q, INPROJ_TM, INPROJ_TN, BF16,
                            "in_proj_att")
        dt_t = _dt_proj(h1, w_in_t, l, o_dt, n_ssd_heads, INPROJ_TM)
        k_off = att_w
        v_off = k_off + att_w
        gs_off = v_off + att_w
        ga_off = gs_off + d

        y_ssd = _ssd(proj_s.reshape(bsz, s, o_dt), dt_t, conv_w[l], conv_b[l], dt_bias[l], a_log[l],
                     d_skip[l], ssd_norm_w[l], d_inner)

        cos2, sin2 = _rope_tables(positions)
        y_att = _attention(proj_a.reshape(bsz, s, -1), cos2, sin2, 0, k_off // ATT_HEAD_DIM,
                           v_off // ATT_HEAD_DIM, n_heads)

        merged = _merge(y_ssd.reshape(t, d_inner), y_att.reshape(t, att_w),
                        w_ssd_br[l].astype(BF16), w_attn_br[l].astype(BF16), proj_a, gs_off, ga_off)
        x1, h2b, logits_t = _outproj(merged, w_out[l].astype(BF16), x, g_post_mix[l], gt1,
                                     g_pre_ffn[l], sc2, sh2, w_router[l])

        selw, rank, cnt = _router(logits_t, router_bias[l])
        n_tiles = t // MOE_TILE
        n_blocks = -(-(t * TOP_K + n_tiles * n_experts * (SEG_ALIGN - 1)) // MOE_ROWS) + n_experts
        seg_start, c16, goff, tail_start, tail_len, blk_exp, blk_valid, blk_slot = _dispatch_plan(
            cnt[:, :, 0].astype(jnp.int32), n_blocks)
        x_sorted = _dispatch(seg_start, c16, goff, tail_start, tail_len, rank, h2b.reshape(t, d),
                             n_blocks * MOE_ROWS)
        y_rows = _expert_ffn(blk_exp, blk_valid, blk_slot, x_sorted, w_gate[l], w_up[l], w_down[l])
        routed = _combine(seg_start, c16, goff, rank, selw, y_rows)
        x = _shared_final(h2b, ws_gate[l].astype(BF16), ws_up[l].astype(BF16),
                          ws_down[l].astype(BF16), routed, x1, gt2, g_post_ffn[l])
    return x
```

```python
import functools
import math

import jax
import jax.numpy as jnp
from jax import lax
from jax.experimental import pallas as pl
from jax.experimental.pallas import tpu as pltpu

F32 = jnp.float32
BF16 = jnp.bfloat16

NORM_EPS = 1e-6
ROPE_THETA = 10000.0

SSD_HEAD_DIM = 64
SSD_GROUPS = 8
SSD_STATE = 128
SSD_CONV = 4
SSD_CHUNK = 128
SSD_GROUPS_PER_STEP = 8

ATT_HEAD_DIM = 128
MOBA_BLOCK = 256
MOBA_TOPK = 3

TOP_K = 8
N_EXPERT_GROUPS = 8
TOPK_GROUPS = 4
ROUTED_SCALE = 2.5
MOE_ROWS = 512
MOE_TILE = 512
SEG_ALIGN = 16
SEG_CHUNKS = (512, 256, 128, 64, 32, 16)
PERM_WINDOW = 128

INPROJ_TM, INPROJ_TN = 1024, 1024
MERGE_TM, MERGE_TN = 512, 512
OUTPROJ_TILE = 512
OUTPROJ_SUBTILE = 256

LANES = 128
SUBLANES = 8
VMEM_LIMIT = 56 * 1024 * 1024
NEG_BIG = -1e30


def _cparams(sem):
    return pltpu.CompilerParams(dimension_semantics=sem, vmem_limit_bytes=VMEM_LIMIT)


def _silu(x):
    return x * jax.nn.sigmoid(x)


def _rms(x):
    return x * lax.rsqrt(jnp.mean(x * x, axis=-1, keepdims=True) + NORM_EPS)


def _split3(a):
    a1 = a.astype(BF16)
    r1 = a - a1.astype(F32)
    a2 = r1.astype(BF16)
    a3 = (r1 - a2.astype(F32)).astype(BF16)
    return a1, a2, a3


def _dot_exact_rhs(a, e_bf16):
    out = None
    for p in _split3(a):
        t = jnp.dot(p, e_bf16, preferred_element_type=F32)
        out = t if out is None else out + t
    return out


def _dot_nt(a, b):
    return lax.dot_general(a, b, (((1,), (1,)), ((), ())), preferred_element_type=F32)


def _dot_nt_x3(a, b):
    a1 = a.astype(BF16)
    a2 = (a - a1.astype(F32)).astype(BF16)
    b1 = b.astype(BF16)
    b2 = (b - b1.astype(F32)).astype(BF16)
    return _dot_nt(a1, b1) + (_dot_nt(a1, b2) + _dot_nt(a2, b1))


def _ada_kernel(c_ref, w_ref, b_ref, o_ref):
    cond = _silu(c_ref[...])
    o_ref[...] = jnp.dot(cond.astype(BF16), w_ref[...].astype(BF16),
                         preferred_element_type=F32) + b_ref[...]


def _ada(c, w_ada, b_ada):
    bsz, d = c.shape
    n = w_ada.shape[1]
    tn = 1024
    return pl.pallas_call(
        _ada_kernel,
        grid=(n // tn,),
        in_specs=[pl.BlockSpec((bsz, d), lambda j: (0, 0)),
                  pl.BlockSpec((d, tn), lambda j: (0, j)),
                  pl.BlockSpec((1, tn), lambda j: (0, j))],
        out_specs=pl.BlockSpec((bsz, tn), lambda j: (0, j)),
        out_shape=jax.ShapeDtypeStruct((bsz, n), F32),
        compiler_params=_cparams(("arbitrary",)),
        name="ada_mod",
    )(c, w_ada, b_ada.reshape(1, n))


def _prenorm_kernel(x_ref, g_ref, sc_ref, sh_ref, o_ref):
    y = _rms(x_ref[0]) * g_ref[...]
    o_ref[0] = (y * (1.0 + sc_ref[0]) + sh_ref[0]).astype(o_ref.dtype)


def _prenorm(x, g, sc, sh):
    bsz, s, d = x.shape
    ts = 512
    vec = pl.BlockSpec((1, 1, d), lambda b, i: (b, 0, 0))
    return pl.pallas_call(
        _prenorm_kernel,
        grid=(bsz, s // ts),
        in_specs=[pl.BlockSpec((1, ts, d), lambda b, i: (b, i, 0)),
                  pl.BlockSpec((1, d), lambda b, i: (0, 0)), vec, vec],
        out_specs=pl.BlockSpec((1, ts, d), lambda b, i: (b, i, 0)),
        out_shape=jax.ShapeDtypeStruct((bsz, s, d), BF16),
        compiler_params=_cparams(("arbitrary", "arbitrary")),
        name="prenorm",
    )(x, g.reshape(1, d), sc.reshape(bsz, 1, d), sh.reshape(bsz, 1, d))


def _mm_wt_kernel(shift, a_ref, *refs):
    w_ref = refs[0]
    o_ref, wb = refs[-2:]

    @pl.when(pl.program_id(1) == 0)
    def _():
        w = w_ref[0]
        if shift:
            w = jnp.concatenate([w[shift:], refs[1][0]], axis=0)
        wb[...] = w.T.astype(BF16)

    o_ref[...] = jnp.dot(a_ref[...], wb[...], preferred_element_type=F32).astype(o_ref.dtype)


def _dt_kernel(a_ref, w_ref, o_ref):
    o_ref[...] = _dot_nt(w_ref[0].astype(BF16), a_ref[...])


def _dt_proj(a, wt_stack, layer, row0, nh, tm):
    m, k = a.shape
    assert row0 % nh == 0
    return pl.pallas_call(
        _dt_kernel,
        grid=(m // tm,),
        in_specs=[pl.BlockSpec((tm, k), lambda i: (i, 0)),
                  pl.BlockSpec((1, nh, k), lambda i: (layer, row0 // nh, 0))],
        out_specs=pl.BlockSpec((nh, tm), lambda i: (0, i)),
        out_shape=jax.ShapeDtypeStruct((nh, m), F32),
        compiler_params=_cparams(("arbitrary",)),
        name="in_proj_dt",
    )(a, wt_stack)


def _matmul_wt(a, wt_stack, layer, row0, n, tm, tn, out_dtype, name):
    m, k = a.shape
    shift = row0 % tn
    base = row0 - shift
    assert n % tn == 0 and shift % SUBLANES == 0
    a_spec = pl.BlockSpec((tm, k), lambda j, i: (i, 0))
    in_specs = [a_spec, pl.BlockSpec((1, tn, k), lambda j, i: (layer, base // tn + j, 0))]
    args = [a, wt_stack]
    if shift:
        assert (base + tn) % shift == 0 and tn % shift == 0
        in_specs.append(pl.BlockSpec((1, shift, k), lambda j, i: (layer, (base + (j + 1) * tn) // shift, 0)))
        args.append(wt_stack)
    return pl.pallas_call(
        functools.partial(_mm_wt_kernel, shift),
        grid=(n // tn, m // tm),
        in_specs=in_specs,
        out_specs=pl.BlockSpec((tm, tn), lambda j, i: (i, j)),
        out_shape=jax.ShapeDtypeStruct((m, n), out_dtype),
        scratch_shapes=[pltpu.VMEM((k, tn), BF16)],
        compiler_params=_cparams(("arbitrary", "arbitrary")),
        name=name,
    )(*args)


def _softplus(x):
    return jnp.maximum(x, 0.0) + jnp.log(1.0 + jnp.exp(-jnp.abs(x)))


def _ssd_kernel(x_ref, b_ref, c_ref, z_ref, dt_ref, cwx_ref, cwb_ref, cwc_ref,
                cbx_ref, cbb_ref, cbc_ref, dtb_ref, alog_ref, dsk_ref, nw_ref,
                o_ref, cbuf, state):
    q = SSD_CHUNK
    gps = dtb_ref.shape[0]
    gw = x_ref.shape[2] // gps
    nst = b_ref.shape[2] // gps
    hg = gw // SSD_HEAD_DIM
    pad = SUBLANES

    @pl.when(pl.program_id(2) == 0)
    def _():
        cbuf[:, 0:pad, :] = jnp.zeros((gps, pad, gw + 2 * nst), F32)
        state[...] = jnp.zeros_like(state)

    hrow = lax.broadcasted_iota(jnp.int32, (LANES, q), 0)
    ri = lax.broadcasted_iota(jnp.int32, (q, q), 0)
    ci = lax.broadcasted_iota(jnp.int32, (q, q), 1)
    triu = jnp.where(ri <= ci, 1.0, 0.0).astype(BF16)
    er = lax.broadcasted_iota(jnp.int32, (LANES, gw), 0)
    ec = lax.broadcasted_iota(jnp.int32, (LANES, gw), 1)
    e_ch = jnp.where(ec // SSD_HEAD_DIM == er, 1.0, 0.0).astype(BF16)
    er2 = lax.broadcasted_iota(jnp.int32, (LANES, hg * q), 0)
    ec2 = lax.broadcasted_iota(jnp.int32, (LANES, hg * q), 1)
    e_t = jnp.where(ec2 // q == er2, 1.0, 0.0).astype(BF16)
    ri2 = lax.broadcasted_iota(jnp.int32, (q, hg * q), 0)
    ci2 = lax.broadcasted_iota(jnp.int32, (q, hg * q), 1) % q
    first_head = lax.broadcasted_iota(jnp.int32, (1, LANES), 1) < SSD_HEAD_DIM

    for u in range(gps):
        xc = slice(u * gw, (u + 1) * gw)
        nc = slice(u * nst, (u + 1) * nst)
        xin = jnp.concatenate([x_ref[0, :, xc], b_ref[0, :, nc], c_ref[0, :, nc]],
                              axis=-1).astype(F32)
        cbuf[u, pad:pad + q, :] = xin
        w = jnp.concatenate([cwx_ref[:, xc], cwb_ref[:, nc], cwc_ref[:, nc]], axis=-1)
        bias = jnp.concatenate([cbx_ref[:, xc], cbb_ref[:, nc], cbc_ref[:, nc]], axis=-1)
        acc = bias + w[SSD_CONV - 1:SSD_CONV, :] * xin
        for s in range(1, SSD_CONV):
            acc = acc + w[SSD_CONV - 1 - s:SSD_CONV - s, :] * cbuf[u, pad - s:pad - s + q, :]
        cbuf[u, pad - (SSD_CONV - 1):pad, :] = cbuf[u, pad + q - (SSD_CONV - 1):pad + q, :]
        xbc = _silu(acc)
        xs = xbc[:, :gw]
        bm = xbc[:, gw:gw + nst].astype(BF16)
        cm = xbc[:, gw + nst:].astype(BF16)

        dt_raw = jnp.concatenate([dt_ref[u * hg:(u + 1) * hg, :], jnp.zeros((LANES - hg, q), F32)],
                                 axis=0)
        dt_r = jnp.where(hrow < hg, _softplus(dt_raw + dtb_ref[u]), 0.0)
        a_r = dt_r * (-math.log2(math.e) * jnp.exp(alog_ref[u]))
        acum_r = _dot_exact_rhs(a_r, triu)
        acum_c = acum_r.T
        dt_c = dt_r.T

        dt_full = _dot_exact_rhs(dt_c, e_ch)
        acum_colb = _dot_exact_rhs(acum_c, e_t)
        acum_full = jnp.concatenate(
            [jnp.where(first_head, acum_colb[:, (2 * pr) * q:(2 * pr + 1) * q],
                       acum_colb[:, (2 * pr + 1) * q:(2 * pr + 2) * q])
             for pr in range(hg // 2)], axis=-1)
        acum_rowb = jnp.concatenate(
            [jnp.broadcast_to(acum_r[h:h + 1, :], (q, q)) for h in range(hg)], axis=-1)

        lmat = jnp.exp2(jnp.where(ri2 >= ci2, acum_colb - acum_rowb, NEG_BIG))
        cb = _dot_nt(cm, bm)
        mcat = (jnp.concatenate([cb] * hg, axis=-1) * lmat).astype(BF16)
        xdt = xs * dt_full
        y_pairs = []
        for pr in range(gw // LANES):
            xp = xdt[:, pr * LANES:(pr + 1) * LANES]
            xbd = jnp.concatenate([jnp.where(first_head, xp, 0.0).astype(BF16),
                                   jnp.where(first_head, 0.0, xp).astype(BF16)], axis=0)
            y_pairs.append(jnp.dot(mcat[:, pr * 2 * q:(pr + 1) * 2 * q], xbd,
                                   preferred_element_type=F32))
        y = jnp.concatenate(y_pairs, axis=-1)

        st = state[u]
        y = y + jnp.dot(cm, st.astype(BF16), preferred_element_type=F32) * jnp.exp2(acum_full)
        acum_last = acum_full[q - 1:q, :]
        xw = (xdt * jnp.exp2(acum_last - acum_full)).astype(BF16)
        bt = xbc[:, gw:gw + nst].T.astype(BF16)
        state[u] = st * jnp.exp2(acum_last) + jnp.dot(bt, xw, preferred_element_type=F32)

        y = y + xs * dsk_ref[:, xc]
        y = y * _silu(z_ref[0, :, xc].astype(F32))
        o_ref[0, :, xc] = (_rms(y) * nw_ref[:, xc]).astype(o_ref.dtype)


def _ssd(proj, dt_rows, conv_w, conv_b, dt_bias, a_log, d_skip, norm_w, d_inner):
    bsz, s, _ = proj.shape
    g = SSD_GROUPS
    gps = SSD_GROUPS_PER_STEP
    assert SSD_CHUNK == LANES and 2 * SSD_HEAD_DIM == LANES
    gw = d_inner // g
    hg = gw // SSD_HEAD_DIM
    nst = SSD_STATE
    q = SSD_CHUNK
    nc = s // q
    bw, bn = gps * gw, gps * nst
    x0 = d_inner // bw
    b0 = (2 * d_inner) // bn
    c0 = (2 * d_inner + g * nst) // bn
    cwb0 = d_inner // bn
    cwc0 = (d_inner + g * nst) // bn
    conv_b2 = conv_b.reshape(1, -1)
    head_pad = lambda v: jnp.pad(v.reshape(g, hg), ((0, 0), (0, LANES - hg))).reshape(g, LANES, 1)
    return pl.pallas_call(
        _ssd_kernel,
        grid=(bsz, g // gps, nc),
        in_specs=[
            pl.BlockSpec((1, q, bw), lambda b, gi, c: (b, c, x0 + gi)),
            pl.BlockSpec((1, q, bn), lambda b, gi, c: (b, c, b0 + gi)),
            pl.BlockSpec((1, q, bn), lambda b, gi, c: (b, c, c0 + gi)),
            pl.BlockSpec((1, q, bw), lambda b, gi, c: (b, c, gi)),
            pl.BlockSpec((gps * hg, q), lambda b, gi, c: (gi, b * nc + c)),
            pl.BlockSpec((SSD_CONV, bw), lambda b, gi, c: (0, gi)),
            pl.BlockSpec((SSD_CONV, bn), lambda b, gi, c: (0, cwb0 + gi)),
            pl.BlockSpec((SSD_CONV, bn), lambda b, gi, c: (0, cwc0 + gi)),
            pl.BlockSpec((1, bw), lambda b, gi, c: (0, gi)),
            pl.BlockSpec((1, bn), lambda b, gi, c: (0, cwb0 + gi)),
            pl.BlockSpec((1, bn), lambda b, gi, c: (0, cwc0 + gi)),
            pl.BlockSpec((gps, LANES, 1), lambda b, gi, c: (gi, 0, 0)),
            pl.BlockSpec((gps, LANES, 1), lambda b, gi, c: (gi, 0, 0)),
            pl.BlockSpec((1, bw), lambda b, gi, c: (0, gi)),
            pl.BlockSpec((1, bw), lambda b, gi, c: (0, gi)),
        ],
        out_specs=pl.BlockSpec((1, q, bw), lambda b, gi, c: (b, c, gi)),
        out_shape=jax.ShapeDtypeStruct((bsz, s, d_inner), BF16),
        scratch_shapes=[pltpu.VMEM((gps, SUBLANES + q, gw + 2 * nst), F32),
                        pltpu.VMEM((gps, nst, gw), F32)],
        compiler_params=_cparams(("arbitrary", "arbitrary", "arbitrary")),
        name="ssd_scan",
    )(proj, proj, proj, proj, dt_rows, conv_w, conv_w, conv_w, conv_b2, conv_b2, conv_b2,
      head_pad(dt_bias), head_pad(a_log),
      jnp.repeat(d_skip, SSD_HEAD_DIM).reshape(1, d_inner), norm_w.reshape(1, d_inner))


def _rope_kernel(pos_ref, inv_ref, cos_ref, sin_ref):
    ang = pos_ref[...].astype(F32) * inv_ref[...]
    lane = lax.broadcasted_iota(jnp.int32, ang.shape, 1)
    cos_ref[...] = jnp.cos(ang)
    sin_ref[...] = jnp.where(lane < ATT_HEAD_DIM // 2, -1.0, 1.0) * jnp.sin(ang)


def _rope_tables(positions):
    t = positions.size
    half = ATT_HEAD_DIM // 2
    inv = 1.0 / (ROPE_THETA ** (jnp.arange(half, dtype=F32) / half))
    inv2 = jnp.concatenate([inv, inv]).reshape(1, ATT_HEAD_DIM)
    tt = min(2048, t)
    return pl.pallas_call(
        _rope_kernel,
        grid=(t // tt,),
        in_specs=[pl.BlockSpec((tt, 1), lambda i: (i, 0)),
                  pl.BlockSpec((1, ATT_HEAD_DIM), lambda i: (0, 0))],
        out_specs=[pl.BlockSpec((tt, ATT_HEAD_DIM), lambda i: (i, 0))] * 2,
        out_shape=[jax.ShapeDtypeStruct((t, ATT_HEAD_DIM), F32)] * 2,
        compiler_params=_cparams(("arbitrary",)),
        name="rope_tables",
    )(positions.reshape(t, 1), inv2)


def _rot(x, cos2, sin2):
    return x * cos2 + pltpu.roll(x, ATT_HEAD_DIM // 2, axis=1) * sin2


def _attn_kernel(q_ref, k_ref, v_ref, cos_ref, sin_ref, o_ref, krot, vt, kmean):
    blk = MOBA_BLOCK
    nb = k_ref.shape[1] // blk
    log2e_scale = ATT_HEAD_DIM ** -0.5 * math.log2(math.e)

    kmean[...] = jnp.zeros_like(kmean)
    for j in range(nb):
        rows = slice(j * blk, (j + 1) * blk)
        kr = _rot(k_ref[0, rows, :].astype(F32), cos_ref[0, rows, :], sin_ref[0, rows, :])
        krot[rows, :] = kr.astype(BF16)
        kmean[j:j + 1, :] = jnp.mean(kr, axis=0, keepdims=True)
        vt[0:ATT_HEAD_DIM, rows] = v_ref[0, rows, :].astype(F32).T.astype(BF16)
    vt[ATT_HEAD_DIM:, :] = jnp.ones((vt.shape[0] - ATT_HEAD_DIM, vt.shape[1]), BF16)
    km = kmean[...]

    ki = lax.broadcasted_iota(jnp.int32, (blk, blk), 0)
    qj = lax.broadcasted_iota(jnp.int32, (blk, blk), 1)
    sub = lax.broadcasted_iota(jnp.int32, (km.shape[0], blk), 0)
    for qi in range(nb):
        qrows = slice(qi * blk, (qi + 1) * blk)
        qr = _rot(q_ref[0, qrows, :].astype(F32), cos_ref[0, qrows, :], sin_ref[0, qrows, :])
        qs = (qr * log2e_scale).T.astype(BF16)
        gate = _dot_nt_x3(km, qr)
        scores = [jnp.where(ki <= qj, jnp.dot(krot[qrows, :], qs, preferred_element_type=F32), NEG_BIG)]
        for j in range(qi):
            gj = gate[j:j + 1, :]
            ahead = (gate > gj) | ((gate == gj) & (sub < j))
            rank = jnp.sum(jnp.where((sub < qi) & ahead, 1.0, 0.0), axis=0, keepdims=True)
            bias = jnp.where(rank < MOBA_TOPK, 0.0, NEG_BIG)
            scores.append(jnp.dot(krot[j * blk:(j + 1) * blk, :], qs,
                                  preferred_element_type=F32) + bias)
        m = scores[0]
        for sj in scores[1:]:
            m = jnp.maximum(m, sj)
        m = jnp.max(m, axis=0, keepdims=True)
        acc = None
        for j, sj in enumerate(scores):
            src = qi if j == 0 else j - 1
            p = jnp.exp2(sj - m)
            aj = jnp.dot(vt[:, src * blk:(src + 1) * blk], p.astype(BF16),
                         preferred_element_type=F32)
            acc = aj if acc is None else acc + aj
        l = acc[ATT_HEAD_DIM:ATT_HEAD_DIM + 1, :]
        o_ref[0, qrows, :] = (acc[:ATT_HEAD_DIM, :] / l).T.astype(o_ref.dtype)


def _attention(proj, cos2, sin2, q0, k0, v0, n_heads):
    bsz, s, _ = proj.shape
    dh = ATT_HEAD_DIM
    col = lambda c0: pl.BlockSpec((1, s, dh), lambda b, h: (b, 0, c0 + h))
    tab = pl.BlockSpec((1, s, dh), lambda b, h: (b, 0, 0))
    return pl.pallas_call(
        _attn_kernel,
        grid=(bsz, n_heads),
        in_specs=[col(q0), col(k0), col(v0), tab, tab],
        out_specs=pl.BlockSpec((1, s, dh), lambda b, h: (b, 0, h)),
        out_shape=jax.ShapeDtypeStruct((bsz, s, n_heads * dh), BF16),
        scratch_shapes=[pltpu.VMEM((s, dh), BF16), pltpu.VMEM((dh + 2 * SUBLANES, s), BF16),
                        pltpu.VMEM((-(-(s // MOBA_BLOCK) // SUBLANES) * SUBLANES, dh), F32)],
        compiler_params=_cparams(("arbitrary", "arbitrary")),
        name="moba_attn",
    )(proj, proj, proj, cos2.reshape(bsz, s, dh), sin2.reshape(bsz, s, dh))


def _merge_kernel(ys_ref, ya_ref, w1_ref, w2_ref, gs_ref, ga_ref, o_ref):
    b1 = jnp.dot(ys_ref[...], w1_ref[...], preferred_element_type=F32)
    b2 = jnp.dot(ya_ref[...], w2_ref[...], preferred_element_type=F32)
    o_ref[...] = (jax.nn.sigmoid(gs_ref[...].astype(F32)) * b1
                  + jax.nn.sigmoid(ga_ref[...].astype(F32)) * b2).astype(o_ref.dtype)


def _merge(y_ssd, y_att, w1, w2, proj2d, gs0, ga0):
    m, k1 = y_ssd.shape
    k2 = y_att.shape[1]
    n = w1.shape[1]
    tm, tn = MERGE_TM, MERGE_TN
    return pl.pallas_call(
        _merge_kernel,
        grid=(n // tn, m // tm),
        in_specs=[pl.BlockSpec((tm, k1), lambda j, i: (i, 0)),
                  pl.BlockSpec((tm, k2), lambda j, i: (i, 0)),
                  pl.BlockSpec((k1, tn), lambda j, i: (0, j)),
                  pl.BlockSpec((k2, tn), lambda j, i: (0, j)),
                  pl.BlockSpec((tm, tn), lambda j, i: (i, gs0 // tn + j)),
                  pl.BlockSpec((tm, tn), lambda j, i: (i, ga0 // tn + j))],
        out_specs=pl.BlockSpec((tm, tn), lambda j, i: (i, j)),
        out_shape=jax.ShapeDtypeStruct((m, n), BF16),
        compiler_params=_cparams(("arbitrary", "arbitrary")),
        name="branch_merge",
    )(y_ssd, y_att, w1, w2, proj2d, proj2d)


def _outproj_kernel(m_ref, w_ref, x_ref, gpost_ref, gt_ref, gpre_ref, sc_ref, sh_ref, wr_ref,
                    x1_ref, h2b_ref, lg_ref):
    for r in range(m_ref.shape[1] // OUTPROJ_SUBTILE):
        rows = slice(r * OUTPROJ_SUBTILE, (r + 1) * OUTPROJ_SUBTILE)
        y = jnp.dot(m_ref[0, rows, :], w_ref[...], preferred_element_type=F32)
        x1 = x_ref[0, rows, :] + gt_ref[0] * (_rms(y) * gpost_ref[...])
        h2 = (_rms(x1) * gpre_ref[...]) * (1.0 + sc_ref[0]) + sh_ref[0]
        x1_ref[0, rows, :] = x1
        h2b_ref[0, rows, :] = h2.astype(BF16)
        h_hi = h2.astype(BF16)
        h_lo = (h2 - h_hi.astype(F32)).astype(BF16)
        wr = wr_ref[...]
        w_hi = wr.astype(BF16)
        w_lo = (wr - w_hi.astype(F32)).astype(BF16)
        lg = (jnp.dot(h_hi, w_hi, preferred_element_type=F32)
              + (jnp.dot(h_hi, w_lo, preferred_element_type=F32)
                 + jnp.dot(h_lo, w_hi, preferred_element_type=F32)))
        lg_ref[:, rows] = lg.T[:lg_ref.shape[0], :]


def _outproj(merged, w_out, x, g_post, gt1, g_pre, sc2, sh2, w_router):
    bsz, s, d = x.shape
    ne = w_router.shape[1]
    w_router_p = jnp.pad(w_router, ((0, 0), (0, LANES - ne)))
    tm = OUTPROJ_TILE
    nt = s // tm
    row = pl.BlockSpec((1, tm, d), lambda i: (i // nt, i % nt, 0))
    vec = pl.BlockSpec((1, 1, d), lambda i: (i // nt, 0, 0))
    par = pl.BlockSpec((1, d), lambda i: (0, 0))
    return pl.pallas_call(
        _outproj_kernel,
        grid=(bsz * nt,),
        in_specs=[row, pl.BlockSpec((d, d), lambda i: (0, 0)), row, par, vec, par, vec, vec,
                  pl.BlockSpec((d, LANES), lambda i: (0, 0))],
        out_specs=[row, row, pl.BlockSpec((ne, tm), lambda i: (0, i))],
        out_shape=[jax.ShapeDtypeStruct((bsz, s, d), F32),
                   jax.ShapeDtypeStruct((bsz, s, d), BF16),
                   jax.ShapeDtypeStruct((ne, bsz * s), F32)],
        compiler_params=_cparams(("arbitrary",)),
        name="out_proj_norms",
    )(merged.reshape(bsz, s, d), w_out, x, g_post.reshape(1, d), gt1.reshape(bsz, 1, d),
      g_pre.reshape(1, d), sc2.reshape(bsz, 1, d), sh2.reshape(bsz, 1, d), w_router_p)


def _router_kernel(lg_ref, rb_ref, w_ref, rank_ref, cnt_ref, gsc):
    ne, tn = lg_ref.shape
    ng = N_EXPERT_GROUPS
    eg = ne // ng
    scores = jax.nn.sigmoid(lg_ref[...])
    biased = scores + rb_ref[...]
    i8 = lax.broadcasted_iota(jnp.int32, (eg, tn), 0)
    for g in range(ng):
        v = biased[g * eg:(g + 1) * eg, :]
        m1 = jnp.max(v, axis=0, keepdims=True)
        first = jnp.min(jnp.where(v == m1, i8, eg), axis=0, keepdims=True)
        m2 = jnp.max(jnp.where(i8 == first, -jnp.inf, v), axis=0, keepdims=True)
        gsc[g:g + 1, :] = m1 + m2
    gs = gsc[...]
    gi = lax.broadcasted_iota(jnp.int32, (ng, tn), 0)
    masked = []
    for g in range(ng):
        sg = gs[g:g + 1, :]
        ahead = (gs > sg) | ((gs == sg) & (gi < g))
        rank = jnp.sum(jnp.where(ahead, 1.0, 0.0), axis=0, keepdims=True)
        masked.append(jnp.where(rank < TOPK_GROUPS, biased[g * eg:(g + 1) * eg, :], -jnp.inf))
    cur = jnp.concatenate(masked, axis=0)
    sub = lax.broadcasted_iota(jnp.int32, (ne, tn), 0)
    sel = jnp.zeros((ne, tn), F32)
    for k in range(TOP_K):
        mx = jnp.max(cur, axis=0, keepdims=True)
        idx = jnp.min(jnp.where(cur == mx, sub, ne), axis=0, keepdims=True)
        hit = sub == idx
        sel = jnp.where(hit, 1.0, sel)
        cur = jnp.where(hit, -jnp.inf, cur)
    picked = sel * scores
    wsum = jnp.sum(picked, axis=0, keepdims=True)
    w_ref[...] = picked / wsum * ROUTED_SCALE
    ti = lax.broadcasted_iota(jnp.int32, (tn, tn), 0)
    tj = lax.broadcasted_iota(jnp.int32, (tn, tn), 1)
    before = jnp.where(ti < tj, 1.0, 0.0).astype(BF16)
    rank = jnp.dot(sel.astype(BF16), before, preferred_element_type=F32)
    rank_ref[...] = jnp.where(sel > 0.0, rank, NEG_BIG)
    cnt_ref[0] = jnp.broadcast_to(jnp.sum(sel, axis=1, keepdims=True), (ne, LANES))


def _router(logits_t, router_bias):
    ne, t = logits_t.shape
    tn = MOE_TILE
    tile = pl.BlockSpec((ne, tn), lambda i: (0, i))
    return pl.pallas_call(
        _router_kernel,
        grid=(t // tn,),
        in_specs=[tile, pl.BlockSpec((ne, 1), lambda i: (0, 0))],
        out_specs=[tile, tile, pl.BlockSpec((1, ne, LANES), lambda i: (i, 0, 0))],
        out_shape=[jax.ShapeDtypeStruct((ne, t), F32), jax.ShapeDtypeStruct((ne, t), F32),
                   jax.ShapeDtypeStruct((t // tn, ne, LANES), F32)],
        scratch_shapes=[pltpu.VMEM((N_EXPERT_GROUPS, tn), F32)],
        compiler_params=_cparams(("arbitrary",)),
        name="router_topk",
    )(logits_t, router_bias.reshape(ne, 1))


def _tile_rows(n_experts):
    return MOE_TILE * TOP_K + n_experts * SEG_ALIGN


def _segment_copies(src, dst, sem, n, src_off, dst_off, wait):
    off = jnp.int32(0)
    for p in SEG_CHUNKS:
        bit = n & p

        @pl.when(bit != 0)
        def _(p=p, off=off):
            cp = pltpu.make_async_copy(
                src.at[pl.ds(pl.multiple_of(src_off + off, SEG_ALIGN), p), :],
                dst.at[pl.ds(pl.multiple_of(dst_off + off, SEG_ALIGN), p), :], sem)
            if wait:
                cp.wait()
            else:
                cp.start()
        off = off + bit


def _for_each_segment(ne, fn):
    def body(e, carry):
        fn(e)
        return carry
    lax.fori_loop(0, ne, body, 0)


def _build_perm(pbuf, posb, ss_ref, c16_ref, tile, ne, value_row):
    tn = posb.shape[1]
    pbuf[...] = jnp.zeros_like(pbuf)
    win = lax.broadcasted_iota(jnp.int32, (PERM_WINDOW, tn), 0)

    def per_expert(e):
        base = ss_ref[tile, e]
        windows = lax.shift_right_logical(c16_ref[tile, e] + (PERM_WINDOW - 1),
                                          PERM_WINDOW.bit_length() - 1)
        prow = posb[pl.ds(e, 1), :]
        vrow = value_row(e)

        def per_window(g, carry):
            r0 = pl.multiple_of(base + g * PERM_WINDOW, SEG_ALIGN)
            hit = prow == (win + r0).astype(F32)
            pbuf[pl.ds(r0, PERM_WINDOW), :] = jnp.where(hit, vrow, 0.0).astype(BF16)
            return carry
        lax.fori_loop(0, windows, per_window, 0)
    _for_each_segment(ne, per_expert)


def _dispatch_kernel(ss_ref, c16_ref, go_ref, ts_ref, tl_ref, rank_ref, ssv_ref, h_ref, xs_hbm,
                     pbuf, xs, posb, sem):
    i = pl.program_id(0)
    n = pl.num_programs(0)
    ne = rank_ref.shape[0]
    tn = rank_ref.shape[1]
    posb[...] = rank_ref[...] + ssv_ref[0]
    _build_perm(pbuf, posb, ss_ref, c16_ref, i, ne, lambda e: 1.0)

    def drain(tile):
        _for_each_segment(ne, lambda e: _segment_copies(
            xs, xs_hbm, sem, c16_ref[tile, e], ss_ref[tile, e], go_ref[tile, e], True))

    @pl.when(i > 0)
    def _():
        drain(i - 1)

    used = ss_ref[i, ne - 1] + c16_ref[i, ne - 1]
    def chunk(ch):
        rows = slice(ch * tn, (ch + 1) * tn)
        xs[rows, :] = jnp.dot(pbuf[rows, :], h_ref[...], preferred_element_type=F32).astype(BF16)

    for ch in range(TOP_K):
        chunk(ch)
    for ch in range(TOP_K, xs.shape[0] // tn):
        pl.when(ch * tn < used)(functools.partial(chunk, ch))

    _for_each_segment(ne, lambda e: _segment_copies(
        xs, xs_hbm, sem, c16_ref[i, e], ss_ref[i, e], go_ref[i, e], False))

    @pl.when(i == n - 1)
    def _():
        drain(i)
        xs[0:MOE_ROWS, :] = jnp.zeros((MOE_ROWS, xs.shape[1]), BF16)
        first_free = lax.shift_right_logical(ts_ref[ne], MOE_ROWS.bit_length() - 1)
        n_blocks = xs_hbm.shape[0] // MOE_ROWS
        for wait in (False, True):
            _for_each_segment(ne, lambda e: _segment_copies(
                xs, xs_hbm, sem, tl_ref[e], 0, ts_ref[e], wait))

            def free_block(b, carry, wait=wait):
                cp = pltpu.make_async_copy(
                    xs.at[pl.ds(0, MOE_ROWS), :],
                    xs_hbm.at[pl.ds(pl.multiple_of(b * MOE_ROWS, MOE_ROWS), MOE_ROWS), :], sem)
                if wait:
                    cp.wait()
                else:
                    cp.start()
                return carry
            lax.fori_loop(first_free, n_blocks, free_block, 0)


def _dispatch(seg_start, c16, goff, tail_start, tail_len, rank, h2b, n_rows):
    ne, t = rank.shape
    d = h2b.shape[1]
    tn = MOE_TILE
    rt = _tile_rows(ne)
    grid_spec = pltpu.PrefetchScalarGridSpec(
        num_scalar_prefetch=5,
        grid=(t // tn,),
        in_specs=[pl.BlockSpec((ne, tn), lambda i, *_: (0, i)),
                  pl.BlockSpec((1, ne, 1), lambda i, *_: (i, 0, 0)),
                  pl.BlockSpec((tn, d), lambda i, *_: (i, 0))],
        out_specs=pl.BlockSpec(memory_space=pl.ANY),
        scratch_shapes=[pltpu.VMEM((rt + PERM_WINDOW, tn), BF16), pltpu.VMEM((rt, d), BF16),
                        pltpu.VMEM((ne, tn), F32), pltpu.SemaphoreType.DMA],
    )
    return pl.pallas_call(
        _dispatch_kernel,
        grid_spec=grid_spec,
        out_shape=jax.ShapeDtypeStruct((n_rows, d), BF16),
        compiler_params=_cparams(("arbitrary",)),
        name="moe_dispatch",
    )(seg_start, c16, goff, tail_start, tail_len, rank,
      seg_start.astype(F32).reshape(t // tn, ne, 1), h2b)


def _combine_kernel(ss_ref, c16_ref, go_ref, rank_ref, w_ref, ssv_ref, y_hbm, o_ref,
                    pbuf, ys, posb, sem):
    i = pl.program_id(0)
    ne = rank_ref.shape[0]
    tn = rank_ref.shape[1]
    ys[...] = jnp.zeros_like(ys)
    _for_each_segment(ne, lambda e: _segment_copies(
        y_hbm, ys, sem, c16_ref[i, e], go_ref[i, e], ss_ref[i, e], False))
    posb[...] = rank_ref[...] + ssv_ref[0]
    _build_perm(pbuf, posb, ss_ref, c16_ref, i, ne, lambda e: w_ref[pl.ds(e, 1), :])
    _for_each_segment(ne, lambda e: _segment_copies(
        y_hbm, ys, sem, c16_ref[i, e], go_ref[i, e], ss_ref[i, e], True))
    used = ss_ref[i, ne - 1] + c16_ref[i, ne - 1]
    def chunk(ch):
        rows = slice(ch * tn, (ch + 1) * tn)
        return lax.dot_general(pbuf[rows, :], ys[rows, :], (((0,), (0,)), ((), ())),
                               preferred_element_type=F32)

    always = TOP_K
    acc = chunk(0)
    for ch in range(1, always):
        acc = acc + chunk(ch)
    o_ref[...] = acc
    for ch in range(always, ys.shape[0] // tn):
        @pl.when(ch * tn < used)
        def _(ch=ch):
            o_ref[...] += chunk(ch)


def _combine(seg_start, c16, goff, rank, selw, y_rows):
    ne, t = rank.shape
    d = y_rows.shape[1]
    tn = MOE_TILE
    rt = _tile_rows(ne)
    tile = pl.BlockSpec((ne, tn), lambda i, *_: (0, i))
    grid_spec = pltpu.PrefetchScalarGridSpec(
        num_scalar_prefetch=3,
        grid=(t // tn,),
        in_specs=[tile, tile, pl.BlockSpec((1, ne, 1), lambda i, *_: (i, 0, 0)),
                  pl.BlockSpec(memory_space=pl.ANY)],
        out_specs=pl.BlockSpec((tn, d), lambda i, *_: (i, 0)),
        scratch_shapes=[pltpu.VMEM((rt + PERM_WINDOW, tn), BF16), pltpu.VMEM((rt, d), BF16),
                        pltpu.VMEM((ne, tn), F32), pltpu.SemaphoreType.DMA],
    )
    return pl.pallas_call(
        _combine_kernel,
        grid_spec=grid_spec,
        out_shape=jax.ShapeDtypeStruct((t, d), F32),
        compiler_params=_cparams(("arbitrary",)),
        name="moe_combine",
    )(seg_start, c16, goff, rank, selw, seg_start.astype(F32).reshape(t // tn, ne, 1), y_rows)


EXPERT_LOOKAHEAD = 2


def _expert_kernel(be_ref, nv_ref, sl_ref, x_ref, wg_ref, wu_ref, wd_ref, o_ref, wg, wu, wd):
    s = pl.program_id(0)
    n = pl.num_programs(0) - EXPERT_LOOKAHEAD
    clampi = lambda j: jnp.clip(j, 0, n - 1)

    def starts_expert(j):
        jc = clampi(j)
        first = (jc == 0) | (be_ref[jc] != be_ref[clampi(jc - 1)])
        return (j >= 0) & (j < n) & (nv_ref[jc] > 0) & first

    i = s - EXPERT_LOOKAHEAD
    ic = clampi(i)
    slot = sl_ref[ic]

    @pl.when(starts_expert(i))
    def _():
        wd[slot] = wd_ref[0].astype(BF16)

    @pl.when((i >= 0) & (nv_ref[ic] == 0))
    def _():
        o_ref[...] = jnp.zeros_like(o_ref)

    @pl.when((i >= 0) & (nv_ref[ic] > 0))
    def _():
        x = x_ref[...]
        gate = jnp.dot(x, wg[slot], preferred_element_type=F32)
        up = jnp.dot(x, wu[slot], preferred_element_type=F32)
        act = (_silu(gate) * up).astype(BF16)
        o_ref[...] = jnp.dot(act, wd[slot], preferred_element_type=F32).astype(o_ref.dtype)

    @pl.when(starts_expert(s))
    def _():
        wg[sl_ref[clampi(s)]] = wg_ref[0].astype(BF16)

    @pl.when(starts_expert(s - 1))
    def _():
        wu[sl_ref[clampi(s - 1)]] = wu_ref[0].astype(BF16)


def _expert_ffn(blk_exp, blk_valid, blk_slot, x_sorted, w_gate, w_up, w_down):
    n_rows, d = x_sorted.shape
    f = w_gate.shape[2]
    rows = MOE_ROWS
    n = n_rows // rows
    blk = lambda back: (lambda s, be, nv, sl: (jnp.clip(s - back, 0, n - 1), 0))
    wgt = lambda back: (lambda s, be, nv, sl: (be[jnp.clip(s - back, 0, n - 1)], 0, 0))
    grid_spec = pltpu.PrefetchScalarGridSpec(
        num_scalar_prefetch=3,
        grid=(n + EXPERT_LOOKAHEAD,),
        in_specs=[pl.BlockSpec((rows, d), blk(2)),
                  pl.BlockSpec((1, d, f), wgt(0)),
                  pl.BlockSpec((1, d, f), wgt(1)),
                  pl.BlockSpec((1, f, d), wgt(2))],
        out_specs=pl.BlockSpec((rows, d), blk(2)),
        scratch_shapes=[pltpu.VMEM((2, d, f), BF16), pltpu.VMEM((2, d, f), BF16),
                        pltpu.VMEM((2, f, d), BF16)],
    )
    return pl.pallas_call(
        _expert_kernel,
        grid_spec=grid_spec,
        out_shape=jax.ShapeDtypeStruct((n_rows, d), BF16),
        compiler_params=_cparams(("arbitrary",)),
        name="expert_ffn",
    )(blk_exp, blk_valid, blk_slot, x_sorted, w_gate, w_up, w_down)


def _shared_kernel(x_ref, wg_ref, wu_ref, wd_ref, r_ref, x1_ref, gt_ref, g_ref, o_ref):
    x = x_ref[0]
    gate = jnp.dot(x, wg_ref[...], preferred_element_type=F32)
    up = jnp.dot(x, wu_ref[...], preferred_element_type=F32)
    act = (_silu(gate) * up).astype(BF16)
    y = r_ref[0] + jnp.dot(act, wd_ref[...], preferred_element_type=F32)
    o_ref[0] = x1_ref[0] + gt_ref[0] * (_rms(y) * g_ref[...])


def _shared_final(h2b, wg, wu, wd, routed, x1, gt2, g_post):
    bsz, s, d = x1.shape
    f = wg.shape[1]
    tm = 512
    nt = s // tm
    row = pl.BlockSpec((1, tm, d), lambda i: (i // nt, i % nt, 0))
    full = lambda shape: pl.BlockSpec(shape, lambda i: (0, 0))
    return pl.pallas_call(
        _shared_kernel,
        grid=(bsz * nt,),
        in_specs=[row, full((d, f)), full((d, f)), full((f, d)), row, row,
                  pl.BlockSpec((1, 1, d), lambda i: (i // nt, 0, 0)), full((1, d))],
        out_specs=row,
        out_shape=jax.ShapeDtypeStruct((bsz, s, d), F32),
        compiler_params=_cparams(("arbitrary",)),
        name="shared_ffn_final",
    )(h2b, wg, wu, wd, routed.reshape(bsz, s, d), x1, gt2.reshape(bsz, 1, d), g_post.reshape(1, d))


def _dispatch_plan(counts, n_blocks):
    n_tiles, ne = counts.shape
    c16 = (counts + SEG_ALIGN - 1) // SEG_ALIGN * SEG_ALIGN
    seg_start = jnp.cumsum(c16, axis=1) - c16
    tot = jnp.sum(c16, axis=0)
    padded = (tot + MOE_ROWS - 1) // MOE_ROWS * MOE_ROWS
    pad_end = jnp.cumsum(padded)
    pad_start = pad_end - padded
    goff = pad_start[None, :] + jnp.cumsum(c16, axis=0) - c16
    blk_row = jnp.arange(n_blocks, dtype=jnp.int32) * MOE_ROWS
    blk_exp = jnp.sum(pad_end[None, :] <= blk_row[:, None], axis=1)
    blk_exp = jnp.minimum(blk_exp, ne - 1).astype(jnp.int32)
    owns = (pad_start[None, :] <= blk_row[:, None]) & (blk_row[:, None] < pad_end[None, :])
    rows_left = jnp.clip((pad_start + tot)[None, :] - blk_row[:, None], 0, MOE_ROWS)
    blk_valid = jnp.sum(jnp.where(owns, rows_left, 0), axis=1).astype(jnp.int32)
    blk_slot = jnp.sum((padded > 0)[None, :] & (pad_start[None, :] <= blk_row[:, None]), axis=1) % 2
    i32 = lambda v: v.astype(jnp.int32)
    tail_start = jnp.concatenate([pad_start + tot, pad_end[-1:]])
    return (i32(seg_start), i32(c16), i32(goff), i32(tail_start), i32(padded - tot),
            blk_exp, blk_valid, i32(blk_slot))


def kernel(x, c, positions, w_ada, b_ada, g_pre_mix, g_post_mix, g_pre_ffn, g_post_ffn, w_in,
           conv_w, conv_b, dt_bias, a_log, d_skip, ssd_norm_w, w_ssd_br, w_attn_br, w_out,
           w_router, router_bias, w_gate, w_up, w_down, ws_gate, ws_up, ws_down):
    bsz, s, d = x.shape
    t = bsz * s
    for l in range(w_ada.shape[0]):
        d_inner = w_ssd_br.shape[1]
        att_w = w_attn_br.shape[1]
        n_ssd_heads = dt_bias.shape[1]
        conv_dim = conv_w.shape[2]
        n_heads = att_w // ATT_HEAD_DIM
        n_experts = w_router.shape[2]

        mod = _ada(c, w_ada[l], b_ada[l])
        sh1, sc1, gt1, sh2, sc2, gt2 = jnp.split(mod, 6, axis=-1)

        o_dt = d_inner + conv_dim
        o_q = o_dt + n_ssd_heads
        w_in_t = jnp.swapaxes(w_in, 1, 2)
        h1 = _prenorm(x, g_pre_mix[l], sc1, sh1).reshape(t, d)
        proj_s = _matmul_wt(h1, w_in_t, l, 0, o_dt, INPROJ_TM, INPROJ_TN, BF16, "in_proj_ssd")
        proj_a = _matmul_wt(h1, w_in_t, l, o_q, w_in.shape[2] - o_q, INPROJ_TM, INPROJ_TN, BF16,
                            "in_proj_att")
        dt_t = _dt_proj(h1, w_in_t, l, o_dt, n_ssd_heads, INPROJ_TM)
        k_off = att_w
        v_off = k_off + att_w
        gs_off = v_off + att_w
        ga_off = gs_off + d

        y_ssd = _ssd(proj_s.reshape(bsz, s, o_dt), dt_t, conv_w[l], conv_b[l], dt_bias[l], a_log[l],
                     d_skip[l], ssd_norm_w[l], d_inner)

        cos2, sin2 = _rope_tables(positions)
        y_att = _attention(proj_a.reshape(bsz, s, -1), cos2, sin2, 0, k_off // ATT_HEAD_DIM,
                           v_off // ATT_HEAD_DIM, n_heads)

        merged = _merge(y_ssd.reshape(t, d_inner), y_att.reshape(t, att_w),
                        w_ssd_br[l].astype(BF16), w_attn_br[l].astype(BF16), proj_a, gs_off, ga_off)
        x1, h2b, logits_t = _outproj(merged, w_out[l].astype(BF16), x, g_post_mix[l], gt1,
                                     g_pre_ffn[l], sc2, sh2, w_router[l])

        selw, rank, cnt = _router(logits_t, router_bias[l])
        n_tiles = t // MOE_TILE
        n_blocks = -(-(t * TOP_K + n_tiles * n_experts * (SEG_ALIGN - 1)) // MOE_ROWS) + n_experts
        seg_start, c16, goff, tail_start, tail_len, blk_exp, blk_valid, blk_slot = _dispatch_plan(
            cnt[:, :, 0].astype(jnp.int32), n_blocks)
        x_sorted = _dispatch(seg_start, c16, goff, tail_start, tail_len, rank, h2b.reshape(t, d),
                             n_blocks * MOE_ROWS)
        y_rows = _expert_ffn(blk_exp, blk_valid, blk_slot, x_sorted, w_gate[l], w_up[l], w_down[l])
        routed = _combine(seg_start, c16, goff, rank, selw, y_rows)
        x = _shared_final(h2b, ws_gate[l].astype(BF16), ws_up[l].astype(BF16),
                          ws_down[l].astype(BF16), routed, x1, gt2, g_post_ffn[l])
    return x
```
